```python
import math
import jax, jax.numpy as jnp
from jax import lax
import numpy as np

D_MODEL = 1024
BATCH = 8
SEQ = 4096
DEPTH = 2

HEAD_DIM = 64
MIX_WIDTH = D_MODEL
A_WIDTH = MIX_WIDTH // 4
A_HEADS = A_WIDTH // HEAD_DIM
CHUNK = 128
B_WIDTH = MIX_WIDTH // 2
B_HEADS = B_WIDTH // HEAD_DIM
B_KV_HEADS = B_HEADS // 4
KV_WIDTH = B_KV_HEADS * HEAD_DIM
WINDOW = 128
BLOCK = WINDOW
N_BUCKETS = 32
MAX_DISTANCE = 128
C_WIDTH = MIX_WIDTH - A_WIDTH - B_WIDTH
CONV_WIDTH = 31
IN_WIDTH = 2 * A_WIDTH + B_WIDTH + 2 * KV_WIDTH + 2 * C_WIDTH
N_EXPERTS = 32
N_GROUPS = 4
EXPERTS_PER_GROUP = N_EXPERTS // N_GROUPS
TOP_K = 2
D_EXPERT = D_MODEL // 2
MOE_BLOCK = 128
ALPHA = (2 * DEPTH) ** 0.25
BETA = (8 * DEPTH) ** -0.25
LN_EPS = 1e-5
ADA_SCALE = 0.2
NEG_INF = -1e30

kernel_name = 'hybrid_gmlp_swa_conformer_grouped_moe_encoder'


def _layer_norm(x, g, b):
    xf = x.astype(jnp.float32)
    mu = jnp.mean(xf, axis=-1, keepdims=True)
    var = jnp.mean(jnp.square(xf - mu), axis=-1, keepdims=True)
    return ((xf - mu) * lax.rsqrt(var + LN_EPS)).astype(x.dtype) * g + b


def _t5_bucket(rel):
    nb = N_BUCKETS // 2
    max_exact = nb // 2
    ret = jnp.where(rel > 0, nb, 0)
    n = jnp.abs(rel)
    nf = jnp.maximum(n, 1).astype(jnp.float32)
    large = max_exact + (jnp.log(nf / max_exact) / math.log(MAX_DISTANCE / max_exact)
                         * (nb - max_exact)).astype(jnp.int32)
    large = jnp.minimum(large, nb - 1)
    return ret + jnp.where(n < max_exact, n, large)


def _spatial_gating(z, ln_g, ln_b, ws, bs):
    bsz, seq, _ = z.shape
    u, v = jnp.split(z, 2, axis=-1)
    v = _layer_norm(v, ln_g, ln_b).reshape(bsz, seq // CHUNK, CHUNK, A_HEADS, HEAD_DIM)
    mixed = jnp.einsum('hpq,bnqhd->bnphd', ws, v) + bs.T[None, None, :, :, None]
    return u * mixed.reshape(bsz, seq, A_WIDTH)


def _banded_gqa(q, k, v, sink, rel_bias):
    bsz, seq = q.shape[:2]
    nblk = seq // BLOCK
    grp = B_HEADS // B_KV_HEADS
    qb = q.reshape(bsz, nblk, BLOCK, B_KV_HEADS, grp, HEAD_DIM)

    def band(t):
        tp = jnp.pad(t, ((0, 0), (BLOCK, BLOCK), (0, 0), (0, 0)))
        tp = tp.reshape(bsz, nblk + 2, BLOCK, B_KV_HEADS, HEAD_DIM)
        return jnp.concatenate([tp[:, :-2], tp[:, 1:-1], tp[:, 2:]], axis=2)

    kb, vb = band(k), band(v)
    s = jnp.einsum('bnqkgd,bnskd->bnkgqs', qb, kb,
                   preferred_element_type=jnp.float32) * (HEAD_DIM ** -0.5)
    qi = jnp.arange(BLOCK)
    kj = jnp.arange(3 * BLOCK)
    rel = kj[None, :] - BLOCK - qi[:, None]
    bias = rel_bias.astype(jnp.float32)[_t5_bucket(rel)]
    bias = jnp.transpose(bias, (2, 0, 1)).reshape(B_KV_HEADS, grp, BLOCK, 3 * BLOCK)
    kpos = (jnp.arange(nblk)[:, None] - 1) * BLOCK + kj[None, :]
    valid = (kpos >= 0) & (kpos < seq)
    mask = (jnp.abs(rel) <= WINDOW)[None] & valid[:, None, :]
    s = jnp.where(mask[None, :, None, None], s + bias, NEG_INF)
    sink_col = jnp.broadcast_to(sink.astype(jnp.float32).reshape(B_KV_HEADS, grp, 1, 1),
                                s.shape[:-1] + (1,))
    p = jax.nn.softmax(jnp.concatenate([s, sink_col], axis=-1), axis=-1)[..., :-1]
    o = jnp.einsum('bnkgqs,bnskd->bnqkgd', p.astype(vb.dtype), vb)
    return o.reshape(bsz, seq, B_WIDTH)


def _conformer_conv(z, w_dw, b_dw, ln_g, ln_b):
    a, g = jnp.split(z, 2, axis=-1)
    y = a * jax.nn.sigmoid(g)
    pad = CONV_WIDTH // 2
    y = lax.conv_general_dilated(y, w_dw[:, None, :], window_strides=(1,), padding=[(pad, pad)],
                                 dimension_numbers=('NWC', 'WIO', 'NWC'),
                                 feature_group_count=C_WIDTH) + b_dw
    return jax.nn.silu(_layer_norm(y, ln_g, ln_b))


def _route(h, router_w, router_bias):
    scores = jax.nn.sigmoid(jnp.einsum('td,de->te', h, router_w,
                                       preferred_element_type=jnp.float32))
    sel = (scores + router_bias.astype(jnp.float32)).reshape(-1, N_GROUPS, EXPERTS_PER_GROUP)
    group_score = jnp.sum(lax.top_k(sel, TOP_K)[0], axis=-1)
    g_idx = jnp.argmax(group_score, axis=-1).astype(jnp.int32)
    in_group = jnp.take_along_axis(sel, g_idx[:, None, None], axis=1)[:, 0]
    _, local = lax.top_k(in_group, TOP_K)
    expert = (g_idx[:, None] * EXPERTS_PER_GROUP + local).astype(jnp.int32)
    gates = jnp.take_along_axis(scores, expert, axis=1)
    gates = gates / jnp.sum(gates, axis=-1, keepdims=True)
    return expert, gates


def _moe(h, router_w, router_bias, w_gate, w_up, w_down):
    bsz, seq, d = h.shape
    t = bsz * seq
    hf = h.reshape(t, d)
    expert, gates = _route(hf, router_w, router_bias)
    tk = t * TOP_K
    e_flat = expert.reshape(tk)
    tok_flat = jnp.repeat(jnp.arange(t, dtype=jnp.int32), TOP_K)
    g_flat = gates.reshape(tk)
    order = jnp.argsort(e_flat)
    e_s, tok_s, g_s = e_flat[order], tok_flat[order], g_flat[order]
    counts = jnp.bincount(e_flat, length=N_EXPERTS)
    padded = (counts + MOE_BLOCK - 1) // MOE_BLOCK * MOE_BLOCK
    start = jnp.cumsum(counts) - counts
    pend = jnp.cumsum(padded)
    pstart = pend - padded
    dest = pstart[e_s] + jnp.arange(tk, dtype=jnp.int32) - start[e_s]
    n_blocks = -(-tk // MOE_BLOCK) + N_EXPERTS
    rows = n_blocks * MOE_BLOCK
    buf_tok = jnp.zeros((rows,), jnp.int32).at[dest].set(tok_s)
    buf_g = jnp.zeros((rows,), jnp.float32).at[dest].set(g_s)
    block_expert = jnp.minimum(
        jnp.searchsorted(pend, jnp.arange(n_blocks) * MOE_BLOCK, side='right'),
        N_EXPERTS - 1).astype(jnp.int32)
    xin = hf[buf_tok].reshape(n_blocks, MOE_BLOCK, d)

    def expert_block(args):
        xb, e = args
        return (jax.nn.silu(xb @ w_gate[e]) * (xb @ w_up[e])) @ w_down[e]

    yb = lax.map(expert_block, (xin, block_expert))
    out = jnp.zeros((t, d), h.dtype).at[buf_tok].add(
        yb.reshape(rows, d) * buf_g[:, None].astype(h.dtype))
    return out.reshape(bsz, seq, d)


def setup_inputs(seed: int = 0) -> dict:
    key = jax.random.key(seed)
    ks = iter(jax.random.split(key, 40))

    def nrm(shape, s=1.0):
        return s * jax.random.normal(next(ks), shape, jnp.float32)

    L = DEPTH
    return {
        'x': nrm((BATCH, SEQ, D_MODEL)),
        'c': nrm((BATCH, D_MODEL)),
        'ada_w': nrm((L, D_MODEL, 6 * D_MODEL), ADA_SCALE * D_MODEL ** -0.5),
        'ada_b': nrm((L, 6 * D_MODEL), 0.02),
        'w_in': nrm((L, D_MODEL, IN_WIDTH), D_MODEL ** -0.5),
        'b_in': nrm((L, IN_WIDTH), 0.02),
        'gmlp_ln_g': 1.0 + nrm((L, A_WIDTH), 0.02),
        'gmlp_ln_b': nrm((L, A_WIDTH), 0.02),
        'gmlp_ws': nrm((L, A_HEADS, CHUNK, CHUNK), CHUNK ** -0.5),
        'gmlp_bs': 1.0 + nrm((L, A_HEADS, CHUNK), 0.02),
        'attn_sink': nrm((L, B_HEADS), 0.5),
        'conv_w': nrm((L, CONV_WIDTH, C_WIDTH), CONV_WIDTH ** -0.5),
        'conv_b': nrm((L, C_WIDTH), 0.02),
        'conv_ln_g': 1.0 + nrm((L, C_WIDTH), 0.02),
        'conv_ln_b': nrm((L, C_WIDTH), 0.02),
        'w_out': nrm((L, MIX_WIDTH, D_MODEL), BETA * MIX_WIDTH ** -0.5),
        'b_out': nrm((L, D_MODEL), 0.02),
        'ln_mix_g': 1.0 + nrm((L, D_MODEL), 0.02),
        'ln_mix_b': nrm((L, D_MODEL), 0.02),
        'w_gate': nrm((L, N_EXPERTS, D_MODEL, D_EXPERT), D_MODEL ** -0.5),
        'w_up': nrm((L, N_EXPERTS, D_MODEL, D_EXPERT), D_MODEL ** -0.5),
        'w_down': nrm((L, N_EXPERTS, D_EXPERT, D_MODEL), BETA * D_EXPERT ** -0.5),
        'ln_ffn_g': 1.0 + nrm((L, D_MODEL), 0.02),
        'ln_ffn_b': nrm((L, D_MODEL), 0.02),
        'rel_bias': nrm((N_BUCKETS, B_HEADS), 0.5),
        'router_w': nrm((D_MODEL, N_EXPERTS), D_MODEL ** -0.5),
        'router_bias': nrm((N_EXPERTS,), 0.01),
    }


def reference(x, c, ada_w, ada_b, w_in, b_in, gmlp_ln_g, gmlp_ln_b, gmlp_ws, gmlp_bs,
              attn_sink, conv_w, conv_b, conv_ln_g, conv_ln_b, w_out, b_out,
              ln_mix_g, ln_mix_b, w_gate, w_up, w_down, ln_ffn_g, ln_ffn_b,
              rel_bias, router_w, router_bias):
    bsz, seq, _ = x.shape
    i1 = 2 * A_WIDTH
    i2 = i1 + B_WIDTH
    i3 = i2 + KV_WIDTH
    i4 = i3 + KV_WIDTH
    for l in range(DEPTH):
        mod = jnp.einsum('bd,de->be', jax.nn.silu(c), ada_w[l]) + ada_b[l]
        sh1, sc1, g1, sh2, sc2, g2 = jnp.split(mod[:, None, :], 6, axis=-1)

        h = x * (1.0 + sc1) + sh1
        z = h @ w_in[l] + b_in[l]
        ya = _spatial_gating(jax.nn.gelu(z[..., :i1]), gmlp_ln_g[l], gmlp_ln_b[l],
                             gmlp_ws[l], gmlp_bs[l])
        q = z[..., i1:i2].reshape(bsz, seq, B_HEADS, HEAD_DIM)
        k = z[..., i2:i3].reshape(bsz, seq, B_KV_HEADS, HEAD_DIM)
        v = z[..., i3:i4].reshape(bsz, seq, B_KV_HEADS, HEAD_DIM)
        yb = _banded_gqa(q, k, v, attn_sink[l], rel_bias)
        yc = _conformer_conv(z[..., i4:], conv_w[l], conv_b[l], conv_ln_g[l], conv_ln_b[l])
        y = jnp.concatenate([ya, yb, yc], axis=-1) @ w_out[l] + b_out[l]
        x = _layer_norm(ALPHA * x + (1.0 + g1) * y, ln_mix_g[l], ln_mix_b[l])

        h = x * (1.0 + sc2) + sh2
        y = _moe(h, router_w, router_bias, w_gate[l], w_up[l], w_down[l])
        x = _layer_norm(ALPHA * x + (1.0 + g2) * y, ln_ffn_g[l], ln_ffn_b[l])
    return x
```

```python
import functools
import math

import jax
import jax.numpy as jnp
from jax import lax
from jax.experimental import pallas as pl
from jax.experimental.pallas import tpu as pltpu

F32 = jnp.float32
BF16 = jnp.bfloat16
I32 = jnp.int32

D_MODEL = 1024
DEPTH = 2
HEAD_DIM = 64
A_WIDTH = 256
A_HEADS = 4
CHUNK = 128
B_WIDTH = 512
B_HEADS = 8
B_KV_HEADS = 2
KV_WIDTH = B_KV_HEADS * HEAD_DIM
WINDOW = 128
N_BUCKETS = 32
MAX_DISTANCE = 128
C_WIDTH = 256
CONV_WIDTH = 31
CONV_PAD = CONV_WIDTH // 2
IN_WIDTH = 2 * A_WIDTH + B_WIDTH + 2 * KV_WIDTH + 2 * C_WIDTH
N_EXPERTS = 32
N_GROUPS = 4
EXPERTS_PER_GROUP = N_EXPERTS // N_GROUPS
TOP_K = 2
D_EXPERT = D_MODEL // 2
ALPHA = (2 * DEPTH) ** 0.25
LN_EPS = 1e-5
NEG_INF = -1e30

LANES = 128
BF16_SUBLANES = 16
VMEM_LIMIT = 48 * 1024 * 1024

ADA_TN = 1536
TS = 512
TT = 512
CH = BF16_SUBLANES
R_TILE = 1536
BM = 512
NCH = BM // CH
HALO = 16

assert R_TILE >= TOP_K * TT + N_EXPERTS * (CH - 1) + CH
assert R_TILE >= 2 * NCH * CH


def _sigmoid(x):
    return 1.0 / (1.0 + jnp.exp(-x))


def _gelu_tanh(x):
    return x * (0.5 * (1.0 + jnp.tanh(0.7978845608028654 * (x + 0.044715 * (x * x * x)))))


def _ln(x, g, b):
    mu = jnp.mean(x, axis=-1, keepdims=True)
    xc = x - mu
    var = jnp.mean(xc * xc, axis=-1, keepdims=True)
    return xc * lax.rsqrt(var + LN_EPS) * g + b


def _dot(a, b):
    return jnp.dot(a, b, preferred_element_type=F32)


def _dot_nt(a, b):
    return lax.dot_general(a, b, (((1,), (1,)), ((), ())), preferred_element_type=F32)


def _ada_kernel(c_ref, w_ref, b_ref, o_ref):
    c = c_ref[...]
    s = (c * _sigmoid(c)).astype(BF16)
    o_ref[0] = _dot(s, w_ref[0].astype(BF16)) + b_ref[0]


def _ada_call(c, ada_w, ada_b):
    nl, d, n = ada_w.shape
    bsz = c.shape[0]
    return pl.pallas_call(
        _ada_kernel,
        out_shape=jax.ShapeDtypeStruct((nl, bsz, n), F32),
        grid=(nl, n // ADA_TN),
        in_specs=[
            pl.BlockSpec((bsz, d), lambda l, j: (0, 0)),
            pl.BlockSpec((1, d, ADA_TN), lambda l, j: (l, 0, j)),
            pl.BlockSpec((1, 1, ADA_TN), lambda l, j: (l, 0, j)),
        ],
        out_specs=pl.BlockSpec((1, bsz, ADA_TN), lambda l, j: (l, 0, j)),
        compiler_params=pltpu.CompilerParams(
            dimension_semantics=("arbitrary", "arbitrary"), vmem_limit_bytes=VMEM_LIMIT),
        name="ada_mod",
    )(c, ada_w, ada_b.reshape(nl, 1, n))


def _inproj_kernel(x_ref, mod_ref, w_ref, b_ref, lng_ref, lnb_ref, ws_ref, bsb_ref,
                   ya_ref, q_ref, k_ref, v_ref, yg_ref):
    x = x_ref[0]
    m = mod_ref[...]
    hb = (x * (1.0 + m[1:2]) + m[0:1]).astype(BF16)

    c0 = 0
    u = _gelu_tanh(_dot(hb, w_ref[:, c0:c0 + A_WIDTH]) + b_ref[:, c0:c0 + A_WIDTH])
    c0 += A_WIDTH
    v = _gelu_tanh(_dot(hb, w_ref[:, c0:c0 + A_WIDTH]) + b_ref[:, c0:c0 + A_WIDTH])
    c0 += A_WIDTH
    vb = _ln(v, lng_ref[...], lnb_ref[...]).astype(BF16)
    head_of_lane = lax.broadcasted_iota(I32, (CHUNK, A_WIDTH), 1) // HEAD_DIM
    for ch in range(TS // CHUNK):
        vc = vb[ch * CHUNK:(ch + 1) * CHUNK]
        acc = bsb_ref[...]
        for hh in range(A_HEADS):
            acc = acc + _dot(ws_ref[hh], jnp.where(head_of_lane == hh, vc, jnp.zeros_like(vc)))
        ya_ref[0, ch * CHUNK:(ch + 1) * CHUNK, :] = (u[ch * CHUNK:(ch + 1) * CHUNK] * acc).astype(BF16)

    zq = _dot(hb, w_ref[:, c0:c0 + B_WIDTH]) + b_ref[:, c0:c0 + B_WIDTH]
    q_ref[0] = (zq * (HEAD_DIM ** -0.5)).astype(BF16)
    c0 += B_WIDTH
    zkv = _dot(hb, w_ref[:, c0:c0 + 2 * KV_WIDTH]) + b_ref[:, c0:c0 + 2 * KV_WIDTH]
    k_ref[0] = zkv[:, :KV_WIDTH].astype(BF16)
    v_ref[0] = zkv[:, KV_WIDTH:].astype(BF16)
    c0 += 2 * KV_WIDTH
    za = _dot(hb, w_ref[:, c0:c0 + C_WIDTH]) + b_ref[:, c0:c0 + C_WIDTH]
    c0 += C_WIDTH
    zg = _dot(hb, w_ref[:, c0:c0 + C_WIDTH]) + b_ref[:, c0:c0 + C_WIDTH]
    yg_ref[0] = (za * _sigmoid(zg)).astype(BF16)


def _inproj_call(x, mod, layer, w_in, b_in, ln_g, ln_b, ws, bsb):
    bsz, seq, d = x.shape
    grid = (bsz, seq // TS)
    const2 = lambda b, i: (0, 0)
    row = lambda b, i: (b, i, 0)

    def out(width):
        return jax.ShapeDtypeStruct((bsz, seq, width), BF16), pl.BlockSpec((1, TS, width), row)

    outs = [out(A_WIDTH), out(B_WIDTH), out(KV_WIDTH), out(KV_WIDTH), out(C_WIDTH)]
    return pl.pallas_call(
        _inproj_kernel,
        out_shape=[o[0] for o in outs],
        grid=grid,
        in_specs=[
            pl.BlockSpec((1, TS, d), row),
            pl.BlockSpec((None, None, 6, d), lambda b, i: (layer, b, 0, 0)),
            pl.BlockSpec((d, IN_WIDTH), const2),
            pl.BlockSpec((1, IN_WIDTH), const2),
            pl.BlockSpec((1, A_WIDTH), const2),
            pl.BlockSpec((1, A_WIDTH), const2),
            pl.BlockSpec((A_HEADS, CHUNK, CHUNK), lambda b, i: (0, 0, 0)),
            pl.BlockSpec((CHUNK, A_WIDTH), const2),
        ],
        out_specs=[o[1] for o in outs],
        compiler_params=pltpu.CompilerParams(
            dimension_semantics=("arbitrary", "arbitrary"), vmem_limit_bytes=VMEM_LIMIT),
        name="inproj_gmlp",
    )(x, mod, w_in, b_in, ln_g, ln_b, ws, bsb)


def _first_argmax(vals, iota_f, width):
    m = jnp.max(vals, axis=0, keepdims=True)
    idx = jnp.min(jnp.where(vals == m, iota_f, float(width)), axis=0, keepdims=True)
    return m, idx


def _mixer_kernel(seq_len, q_ref, kp_ref, kc_ref, kn_ref, vp_ref, vc_ref, vn_ref, bias_ref, sink_ref,
                  ygp_ref, ygc_ref, ygn_ref, ya_ref, x_ref, mod_ref, wo_ref, bo_ref,
                  cw_ref, cb_ref, clg_ref, clb_ref, lng_ref, lnb_ref, rwh_ref, rwl_ref, rb_ref,
                  x1_ref, h2_ref, eid_ref, gate_ref, cnt_ref, conv_scr, yb_scr):
    i = pl.program_id(1)
    n_i = pl.num_programs(1)
    t0 = i * TT
    m = mod_ref[...]

    kfull = jnp.concatenate([kp_ref[0], kc_ref[0], kn_ref[0]], axis=0)
    vfull = jnp.concatenate([vp_ref[0], vc_ref[0], vn_ref[0]], axis=0)
    grp = B_HEADS // B_KV_HEADS
    col = lax.broadcasted_iota(I32, (1, 3 * WINDOW), 1)
    for jb in range(TT // WINDOW):
        kpos = t0 + (jb - 1) * WINDOW + col
        valid = (kpos >= 0) & (kpos < seq_len)
        kb = kfull[jb * WINDOW:(jb + 3) * WINDOW]
        vb = vfull[jb * WINDOW:(jb + 3) * WINDOW]
        for h in range(B_HEADS):
            g = h // grp
            qh = q_ref[0, jb * WINDOW:(jb + 1) * WINDOW, h * HEAD_DIM:(h + 1) * HEAD_DIM]
            s = _dot_nt(qh, kb[:, g * HEAD_DIM:(g + 1) * HEAD_DIM]) + bias_ref[h]
            s = jnp.where(valid, s, NEG_INF)
            sink = sink_ref[:, h:h + 1]
            mx = jnp.maximum(jnp.max(s, axis=-1, keepdims=True), sink)
            p = jnp.exp(s - mx)
            den = jnp.sum(p, axis=-1, keepdims=True) + jnp.exp(sink - mx)
            o = _dot(p.astype(BF16), vb[:, g * HEAD_DIM:(g + 1) * HEAD_DIM]) / den
            yb_scr[jb * WINDOW:(jb + 1) * WINDOW, h * HEAD_DIM:(h + 1) * HEAD_DIM] = o

    prev_ok = jnp.where(i > 0, 1.0, 0.0)
    next_ok = jnp.where(i < n_i - 1, 1.0, 0.0)
    conv_scr[0:HALO, :] = ygp_ref[0].astype(F32) * prev_ok
    conv_scr[HALO:HALO + TT, :] = ygc_ref[0].astype(F32)
    conv_scr[HALO + TT:HALO + TT + HALO, :] = ygn_ref[0].astype(F32) * next_ok
    acc = jnp.zeros((TT, C_WIDTH), F32) + cb_ref[...]
    for w in range(CONV_WIDTH):
        off = HALO - CONV_PAD + w
        acc = acc + conv_scr[off:off + TT, :] * cw_ref[w:w + 1, :]
    yc = _ln(acc, clg_ref[...], clb_ref[...])
    yc = yc * _sigmoid(yc)

    y = (_dot(ya_ref[0], wo_ref[0:A_WIDTH, :])
         + _dot(yb_scr[...].astype(BF16), wo_ref[A_WIDTH:A_WIDTH + B_WIDTH, :])
         + _dot(yc.astype(BF16), wo_ref[A_WIDTH + B_WIDTH:, :]) + bo_ref[...])
    x1 = _ln(ALPHA * x_ref[0] + (1.0 + m[2:3]) * y, lng_ref[...], lnb_ref[...])
    x1_ref[0] = x1

    h2 = x1 * (1.0 + m[4:5]) + m[3:4]
    hi = h2.astype(BF16)
    h2_ref[0] = hi
    lo = (h2 - hi.astype(F32)).astype(BF16)
    logits = _dot_nt(rwh_ref[...], hi) + _dot_nt(rwh_ref[...], lo) + _dot_nt(rwl_ref[...], hi)
    scores = _sigmoid(logits)
    sel = scores + rb_ref[...]
    iota_f = lax.broadcasted_iota(I32, (EXPERTS_PER_GROUP, TT), 0).astype(F32)
    best = None
    for g in range(N_GROUPS):
        sl = slice(g * EXPERTS_PER_GROUP, (g + 1) * EXPERTS_PER_GROUP)
        sg = sel[sl]
        m1, i1 = _first_argmax(sg, iota_f, EXPERTS_PER_GROUP)
        m2, i2 = _first_argmax(jnp.where(iota_f == i1, -jnp.inf, sg), iota_f, EXPERTS_PER_GROUP)
        sc = scores[sl]
        s1 = jnp.sum(jnp.where(iota_f == i1, sc, 0.0), axis=0, keepdims=True)
        s2 = jnp.sum(jnp.where(iota_f == i2, sc, 0.0), axis=0, keepdims=True)
        cand = (m1 + m2, i1 + g * EXPERTS_PER_GROUP, i2 + g * EXPERTS_PER_GROUP, s1, s2)
        if best is None:
            best = cand
        else:
            take = cand[0] > best[0]
            best = tuple(jnp.where(take, c, b) for c, b in zip(cand, best))
    _, e1, e2, s1, s2 = best
    eid = jnp.concatenate([e1, e2], axis=0).astype(I32)
    eid_ref[0] = eid
    gate_ref[0] = jnp.concatenate([s1, s2], axis=0) / (s1 + s2)
    iota_e = lax.broadcasted_iota(I32, (N_EXPERTS, TT), 0)
    member = jnp.where((iota_e == eid[0:1]) | (iota_e == eid[1:2]), 1.0, 0.0)
    cnt_ref[0] = jnp.broadcast_to(jnp.sum(member, axis=1, keepdims=True), (N_EXPERTS, LANES))


def _mixer_call(seq_len, q, k, v, bias, sink, yg, ya, x, mod, layer, w_out, b_out, conv_w, conv_b,
                conv_ln_g, conv_ln_b, ln_g, ln_b, rw_hi, rw_lo, rbias):
    bsz, seq, d = x.shape
    nt = seq // TT
    kb = TT // WINDOW
    hb = TT // HALO
    const2 = lambda b, i: (0, 0)
    row = lambda b, i: (b, i, 0)
    prev_k = lambda b, i: (b, jnp.maximum(i * kb - 1, 0), 0)
    next_k = lambda b, i: (b, jnp.minimum((i + 1) * kb, seq // WINDOW - 1), 0)
    prev_h = lambda b, i: (b, jnp.maximum(i * hb - 1, 0), 0)
    next_h = lambda b, i: (b, jnp.minimum((i + 1) * hb, seq // HALO - 1), 0)
    tile = lambda b, i: (b * nt + i, 0, 0)
    return pl.pallas_call(
        functools.partial(_mixer_kernel, seq_len),
        out_shape=[
            jax.ShapeDtypeStruct((bsz, seq, d), F32),
            jax.ShapeDtypeStruct((bsz, seq, d), BF16),
            jax.ShapeDtypeStruct((bsz * nt, TOP_K, TT), I32),
            jax.ShapeDtypeStruct((bsz * nt, TOP_K, TT), F32),
            jax.ShapeDtypeStruct((bsz * nt, N_EXPERTS, LANES), F32),
        ],
        grid=(bsz, nt),
        in_specs=[
            pl.BlockSpec((1, TT, B_WIDTH), row),
            pl.BlockSpec((1, WINDOW, KV_WIDTH), prev_k),
            pl.BlockSpec((1, TT, KV_WIDTH), row),
            pl.BlockSpec((1, WINDOW, KV_WIDTH), next_k),
            pl.BlockSpec((1, WINDOW, KV_WIDTH), prev_k),
            pl.BlockSpec((1, TT, KV_WIDTH), row),
            pl.BlockSpec((1, WINDOW, KV_WIDTH), next_k),
            pl.BlockSpec((B_HEADS, WINDOW, 3 * WINDOW), lambda b, i: (0, 0, 0)),
            pl.BlockSpec((1, B_HEADS), const2),
            pl.BlockSpec((1, HALO, C_WIDTH), prev_h),
            pl.BlockSpec((1, TT, C_WIDTH), row),
            pl.BlockSpec((1, HALO, C_WIDTH), next_h),
            pl.BlockSpec((1, TT, A_WIDTH), row),
            pl.BlockSpec((1, TT, d), row),
            pl.BlockSpec((None, None, 6, d), lambda b, i: (layer, b, 0, 0)),
            pl.BlockSpec((d, d), const2),
            pl.BlockSpec((1, d), const2),
            pl.BlockSpec((CONV_WIDTH, C_WIDTH), const2),
            pl.BlockSpec((1, C_WIDTH), const2),
            pl.BlockSpec((1, C_WIDTH), const2),
            pl.BlockSpec((1, C_WIDTH), const2),
            pl.BlockSpec((1, d), const2),
            pl.BlockSpec((1, d), const2),
            pl.BlockSpec((N_EXPERTS, d), const2),
            pl.BlockSpec((N_EXPERTS, d), const2),
            pl.BlockSpec((N_EXPERTS, 1), const2),
        ],
        out_specs=[
            pl.BlockSpec((1, TT, d), row),
            pl.BlockSpec((1, TT, d), row),
            pl.BlockSpec((1, TOP_K, TT), tile),
            pl.BlockSpec((1, TOP_K, TT), tile),
            pl.BlockSpec((1, N_EXPERTS, LANES), tile),
        ],
        scratch_shapes=[
            pltpu.VMEM((TT + 2 * HALO, C_WIDTH), F32),
            pltpu.VMEM((TT, B_WIDTH), F32),
        ],
        compiler_params=pltpu.CompilerParams(
            dimension_semantics=("arbitrary", "arbitrary"), vmem_limit_bytes=VMEM_LIMIT),
        name="mixer_out_router",
    )(q, k, k, k, v, v, v, bias, sink, yg, yg, yg, ya, x, mod, w_out, b_out, conv_w, conv_b,
      conv_ln_g, conv_ln_b, ln_g, ln_b, rw_hi, rw_lo, rbias)


def _sort_kernel(n_tiles, eid_ref, padoff_ref, h2_ref, xs_ref, pos_ref):
    n = pl.program_id(0)

    @pl.when(n < n_tiles)
    def _():
        eid = eid_ref[0]
        iota_e = lax.broadcasted_iota(I32, (N_EXPERTS, TT), 0)
        e0 = iota_e == eid[0:1]
        e1 = iota_e == eid[1:2]
        member = jnp.where(e0 | e1, 1.0, 0.0).astype(BF16)
        r_i = lax.broadcasted_iota(I32, (TT, TT), 0)
        c_i = lax.broadcasted_iota(I32, (TT, TT), 1)
        before = jnp.where(r_i < c_i, 1.0, 0.0).astype(BF16)
        rank = _dot(member, before)
        posf = padoff_ref[0] + rank
        pos0 = jnp.sum(jnp.where(e0, posf, 0.0), axis=0, keepdims=True).astype(I32)
        pos1 = jnp.sum(jnp.where(e1, posf, 0.0), axis=0, keepdims=True).astype(I32)
        pos_ref[0] = jnp.concatenate([pos0, pos1], axis=0)
        iota_r = lax.broadcasted_iota(I32, (R_TILE, TT), 0)
        onehot = jnp.where((iota_r == pos0) | (iota_r == pos1), 1.0, 0.0).astype(BF16)
        xs_ref[...] = _dot(onehot, h2_ref[...]).astype(BF16)

    @pl.when(n >= n_tiles)
    def _():
        xs_ref[...] = jnp.zeros_like(xs_ref)
        pos_ref[0] = jnp.zeros((TOP_K, TT), I32)


def _sort_call(eid, padoff, h2):
    n_tiles = eid.shape[0]
    d = h2.shape[-1]
    clamp = lambda n: jnp.minimum(n, n_tiles - 1)
    return pl.pallas_call(
        functools.partial(_sort_kernel, n_tiles),
        out_shape=[
            jax.ShapeDtypeStruct(((n_tiles + 1) * R_TILE, d), BF16),
            jax.ShapeDtypeStruct((n_tiles + 1, TOP_K, TT), I32),
        ],
        grid=(n_tiles + 1,),
        in_specs=[
            pl.BlockSpec((1, TOP_K, TT), lambda n: (clamp(n), 0, 0)),
            pl.BlockSpec((1, N_EXPERTS, 1), lambda n: (clamp(n), 0, 0)),
            pl.BlockSpec((TT, d), lambda n: (clamp(n), 0)),
        ],
        out_specs=[
            pl.BlockSpec((R_TILE, d), lambda n: (n, 0)),
            pl.BlockSpec((1, TOP_K, TT), lambda n: (n, 0, 0)),
        ],
        compiler_params=pltpu.CompilerParams(
            dimension_semantics=("arbitrary",), vmem_limit_bytes=VMEM_LIMIT),
        name="moe_sort",
    )(eid, padoff, h2)


def _expert_kernel(src_ref, dst_ref, bexp_ref, nused_ref, xs_hbm, wg_ref, wu_ref, wd_ref, ys_hbm,
                   xbuf, ybuf, wgb, wub, wdb, sem_in, sem_out):
    blk = pl.program_id(0)
    n_blk = pl.num_programs(0)
    nused = nused_ref[0]
    slot = blk % 2

    def in_copy(b, s, c):
        return pltpu.make_async_copy(
            xs_hbm.at[pl.ds(pl.multiple_of(src_ref[b * NCH + c] * CH, CH), CH)],
            xbuf.at[s, pl.ds(c * CH, CH)], sem_in.at[s])

    def out_copy(b, s, c):
        return pltpu.make_async_copy(
            ybuf.at[s, pl.ds(c * CH, CH)],
            ys_hbm.at[pl.ds(pl.multiple_of(dst_ref[b * NCH + c] * CH, CH), CH)], sem_out.at[s])

    @pl.when((blk == 0) & (nused > 0))
    def _():
        for c in range(NCH):
            in_copy(0, 0, c).start()

    @pl.when(blk + 1 < nused)
    def _():
        for c in range(NCH):
            in_copy(blk + 1, 1 - slot, c).start()

    @pl.when(blk < nused)
    def _():
        expert = bexp_ref[blk]
        prev_expert = bexp_ref[jnp.maximum(blk - 1, 0)]

        @pl.when((blk == 0) | (expert != prev_expert))
        def _():
            wgb[...] = wg_ref[...].astype(BF16)
            wub[...] = wu_ref[...].astype(BF16)
            wdb[...] = wd_ref[...].astype(BF16)

        for c in range(NCH):
            in_copy(blk, slot, c).wait()
        x = xbuf[slot]
        g = _dot(x, wgb[...])
        u = _dot(x, wub[...])
        hmid = ((g * _sigmoid(g)) * u).astype(BF16)
        y = _dot(hmid, wdb[...]).astype(BF16)

        @pl.when(blk >= 2)
        def _():
            for c in range(NCH):
                out_copy(blk - 2, slot, c).wait()

        ybuf[slot] = y
        for c in range(NCH):
            out_copy(blk, slot, c).start()

    @pl.when(blk == n_blk - 1)
    def _():
        @pl.when(nused >= 2)
        def _():
            for c in range(NCH):
                out_copy(nused - 2, nused % 2, c).wait()

        @pl.when(nused >= 1)
        def _():
            for c in range(NCH):
                out_copy(nused - 1, (nused - 1) % 2, c).wait()


def _expert_call(n_blocks, src, dst, bexp, nused, xs, layer, w_gate, w_up, w_down):
    d = xs.shape[-1]
    de = w_gate.shape[-1]
    grid_spec = pltpu.PrefetchScalarGridSpec(
        num_scalar_prefetch=4,
        grid=(n_blocks,),
        in_specs=[
            pl.BlockSpec(memory_space=pl.ANY),
            pl.BlockSpec((None, None, d, de), lambda b, s, t, e, n: (layer, e[b], 0, 0)),
            pl.BlockSpec((None, None, d, de), lambda b, s, t, e, n: (layer, e[b], 0, 0)),
            pl.BlockSpec((None, None, de, d), lambda b, s, t, e, n: (layer, e[b], 0, 0)),
        ],
        out_specs=pl.BlockSpec(memory_space=pl.ANY),
        scratch_shapes=[
            pltpu.VMEM((2, BM, d), BF16),
            pltpu.VMEM((2, BM, d), BF16),
            pltpu.VMEM((d, de), BF16),
            pltpu.VMEM((d, de), BF16),
            pltpu.VMEM((de, d), BF16),
            pltpu.SemaphoreType.DMA((2,)),
            pltpu.SemaphoreType.DMA((2,)),
        ],
    )
    return pl.pallas_call(
        _expert_kernel,
        out_shape=jax.ShapeDtypeStruct(xs.shape, xs.dtype),
        grid_spec=grid_spec,
        input_output_aliases={4: 0},
        compiler_params=pltpu.CompilerParams(
            dimension_semantics=("arbitrary",), vmem_limit_bytes=VMEM_LIMIT),
        name="moe_experts",
    )(src, dst, bexp, nused, xs, w_gate, w_up, w_down)


def _combine_kernel(ys_ref, pos_ref, gate_ref, x1_ref, mod_ref, lng_ref, lnb_ref, o_ref):
    pos = pos_ref[0]
    gate = gate_ref[0]
    m = mod_ref[...]
    iota_r = lax.broadcasted_iota(I32, (R_TILE, TT), 0)
    row_gate = (jnp.where(iota_r == pos[0:1], gate[0:1], 0.0)
                + jnp.where(iota_r == pos[1:2], gate[1:2], 0.0))
    gs = jnp.sum(row_gate, axis=1, keepdims=True)
    ysc = (ys_ref[...].astype(F32) * gs).astype(BF16)
    posc = jnp.transpose(pos.astype(F32))
    iota_c = lax.broadcasted_iota(I32, (TT, R_TILE), 1).astype(F32)
    pick = jnp.where((iota_c == posc[:, 0:1]) | (iota_c == posc[:, 1:2]), 1.0, 0.0).astype(BF16)
    y = _dot(pick, ysc)
    o_ref[...] = _ln(ALPHA * x1_ref[...] + (1.0 + m[5:6]) * y, lng_ref[...], lnb_ref[...])


def _combine_call(ys, pos, gate, x1, mod, layer, tiles_per_seq, ln_g, ln_b):
    t, d = x1.shape
    n_tiles = t // TT
    const2 = lambda n: (0, 0)
    return pl.pallas_call(
        _combine_kernel,
        out_shape=jax.ShapeDtypeStruct((t, d), F32),
        grid=(n_tiles,),
        in_specs=[
            pl.BlockSpec((R_TILE, d), lambda n: (n, 0)),
            pl.BlockSpec((1, TOP_K, TT), lambda n: (n, 0, 0)),
            pl.BlockSpec((1, TOP_K, TT), lambda n: (n, 0, 0)),
            pl.BlockSpec((TT, d), lambda n: (n, 0)),
            pl.BlockSpec((None, None, 6, d), lambda n: (layer, n // tiles_per_seq, 0, 0)),
            pl.BlockSpec((1, d), const2),
            pl.BlockSpec((1, d), const2),
        ],
        out_specs=pl.BlockSpec((TT, d), lambda n: (n, 0)),
        compiler_params=pltpu.CompilerParams(
            dimension_semantics=("arbitrary",), vmem_limit_bytes=VMEM_LIMIT),
        name="moe_combine",
    )(ys, pos, gate, x1, mod, ln_g, ln_b)


def _dispatch_plan(cnt, n_blocks):
    n_tiles = cnt.shape[0]
    chunks_per_tile = R_TILE // CH
    nch = (cnt + CH - 1) // CH
    padoff_ch = jnp.cumsum(nch, axis=1) - nch
    tot = jnp.sum(nch, axis=0)
    totpad = (tot + NCH - 1) // NCH * NCH
    eend = jnp.cumsum(totpad)
    ebase = eend - totpad
    tend = jnp.cumsum(nch, axis=0)
    tbase = tend - nch
    slot = jnp.arange(n_blocks * NCH, dtype=I32)
    e_s = jnp.minimum(jnp.sum(eend[None, :] <= slot[:, None], axis=1), N_EXPERTS - 1).astype(I32)
    local = slot - ebase[e_s]
    valid = (local < tot[e_s]) & (slot < eend[-1])
    tend_e = tend.T[e_s]
    n_s = jnp.minimum(jnp.sum(tend_e <= local[:, None], axis=1), n_tiles - 1).astype(I32)
    src = n_s * chunks_per_tile + padoff_ch[n_s, e_s] + (local - tbase[n_s, e_s])
    zero_chunk = chunks_per_tile - 1
    src = jnp.where(valid, src, zero_chunk).astype(I32)
    blk = slot // NCH
    scratch = n_tiles * chunks_per_tile + (blk % 2) * NCH + slot % NCH
    dst = jnp.where(valid, src, scratch).astype(I32)
    bexp = e_s[::NCH]
    nused = (eend[-1] // NCH).astype(I32).reshape(1)
    padoff = (padoff_ch * CH).astype(F32)[:, :, None]
    return padoff, src, dst, bexp, nused


def _t5_bucket(rel):
    nb = N_BUCKETS // 2
    max_exact = nb // 2
    ret = jnp.where(rel > 0, nb, 0)
    n = jnp.abs(rel)
    nf = jnp.maximum(n, 1).astype(jnp.float32)
    large = max_exact + (jnp.log(nf / max_exact) / math.log(MAX_DISTANCE / max_exact)
                         * (nb - max_exact)).astype(jnp.int32)
    large = jnp.minimum(large, nb - 1)
    return ret + jnp.where(n < max_exact, n, large)


def _band_bias(rel_bias):
    qi = jnp.arange(WINDOW)
    kj = jnp.arange(3 * WINDOW)
    rel = kj[None, :] - WINDOW - qi[:, None]
    bias = rel_bias.astype(F32)[_t5_bucket(rel)]
    bias = jnp.transpose(bias, (2, 0, 1))
    return jnp.where((jnp.abs(rel) <= WINDOW)[None], bias, NEG_INF)


def kernel(x, c, ada_w, ada_b, w_in, b_in, gmlp_ln_g, gmlp_ln_b, gmlp_ws, gmlp_bs, attn_sink, conv_w,
           conv_b, conv_ln_g, conv_ln_b, w_out, b_out, ln_mix_g, ln_mix_b, w_gate, w_up, w_down,
           ln_ffn_g, ln_ffn_b, rel_bias, router_w, router_bias):
    bsz, seq, d = x.shape
    n_layers = ada_w.shape[0]
    t = bsz * seq
    nt = seq // TT
    n_tiles = t // TT
    max_chunks = n_tiles * (TOP_K * TT // CH + N_EXPERTS) + N_EXPERTS * (NCH - 1)
    n_blocks = -(-max_chunks // NCH)

    mod = _ada_call(c, ada_w, ada_b).reshape(n_layers, bsz, 6, d)
    bias = _band_bias(rel_bias)
    rw_t = router_w.T
    rw_hi = rw_t.astype(BF16)
    rw_lo = (rw_t - rw_hi.astype(F32)).astype(BF16)
    rbias = router_bias.astype(F32).reshape(N_EXPERTS, 1)
    row = lambda a: a.reshape(1, -1)

    for l in range(n_layers):
        bsb = jnp.repeat(gmlp_bs[l].T, HEAD_DIM, axis=1)
        ya, q, k, v, yg = _inproj_call(
            x, mod, l, w_in[l].astype(BF16), row(b_in[l]), row(gmlp_ln_g[l]), row(gmlp_ln_b[l]),
            gmlp_ws[l].astype(BF16), bsb)
        x1, h2, eid, gate, cntb = _mixer_call(
            seq, q, k, v, bias, row(attn_sink[l]), yg, ya, x, mod, l, w_out[l].astype(BF16),
            row(b_out[l]), conv_w[l], row(conv_b[l]), row(conv_ln_g[l]), row(conv_ln_b[l]),
            row(ln_mix_g[l]), row(ln_mix_b[l]), rw_hi, rw_lo, rbias)
        cnt = cntb[:, :, 0].astype(I32)
        padoff, src, dst, bexp, nused = _dispatch_plan(cnt, n_blocks)
        xs, pos = _sort_call(eid, padoff, h2.reshape(t, d))
        ys = _expert_call(n_blocks, src, dst, bexp, nused, xs, l, w_gate, w_up, w_down)
        x = _combine_call(ys, pos, gate, x1.reshape(t, d), mod, l, nt, row(ln_ffn_g[l]),
                          row(ln_ffn_b[l])).reshape(bsz, seq, d)
    return x
```

```python
import functools
import math

import jax
import jax.numpy as jnp
from jax import lax
from jax.experimental import pallas as pl
from jax.experimental.pallas import tpu as pltpu

F32 = jnp.float32
BF16 = jnp.bfloat16
I32 = jnp.int32

D_MODEL = 1024
DEPTH = 2
HEAD_DIM = 64
A_WIDTH = 256
A_HEADS = 4
CHUNK = 128
B_WIDTH = 512
B_HEADS = 8
B_KV_HEADS = 2
KV_WIDTH = B_KV_HEADS * HEAD_DIM
WINDOW = 128
N_BUCKETS = 32
MAX_DISTANCE = 128
C_WIDTH = 256
CONV_WIDTH = 31
CONV_PAD = CONV_WIDTH // 2
IN_WIDTH = 2 * A_WIDTH + B_WIDTH + 2 * KV_WIDTH + 2 * C_WIDTH
N_EXPERTS = 32
N_GROUPS = 4
EXPERTS_PER_GROUP = N_EXPERTS // N_GROUPS
TOP_K = 2
D_EXPERT = D_MODEL // 2
ALPHA = (2 * DEPTH) ** 0.25
LN_EPS = 1e-5
NEG_INF = -1e30
LOG2E = 1.4426950408889634
Q_SCALE = HEAD_DIM ** -0.5 * LOG2E

LANES = 128
SUBLANES = 8
BF16_SUBLANES = 16
VMEM_LIMIT = 48 * 1024 * 1024

ADA_TN = 1536
TS = 512
TT = 512
CH = BF16_SUBLANES
R_TILE = 1536
BM = 512
NCH = BM // CH
HALO = 16

assert R_TILE >= TOP_K * TT + N_EXPERTS * (CH - 1) + CH
assert R_TILE >= 2 * NCH * CH


def _sigmoid(x):
    return 1.0 / (1.0 + jnp.exp(-x))


def _gelu_tanh(x):
    return x * (0.5 * (1.0 + jnp.tanh(0.7978845608028654 * (x + 0.044715 * (x * x * x)))))


def _ln(x, g, b):
    mu = jnp.mean(x, axis=-1, keepdims=True)
    xc = x - mu
    var = jnp.mean(xc * xc, axis=-1, keepdims=True)
    return xc * lax.rsqrt(var + LN_EPS) * g + b


def _dot(a, b):
    return jnp.dot(a, b, preferred_element_type=F32)


def _dot_nt(a, b):
    return lax.dot_general(a, b, (((1,), (1,)), ((), ())), preferred_element_type=F32)


def _ada_kernel(c_ref, w_ref, b_ref, o_ref):
    c = c_ref[...]
    s = (c * _sigmoid(c)).astype(BF16)
    o_ref[0] = _dot(s, w_ref[0].astype(BF16)) + b_ref[0]


def _ada_call(c, ada_w, ada_b):
    nl, d, n = ada_w.shape
    bsz = c.shape[0]
    return pl.pallas_call(
        _ada_kernel,
        out_shape=jax.ShapeDtypeStruct((nl, bsz, n), F32),
        grid=(nl, n // ADA_TN),
        in_specs=[
            pl.BlockSpec((bsz, d), lambda l, j: (0, 0)),
            pl.BlockSpec((1, d, ADA_TN), lambda l, j: (l, 0, j)),
            pl.BlockSpec((1, 1, ADA_TN), lambda l, j: (l, 0, j)),
        ],
        out_specs=pl.BlockSpec((1, bsz, ADA_TN), lambda l, j: (l, 0, j)),
        compiler_params=pltpu.CompilerParams(
            dimension_semantics=("arbitrary", "arbitrary"), vmem_limit_bytes=VMEM_LIMIT),
        name="ada_mod",
    )(c, ada_w, ada_b.reshape(nl, 1, n))


def _inproj_kernel(x_ref, mod_ref, w_ref, b_ref, wqt_ref, bq_ref, wvt_ref, bv_ref, lng_ref, lnb_ref,
                   ws_ref, bsb_ref, ya_ref, qt_ref, k_ref, vt_ref, yg_ref):
    x = x_ref[0]
    m = mod_ref[...]
    hb = (x * (1.0 + m[1:2]) + m[0:1]).astype(BF16)

    c0 = 0
    u = _gelu_tanh(_dot(hb, w_ref[:, c0:c0 + A_WIDTH]) + b_ref[:, c0:c0 + A_WIDTH])
    c0 += A_WIDTH
    v = _gelu_tanh(_dot(hb, w_ref[:, c0:c0 + A_WIDTH]) + b_ref[:, c0:c0 + A_WIDTH])
    c0 += A_WIDTH
    vb = _ln(v, lng_ref[...], lnb_ref[...]).astype(BF16)
    head_of_lane = lax.broadcasted_iota(I32, (CHUNK, A_WIDTH), 1) // HEAD_DIM
    for ch in range(TS // CHUNK):
        vc = vb[ch * CHUNK:(ch + 1) * CHUNK]
        acc = bsb_ref[...]
        for hh in range(A_HEADS):
            acc = acc + _dot(ws_ref[hh], jnp.where(head_of_lane == hh, vc, jnp.zeros_like(vc)))
        ya_ref[0, ch * CHUNK:(ch + 1) * CHUNK, :] = (u[ch * CHUNK:(ch + 1) * CHUNK] * acc).astype(BF16)

    qt_ref[0] = ((_dot_nt(wqt_ref[...], hb) + bq_ref[...]) * Q_SCALE).astype(BF16)
    c0 += B_WIDTH
    k_ref[0] = (_dot(hb, w_ref[:, c0:c0 + KV_WIDTH]) + b_ref[:, c0:c0 + KV_WIDTH]).astype(BF16)
    c0 += KV_WIDTH
    vt_ref[0] = (_dot_nt(wvt_ref[...], hb) + bv_ref[...]).astype(BF16)
    c0 += KV_WIDTH
    za = _dot(hb, w_ref[:, c0:c0 + C_WIDTH]) + b_ref[:, c0:c0 + C_WIDTH]
    c0 += C_WIDTH
    zg = _dot(hb, w_ref[:, c0:c0 + C_WIDTH]) + b_ref[:, c0:c0 + C_WIDTH]
    yg_ref[0] = (za * _sigmoid(zg)).astype(BF16)


def _inproj_call(x, mod, layer, w_in, b_in, ln_g, ln_b, ws, bsb):
    bsz, seq, d = x.shape
    grid = (bsz, seq // TS)
    const2 = lambda b, i: (0, 0)
    row = lambda b, i: (b, i, 0)
    colblk = lambda b, i: (b, 0, i)
    q0 = 2 * A_WIDTH
    v0 = q0 + B_WIDTH + KV_WIDTH
    wb = w_in.astype(BF16)
    wqt = w_in[:, q0:q0 + B_WIDTH].T.astype(BF16)
    wvt = w_in[:, v0:v0 + KV_WIDTH].T.astype(BF16)
    bq = b_in[q0:q0 + B_WIDTH].reshape(B_WIDTH, 1)
    bv = b_in[v0:v0 + KV_WIDTH].reshape(KV_WIDTH, 1)

    def out(width):
        return jax.ShapeDtypeStruct((bsz, seq, width), BF16), pl.BlockSpec((1, TS, width), row)

    def out_t(width):
        return jax.ShapeDtypeStruct((bsz, width, seq), BF16), pl.BlockSpec((1, width, TS), colblk)

    outs = [out(A_WIDTH), out_t(B_WIDTH), out(KV_WIDTH), out_t(KV_WIDTH), out(C_WIDTH)]
    return pl.pallas_call(
        _inproj_kernel,
        out_shape=[o[0] for o in outs],
        grid=grid,
        in_specs=[
            pl.BlockSpec((1, TS, d), row),
            pl.BlockSpec((None, None, 6, d), lambda b, i: (layer, b, 0, 0)),
            pl.BlockSpec((d, IN_WIDTH), const2),
            pl.BlockSpec((1, IN_WIDTH), const2),
            pl.BlockSpec((B_WIDTH, d), const2),
            pl.BlockSpec((B_WIDTH, 1), const2),
            pl.BlockSpec((KV_WIDTH, d), const2),
            pl.BlockSpec((KV_WIDTH, 1), const2),
            pl.BlockSpec((1, A_WIDTH), const2),
            pl.BlockSpec((1, A_WIDTH), const2),
            pl.BlockSpec((A_HEADS, CHUNK, CHUNK), lambda b, i: (0, 0, 0)),
            pl.BlockSpec((CHUNK, A_WIDTH), const2),
        ],
        out_specs=[o[1] for o in outs],
        compiler_params=pltpu.CompilerParams(
            dimension_semantics=("arbitrary", "arbitrary"), vmem_limit_bytes=VMEM_LIMIT),
        name="inproj_gmlp",
    )(x, mod, wb, b_in.reshape(1, -1), wqt, bq, wvt, bv, ln_g, ln_b, ws, bsb)


def _first_argmax(vals, iota_f, width):
    m = jnp.max(vals, axis=0, keepdims=True)
    idx = jnp.min(jnp.where(vals == m, iota_f, float(width)), axis=0, keepdims=True)
    return m, idx


def _mixer_kernel(seq_len, qt_ref, kp_ref, kc_ref, kn_ref, vtp_ref, vtc_ref, vtn_ref, bias_ref, sink_ref,
                  ygp_ref, ygc_ref, ygn_ref, ya_ref, x_ref, mod_ref, wo_ref, bo_ref,
                  cw_ref, cb_ref, clg_ref, clb_ref, lng_ref, lnb_ref, rwh_ref, rwl_ref, rb_ref,
                  x1_ref, h2_ref, eid_ref, gate_ref, cnt_ref, conv_scr, z_scr, ot_scr, s_scr):
    i = pl.program_id(1)
    n_i = pl.num_programs(1)
    t0 = i * TT
    m = mod_ref[...]

    kfull = jnp.concatenate([kp_ref[0], kc_ref[0], kn_ref[0]], axis=0)
    vtfull = jnp.concatenate([vtp_ref[0], vtc_ref[0], vtn_ref[0]], axis=1)
    grp = B_HEADS // B_KV_HEADS
    n_qb = TT // WINDOW
    units = [(jb, g) for jb in range(n_qb) for g in range(B_KV_HEADS)]
    key_i = lax.broadcasted_iota(I32, (3 * WINDOW, 1), 0)

    def scores(u):
        jb, g = units[u]
        kb = kfull[jb * WINDOW:(jb + 3) * WINDOW, g * HEAD_DIM:(g + 1) * HEAD_DIM]
        qt = jnp.concatenate(
            [qt_ref[0, h * HEAD_DIM:(h + 1) * HEAD_DIM, jb * WINDOW:(jb + 1) * WINDOW]
             for h in range(g * grp, (g + 1) * grp)], axis=1)
        s = _dot(kb, qt) + bias_ref[g]
        if jb == 0 or jb == n_qb - 1:
            kpos = t0 + (jb - 1) * WINDOW + key_i
            s = jnp.where((kpos >= 0) & (kpos < seq_len), s, NEG_INF)
        s_scr[u % 2] = s

    def values(u):
        jb, g = units[u]
        s = s_scr[u % 2]
        sink = sink_ref[g:g + 1, :]
        mx = jnp.maximum(jnp.max(s, axis=0, keepdims=True), sink)
        p = jnp.exp2(s - mx)
        den = jnp.sum(p, axis=0, keepdims=True) + jnp.exp2(sink - mx)
        vt = vtfull[g * HEAD_DIM:(g + 1) * HEAD_DIM, jb * WINDOW:(jb + 3) * WINDOW]
        ot = _dot(vt, p.astype(BF16)) / den
        for hh in range(grp):
            h = g * grp + hh
            ot_scr[h * HEAD_DIM:(h + 1) * HEAD_DIM, jb * WINDOW:(jb + 1) * WINDOW] = (
                ot[:, hh * WINDOW:(hh + 1) * WINDOW])

    scores(0)
    for u in range(len(units)):
        if u + 1 < len(units):
            scores(u + 1)
        values(u)
    yb = jnp.transpose(ot_scr[...]).astype(BF16)

    prev_ok = jnp.where(i > 0, 1.0, 0.0)
    next_ok = jnp.where(i < n_i - 1, 1.0, 0.0)
    conv_scr[0:HALO, :] = ygp_ref[0].astype(F32) * prev_ok
    conv_scr[HALO:HALO + TT, :] = ygc_ref[0].astype(F32)
    conv_scr[HALO + TT:HALO + TT + HALO, :] = ygn_ref[0].astype(F32) * next_ok
    first = HALO - CONV_PAD
    acc = jnp.zeros((TT, C_WIDTH), F32) + cb_ref[...]
    for r in range(SUBLANES):
        z = None
        for a in range(-(-(first + CONV_WIDTH) // SUBLANES)):
            w = a * SUBLANES + r - first
            if 0 <= w < CONV_WIDTH:
                term = conv_scr[a * SUBLANES:a * SUBLANES + TT + SUBLANES, :] * cw_ref[w:w + 1, :]
                z = term if z is None else z + term
        if r == 0:
            acc = acc + z[0:TT]
        else:
            z_scr[r] = z
            acc = acc + z_scr[r, r:r + TT, :]
    yc = _ln(acc, clg_ref[...], clb_ref[...])
    yc = yc * _sigmoid(yc)

    y = (_dot(ya_ref[0], wo_ref[0:A_WIDTH, :])
         + _dot(yb, wo_ref[A_WIDTH:A_WIDTH + B_WIDTH, :])
         + _dot(yc.astype(BF16), wo_ref[A_WIDTH + B_WIDTH:, :]) + bo_ref[...])
    x1 = _ln(ALPHA * x_ref[0] + (1.0 + m[2:3]) * y, lng_ref[...], lnb_ref[...])
    x1_ref[0] = x1

    h2 = x1 * (1.0 + m[4:5]) + m[3:4]
    hi = h2.astype(BF16)
    h2_ref[0] = hi
    lo = (h2 - hi.astype(F32)).astype(BF16)
    logits = _dot_nt(rwh_ref[...], hi) + _dot_nt(rwh_ref[...], lo) + _dot_nt(rwl_ref[...], hi)
    scores = _sigmoid(logits)
    sel = scores + rb_ref[...]
    iota_f = lax.broadcasted_iota(I32, (EXPERTS_PER_GROUP, TT), 0).astype(F32)
    best = None
    for g in range(N_GROUPS):
        sl = slice(g * EXPERTS_PER_GROUP, (g + 1) * EXPERTS_PER_GROUP)
        sg = sel[sl]
        m1, i1 = _first_argmax(sg, iota_f, EXPERTS_PER_GROUP)
        m2, i2 = _first_argmax(jnp.where(iota_f == i1, -jnp.inf, sg), iota_f, EXPERTS_PER_GROUP)
        sc = scores[sl]
        s1 = jnp.sum(jnp.where(iota_f == i1, sc, 0.0), axis=0, keepdims=True)
        s2 = jnp.sum(jnp.where(iota_f == i2, sc, 0.0), axis=0, keepdims=True)
        cand = (m1 + m2, i1 + g * EXPERTS_PER_GROUP, i2 + g * EXPERTS_PER_GROUP, s1, s2)
        if best is None:
            best = cand
        else:
            take = cand[0] > best[0]
            best = tuple(jnp.where(take, c, b) for c, b in zip(cand, best))
    _, e1, e2, s1, s2 = best
    eid = jnp.concatenate([e1, e2], axis=0).astype(I32)
    eid_ref[0] = eid
    gate_ref[0] = jnp.concatenate([s1, s2], axis=0) / (s1 + s2)
    iota_e = lax.broadcasted_iota(I32, (N_EXPERTS, TT), 0)
    member = jnp.where((iota_e == eid[0:1]) | (iota_e == eid[1:2]), 1.0, 0.0)
    cnt_ref[0] = jnp.broadcast_to(jnp.sum(member, axis=1, keepdims=True), (N_EXPERTS, LANES))


def _mixer_call(seq_len, qt, k, vt, bias, sink, yg, ya, x, mod, layer, w_out, b_out, conv_w, conv_b,
                conv_ln_g, conv_ln_b, ln_g, ln_b, rw_hi, rw_lo, rbias):
    bsz, seq, d = x.shape
    nt = seq // TT
    kb = TT // WINDOW
    hb = TT // HALO
    grp = B_HEADS // B_KV_HEADS
    const2 = lambda b, i: (0, 0)
    row = lambda b, i: (b, i, 0)
    colblk = lambda b, i: (b, 0, i)
    prev_k = lambda b, i: (b, jnp.maximum(i * kb - 1, 0), 0)
    next_k = lambda b, i: (b, jnp.minimum((i + 1) * kb, seq // WINDOW - 1), 0)
    prev_v = lambda b, i: (b, 0, jnp.maximum(i * kb - 1, 0))
    next_v = lambda b, i: (b, 0, jnp.minimum((i + 1) * kb, seq // WINDOW - 1))
    prev_h = lambda b, i: (b, jnp.maximum(i * hb - 1, 0), 0)
    next_h = lambda b, i: (b, jnp.minimum((i + 1) * hb, seq // HALO - 1), 0)
    tile = lambda b, i: (b * nt + i, 0, 0)
    return pl.pallas_call(
        functools.partial(_mixer_kernel, seq_len),
        out_shape=[
            jax.ShapeDtypeStruct((bsz, seq, d), F32),
            jax.ShapeDtypeStruct((bsz, seq, d), BF16),
            jax.ShapeDtypeStruct((bsz * nt, TOP_K, TT), I32),
            jax.ShapeDtypeStruct((bsz * nt, TOP_K, TT), F32),
            jax.ShapeDtypeStruct((bsz * nt, N_EXPERTS, LANES), F32),
        ],
        grid=(bsz, nt),
        in_specs=[
            pl.BlockSpec((1, B_WIDTH, TT), colblk),
            pl.BlockSpec((1, WINDOW, KV_WIDTH), prev_k),
            pl.BlockSpec((1, TT, KV_WIDTH), row),
            pl.BlockSpec((1, WINDOW, KV_WIDTH), next_k),
            pl.BlockSpec((1, KV_WIDTH, WINDOW), prev_v),
            pl.BlockSpec((1, KV_WIDTH, TT), colblk),
            pl.BlockSpec((1, KV_WIDTH, WINDOW), next_v),
            pl.BlockSpec((B_KV_HEADS, 3 * WINDOW, grp * WINDOW), lambda b, i: (0, 0, 0)),
            pl.BlockSpec((B_KV_HEADS, grp * WINDOW), const2),
            pl.BlockSpec((1, HALO, C_WIDTH), prev_h),
            pl.BlockSpec((1, TT, C_WIDTH), row),
            pl.BlockSpec((1, HALO, C_WIDTH), next_h),
            pl.BlockSpec((1, TT, A_WIDTH), row),
            pl.BlockSpec((1, TT, d), row),
            pl.BlockSpec((None, None, 6, d), lambda b, i: (layer, b, 0, 0)),
            pl.BlockSpec((d, d), const2),
            pl.BlockSpec((1, d), const2),
            pl.BlockSpec((CONV_WIDTH, C_WIDTH), const2),
            pl.BlockSpec((1, C_WIDTH), const2),
            pl.BlockSpec((1, C_WIDTH), const2),
            pl.BlockSpec((1, C_WIDTH), const2),
            pl.BlockSpec((1, d), const2),
            pl.BlockSpec((1, d), const2),
            pl.BlockSpec((N_EXPERTS, d), const2),
            pl.BlockSpec((N_EXPERTS, d), const2),
            pl.BlockSpec((N_EXPERTS, 1), const2),
        ],
        out_specs=[
            pl.BlockSpec((1, TT, d), row),
            pl.BlockSpec((1, TT, d), row),
            pl.BlockSpec((1, TOP_K, TT), tile),
            pl.BlockSpec((1, TOP_K, TT), tile),
            pl.BlockSpec((1, N_EXPERTS, LANES), tile),
        ],
        scratch_shapes=[
            pltpu.VMEM((TT + 2 * HALO, C_WIDTH), F32),
            pltpu.VMEM((SUBLANES, TT + SUBLANES, C_WIDTH), F32),
            pltpu.VMEM((B_WIDTH, TT), F32),
            pltpu.VMEM((2, 3 * WINDOW, grp * WINDOW), F32),
        ],
        compiler_params=pltpu.CompilerParams(
            dimension_semantics=("arbitrary", "arbitrary"), vmem_limit_bytes=VMEM_LIMIT),
        name="mixer_out_router",
    )(qt, k, k, k, vt, vt, vt, bias, sink, yg, yg, yg, ya, x, mod, w_out, b_out, conv_w, conv_b,
      conv_ln_g, conv_ln_b, ln_g, ln_b, rw_hi, rw_lo, rbias)


def _sort_kernel(n_tiles, eid_ref, padoff_ref, h2_ref, xs_ref, pos_ref):
    n = pl.program_id(0)

    @pl.when(n < n_tiles)
    def _():
        eid = eid_ref[0]
        iota_e = lax.broadcasted_iota(I32, (N_EXPERTS, TT), 0)
        e0 = iota_e == eid[0:1]
        e1 = iota_e == eid[1:2]
        member = jnp.where(e0 | e1, 1.0, 0.0).astype(BF16)
        r_i = lax.broadcasted_iota(I32, (TT, TT), 0)
        c_i = lax.broadcasted_iota(I32, (TT, TT), 1)
        before = jnp.where(r_i < c_i, 1.0, 0.0).astype(BF16)
        rank = _dot(member, before)
        posf = padoff_ref[0] + rank
        pos0 = jnp.sum(jnp.where(e0, posf, 0.0), axis=0, keepdims=True).astype(I32)
        pos1 = jnp.sum(jnp.where(e1, posf, 0.0), axis=0, keepdims=True).astype(I32)
        pos_ref[0] = jnp.concatenate([pos0, pos1], axis=0)
        iota_r = lax.broadcasted_iota(I32, (R_TILE, TT), 0)
        onehot = jnp.where((iota_r == pos0) | (iota_r == pos1), 1.0, 0.0).astype(BF16)
        xs_ref[...] = _dot(onehot, h2_ref[...]).astype(BF16)

    @pl.when(n >= n_tiles)
    def _():
        xs_ref[...] = jnp.zeros_like(xs_ref)
        pos_ref[0] = jnp.zeros((TOP_K, TT), I32)


def _sort_call(eid, padoff, h2):
    n_tiles = eid.shape[0]
    d = h2.shape[-1]
    clamp = lambda n: jnp.minimum(n, n_tiles - 1)
    return pl.pallas_call(
        functools.partial(_sort_kernel, n_tiles),
        out_shape=[
            jax.ShapeDtypeStruct(((n_tiles + 1) * R_TILE, d), BF16),
            jax.ShapeDtypeStruct((n_tiles + 1, TOP_K, TT), I32),
        ],
        grid=(n_tiles + 1,),
        in_specs=[
            pl.BlockSpec((1, TOP_K, TT), lambda n: (clamp(n), 0, 0)),
            pl.BlockSpec((1, N_EXPERTS, 1), lambda n: (clamp(n), 0, 0)),
            pl.BlockSpec((TT, d), lambda n: (clamp(n), 0)),
        ],
        out_specs=[
            pl.BlockSpec((R_TILE, d), lambda n: (n, 0)),
            pl.BlockSpec((1, TOP_K, TT), lambda n: (n, 0, 0)),
        ],
        compiler_params=pltpu.CompilerParams(
            dimension_semantics=("arbitrary",), vmem_limit_bytes=VMEM_LIMIT),
        name="moe_sort",
    )(eid, padoff, h2)


def _expert_kernel(src_ref, dst_ref, bexp_ref, nused_ref, xs_hbm, wg_ref, wu_ref, wd_ref, ys_hbm,
                   xbuf, ybuf, wgb, wub, wdb, sem_in, sem_out):
    blk = pl.program_id(0)
    n_blk = pl.num_programs(0)
    nused = nused_ref[0]
    slot = blk % 2

    def in_copy(b, s, c):
        return pltpu.make_async_copy(
            xs_hbm.at[pl.ds(pl.multiple_of(src_ref[b * NCH + c] * CH, CH), CH)],
            xbuf.at[s, pl.ds(c * CH, CH)], sem_in.at[s])

    def out_copy(b, s, c):
        return pltpu.make_async_copy(
            ybuf.at[s, pl.ds(c * CH, CH)],
            ys_hbm.at[pl.ds(pl.multiple_of(dst_ref[b * NCH + c] * CH, CH), CH)], sem_out.at[s])

    @pl.when((blk == 0) & (nused > 0))
    def _():
        for c in range(NCH):
            in_copy(0, 0, c).start()

    @pl.when(blk + 1 < nused)
    def _():
        for c in range(NCH):
            in_copy(blk + 1, 1 - slot, c).start()

    @pl.when(blk < nused)
    def _():
        expert = bexp_ref[blk]
        prev_expert = bexp_ref[jnp.maximum(blk - 1, 0)]

        @pl.when((blk == 0) | (expert != prev_expert))
        def _():
            wgb[...] = wg_ref[...].astype(BF16)
            wub[...] = wu_ref[...].astype(BF16)
            wdb[...] = wd_ref[...].astype(BF16)

        for c in range(NCH):
            in_copy(blk, slot, c).wait()
        x = xbuf[slot]
        g = _dot(x, wgb[...])
        u = _dot(x, wub[...])
        hmid = ((g * _sigmoid(g)) * u).astype(BF16)
        y = _dot(hmid, wdb[...]).astype(BF16)

        @pl.when(blk >= 2)
        def _():
            for c in range(NCH):
                out_copy(blk - 2, slot, c).wait()

        ybuf[slot] = y
        for c in range(NCH):
            out_copy(blk, slot, c).start()

    @pl.when(blk == n_blk - 1)
    def _():
        @pl.when(nused >= 2)
        def _():
            for c in range(NCH):
                out_copy(nused - 2, nused % 2, c).wait()

        @pl.when(nused >= 1)
        def _():
            for c in range(NCH):
                out_copy(nused - 1, (nused - 1) % 2, c).wait()


def _expert_call(n_blocks, src, dst, bexp, nused, xs, layer, w_gate, w_up, w_down):
    d = xs.shape[-1]
    de = w_gate.shape[-1]
    grid_spec = pltpu.PrefetchScalarGridSpec(
        num_scalar_prefetch=4,
        grid=(n_blocks,),
        in_specs=[
            pl.BlockSpec(memory_space=pl.ANY),
            pl.BlockSpec((None, None, d, de), lambda b, s, t, e, n: (layer, e[b], 0, 0)),
            pl.BlockSpec((None, None, d, de), lambda b, s, t, e, n: (layer, e[b], 0, 0)),
            pl.BlockSpec((None, None, de, d), lambda b, s, t, e, n: (layer, e[b], 0, 0)),
        ],
        out_specs=pl.BlockSpec(memory_space=pl.ANY),
        scratch_shapes=[
            pltpu.VMEM((2, BM, d), BF16),
            pltpu.VMEM((2, BM, d), BF16),
            pltpu.VMEM((d, de), BF16),
            pltpu.VMEM((d, de), BF16),
            pltpu.VMEM((de, d), BF16),
            pltpu.SemaphoreType.DMA((2,)),
            pltpu.SemaphoreType.DMA((2,)),
        ],
    )
    return pl.pallas_call(
        _expert_kernel,
        out_shape=jax.ShapeDtypeStruct(xs.shape, xs.dtype),
        grid_spec=grid_spec,
        input_output_aliases={4: 0},
        compiler_params=pltpu.CompilerParams(
            dimension_semantics=("arbitrary",), vmem_limit_bytes=VMEM_LIMIT),
        name="moe_experts",
    )(src, dst, bexp, nused, xs, w_gate, w_up, w_down)


def _combine_kernel(ys_ref, pos_ref, gate_ref, x1_ref, mod_ref, lng_ref, lnb_ref, o_ref):
    pos = pos_ref[0]
    gate = gate_ref[0]
    m = mod_ref[...]
    iota_r = lax.broadcasted_iota(I32, (R_TILE, TT), 0)
    row_gate = (jnp.where(iota_r == pos[0:1], gate[0:1], 0.0)
                + jnp.where(iota_r == pos[1:2], gate[1:2], 0.0))
    gs = jnp.sum(row_gate, axis=1, keepdims=True)
    ysc = (ys_ref[...].astype(F32) * gs).astype(BF16)
    posc = jnp.transpose(pos.astype(F32))
    iota_c = lax.broadcasted_iota(I32, (TT, R_TILE), 1).astype(F32)
    pick = jnp.where((iota_c == posc[:, 0:1]) | (iota_c == posc[:, 1:2]), 1.0, 0.0).astype(BF16)
    y = _dot(pick, ysc)
    o_ref[...] = _ln(ALPHA * x1_ref[...] + (1.0 + m[5:6]) * y, lng_ref[...], lnb_ref[...])


def _combine_call(ys, pos, gate, x1, mod, layer, tiles_per_seq, ln_g, ln_b):
    t, d = x1.shape
    n_tiles = t // TT
    const2 = lambda n: (0, 0)
    return pl.pallas_call(
        _combine_kernel,
        out_shape=jax.ShapeDtypeStruct((t, d), F32),
        grid=(n_tiles,),
        in_specs=[
            pl.BlockSpec((R_TILE, d), lambda n: (n, 0)),
            pl.BlockSpec((1, TOP_K, TT), lambda n: (n, 0, 0)),
            pl.BlockSpec((1, TOP_K, TT), lambda n: (n, 0, 0)),
            pl.BlockSpec((TT, d), lambda n: (n, 0)),
            pl.BlockSpec((None, None, 6, d), lambda n: (layer, n // tiles_per_seq, 0, 0)),
            pl.BlockSpec((1, d), const2),
            pl.BlockSpec((1, d), const2),
        ],
        out_specs=pl.BlockSpec((TT, d), lambda n: (n, 0)),
        compiler_params=pltpu.CompilerParams(
            dimension_semantics=("arbitrary",), vmem_limit_bytes=VMEM_LIMIT),
        name="moe_combine",
    )(ys, pos, gate, x1, mod, ln_g, ln_b)


def _dispatch_plan(cnt, n_blocks):
    n_tiles = cnt.shape[0]
    chunks_per_tile = R_TILE // CH
    nch = (cnt + CH - 1) // CH
    padoff_ch = jnp.cumsum(nch, axis=1) - nch
    tot = jnp.sum(nch, axis=0)
    totpad = (tot + NCH - 1) // NCH * NCH
    eend = jnp.cumsum(totpad)
    ebase = eend - totpad
    tend = jnp.cumsum(nch, axis=0)
    tbase = tend - nch
    slot = jnp.arange(n_blocks * NCH, dtype=I32)
    e_s = jnp.minimum(jnp.sum(eend[None, :] <= slot[:, None], axis=1), N_EXPERTS - 1).astype(I32)
    local = slot - ebase[e_s]
    valid = (local < tot[e_s]) & (slot < eend[-1])
    tend_e = tend.T[e_s]
    n_s = jnp.minimum(jnp.sum(tend_e <= local[:, None], axis=1), n_tiles - 1).astype(I32)
    src = n_s * chunks_per_tile + padoff_ch[n_s, e_s] + (local - tbase[n_s, e_s])
    zero_chunk = chunks_per_tile - 1
    src = jnp.where(valid, src, zero_chunk).astype(I32)
    blk = slot // NCH
    scratch = n_tiles * chunks_per_tile + (blk % 2) * NCH + slot % NCH
    dst = jnp.where(valid, src, scratch).astype(I32)
    bexp = e_s[::NCH]
    nused = (eend[-1] // NCH).astype(I32).reshape(1)
    padoff = (padoff_ch * CH).astype(F32)[:, :, None]
    return padoff, src, dst, bexp, nused


def _t5_bucket(rel):
    nb = N_BUCKETS // 2
    max_exact = nb // 2
    ret = jnp.where(rel > 0, nb, 0)
    n = jnp.abs(rel)
    nf = jnp.maximum(n, 1).astype(jnp.float32)
    large = max_exact + (jnp.log(nf / max_exact) / math.log(MAX_DISTANCE / max_exact)
                         * (nb - max_exact)).astype(jnp.int32)
    large = jnp.minimum(large, nb - 1)
    return ret + jnp.where(n < max_exact, n, large)


def _band_bias(rel_bias):
    qi = jnp.arange(WINDOW)
    kj = jnp.arange(3 * WINDOW)
    rel = kj[None, :] - WINDOW - qi[:, None]
    bias = rel_bias.astype(F32)[_t5_bucket(rel)] * LOG2E
    bias = jnp.where((jnp.abs(rel) <= WINDOW)[:, :, None], bias, NEG_INF)
    grp = B_HEADS // B_KV_HEADS
    bias = jnp.transpose(bias, (2, 1, 0)).reshape(B_KV_HEADS, grp, 3 * WINDOW, WINDOW)
    return jnp.transpose(bias, (0, 2, 1, 3)).reshape(B_KV_HEADS, 3 * WINDOW, grp * WINDOW)


def kernel(x, c, ada_w, ada_b, w_in, b_in, gmlp_ln_g, gmlp_ln_b, gmlp_ws, gmlp_bs, attn_sink, conv_w,
           conv_b, conv_ln_g, conv_ln_b, w_out, b_out, ln_mix_g, ln_mix_b, w_gate, w_up, w_down,
           ln_ffn_g, ln_ffn_b, rel_bias, router_w, router_bias):
    bsz, seq, d = x.shape
    n_layers = ada_w.shape[0]
    t = bsz * seq
    nt = seq // TT
    n_tiles = t // TT
    max_chunks = n_tiles * (TOP_K * TT // CH + N_EXPERTS) + N_EXPERTS * (NCH - 1)
    n_blocks = -(-max_chunks // NCH)

    mod = _ada_call(c, ada_w, ada_b).reshape(n_layers, bsz, 6, d)
    bias = _band_bias(rel_bias)
    rw_t = router_w.T
    rw_hi = rw_t.astype(BF16)
    rw_lo = (rw_t - rw_hi.astype(F32)).astype(BF16)
    rbias = router_bias.astype(F32).reshape(N_EXPERTS, 1)
    row = lambda a: a.reshape(1, -1)

    for l in range(n_layers):
        bsb = jnp.repeat(gmlp_bs[l].T, HEAD_DIM, axis=1)
        ya, qt, k, vt, yg = _inproj_call(
            x, mod, l, w_in[l], b_in[l], row(gmlp_ln_g[l]), row(gmlp_ln_b[l]),
            gmlp_ws[l].astype(BF16), bsb)
        sink = jnp.repeat(attn_sink[l].astype(F32) * LOG2E, WINDOW).reshape(B_KV_HEADS, -1)
        x1, h2, eid, gate, cntb = _mixer_call(
            seq, qt, k, vt, bias, sink, yg, ya, x, mod, l, w_out[l].astype(BF16),
            row(b_out[l]), conv_w[l], row(conv_b[l]), row(conv_ln_g[l]), row(conv_ln_b[l]),
            row(ln_mix_g[l]), row(ln_mix_b[l]), rw_hi, rw_lo, rbias)
        cnt = cntb[:, :, 0].astype(I32)
        padoff, src, dst, bexp, nused = _dispatch_plan(cnt, n_blocks)
        xs, pos = _sort_call(eid, padoff, h2.reshape(t, d))
        ys = _expert_call(n_blocks, src, dst, bexp, nused, xs, l, w_gate, w_up, w_down)
        x = _combine_call(ys, pos, gate, x1.reshape(t, d), mod, l, nt, row(ln_ffn_g[l]),
                          row(ln_ffn_b[l])).reshape(bsz, seq, d)
    return x
```

```python
import functools
import math

import jax
import jax.numpy as jnp
from jax import lax
from jax.experimental import pallas as pl
from jax.experimental.pallas import tpu as pltpu

F32 = jnp.float32
BF16 = jnp.bfloat16
I32 = jnp.int32

D_MODEL = 1024
DEPTH = 2
HEAD_DIM = 64
A_WIDTH = 256
A_HEADS = 4
CHUNK = 128
B_WIDTH = 512
B_HEADS = 8
B_KV_HEADS = 2
KV_WIDTH = B_KV_HEADS * HEAD_DIM
WINDOW = 128
N_BUCKETS = 32
MAX_DISTANCE = 128
C_WIDTH = 256
CONV_WIDTH = 31
CONV_PAD = CONV_WIDTH // 2
IN_WIDTH = 2 * A_WIDTH + B_WIDTH + 2 * KV_WIDTH + 2 * C_WIDTH
N_EXPERTS = 32
N_GROUPS = 4
EXPERTS_PER_GROUP = N_EXPERTS // N_GROUPS
TOP_K = 2
D_EXPERT = D_MODEL // 2
ALPHA = (2 * DEPTH) ** 0.25
LN_EPS = 1e-5
NEG_INF = -1e30
LOG2E = 1.4426950408889634
Q_SCALE = HEAD_DIM ** -0.5 * LOG2E

LANES = 128
SUBLANES = 8
BF16_SUBLANES = 16
VMEM_LIMIT = 48 * 1024 * 1024

ADA_TN = 1536
TS = 512
TT = 512
CH = BF16_SUBLANES
R_TILE = 1536
BM = 512
NCH = BM // CH
HALO = 16

assert R_TILE >= TOP_K * TT + N_EXPERTS * (CH - 1) + CH
assert R_TILE >= 2 * NCH * CH


def _sigmoid(x):
    return 1.0 / (1.0 + jnp.exp(-x))


def _gelu_tanh(x):
    return x * (0.5 * (1.0 + jnp.tanh(0.7978845608028654 * (x + 0.044715 * (x * x * x)))))


def _ln(x, g, b):
    mu = jnp.mean(x, axis=-1, keepdims=True)
    xc = x - mu
    var = jnp.mean(xc * xc, axis=-1, keepdims=True)
    return xc * lax.rsqrt(var + LN_EPS) * g + b


def _dot(a, b):
    return jnp.dot(a, b, preferred_element_type=F32)


def _dot_nt(a, b):
    return lax.dot_general(a, b, (((1,), (1,)), ((), ())), preferred_element_type=F32)


def _ada_kernel(c_ref, w_ref, b_ref, o_ref):
    c = c_ref[...]
    s = (c * _sigmoid(c)).astype(BF16)
    o_ref[0] = _dot(s, w_ref[0].astype(BF16)) + b_ref[0]


def _ada_call(c, ada_w, ada_b):
    nl, d, n = ada_w.shape
    bsz = c.shape[0]
    return pl.pallas_call(
        _ada_kernel,
        out_shape=jax.ShapeDtypeStruct((nl, bsz, n), F32),
        grid=(nl, n // ADA_TN),
        in_specs=[
            pl.BlockSpec((bsz, d), lambda l, j: (0, 0)),
            pl.BlockSpec((1, d, ADA_TN), lambda l, j: (l, 0, j)),
            pl.BlockSpec((1, 1, ADA_TN), lambda l, j: (l, 0, j)),
        ],
        out_specs=pl.BlockSpec((1, bsz, ADA_TN), lambda l, j: (l, 0, j)),
        compiler_params=pltpu.CompilerParams(
            dimension_semantics=("arbitrary", "arbitrary"), vmem_limit_bytes=VMEM_LIMIT),
        name="ada_mod",
    )(c, ada_w, ada_b.reshape(nl, 1, n))


def _inproj_kernel(x_ref, mod_ref, w_ref, b_ref, wqt_ref, bq_ref, wvt_ref, bv_ref, lng_ref, lnb_ref,
                   ws_ref, bsb_ref, ya_ref, qt_ref, k_ref, vt_ref, yg_ref):
    x = x_ref[0]
    m = mod_ref[...]
    hb = (x * (1.0 + m[1:2]) + m[0:1]).astype(BF16)

    c0 = 0
    u = _gelu_tanh(_dot(hb, w_ref[:, c0:c0 + A_WIDTH]) + b_ref[:, c0:c0 + A_WIDTH])
    c0 += A_WIDTH
    v = _gelu_tanh(_dot(hb, w_ref[:, c0:c0 + A_WIDTH]) + b_ref[:, c0:c0 + A_WIDTH])
    c0 += A_WIDTH
    vb = _ln(v, lng_ref[...], lnb_ref[...]).astype(BF16)
    head_of_lane = lax.broadcasted_iota(I32, (CHUNK, A_WIDTH), 1) // HEAD_DIM
    for ch in range(TS // CHUNK):
        vc = vb[ch * CHUNK:(ch + 1) * CHUNK]
        acc = bsb_ref[...]
        for hh in range(A_HEADS):
            acc = acc + _dot(ws_ref[hh], jnp.where(head_of_lane == hh, vc, jnp.zeros_like(vc)))
        ya_ref[0, ch * CHUNK:(ch + 1) * CHUNK, :] = (u[ch * CHUNK:(ch + 1) * CHUNK] * acc).astype(BF16)

    qt_ref[0] = ((_dot_nt(wqt_ref[...], hb) + bq_ref[...]) * Q_SCALE).astype(BF16)
    c0 += B_WIDTH
    k_ref[0] = (_dot(hb, w_ref[:, c0:c0 + KV_WIDTH]) + b_ref[:, c0:c0 + KV_WIDTH]).astype(BF16)
    c0 += KV_WIDTH
    vt_ref[0] = (_dot_nt(wvt_ref[...], hb) + bv_ref[...]).astype(BF16)
    c0 += KV_WIDTH
    za = _dot(hb, w_ref[:, c0:c0 + C_WIDTH]) + b_ref[:, c0:c0 + C_WIDTH]
    c0 += C_WIDTH
    zg = _dot(hb, w_ref[:, c0:c0 + C_WIDTH]) + b_ref[:, c0:c0 + C_WIDTH]
    yg_ref[0] = (za * _sigmoid(zg)).astype(BF16)


def _inproj_call(x, mod, layer, w_in, b_in, ln_g, ln_b, ws, bsb):
    bsz, seq, d = x.shape
    grid = (bsz, seq // TS)
    const2 = lambda b, i: (0, 0)
    row = lambda b, i: (b, i, 0)
    colblk = lambda b, i: (b, 0, i)
    q0 = 2 * A_WIDTH
    v0 = q0 + B_WIDTH + KV_WIDTH
    wb = w_in.astype(BF16)
    wqt = w_in[:, q0:q0 + B_WIDTH].T.astype(BF16)
    wvt = w_in[:, v0:v0 + KV_WIDTH].T.astype(BF16)
    bq = b_in[q0:q0 + B_WIDTH].reshape(B_WIDTH, 1)
    bv = b_in[v0:v0 + KV_WIDTH].reshape(KV_WIDTH, 1)

    def out(width):
        return jax.ShapeDtypeStruct((bsz, seq, width), BF16), pl.BlockSpec((1, TS, width), row)

    def out_t(width):
        return jax.ShapeDtypeStruct((bsz, width, seq), BF16), pl.BlockSpec((1, width, TS), colblk)

    outs = [out(A_WIDTH), out_t(B_WIDTH), out(KV_WIDTH), out_t(KV_WIDTH), out(C_WIDTH)]
    return pl.pallas_call(
        _inproj_kernel,
        out_shape=[o[0] for o in outs],
        grid=grid,
        in_specs=[
            pl.BlockSpec((1, TS, d), row),
            pl.BlockSpec((None, None, 6, d), lambda b, i: (layer, b, 0, 0)),
            pl.BlockSpec((d, IN_WIDTH), const2),
            pl.BlockSpec((1, IN_WIDTH), const2),
            pl.BlockSpec((B_WIDTH, d), const2),
            pl.BlockSpec((B_WIDTH, 1), const2),
            pl.BlockSpec((KV_WIDTH, d), const2),
            pl.BlockSpec((KV_WIDTH, 1), const2),
            pl.BlockSpec((1, A_WIDTH), const2),
            pl.BlockSpec((1, A_WIDTH), const2),
            pl.BlockSpec((A_HEADS, CHUNK, CHUNK), lambda b, i: (0, 0, 0)),
            pl.BlockSpec((CHUNK, A_WIDTH), const2),
        ],
        out_specs=[o[1] for o in outs],
        compiler_params=pltpu.CompilerParams(
            dimension_semantics=("arbitrary", "arbitrary"), vmem_limit_bytes=VMEM_LIMIT),
        name="inproj_gmlp",
    )(x, mod, wb, b_in.reshape(1, -1), wqt, bq, wvt, bv, ln_g, ln_b, ws, bsb)


def _first_argmax(vals, iota_f, width):
    m = jnp.max(vals, axis=0, keepdims=True)
    idx = jnp.min(jnp.where(vals == m, iota_f, float(width)), axis=0, keepdims=True)
    return m, idx


def _mixer_kernel(seq_len, qt_ref, kp_ref, kc_ref, kn_ref, vtp_ref, vtc_ref, vtn_ref, bias_ref, sink_ref,
                  ygp_ref, ygc_ref, ygn_ref, ya_ref, x_ref, mod_ref, wo_ref, bo_ref,
                  cw_ref, cb_ref, clg_ref, clb_ref, lng_ref, lnb_ref, rwh_ref, rwl_ref, rb_ref,
                  x1_ref, h2_ref, eid_ref, gate_ref, cnt_ref, conv_scr, z_scr, ot_scr, s_scr):
    i = pl.program_id(1)
    n_i = pl.num_programs(1)
    t0 = i * TT
    m = mod_ref[...]

    kfull = jnp.concatenate([kp_ref[0], kc_ref[0], kn_ref[0]], axis=0)
    vtfull = jnp.concatenate([vtp_ref[0], vtc_ref[0], vtn_ref[0]], axis=1)
    grp = B_HEADS // B_KV_HEADS
    n_qb = TT // WINDOW
    units = [(jb, g) for jb in range(n_qb) for g in range(B_KV_HEADS)]
    key_i = lax.broadcasted_iota(I32, (3 * WINDOW, 1), 0)

    def scores(u):
        jb, g = units[u]
        kb = kfull[jb * WINDOW:(jb + 3) * WINDOW, g * HEAD_DIM:(g + 1) * HEAD_DIM]
        qt = jnp.concatenate(
            [qt_ref[0, h * HEAD_DIM:(h + 1) * HEAD_DIM, jb * WINDOW:(jb + 1) * WINDOW]
             for h in range(g * grp, (g + 1) * grp)], axis=1)
        s = _dot(kb, qt) + bias_ref[g]
        if jb == 0 or jb == n_qb - 1:
            kpos = t0 + (jb - 1) * WINDOW + key_i
            s = jnp.where((kpos >= 0) & (kpos < seq_len), s, NEG_INF)
        s_scr[u % 2] = s

    def values(u):
        jb, g = units[u]
        s = s_scr[u % 2]
        sink = sink_ref[g:g + 1, :]
        mx = jnp.maximum(jnp.max(s, axis=0, keepdims=True), sink)
        p = jnp.exp2(s - mx)
        den = jnp.sum(p, axis=0, keepdims=True) + jnp.exp2(sink - mx)
        vt = vtfull[g * HEAD_DIM:(g + 1) * HEAD_DIM, jb * WINDOW:(jb + 3) * WINDOW]
        ot = _dot(vt, p.astype(BF16)) / den
        for hh in range(grp):
            h = g * grp + hh
            ot_scr[h * HEAD_DIM:(h + 1) * HEAD_DIM, jb * WINDOW:(jb + 1) * WINDOW] = (
                ot[:, hh * WINDOW:(hh + 1) * WINDOW])

    scores(0)
    for u in range(len(units)):
        if u + 1 < len(units):
            scores(u + 1)
        values(u)
    yb = jnp.transpose(ot_scr[...]).astype(BF16)

    prev_ok = jnp.where(i > 0, 1.0, 0.0)
    next_ok = jnp.where(i < n_i - 1, 1.0, 0.0)
    conv_scr[0:HALO, :] = ygp_ref[0].astype(F32) * prev_ok
    conv_scr[HALO:HALO + TT, :] = ygc_ref[0].astype(F32)
    conv_scr[HALO + TT:HALO + TT + HALO, :] = ygn_ref[0].astype(F32) * next_ok
    first = HALO - CONV_PAD
    acc = jnp.zeros((TT, C_WIDTH), F32) + cb_ref[...]
    for r in range(SUBLANES):
        z = None
        for a in range(-(-(first + CONV_WIDTH) // SUBLANES)):
            w = a * SUBLANES + r - first
            if 0 <= w < CONV_WIDTH:
                term = conv_scr[a * SUBLANES:a * SUBLANES + TT + SUBLANES, :] * cw_ref[w:w + 1, :]
                z = term if z is None else z + term
        if r == 0:
            acc = acc + z[0:TT]
        else:
            z_scr[r] = z
            acc = acc + z_scr[r, r:r + TT, :]
    yc = _ln(acc, clg_ref[...], clb_ref[...])
    yc = yc * _sigmoid(yc)

    y = (_dot(ya_ref[0], wo_ref[0:A_WIDTH, :])
         + _dot(yb, wo_ref[A_WIDTH:A_WIDTH + B_WIDTH, :])
         + _dot(yc.astype(BF16), wo_ref[A_WIDTH + B_WIDTH:, :]) + bo_ref[...])
    x1 = _ln(ALPHA * x_ref[0] + (1.0 + m[2:3]) * y, lng_ref[...], lnb_ref[...])
    x1_ref[0] = x1

    h2 = x1 * (1.0 + m[4:5]) + m[3:4]
    hi = h2.astype(BF16)
    h2_ref[0] = hi
    lo = (h2 - hi.astype(F32)).astype(BF16)
    logits = _dot_nt(rwh_ref[...], hi) + _dot_nt(rwh_ref[...], lo) + _dot_nt(rwl_ref[...], hi)
    scores = _sigmoid(logits)
    sel = scores + rb_ref[...]
    iota_f = lax.broadcasted_iota(I32, (EXPERTS_PER_GROUP, TT), 0).astype(F32)
    best = None
    for g in range(N_GROUPS):
        sl = slice(g * EXPERTS_PER_GROUP, (g + 1) * EXPERTS_PER_GROUP)
        sg = sel[sl]
        m1, i1 = _first_argmax(sg, iota_f, EXPERTS_PER_GROUP)
        m2, i2 = _first_argmax(jnp.where(iota_f == i1, -jnp.inf, sg), iota_f, EXPERTS_PER_GROUP)
        sc = scores[sl]
        s1 = jnp.sum(jnp.where(iota_f == i1, sc, 0.0), axis=0, keepdims=True)
        s2 = jnp.sum(jnp.where(iota_f == i2, sc, 0.0), axis=0, keepdims=True)
        cand = (m1 + m2, i1 + g * EXPERTS_PER_GROUP, i2 + g * EXPERTS_PER_GROUP, s1, s2)
        if best is None:
            best = cand
        else:
            take = cand[0] > best[0]
            best = tuple(jnp.where(take, c, b) for c, b in zip(cand, best))
    _, e1, e2, s1, s2 = best
    eid = jnp.concatenate([e1, e2], axis=0).astype(I32)
    eid_ref[0] = eid
    gate_ref[0] = jnp.concatenate([s1, s2], axis=0) / (s1 + s2)
    iota_e = lax.broadcasted_iota(I32, (N_EXPERTS, TT), 0)
    member = jnp.where((iota_e == eid[0:1]) | (iota_e == eid[1:2]), 1.0, 0.0)
    cnt_ref[0] = jnp.broadcast_to(jnp.sum(member, axis=1, keepdims=True), (N_EXPERTS, LANES))


def _mixer_call(seq_len, qt, k, vt, bias, sink, yg, ya, x, mod, layer, w_out, b_out, conv_w, conv_b,
                conv_ln_g, conv_ln_b, ln_g, ln_b, rw_hi, rw_lo, rbias):
    bsz, seq, d = x.shape
    nt = seq // TT
    kb = TT // WINDOW
    hb = TT // HALO
    grp = B_HEADS // B_KV_HEADS
    const2 = lambda b, i: (0, 0)
    row = lambda b, i: (b, i, 0)
    colblk = lambda b, i: (b, 0, i)
    prev_k = lambda b, i: (b, jnp.maximum(i * kb - 1, 0), 0)
    next_k = lambda b, i: (b, jnp.minimum((i + 1) * kb, seq // WINDOW - 1), 0)
    prev_v = lambda b, i: (b, 0, jnp.maximum(i * kb - 1, 0))
    next_v = lambda b, i: (b, 0, jnp.minimum((i + 1) * kb, seq // WINDOW - 1))
    prev_h = lambda b, i: (b, jnp.maximum(i * hb - 1, 0), 0)
    next_h = lambda b, i: (b, jnp.minimum((i + 1) * hb, seq // HALO - 1), 0)
    tile = lambda b, i: (b * nt + i, 0, 0)
    return pl.pallas_call(
        functools.partial(_mixer_kernel, seq_len),
        out_shape=[
            jax.ShapeDtypeStruct((bsz, seq, d), F32),
            jax.ShapeDtypeStruct((bsz, seq, d), BF16),
            jax.ShapeDtypeStruct((bsz * nt, TOP_K, TT), I32),
            jax.ShapeDtypeStruct((bsz * nt, TOP_K, TT), F32),
            jax.ShapeDtypeStruct((bsz * nt, N_EXPERTS, LANES), F32),
        ],
        grid=(bsz, nt),
        in_specs=[
            pl.BlockSpec((1, B_WIDTH, TT), colblk),
            pl.BlockSpec((1, WINDOW, KV_WIDTH), prev_k),
            pl.BlockSpec((1, TT, KV_WIDTH), row),
            pl.BlockSpec((1, WINDOW, KV_WIDTH), next_k),
            pl.BlockSpec((1, KV_WIDTH, WINDOW), prev_v),
            pl.BlockSpec((1, KV_WIDTH, TT), colblk),
            pl.BlockSpec((1, KV_WIDTH, WINDOW), next_v),
            pl.BlockSpec((B_KV_HEADS, 3 * WINDOW, grp * WINDOW), lambda b, i: (0, 0, 0)),
            pl.BlockSpec((B_KV_HEADS, grp * WINDOW), const2),
            pl.BlockSpec((1, HALO, C_WIDTH), prev_h),
            pl.BlockSpec((1, TT, C_WIDTH), row),
            pl.BlockSpec((1, HALO, C_WIDTH), next_h),
            pl.BlockSpec((1, TT, A_WIDTH), row),
            pl.BlockSpec((1, TT, d), row),
            pl.BlockSpec((None, None, 6, d), lambda b, i: (layer, b, 0, 0)),
            pl.BlockSpec((d, d), const2),
            pl.BlockSpec((1, d), const2),
            pl.BlockSpec((CONV_WIDTH, C_WIDTH), const2),
            pl.BlockSpec((1, C_WIDTH), const2),
            pl.BlockSpec((1, C_WIDTH), const2),
            pl.BlockSpec((1, C_WIDTH), const2),
            pl.BlockSpec((1, d), const2),
            pl.BlockSpec((1, d), const2),
            pl.BlockSpec((N_EXPERTS, d), const2),
            pl.BlockSpec((N_EXPERTS, d), const2),
            pl.BlockSpec((N_EXPERTS, 1), const2),
        ],
        out_specs=[
            pl.BlockSpec((1, TT, d), row),
            pl.BlockSpec((1, TT, d), row),
            pl.BlockSpec((1, TOP_K, TT), tile),
            pl.BlockSpec((1, TOP_K, TT), tile),
            pl.BlockSpec((1, N_EXPERTS, LANES), tile),
        ],
        scratch_shapes=[
            pltpu.VMEM((TT + 2 * HALO, C_WIDTH), F32),
            pltpu.VMEM((SUBLANES, TT + SUBLANES, C_WIDTH), F32),
            pltpu.VMEM((B_WIDTH, TT), F32),
            pltpu.VMEM((2, 3 * WINDOW, grp * WINDOW), F32),
        ],
        compiler_params=pltpu.CompilerParams(
            dimension_semantics=("arbitrary", "arbitrary"), vmem_limit_bytes=VMEM_LIMIT),
        name="mixer_out_router",
    )(qt, k, k, k, vt, vt, vt, bias, sink, yg, yg, yg, ya, x, mod, w_out, b_out, conv_w, conv_b,
      conv_ln_g, conv_ln_b, ln_g, ln_b, rw_hi, rw_lo, rbias)


def _sort_kernel(n_tiles, eid_ref, padoff_ref, h2_ref, xs_ref, pos_ref):
    n = pl.program_id(0)

    @pl.when(n < n_tiles)
    def _():
        eid = eid_ref[0]
        iota_e = lax.broadcasted_iota(I32, (N_EXPERTS, TT), 0)
        e0 = iota_e == eid[0:1]
        e1 = iota_e == eid[1:2]
        member = jnp.where(e0 | e1, 1.0, 0.0).astype(BF16)
        r_i = lax.broadcasted_iota(I32, (TT, TT), 0)
        c_i = lax.broadcasted_iota(I32, (TT, TT), 1)
        before = jnp.where(r_i < c_i, 1.0, 0.0).astype(BF16)
        rank = _dot(member, before)
        posf = padoff_ref[0] + rank
        pos0 = jnp.sum(jnp.where(e0, posf, 0.0), axis=0, keepdims=True).astype(I32)
        pos1 = jnp.sum(jnp.where(e1, posf, 0.0), axis=0, keepdims=True).astype(I32)
        pos_ref[0] = jnp.concatenate([pos0, pos1], axis=0)
        iota_r = lax.broadcasted_iota(I32, (R_TILE, TT), 0)
        onehot = jnp.where((iota_r == pos0) | (iota_r == pos1), 1.0, 0.0).astype(BF16)
        xs_ref[...] = _dot(onehot, h2_ref[...]).astype(BF16)

    @pl.when(n >= n_tiles)
    def _():
        xs_ref[...] = jnp.zeros_like(xs_ref)
        pos_ref[0] = jnp.zeros((TOP_K, TT), I32)


def _sort_call(eid, padoff, h2):
    n_tiles = eid.shape[0]
    d = h2.shape[-1]
    clamp = lambda n: jnp.minimum(n, n_tiles - 1)
    return pl.pallas_call(
        functools.partial(_sort_kernel, n_tiles),
        out_shape=[
            jax.ShapeDtypeStruct(((n_tiles + 1) * R_TILE, d), BF16),
            jax.ShapeDtypeStruct((n_tiles + 1, TOP_K, TT), I32),
        ],
        grid=(n_tiles + 1,),
        in_specs=[
            pl.BlockSpec((1, TOP_K, TT), lambda n: (clamp(n), 0, 0)),
            pl.BlockSpec((1, N_EXPERTS, 1), lambda n: (clamp(n), 0, 0)),
            pl.BlockSpec((TT, d), lambda n: (clamp(n), 0)),
        ],
        out_specs=[
            pl.BlockSpec((R_TILE, d), lambda n: (n, 0)),
            pl.BlockSpec((1, TOP_K, TT), lambda n: (n, 0, 0)),
        ],
        compiler_params=pltpu.CompilerParams(
            dimension_semantics=("arbitrary",), vmem_limit_bytes=VMEM_LIMIT),
        name="moe_sort",
    )(eid, padoff, h2)


def _expert_kernel(src_ref, dst_ref, bexp_ref, nused_ref, xs_hbm, wg_ref, wu_ref, wd_ref, ys_hbm,
                   xbuf, ybuf, wgb, wub, wdb, sem_in, sem_out):
    blk = pl.program_id(0)
    n_blk = pl.num_programs(0)
    nused = nused_ref[0]
    slot = blk % 2
    chunks_per_tile = R_TILE // CH
    zero_chunk = chunks_per_tile - 1
    scratch_chunk = ys_hbm.shape[0] // CH - chunks_per_tile
    de = wgb.shape[1]
    half = de // 2

    def in_copy(chunk, s, c):
        return pltpu.make_async_copy(
            xs_hbm.at[pl.ds(pl.multiple_of(chunk * CH, CH), CH)],
            xbuf.at[s, pl.ds(c * CH, CH)], sem_in.at[s])

    def out_copy(chunk, s, c):
        return pltpu.make_async_copy(
            ybuf.at[s, pl.ds(c * CH, CH)],
            ys_hbm.at[pl.ds(pl.multiple_of(chunk * CH, CH), CH)], sem_out.at[s])

    @pl.when(blk == 0)
    def _():
        ybuf[...] = jnp.zeros_like(ybuf)
        for c in range(NCH):
            in_copy(src_ref[c], 0, c).start()
        for c in range(NCH):
            out_copy(scratch_chunk + c, 0, c).start()

    @pl.when(blk < nused)
    def _():
        expert = bexp_ref[blk]
        prev_expert = bexp_ref[jnp.maximum(blk - 1, 0)]

        @pl.when((blk == 0) | (expert != prev_expert))
        def _():
            wgb[...] = wg_ref[...].astype(BF16)
            wub[...] = wu_ref[...].astype(BF16)
            wdb[...] = wd_ref[...].astype(BF16)

        for c in range(NCH):
            in_copy(0, slot, c).wait()
        live = blk + 1 < nused
        nxt = jnp.minimum(blk + 1, n_blk - 1) * NCH
        prv = jnp.maximum(blk - 1, 0) * NCH
        def start_gather(lo, hi):
            for c in range(lo, hi):
                in_copy(jnp.where(live, src_ref[nxt + c], zero_chunk), 1 - slot, c).start()

        def start_writeback(lo, hi):
            for c in range(lo, hi):
                out_copy(jnp.where(blk > 0, dst_ref[prv + c], scratch_chunk + NCH + c), 1 - slot, c).start()

        x = xbuf[slot]
        q = NCH // 2
        hmid = []
        for j in range(2):
            g = _dot(x, wgb[:, j * half:(j + 1) * half])
            start_gather(j * q, (j + 1) * q)
            u = _dot(x, wub[:, j * half:(j + 1) * half])
            start_writeback(j * q, (j + 1) * q)
            hmid.append(((g * _sigmoid(g)) * u).astype(BF16))
        y = _dot(hmid[0], wdb[0:half, :]) + _dot(hmid[1], wdb[half:, :])
        for c in range(NCH):
            out_copy(0, slot, c).wait()
        ybuf[slot] = y.astype(BF16)

    @pl.when(blk == n_blk - 1)
    def _():
        @pl.when(nused >= 1)
        def _():
            last = (nused - 1) * NCH
            for c in range(NCH):
                out_copy(dst_ref[last + c], (nused - 1) % 2, c).start()
            for c in range(NCH):
                out_copy(0, nused % 2, c).wait()

        for c in range(NCH):
            out_copy(0, jnp.maximum(nused - 1, 0) % 2, c).wait()
        for c in range(NCH):
            in_copy(0, nused % 2, c).wait()


def _expert_call(n_blocks, src, dst, bexp, nused, xs, layer, w_gate, w_up, w_down):
    d = xs.shape[-1]
    de = w_gate.shape[-1]
    grid_spec = pltpu.PrefetchScalarGridSpec(
        num_scalar_prefetch=4,
        grid=(n_blocks,),
        in_specs=[
            pl.BlockSpec(memory_space=pl.ANY),
            pl.BlockSpec((None, None, d, de), lambda b, s, t, e, n: (layer, e[b], 0, 0)),
            pl.BlockSpec((None, None, d, de), lambda b, s, t, e, n: (layer, e[b], 0, 0)),
            pl.BlockSpec((None, None, de, d), lambda b, s, t, e, n: (layer, e[b], 0, 0)),
        ],
        out_specs=pl.BlockSpec(memory_space=pl.ANY),
        scratch_shapes=[
            pltpu.VMEM((2, BM, d), BF16),
            pltpu.VMEM((2, BM, d), BF16),
            pltpu.VMEM((d, de), BF16),
            pltpu.VMEM((d, de), BF16),
            pltpu.VMEM((de, d), BF16),
            pltpu.SemaphoreType.DMA((2,)),
            pltpu.SemaphoreType.DMA((2,)),
        ],
    )
    return pl.pallas_call(
        _expert_kernel,
        out_shape=jax.ShapeDtypeStruct(xs.shape, xs.dtype),
        grid_spec=grid_spec,
        input_output_aliases={4: 0},
        compiler_params=pltpu.CompilerParams(
            dimension_semantics=("arbitrary",), vmem_limit_bytes=VMEM_LIMIT),
        name="moe_experts",
    )(src, dst, bexp, nused, xs, w_gate, w_up, w_down)


def _combine_kernel(ys_ref, pos_ref, gate_ref, x1_ref, mod_ref, lng_ref, lnb_ref, o_ref):
    pos = pos_ref[0]
    gate = gate_ref[0]
    m = mod_ref[...]
    iota_r = lax.broadcasted_iota(I32, (R_TILE, TT), 0)
    row_gate = (jnp.where(iota_r == pos[0:1], gate[0:1], 0.0)
                + jnp.where(iota_r == pos[1:2], gate[1:2], 0.0))
    gs = jnp.sum(row_gate, axis=1, keepdims=True)
    ysc = (ys_ref[...].astype(F32) * gs).astype(BF16)
    posc = jnp.transpose(pos.astype(F32))
    iota_c = lax.broadcasted_iota(I32, (TT, R_TILE), 1).astype(F32)
    pick = jnp.where((iota_c == posc[:, 0:1]) | (iota_c == posc[:, 1:2]), 1.0, 0.0).astype(BF16)
    y = _dot(pick, ysc)
    o_ref[...] = _ln(ALPHA * x1_ref[...] + (1.0 + m[5:6]) * y, lng_ref[...], lnb_ref[...])


def _combine_call(ys, pos, gate, x1, mod, layer, tiles_per_seq, ln_g, ln_b):
    t, d = x1.shape
    n_tiles = t // TT
    const2 = lambda n: (0, 0)
    return pl.pallas_call(
        _combine_kernel,
        out_shape=jax.ShapeDtypeStruct((t, d), F32),
        grid=(n_tiles,),
        in_specs=[
            pl.BlockSpec((R_TILE, d), lambda n: (n, 0)),
            pl.BlockSpec((1, TOP_K, TT), lambda n: (n, 0, 0)),
            pl.BlockSpec((1, TOP_K, TT), lambda n: (n, 0, 0)),
            pl.BlockSpec((TT, d), lambda n: (n, 0)),
            pl.BlockSpec((None, None, 6, d), lambda n: (layer, n // tiles_per_seq, 0, 0)),
            pl.BlockSpec((1, d), const2),
            pl.BlockSpec((1, d), const2),
        ],
        out_specs=pl.BlockSpec((TT, d), lambda n: (n, 0)),
        compiler_params=pltpu.CompilerParams(
            dimension_semantics=("arbitrary",), vmem_limit_bytes=VMEM_LIMIT),
        name="moe_combine",
    )(ys, pos, gate, x1, mod, ln_g, ln_b)


def _dispatch_plan(cnt, n_blocks):
    n_tiles = cnt.shape[0]
    chunks_per_tile = R_TILE // CH
    nch = (cnt + CH - 1) // CH
    padoff_ch = jnp.cumsum(nch, axis=1) - nch
    tot = jnp.sum(nch, axis=0)
    totpad = (tot + NCH - 1) // NCH * NCH
    eend = jnp.cumsum(totpad)
    ebase = eend - totpad
    tbase = jnp.cumsum(nch, axis=0) - nch
    start = (ebase[None, :] + tbase).T.reshape(-1)
    base = (jnp.arange(n_tiles, dtype=I32)[:, None] * chunks_per_tile + padoff_ch).T.reshape(-1)
    vals = jnp.stack([start, base, nch.T.reshape(-1)], axis=1)
    delta = vals - jnp.concatenate([jnp.zeros((1, 3), I32), vals[:-1]], axis=0)
    digits = jnp.concatenate([delta // LANES, delta % LANES], axis=1).astype(BF16)
    slot = jnp.arange(n_blocks * NCH, dtype=I32)
    started = (start[None, :] <= slot[:, None]).astype(BF16)
    got = jnp.dot(started, digits, preferred_element_type=F32).astype(I32)
    seg = got[:, :3] * LANES + got[:, 3:]
    j = slot - seg[:, 0]
    valid = (j < seg[:, 2]) & (slot < eend[-1])
    zero_chunk = chunks_per_tile - 1
    src = jnp.where(valid, seg[:, 1] + j, zero_chunk).astype(I32)
    blk = slot // NCH
    scratch = n_tiles * chunks_per_tile + (blk % 2) * NCH + slot % NCH
    dst = jnp.where(valid, src, scratch).astype(I32)
    first = jnp.arange(n_blocks, dtype=I32) * NCH
    bexp = jnp.minimum(jnp.sum(eend[None, :] <= first[:, None], axis=1), N_EXPERTS - 1).astype(I32)
    nused = (eend[-1] // NCH).astype(I32).reshape(1)
    padoff = (padoff_ch * CH).astype(F32)[:, :, None]
    return padoff, src, dst, bexp, nused


def _t5_bucket(rel):
    nb = N_BUCKETS // 2
    max_exact = nb // 2
    ret = jnp.where(rel > 0, nb, 0)
    n = jnp.abs(rel)
    nf = jnp.maximum(n, 1).astype(jnp.float32)
    large = max_exact + (jnp.log(nf / max_exact) / math.log(MAX_DISTANCE / max_exact)
                         * (nb - max_exact)).astype(jnp.int32)
    large = jnp.minimum(large, nb - 1)
    return ret + jnp.where(n < max_exact, n, large)


def _band_bias(rel_bias):
    qi = jnp.arange(WINDOW)
    kj = jnp.arange(3 * WINDOW)
    rel = kj[None, :] - WINDOW - qi[:, None]
    pick = _t5_bucket(rel)[:, :, None, None] == jnp.arange(N_BUCKETS)[None, None, :, None]
    bias = jnp.sum(jnp.where(pick, rel_bias.astype(F32)[None, None], 0.0), axis=2) * LOG2E
    bias = jnp.where((jnp.abs(rel) <= WINDOW)[:, :, None], bias, NEG_INF)
    grp = B_HEADS // B_KV_HEADS
    bias = jnp.transpose(bias, (2, 1, 0)).reshape(B_KV_HEADS, grp, 3 * WINDOW, WINDOW)
    return jnp.transpose(bias, (0, 2, 1, 3)).reshape(B_KV_HEADS, 3 * WINDOW, grp * WINDOW)


def kernel(x, c, ada_w, ada_b, w_in, b_in, gmlp_ln_g, gmlp_ln_b, gmlp_ws, gmlp_bs, attn_sink, conv_w,
           conv_b, conv_ln_g, conv_ln_b, w_out, b_out, ln_mix_g, ln_mix_b, w_gate, w_up, w_down,
           ln_ffn_g, ln_ffn_b, rel_bias, router_w, router_bias):
    bsz, seq, d = x.shape
    n_layers = ada_w.shape[0]
    t = bsz * seq
    nt = seq // TT
    n_tiles = t // TT
    max_chunks = n_tiles * (TOP_K * TT // CH + N_EXPERTS) + N_EXPERTS * (NCH - 1)
    n_blocks = -(-max_chunks // NCH)

    mod = _ada_call(c, ada_w, ada_b).reshape(n_layers, bsz, 6, d)
    bias = _band_bias(rel_bias)
    rw_t = router_w.T
    rw_hi = rw_t.astype(BF16)
    rw_lo = (rw_t - rw_hi.astype(F32)).astype(BF16)
    rbias = router_bias.astype(F32).reshape(N_EXPERTS, 1)
    row = lambda a: a.reshape(1, -1)

    for l in range(n_layers):
        bsb = jnp.repeat(gmlp_bs[l].T, HEAD_DIM, axis=1)
        ya, qt, k, vt, yg = _inproj_call(
            x, mod, l, w_in[l], b_in[l], row(gmlp_ln_g[l]), row(gmlp_ln_b[l]),
            gmlp_ws[l].astype(BF16), bsb)
        sink = jnp.repeat(attn_sink[l].astype(F32) * LOG2E, WINDOW).reshape(B_KV_HEADS, -1)
        x1, h2, eid, gate, cntb = _mixer_call(
            seq, qt, k, vt, bias, sink, yg, ya, x, mod, l, w_out[l].astype(BF16),
            row(b_out[l]), conv_w[l], row(conv_b[l]), row(conv_ln_g[l]), row(conv_ln_b[l]),
            row(ln_mix_g[l]), row(ln_mix_b[l]), rw_hi, rw_lo, rbias)
        cnt = cntb[:, :, 0].astype(I32)
        padoff, src, dst, bexp, nused = _dispatch_plan(cnt, n_blocks)
        xs, pos = _sort_call(eid, padoff, h2.reshape(t, d))
        ys = _expert_call(n_blocks, src, dst, bexp, nused, xs, l, w_gate, w_up, w_down)
        x = _combine_call(ys, pos, gate, x1.reshape(t, d), mod, l, nt, row(ln_ffn_g[l]),
                          row(ln_ffn_b[l])).reshape(bsz, seq, d)
    return x
```

```python
import functools
import math

import jax
import jax.numpy as jnp
from jax import lax
from jax.experimental import pallas as pl
from jax.experimental.pallas import tpu as pltpu

F32 = jnp.float32
BF16 = jnp.bfloat16
I32 = jnp.int32

D_MODEL = 1024
DEPTH = 2
HEAD_DIM = 64
A_WIDTH = 256
A_HEADS = 4
CHUNK = 128
B_WIDTH = 512
B_HEADS = 8
B_KV_HEADS = 2
KV_WIDTH = B_KV_HEADS * HEAD_DIM
WINDOW = 128
N_BUCKETS = 32
MAX_DISTANCE = 128
C_WIDTH = 256
CONV_WIDTH = 31
CONV_PAD = CONV_WIDTH // 2
IN_WIDTH = 2 * A_WIDTH + B_WIDTH + 2 * KV_WIDTH + 2 * C_WIDTH
N_EXPERTS = 32
N_GROUPS = 4
EXPERTS_PER_GROUP = N_EXPERTS // N_GROUPS
TOP_K = 2
D_EXPERT = D_MODEL // 2
ALPHA = (2 * DEPTH) ** 0.25
LN_EPS = 1e-5
NEG_INF = -1e30
LOG2E = 1.4426950408889634
Q_SCALE = HEAD_DIM ** -0.5 * LOG2E

LANES = 128
SUBLANES = 8
BF16_SUBLANES = 16
VMEM_LIMIT = 48 * 1024 * 1024

ADA_TN = 1536
TS = 512
TT = 512
CH = BF16_SUBLANES
R_TILE = 1536
BM = 512
NCH = BM // CH
N_XBUF = 3
HALO = 16

assert R_TILE >= TOP_K * TT + N_EXPERTS * (CH - 1) + CH
assert R_TILE >= 2 * NCH * CH


def _sigmoid(x):
    return 1.0 / (1.0 + jnp.exp(-x))


def _gelu_tanh(x):
    return x * (0.5 * (1.0 + jnp.tanh(0.7978845608028654 * (x + 0.044715 * (x * x * x)))))


def _ln(x, g, b):
    mu = jnp.mean(x, axis=-1, keepdims=True)
    xc = x - mu
    var = jnp.mean(xc * xc, axis=-1, keepdims=True)
    return xc * lax.rsqrt(var + LN_EPS) * g + b


def _dot(a, b):
    return jnp.dot(a, b, preferred_element_type=F32)


def _dot_nt(a, b):
    return lax.dot_general(a, b, (((1,), (1,)), ((), ())), preferred_element_type=F32)


def _ada_kernel(c_ref, w_ref, b_ref, o_ref):
    c = c_ref[...]
    s = (c * _sigmoid(c)).astype(BF16)
    o_ref[0] = _dot(s, w_ref[0].astype(BF16)) + b_ref[0]


def _ada_call(c, ada_w, ada_b):
    nl, d, n = ada_w.shape
    bsz = c.shape[0]
    return pl.pallas_call(
        _ada_kernel,
        out_shape=jax.ShapeDtypeStruct((nl, bsz, n), F32),
        grid=(nl, n // ADA_TN),
        in_specs=[
            pl.BlockSpec((bsz, d), lambda l, j: (0, 0)),
            pl.BlockSpec((1, d, ADA_TN), lambda l, j: (l, 0, j)),
            pl.BlockSpec((1, 1, ADA_TN), lambda l, j: (l, 0, j)),
        ],
        out_specs=pl.BlockSpec((1, bsz, ADA_TN), lambda l, j: (l, 0, j)),
        compiler_params=pltpu.CompilerParams(
            dimension_semantics=("arbitrary", "arbitrary"), vmem_limit_bytes=VMEM_LIMIT),
        name="ada_mod",
    )(c, ada_w, ada_b.reshape(nl, 1, n))


def _inproj_kernel(x_ref, mod_ref, w_ref, b_ref, wqt_ref, bq_ref, wvt_ref, bv_ref, lng_ref, lnb_ref,
                   ws_ref, bsb_ref, ya_ref, qt_ref, k_ref, vt_ref, yg_ref):
    x = x_ref[0]
    m = mod_ref[...]
    hb = (x * (1.0 + m[1:2]) + m[0:1]).astype(BF16)

    c0 = 0
    u = _gelu_tanh(_dot(hb, w_ref[:, c0:c0 + A_WIDTH]) + b_ref[:, c0:c0 + A_WIDTH])
    c0 += A_WIDTH
    v = _gelu_tanh(_dot(hb, w_ref[:, c0:c0 + A_WIDTH]) + b_ref[:, c0:c0 + A_WIDTH])
    c0 += A_WIDTH
    vb = _ln(v, lng_ref[...], lnb_ref[...]).astype(BF16)
    head_of_lane = lax.broadcasted_iota(I32, (CHUNK, A_WIDTH), 1) // HEAD_DIM
    for ch in range(TS // CHUNK):
        vc = vb[ch * CHUNK:(ch + 1) * CHUNK]
        acc = bsb_ref[...]
        for hh in range(A_HEADS):
            acc = acc + _dot(ws_ref[hh], jnp.where(head_of_lane == hh, vc, jnp.zeros_like(vc)))
        ya_ref[0, ch * CHUNK:(ch + 1) * CHUNK, :] = (u[ch * CHUNK:(ch + 1) * CHUNK] * acc).astype(BF16)

    qt_ref[0] = ((_dot_nt(wqt_ref[...], hb) + bq_ref[...]) * Q_SCALE).astype(BF16)
    c0 += B_WIDTH
    k_ref[0] = (_dot(hb, w_ref[:, c0:c0 + KV_WIDTH]) + b_ref[:, c0:c0 + KV_WIDTH]).astype(BF16)
    c0 += KV_WIDTH
    vt_ref[0] = (_dot_nt(wvt_ref[...], hb) + bv_ref[...]).astype(BF16)
    c0 += KV_WIDTH
    za = _dot(hb, w_ref[:, c0:c0 + C_WIDTH]) + b_ref[:, c0:c0 + C_WIDTH]
    c0 += C_WIDTH
    zg = _dot(hb, w_ref[:, c0:c0 + C_WIDTH]) + b_ref[:, c0:c0 + C_WIDTH]
    yg_ref[0] = (za * _sigmoid(zg)).astype(BF16)


def _inproj_call(x, mod, layer, w_in, b_in, ln_g, ln_b, ws, bsb):
    bsz, seq, d = x.shape
    grid = (bsz, seq // TS)
    const2 = lambda b, i: (0, 0)
    row = lambda b, i: (b, i, 0)
    colblk = lambda b, i: (b, 0, i)
    q0 = 2 * A_WIDTH
    v0 = q0 + B_WIDTH + KV_WIDTH
    wb = w_in.astype(BF16)
    wqt = w_in[:, q0:q0 + B_WIDTH].T.astype(BF16)
    wvt = w_in[:, v0:v0 + KV_WIDTH].T.astype(BF16)
    bq = b_in[q0:q0 + B_WIDTH].reshape(B_WIDTH, 1)
    bv = b_in[v0:v0 + KV_WIDTH].reshape(KV_WIDTH, 1)

    def out(width):
        return jax.ShapeDtypeStruct((bsz, seq, width), BF16), pl.BlockSpec((1, TS, width), row)

    def out_t(width):
        return jax.ShapeDtypeStruct((bsz, width, seq), BF16), pl.BlockSpec((1, width, TS), colblk)

    outs = [out(A_WIDTH), out_t(B_WIDTH), out(KV_WIDTH), out_t(KV_WIDTH), out(C_WIDTH)]
    return pl.pallas_call(
        _inproj_kernel,
        out_shape=[o[0] for o in outs],
        grid=grid,
        in_specs=[
            pl.BlockSpec((1, TS, d), row),
            pl.BlockSpec((None, None, 6, d), lambda b, i: (layer, b, 0, 0)),
            pl.BlockSpec((d, IN_WIDTH), const2),
            pl.BlockSpec((1, IN_WIDTH), const2),
            pl.BlockSpec((B_WIDTH, d), const2),
            pl.BlockSpec((B_WIDTH, 1), const2),
            pl.BlockSpec((KV_WIDTH, d), const2),
            pl.BlockSpec((KV_WIDTH, 1), const2),
            pl.BlockSpec((1, A_WIDTH), const2),
            pl.BlockSpec((1, A_WIDTH), const2),
            pl.BlockSpec((A_HEADS, CHUNK, CHUNK), lambda b, i: (0, 0, 0)),
            pl.BlockSpec((CHUNK, A_WIDTH), const2),
        ],
        out_specs=[o[1] for o in outs],
        compiler_params=pltpu.CompilerParams(
            dimension_semantics=("arbitrary", "arbitrary"), vmem_limit_bytes=VMEM_LIMIT),
        name="inproj_gmlp",
    )(x, mod, wb, b_in.reshape(1, -1), wqt, bq, wvt, bv, ln_g, ln_b, ws, bsb)


def _first_argmax(vals, iota_f, width):
    m = jnp.max(vals, axis=0, keepdims=True)
    idx = jnp.min(jnp.where(vals == m, iota_f, float(width)), axis=0, keepdims=True)
    return m, idx


def _mixer_kernel(seq_len, qt_ref, kp_ref, kc_ref, kn_ref, vtp_ref, vtc_ref, vtn_ref, bias_ref, sink_ref,
                  ygp_ref, ygc_ref, ygn_ref, ya_ref, x_ref, mod_ref, wo_ref, bo_ref,
                  cw_ref, cb_ref, clg_ref, clb_ref, lng_ref, lnb_ref, rwh_ref, rwl_ref, rb_ref,
                  x1_ref, h2_ref, eid_ref, gate_ref, cnt_ref, conv_scr, z_scr, ot_scr, s_scr):
    i = pl.program_id(1)
    n_i = pl.num_programs(1)
    t0 = i * TT
    m = mod_ref[...]

    kfull = jnp.concatenate([kp_ref[0], kc_ref[0], kn_ref[0]], axis=0)
    vtfull = jnp.concatenate([vtp_ref[0], vtc_ref[0], vtn_ref[0]], axis=1)
    grp = B_HEADS // B_KV_HEADS
    n_qb = TT // WINDOW
    units = [(jb, g) for jb in range(n_qb) for g in range(B_KV_HEADS)]
    key_i = lax.broadcasted_iota(I32, (3 * WINDOW, 1), 0)

    def scores(u):
        jb, g = units[u]
        kb = kfull[jb * WINDOW:(jb + 3) * WINDOW, g * HEAD_DIM:(g + 1) * HEAD_DIM]
        qt = jnp.concatenate(
            [qt_ref[0, h * HEAD_DIM:(h + 1) * HEAD_DIM, jb * WINDOW:(jb + 1) * WINDOW]
             for h in range(g * grp, (g + 1) * grp)], axis=1)
        s = _dot(kb, qt) + bias_ref[g]
        if jb == 0 or jb == n_qb - 1:
            kpos = t0 + (jb - 1) * WINDOW + key_i
            s = jnp.where((kpos >= 0) & (kpos < seq_len), s, NEG_INF)
        s_scr[u % 2] = s

    def values(u):
        jb, g = units[u]
        s = s_scr[u % 2]
        sink = sink_ref[g:g + 1, :]
        mx = jnp.maximum(jnp.max(s, axis=0, keepdims=True), sink)
        p = jnp.exp2(s - mx)
        den = jnp.sum(p, axis=0, keepdims=True) + jnp.exp2(sink - mx)
        vt = vtfull[g * HEAD_DIM:(g + 1) * HEAD_DIM, jb * WINDOW:(jb + 3) * WINDOW]
        ot = _dot(vt, p.astype(BF16)) / den
        for hh in range(grp):
            h = g * grp + hh
            ot_scr[h * HEAD_DIM:(h + 1) * HEAD_DIM, jb * WINDOW:(jb + 1) * WINDOW] = (
                ot[:, hh * WINDOW:(hh + 1) * WINDOW])

    scores(0)
    for u in range(len(units)):
        if u + 1 < len(units):
            scores(u + 1)
        values(u)
    yb = jnp.transpose(ot_scr[...]).astype(BF16)

    prev_ok = jnp.where(i > 0, 1.0, 0.0)
    next_ok = jnp.where(i < n_i - 1, 1.0, 0.0)
    conv_scr[0:HALO, :] = ygp_ref[0].astype(F32) * prev_ok
    conv_scr[HALO:HALO + TT, :] = ygc_ref[0].astype(F32)
    conv_scr[HALO + TT:HALO + TT + HALO, :] = ygn_ref[0].astype(F32) * next_ok
    first = HALO - CONV_PAD
    acc = jnp.zeros((TT, C_WIDTH), F32) + cb_ref[...]
    for r in range(SUBLANES):
        z = None
        for a in range(-(-(first + CONV_WIDTH) // SUBLANES)):
            w = a * SUBLANES + r - first
            if 0 <= w < CONV_WIDTH:
                term = conv_scr[a * SUBLANES:a * SUBLANES + TT + SUBLANES, :] * cw_ref[w:w + 1, :]
                z = term if z is None else z + term
        if r == 0:
            acc = acc + z[0:TT]
        else:
            z_scr[r] = z
            acc = acc + z_scr[r, r:r + TT, :]
    yc = _ln(acc, clg_ref[...], clb_ref[...])
    yc = yc * _sigmoid(yc)

    y = (_dot(ya_ref[0], wo_ref[0:A_WIDTH, :])
         + _dot(yb, wo_ref[A_WIDTH:A_WIDTH + B_WIDTH, :])
         + _dot(yc.astype(BF16), wo_ref[A_WIDTH + B_WIDTH:, :]) + bo_ref[...])
    x1 = _ln(ALPHA * x_ref[0] + (1.0 + m[2:3]) * y, lng_ref[...], lnb_ref[...])
    x1_ref[0] = x1

    h2 = x1 * (1.0 + m[4:5]) + m[3:4]
    hi = h2.astype(BF16)
    h2_ref[0] = hi
    lo = (h2 - hi.astype(F32)).astype(BF16)
    logits = _dot_nt(rwh_ref[...], hi) + _dot_nt(rwh_ref[...], lo) + _dot_nt(rwl_ref[...], hi)
    scores = _sigmoid(logits)
    sel = scores + rb_ref[...]
    iota_f = lax.broadcasted_iota(I32, (EXPERTS_PER_GROUP, TT), 0).astype(F32)
    best = None
    for g in range(N_GROUPS):
        sl = slice(g * EXPERTS_PER_GROUP, (g + 1) * EXPERTS_PER_GROUP)
        sg = sel[sl]
        m1, i1 = _first_argmax(sg, iota_f, EXPERTS_PER_GROUP)
        m2, i2 = _first_argmax(jnp.where(iota_f == i1, -jnp.inf, sg), iota_f, EXPERTS_PER_GROUP)
        sc = scores[sl]
        s1 = jnp.sum(jnp.where(iota_f == i1, sc, 0.0), axis=0, keepdims=True)
        s2 = jnp.sum(jnp.where(iota_f == i2, sc, 0.0), axis=0, keepdims=True)
        cand = (m1 + m2, i1 + g * EXPERTS_PER_GROUP, i2 + g * EXPERTS_PER_GROUP, s1, s2)
        if best is None:
            best = cand
        else:
            take = cand[0] > best[0]
            best = tuple(jnp.where(take, c, b) for c, b in zip(cand, best))
    _, e1, e2, s1, s2 = best
    eid = jnp.concatenate([e1, e2], axis=0).astype(I32)
    eid_ref[0] = eid
    gate_ref[0] = jnp.concatenate([s1, s2], axis=0) / (s1 + s2)
    iota_e = lax.broadcasted_iota(I32, (N_EXPERTS, TT), 0)
    member = jnp.where((iota_e == eid[0:1]) | (iota_e == eid[1:2]), 1.0, 0.0)
    cnt_ref[0] = jnp.broadcast_to(jnp.sum(member, axis=1, keepdims=True), (N_EXPERTS, LANES))


def _mixer_call(seq_len, qt, k, vt, bias, sink, yg, ya, x, mod, layer, w_out, b_out, conv_w, conv_b,
                conv_ln_g, conv_ln_b, ln_g, ln_b, rw_hi, rw_lo, rbias):
    bsz, seq, d = x.shape
    nt = seq // TT
    kb = TT // WINDOW
    hb = TT // HALO
    grp = B_HEADS // B_KV_HEADS
    const2 = lambda b, i: (0, 0)
    row = lambda b, i: (b, i, 0)
    colblk = lambda b, i: (b, 0, i)
    prev_k = lambda b, i: (b, jnp.maximum(i * kb - 1, 0), 0)
    next_k = lambda b, i: (b, jnp.minimum((i + 1) * kb, seq // WINDOW - 1), 0)
    prev_v = lambda b, i: (b, 0, jnp.maximum(i * kb - 1, 0))
    next_v = lambda b, i: (b, 0, jnp.minimum((i + 1) * kb, seq // WINDOW - 1))
    prev_h = lambda b, i: (b, jnp.maximum(i * hb - 1, 0), 0)
    next_h = lambda b, i: (b, jnp.minimum((i + 1) * hb, seq // HALO - 1), 0)
    tile = lambda b, i: (b * nt + i, 0, 0)
    return pl.pallas_call(
        functools.partial(_mixer_kernel, seq_len),
        out_shape=[
            jax.ShapeDtypeStruct((bsz, seq, d), F32),
            jax.ShapeDtypeStruct((bsz, seq, d), BF16),
            jax.ShapeDtypeStruct((bsz * nt, TOP_K, TT), I32),
            jax.ShapeDtypeStruct((bsz * nt, TOP_K, TT), F32),
            jax.ShapeDtypeStruct((bsz * nt, N_EXPERTS, LANES), F32),
        ],
        grid=(bsz, nt),
        in_specs=[
            pl.BlockSpec((1, B_WIDTH, TT), colblk),
            pl.BlockSpec((1, WINDOW, KV_WIDTH), prev_k),
            pl.BlockSpec((1, TT, KV_WIDTH), row),
            pl.BlockSpec((1, WINDOW, KV_WIDTH), next_k),
            pl.BlockSpec((1, KV_WIDTH, WINDOW), prev_v),
            pl.BlockSpec((1, KV_WIDTH, TT), colblk),
            pl.BlockSpec((1, KV_WIDTH, WINDOW), next_v),
            pl.BlockSpec((B_KV_HEADS, 3 * WINDOW, grp * WINDOW), lambda b, i: (0, 0, 0)),
            pl.BlockSpec((B_KV_HEADS, grp * WINDOW), const2),
            pl.BlockSpec((1, HALO, C_WIDTH), prev_h),
            pl.BlockSpec((1, TT, C_WIDTH), row),
            pl.BlockSpec((1, HALO, C_WIDTH), next_h),
            pl.BlockSpec((1, TT, A_WIDTH), row),
            pl.BlockSpec((1, TT, d), row),
            pl.BlockSpec((None, None, 6, d), lambda b, i: (layer, b, 0, 0)),
            pl.BlockSpec((d, d), const2),
            pl.BlockSpec((1, d), const2),
            pl.BlockSpec((CONV_WIDTH, C_WIDTH), const2),
            pl.BlockSpec((1, C_WIDTH), const2),
            pl.BlockSpec((1, C_WIDTH), const2),
            pl.BlockSpec((1, C_WIDTH), const2),
            pl.BlockSpec((1, d), const2),
            pl.BlockSpec((1, d), const2),
            pl.BlockSpec((N_EXPERTS, d), const2),
            pl.BlockSpec((N_EXPERTS, d), const2),
            pl.BlockSpec((N_EXPERTS, 1), const2),
        ],
        out_specs=[
            pl.BlockSpec((1, TT, d), row),
            pl.BlockSpec((1, TT, d), row),
            pl.BlockSpec((1, TOP_K, TT), tile),
            pl.BlockSpec((1, TOP_K, TT), tile),
            pl.BlockSpec((1, N_EXPERTS, LANES), tile),
        ],
        scratch_shapes=[
            pltpu.VMEM((TT + 2 * HALO, C_WIDTH), F32),
            pltpu.VMEM((SUBLANES, TT + SUBLANES, C_WIDTH), F32),
            pltpu.VMEM((B_WIDTH, TT), F32),
            pltpu.VMEM((2, 3 * WINDOW, grp * WINDOW), F32),
        ],
        compiler_params=pltpu.CompilerParams(
            dimension_semantics=("arbitrary", "arbitrary"), vmem_limit_bytes=VMEM_LIMIT),
        name="mixer_out_router",
    )(qt, k, k, k, vt, vt, vt, bias, sink, yg, yg, yg, ya, x, mod, w_out, b_out, conv_w, conv_b,
      conv_ln_g, conv_ln_b, ln_g, ln_b, rw_hi, rw_lo, rbias)


def _sort_kernel(n_tiles, eid_ref, padoff_ref, h2_ref, xs_ref, pos_ref):
    n = pl.program_id(0)

    @pl.when(n < n_tiles)
    def _():
        eid = eid_ref[0]
        iota_e = lax.broadcasted_iota(I32, (N_EXPERTS, TT), 0)
        e0 = iota_e == eid[0:1]
        e1 = iota_e == eid[1:2]
        member = jnp.where(e0 | e1, 1.0, 0.0).astype(BF16)
        r_i = lax.broadcasted_iota(I32, (TT, TT), 0)
        c_i = lax.broadcasted_iota(I32, (TT, TT), 1)
        before = jnp.where(r_i < c_i, 1.0, 0.0).astype(BF16)
        rank = _dot(member, before)
        posf = padoff_ref[0] + rank
        pos0 = jnp.sum(jnp.where(e0, posf, 0.0), axis=0, keepdims=True).astype(I32)
        pos1 = jnp.sum(jnp.where(e1, posf, 0.0), axis=0, keepdims=True).astype(I32)
        pos_ref[0] = jnp.concatenate([pos0, pos1], axis=0)
        iota_r = lax.broadcasted_iota(I32, (R_TILE, TT), 0)
        onehot = jnp.where((iota_r == pos0) | (iota_r == pos1), 1.0, 0.0).astype(BF16)
        xs_ref[...] = _dot(onehot, h2_ref[...]).astype(BF16)

    @pl.when(n >= n_tiles)
    def _():
        xs_ref[...] = jnp.zeros_like(xs_ref)
        pos_ref[0] = jnp.zeros((TOP_K, TT), I32)


def _sort_call(eid, padoff, h2):
    n_tiles = eid.shape[0]
    d = h2.shape[-1]
    clamp = lambda n: jnp.minimum(n, n_tiles - 1)
    return pl.pallas_call(
        functools.partial(_sort_kernel, n_tiles),
        out_shape=[
            jax.ShapeDtypeStruct(((n_tiles + 1) * R_TILE, d), BF16),
            jax.ShapeDtypeStruct((n_tiles + 1, TOP_K, TT), I32),
        ],
        grid=(n_tiles + 1,),
        in_specs=[
            pl.BlockSpec((1, TOP_K, TT), lambda n: (clamp(n), 0, 0)),
            pl.BlockSpec((1, N_EXPERTS, 1), lambda n: (clamp(n), 0, 0)),
            pl.BlockSpec((TT, d), lambda n: (clamp(n), 0)),
        ],
        out_specs=[
            pl.BlockSpec((R_TILE, d), lambda n: (n, 0)),
            pl.BlockSpec((1, TOP_K, TT), lambda n: (n, 0, 0)),
        ],
        compiler_params=pltpu.CompilerParams(
            dimension_semantics=("arbitrary",), vmem_limit_bytes=VMEM_LIMIT),
        name="moe_sort",
    )(eid, padoff, h2)


def _expert_kernel(src_ref, dst_ref, bexp_ref, nused_ref, xs_hbm, wg_ref, wu_ref, wd_ref, ys_hbm,
                   xbuf, ybuf, wgb, wub, wdb, sem_in, sem_out):
    blk = pl.program_id(0)
    n_blk = pl.num_programs(0)
    nused = nused_ref[0]
    slot = blk % 2
    chunks_per_tile = R_TILE // CH
    zero_chunk = chunks_per_tile - 1
    scratch_chunk = ys_hbm.shape[0] // CH - chunks_per_tile
    de = wgb.shape[1]
    half = de // 2

    def in_copy(chunk, s, c):
        return pltpu.make_async_copy(
            xs_hbm.at[pl.ds(pl.multiple_of(chunk * CH, CH), CH)],
            xbuf.at[s, pl.ds(c * CH, CH)], sem_in.at[s])

    def out_copy(chunk, s, c):
        return pltpu.make_async_copy(
            ybuf.at[s, pl.ds(c * CH, CH)],
            ys_hbm.at[pl.ds(pl.multiple_of(chunk * CH, CH), CH)], sem_out.at[s])

    xslot = blk % N_XBUF

    @pl.when(blk == 0)
    def _():
        ybuf[...] = jnp.zeros_like(ybuf)
        for b in range(N_XBUF - 1):
            for c in range(NCH):
                in_copy(src_ref[b * NCH + c], b, c).start()
        for c in range(NCH):
            out_copy(scratch_chunk + c, 0, c).start()

    @pl.when(blk < nused)
    def _():
        expert = bexp_ref[blk]
        prev_expert = bexp_ref[jnp.maximum(blk - 1, 0)]

        @pl.when((blk == 0) | (expert != prev_expert))
        def _():
            wgb[...] = wg_ref[...].astype(BF16)
            wub[...] = wu_ref[...].astype(BF16)
            wdb[...] = wd_ref[...].astype(BF16)

        for c in range(NCH):
            in_copy(0, xslot, c).wait()
        ahead = blk + N_XBUF - 1
        live = ahead < nused
        nxt = jnp.minimum(ahead, n_blk - 1) * NCH
        prv = jnp.maximum(blk - 1, 0) * NCH

        def start_gather(lo, hi):
            for c in range(lo, hi):
                in_copy(jnp.where(live, src_ref[nxt + c], zero_chunk), ahead % N_XBUF, c).start(priority=1)

        def start_writeback(lo, hi):
            for c in range(lo, hi):
                out_copy(jnp.where(blk > 0, dst_ref[prv + c], scratch_chunk + NCH + c), 1 - slot, c).start()

        x = xbuf[xslot]
        q = NCH // 2
        hmid = []
        for j in range(2):
            g = _dot(x, wgb[:, j * half:(j + 1) * half])
            start_gather(j * q, (j + 1) * q)
            u = _dot(x, wub[:, j * half:(j + 1) * half])
            start_writeback(j * q, (j + 1) * q)
            hmid.append(((g * _sigmoid(g)) * u).astype(BF16))
        y = _dot(hmid[0], wdb[0:half, :]) + _dot(hmid[1], wdb[half:, :])
        for c in range(NCH):
            out_copy(0, slot, c).wait()
        ybuf[slot] = y.astype(BF16)

    @pl.when(blk == n_blk - 1)
    def _():
        @pl.when(nused >= 1)
        def _():
            last = (nused - 1) * NCH
            for c in range(NCH):
                out_copy(dst_ref[last + c], (nused - 1) % 2, c).start()
            for c in range(NCH):
                out_copy(0, nused % 2, c).wait()

        for c in range(NCH):
            out_copy(0, jnp.maximum(nused - 1, 0) % 2, c).wait()
        for b in range(N_XBUF - 1):
            for c in range(NCH):
                in_copy(0, (nused + b) % N_XBUF, c).wait()


def _expert_call(n_blocks, src, dst, bexp, nused, xs, layer, w_gate, w_up, w_down):
    d = xs.shape[-1]
    de = w_gate.shape[-1]
    grid_spec = pltpu.PrefetchScalarGridSpec(
        num_scalar_prefetch=4,
        grid=(n_blocks,),
        in_specs=[
            pl.BlockSpec(memory_space=pl.ANY),
            pl.BlockSpec((None, None, d, de), lambda b, s, t, e, n: (layer, e[b], 0, 0)),
            pl.BlockSpec((None, None, d, de), lambda b, s, t, e, n: (layer, e[b], 0, 0)),
            pl.BlockSpec((None, None, de, d), lambda b, s, t, e, n: (layer, e[b], 0, 0)),
        ],
        out_specs=pl.BlockSpec(memory_space=pl.ANY),
        scratch_shapes=[
            pltpu.VMEM((N_XBUF, BM, d), BF16),
            pltpu.VMEM((2, BM, d), BF16),
            pltpu.VMEM((d, de), BF16),
            pltpu.VMEM((d, de), BF16),
            pltpu.VMEM((de, d), BF16),
            pltpu.SemaphoreType.DMA((N_XBUF,)),
            pltpu.SemaphoreType.DMA((2,)),
        ],
    )
    return pl.pallas_call(
        _expert_kernel,
        out_shape=jax.ShapeDtypeStruct(xs.shape, xs.dtype),
        grid_spec=grid_spec,
        input_output_aliases={4: 0},
        compiler_params=pltpu.CompilerParams(
            dimension_semantics=("arbitrary",), vmem_limit_bytes=VMEM_LIMIT),
        name="moe_experts",
    )(src, dst, bexp, nused, xs, w_gate, w_up, w_down)


def _combine_kernel(ys_ref, pos_ref, gate_ref, x1_ref, mod_ref, lng_ref, lnb_ref, o_ref):
    pos = pos_ref[0]
    gate = gate_ref[0]
    m = mod_ref[...]
    iota_r = lax.broadcasted_iota(I32, (R_TILE, TT), 0)
    row_gate = (jnp.where(iota_r == pos[0:1], gate[0:1], 0.0)
                + jnp.where(iota_r == pos[1:2], gate[1:2], 0.0))
    gs = jnp.sum(row_gate, axis=1, keepdims=True)
    ysc = (ys_ref[...].astype(F32) * gs).astype(BF16)
    posc = jnp.transpose(pos.astype(F32))
    iota_c = lax.broadcasted_iota(I32, (TT, R_TILE), 1).astype(F32)
    pick = jnp.where((iota_c == posc[:, 0:1]) | (iota_c == posc[:, 1:2]), 1.0, 0.0).astype(BF16)
    y = _dot(pick, ysc)
    o_ref[...] = _ln(ALPHA * x1_ref[...] + (1.0 + m[5:6]) * y, lng_ref[...], lnb_ref[...])


def _combine_call(ys, pos, gate, x1, mod, layer, tiles_per_seq, ln_g, ln_b):
    t, d = x1.shape
    n_tiles = t // TT
    const2 = lambda n: (0, 0)
    return pl.pallas_call(
        _combine_kernel,
        out_shape=jax.ShapeDtypeStruct((t, d), F32),
        grid=(n_tiles,),
        in_specs=[
            pl.BlockSpec((R_TILE, d), lambda n: (n, 0)),
            pl.BlockSpec((1, TOP_K, TT), lambda n: (n, 0, 0)),
            pl.BlockSpec((1, TOP_K, TT), lambda n: (n, 0, 0)),
            pl.BlockSpec((TT, d), lambda n: (n, 0)),
            pl.BlockSpec((None, None, 6, d), lambda n: (layer, n // tiles_per_seq, 0, 0)),
            pl.BlockSpec((1, d), const2),
            pl.BlockSpec((1, d), const2),
        ],
        out_specs=pl.BlockSpec((TT, d), lambda n: (n, 0)),
        compiler_params=pltpu.CompilerParams(
            dimension_semantics=("arbitrary",), vmem_limit_bytes=VMEM_LIMIT),
        name="moe_combine",
    )(ys, pos, gate, x1, mod, ln_g, ln_b)


def _dispatch_plan(cnt, n_blocks):
    n_tiles = cnt.shape[0]
    chunks_per_tile = R_TILE // CH
    nch = (cnt + CH - 1) // CH
    padoff_ch = jnp.cumsum(nch, axis=1) - nch
    tot = jnp.sum(nch, axis=0)
    totpad = (tot + NCH - 1) // NCH * NCH
    eend = jnp.cumsum(totpad)
    ebase = eend - totpad
    tbase = jnp.cumsum(nch, axis=0) - nch
    start = (ebase[None, :] + tbase).T.reshape(-1)
    base = (jnp.arange(n_tiles, dtype=I32)[:, None] * chunks_per_tile + padoff_ch).T.reshape(-1)
    vals = jnp.stack([start, base, nch.T.reshape(-1)], axis=1)
    delta = vals - jnp.concatenate([jnp.zeros((1, 3), I32), vals[:-1]], axis=0)
    digits = jnp.concatenate([delta // LANES, delta % LANES], axis=1).astype(BF16)
    slot = jnp.arange(n_blocks * NCH, dtype=I32)
    started = (start[None, :] <= slot[:, None]).astype(BF16)
    got = jnp.dot(started, digits, preferred_element_type=F32).astype(I32)
    seg = got[:, :3] * LANES + got[:, 3:]
    j = slot - seg[:, 0]
    valid = (j < seg[:, 2]) & (slot < eend[-1])
    zero_chunk = chunks_per_tile - 1
    src = jnp.where(valid, seg[:, 1] + j, zero_chunk).astype(I32)
    blk = slot // NCH
    scratch = n_tiles * chunks_per_tile + (blk % 2) * NCH + slot % NCH
    dst = jnp.where(valid, src, scratch).astype(I32)
    first = jnp.arange(n_blocks, dtype=I32) * NCH
    bexp = jnp.minimum(jnp.sum(eend[None, :] <= first[:, None], axis=1), N_EXPERTS - 1).astype(I32)
    nused = (eend[-1] // NCH).astype(I32).reshape(1)
    padoff = (padoff_ch * CH).astype(F32)[:, :, None]
    return padoff, src, dst, bexp, nused


def _t5_bucket(rel):
    nb = N_BUCKETS // 2
    max_exact = nb // 2
    ret = jnp.where(rel > 0, nb, 0)
    n = jnp.abs(rel)
    nf = jnp.maximum(n, 1).astype(jnp.float32)
    large = max_exact + (jnp.log(nf / max_exact) / math.log(MAX_DISTANCE / max_exact)
                         * (nb - max_exact)).astype(jnp.int32)
    large = jnp.minimum(large, nb - 1)
    return ret + jnp.where(n < max_exact, n, large)


def _band_bias(rel_bias):
    qi = jnp.arange(WINDOW)
    kj = jnp.arange(3 * WINDOW)
    rel = kj[None, :] - WINDOW - qi[:, None]
    pick = _t5_bucket(rel)[:, :, None, None] == jnp.arange(N_BUCKETS)[None, None, :, None]
    bias = jnp.sum(jnp.where(pick, rel_bias.astype(F32)[None, None], 0.0), axis=2) * LOG2E
    bias = jnp.where((jnp.abs(rel) <= WINDOW)[:, :, None], bias, NEG_INF)
    grp = B_HEADS // B_KV_HEADS
    bias = jnp.transpose(bias, (2, 1, 0)).reshape(B_KV_HEADS, grp, 3 * WINDOW, WINDOW)
    return jnp.transpose(bias, (0, 2, 1, 3)).reshape(B_KV_HEADS, 3 * WINDOW, grp * WINDOW)


def kernel(x, c, ada_w, ada_b, w_in, b_in, gmlp_ln_g, gmlp_ln_b, gmlp_ws, gmlp_bs, attn_sink, conv_w,
           conv_b, conv_ln_g, conv_ln_b, w_out, b_out, ln_mix_g, ln_mix_b, w_gate, w_up, w_down,
           ln_ffn_g, ln_ffn_b, rel_bias, router_w, router_bias):
    bsz, seq, d = x.shape
    n_layers = ada_w.shape[0]
    t = bsz * seq
    nt = seq // TT
    n_tiles = t // TT
    max_chunks = n_tiles * (TOP_K * TT // CH + N_EXPERTS) + N_EXPERTS * (NCH - 1)
    n_blocks = -(-max_chunks // NCH)

    mod = _ada_call(c, ada_w, ada_b).reshape(n_layers, bsz, 6, d)
    bias = _band_bias(rel_bias)
    rw_t = router_w.T
    rw_hi = rw_t.astype(BF16)
    rw_lo = (rw_t - rw_hi.astype(F32)).astype(BF16)
    rbias = router_bias.astype(F32).reshape(N_EXPERTS, 1)
    row = lambda a: a.reshape(1, -1)

    for l in range(n_layers):
        bsb = jnp.repeat(gmlp_bs[l].T, HEAD_DIM, axis=1)
        ya, qt, k, vt, yg = _inproj_call(
            x, mod, l, w_in[l], b_in[l], row(gmlp_ln_g[l]), row(gmlp_ln_b[l]),
            gmlp_ws[l].astype(BF16), bsb)
        sink = jnp.repeat(attn_sink[l].astype(F32) * LOG2E, WINDOW).reshape(B_KV_HEADS, -1)
        x1, h2, eid, gate, cntb = _mixer_call(
            seq, qt, k, vt, bias, sink, yg, ya, x, mod, l, w_out[l].astype(BF16),
            row(b_out[l]), conv_w[l], row(conv_b[l]), row(conv_ln_g[l]), row(conv_ln_b[l]),
            row(ln_mix_g[l]), row(ln_mix_b[l]), rw_hi, rw_lo, rbias)
        cnt = cntb[:, :, 0].astype(I32)
        padoff, src, dst, bexp, nused = _dispatch_plan(cnt, n_blocks)
        xs, pos = _sort_call(eid, padoff, h2.reshape(t, d))
        ys = _expert_call(n_blocks, src, dst, bexp, nused, xs, l, w_gate, w_up, w_down)
        x = _combine_call(ys, pos, gate, x1.reshape(t, d), mod, l, nt, row(ln_ffn_g[l]),
                          row(ln_ffn_b[l])).reshape(bsz, seq, d)
    return x
```

```python
import functools
import math

import jax
import jax.numpy as jnp
from jax import lax
from jax.experimental import pallas as pl
from jax.experimental.pallas import tpu as pltpu

F32 = jnp.float32
BF16 = jnp.bfloat16
I32 = jnp.int32

D_MODEL = 1024
DEPTH = 2
HEAD_DIM = 64
A_WIDTH = 256
A_HEADS = 4
CHUNK = 128
B_WIDTH = 512
B_HEADS = 8
B_KV_HEADS = 2
KV_WIDTH = B_KV_HEADS * HEAD_DIM
WINDOW = 128
N_BUCKETS = 32
MAX_DISTANCE = 128
C_WIDTH = 256
CONV_WIDTH = 31
CONV_PAD = CONV_WIDTH // 2
IN_WIDTH = 2 * A_WIDTH + B_WIDTH + 2 * KV_WIDTH + 2 * C_WIDTH
N_EXPERTS = 32
N_GROUPS = 4
EXPERTS_PER_GROUP = N_EXPERTS // N_GROUPS
TOP_K = 2
D_EXPERT = D_MODEL // 2
ALPHA = (2 * DEPTH) ** 0.25
LN_EPS = 1e-5
NEG_INF = -1e30
LOG2E = 1.4426950408889634
Q_SCALE = HEAD_DIM ** -0.5 * LOG2E

LANES = 128
SUBLANES = 8
BF16_SUBLANES = 16
VMEM_LIMIT = 48 * 1024 * 1024

ADA_TN = 1536
TS = 512
TT = 512
CH = BF16_SUBLANES
R_TILE = 1536
BM = 512
NCH = BM // CH
N_XBUF = 3
HALO = 16

assert R_TILE >= TOP_K * TT + N_EXPERTS * (CH - 1) + CH
assert R_TILE >= 2 * NCH * CH


def _sigmoid(x):
    return 1.0 / (1.0 + jnp.exp(-x))


def _gelu_tanh(x):
    return x * (0.5 * (1.0 + jnp.tanh(0.7978845608028654 * (x + 0.044715 * (x * x * x)))))


def _ln(x, g, b):
    mu = jnp.mean(x, axis=-1, keepdims=True)
    xc = x - mu
    var = jnp.mean(xc * xc, axis=-1, keepdims=True)
    return xc * lax.rsqrt(var + LN_EPS) * g + b


def _dot(a, b):
    return jnp.dot(a, b, preferred_element_type=F32)


def _dot_nt(a, b):
    return lax.dot_general(a, b, (((1,), (1,)), ((), ())), preferred_element_type=F32)


def _ada_kernel(c_ref, w_ref, b_ref, o_ref):
    c = c_ref[...]
    s = (c * _sigmoid(c)).astype(BF16)
    o_ref[0] = _dot(s, w_ref[0].astype(BF16)) + b_ref[0]


def _ada_call(c, ada_w, ada_b):
    nl, d, n = ada_w.shape
    bsz = c.shape[0]
    return pl.pallas_call(
        _ada_kernel,
        out_shape=jax.ShapeDtypeStruct((nl, bsz, n), F32),
        grid=(nl, n // ADA_TN),
        in_specs=[
            pl.BlockSpec((bsz, d), lambda l, j: (0, 0)),
            pl.BlockSpec((1, d, ADA_TN), lambda l, j: (l, 0, j)),
            pl.BlockSpec((1, 1, ADA_TN), lambda l, j: (l, 0, j)),
        ],
        out_specs=pl.BlockSpec((1, bsz, ADA_TN), lambda l, j: (l, 0, j)),
        compiler_params=pltpu.CompilerParams(
            dimension_semantics=("arbitrary", "arbitrary"), vmem_limit_bytes=VMEM_LIMIT),
        name="ada_mod",
    )(c, ada_w, ada_b.reshape(nl, 1, n))


def _inproj_kernel(x_ref, xp_ref, xn_ref, mod_ref, w_ref, b_ref, wqt_ref, bq_ref, wvt_ref, bv_ref,
                   lng_ref, lnb_ref, ws_ref, bsb_ref, cw_ref, cb_ref, clg_ref, clb_ref,
                   ya_ref, qt_ref, k_ref, vt_ref, yc_ref, conv_scr, z_scr):
    i = pl.program_id(1)
    n_i = pl.num_programs(1)
    m = mod_ref[...]

    def modulate(xv):
        return (xv * (1.0 + m[1:2]) + m[0:1]).astype(BF16)

    hb = modulate(x_ref[0])
    col_u, col_v = 0, A_WIDTH
    col_k = 2 * A_WIDTH + B_WIDTH
    col_a = col_k + 2 * KV_WIDTH
    col_g = col_a + C_WIDTH

    def proj(lhs, c0, width):
        return _dot(lhs, w_ref[:, c0:c0 + width]) + b_ref[:, c0:c0 + width]

    hx = jnp.concatenate([modulate(xp_ref[0]), hb, modulate(xn_ref[0])], axis=0)
    yg = proj(hx, col_a, C_WIDTH) * _sigmoid(proj(hx, col_g, C_WIDTH))
    conv_scr[0:HALO, :] = yg[0:HALO] * jnp.where(i > 0, 1.0, 0.0)
    conv_scr[HALO:HALO + TS, :] = yg[HALO:HALO + TS]
    conv_scr[HALO + TS:, :] = yg[HALO + TS:] * jnp.where(i < n_i - 1, 1.0, 0.0)

    u = _gelu_tanh(proj(hb, col_u, A_WIDTH))
    v = _gelu_tanh(proj(hb, col_v, A_WIDTH))
    qt_ref[0] = ((_dot_nt(wqt_ref[...], hb) + bq_ref[...]) * Q_SCALE).astype(BF16)
    k_ref[0] = proj(hb, col_k, KV_WIDTH).astype(BF16)
    vt_ref[0] = (_dot_nt(wvt_ref[...], hb) + bv_ref[...]).astype(BF16)

    vb = _ln(v, lng_ref[...], lnb_ref[...]).astype(BF16)
    head_of_lane = lax.broadcasted_iota(I32, (CHUNK, A_WIDTH), 1) // HEAD_DIM
    for ch in range(TS // CHUNK):
        vc = vb[ch * CHUNK:(ch + 1) * CHUNK]
        acc = bsb_ref[...]
        for hh in range(A_HEADS):
            acc = acc + _dot(ws_ref[hh], jnp.where(head_of_lane == hh, vc, jnp.zeros_like(vc)))
        ya_ref[0, ch * CHUNK:(ch + 1) * CHUNK, :] = (u[ch * CHUNK:(ch + 1) * CHUNK] * acc).astype(BF16)

    first = HALO - CONV_PAD
    acc = jnp.zeros((TS, C_WIDTH), F32) + cb_ref[...]
    for r in range(SUBLANES):
        z = None
        for a in range(-(-(first + CONV_WIDTH) // SUBLANES)):
            w = a * SUBLANES + r - first
            if 0 <= w < CONV_WIDTH:
                term = conv_scr[a * SUBLANES:a * SUBLANES + TS + SUBLANES, :] * cw_ref[w:w + 1, :]
                z = term if z is None else z + term
        if r == 0:
            acc = acc + z[0:TS]
        else:
            z_scr[r] = z
            acc = acc + z_scr[r, r:r + TS, :]
    yc = _ln(acc, clg_ref[...], clb_ref[...])
    yc_ref[0] = (yc * _sigmoid(yc)).astype(BF16)


def _inproj_call(x, mod, layer, w_in, b_in, ln_g, ln_b, ws, bsb, conv_w, conv_b, conv_ln_g, conv_ln_b):
    bsz, seq, d = x.shape
    grid = (bsz, seq // TS)
    hb = TS // HALO
    const2 = lambda b, i: (0, 0)
    row = lambda b, i: (b, i, 0)
    colblk = lambda b, i: (b, 0, i)
    prev_h = lambda b, i: (b, jnp.maximum(i * hb - 1, 0), 0)
    next_h = lambda b, i: (b, jnp.minimum((i + 1) * hb, seq // HALO - 1), 0)
    q0 = 2 * A_WIDTH
    v0 = q0 + B_WIDTH + KV_WIDTH
    wb = w_in.astype(BF16)
    wqt = w_in[:, q0:q0 + B_WIDTH].T.astype(BF16)
    wvt = w_in[:, v0:v0 + KV_WIDTH].T.astype(BF16)
    bq = b_in[q0:q0 + B_WIDTH].reshape(B_WIDTH, 1)
    bv = b_in[v0:v0 + KV_WIDTH].reshape(KV_WIDTH, 1)

    def out(width):
        return jax.ShapeDtypeStruct((bsz, seq, width), BF16), pl.BlockSpec((1, TS, width), row)

    def out_t(width):
        return jax.ShapeDtypeStruct((bsz, width, seq), BF16), pl.BlockSpec((1, width, TS), colblk)

    outs = [out(A_WIDTH), out_t(B_WIDTH), out(KV_WIDTH), out_t(KV_WIDTH), out(C_WIDTH)]
    return pl.pallas_call(
        _inproj_kernel,
        out_shape=[o[0] for o in outs],
        grid=grid,
        in_specs=[
            pl.BlockSpec((1, TS, d), row),
            pl.BlockSpec((1, HALO, d), prev_h),
            pl.BlockSpec((1, HALO, d), next_h),
            pl.BlockSpec((None, None, 6, d), lambda b, i: (layer, b, 0, 0)),
            pl.BlockSpec((d, IN_WIDTH), const2),
            pl.BlockSpec((1, IN_WIDTH), const2),
            pl.BlockSpec((B_WIDTH, d), const2),
            pl.BlockSpec((B_WIDTH, 1), const2),
            pl.BlockSpec((KV_WIDTH, d), const2),
            pl.BlockSpec((KV_WIDTH, 1), const2),
            pl.BlockSpec((1, A_WIDTH), const2),
            pl.BlockSpec((1, A_WIDTH), const2),
            pl.BlockSpec((A_HEADS, CHUNK, CHUNK), lambda b, i: (0, 0, 0)),
            pl.BlockSpec((CHUNK, A_WIDTH), const2),
            pl.BlockSpec((CONV_WIDTH, C_WIDTH), const2),
            pl.BlockSpec((1, C_WIDTH), const2),
            pl.BlockSpec((1, C_WIDTH), const2),
            pl.BlockSpec((1, C_WIDTH), const2),
        ],
        out_specs=[o[1] for o in outs],
        scratch_shapes=[
            pltpu.VMEM((TS + 2 * HALO, C_WIDTH), F32),
            pltpu.VMEM((SUBLANES, TS + SUBLANES, C_WIDTH), F32),
        ],
        compiler_params=pltpu.CompilerParams(
            dimension_semantics=("arbitrary", "arbitrary"), vmem_limit_bytes=VMEM_LIMIT),
        name="inproj_gmlp_conv",
    )(x, x, x, mod, wb, b_in.reshape(1, -1), wqt, bq, wvt, bv, ln_g, ln_b, ws, bsb,
      conv_w, conv_b, conv_ln_g, conv_ln_b)


def _first_argmax(vals, iota_f, width):
    m = jnp.max(vals, axis=0, keepdims=True)
    idx = jnp.min(jnp.where(vals == m, iota_f, float(width)), axis=0, keepdims=True)
    return m, idx


def _mixer_kernel(seq_len, qt_ref, kp_ref, kc_ref, kn_ref, vtp_ref, vtc_ref, vtn_ref, bias_ref, sink_ref,
                  yc_ref, ya_ref, x_ref, mod_ref, wo_ref, bo_ref, lng_ref, lnb_ref, rw_ref, rb_ref,
                  x1_ref, h2_ref, eid_ref, gate_ref, cnt_ref, ot_scr, s_scr):
    i = pl.program_id(1)
    t0 = i * TT
    m = mod_ref[...]

    kfull = jnp.concatenate([kp_ref[0], kc_ref[0], kn_ref[0]], axis=0)
    vtfull = jnp.concatenate([vtp_ref[0], vtc_ref[0], vtn_ref[0]], axis=1)
    grp = B_HEADS // B_KV_HEADS
    n_qb = TT // WINDOW
    units = [(jb, g) for jb in range(n_qb) for g in range(B_KV_HEADS)]
    key_i = lax.broadcasted_iota(I32, (3 * WINDOW, 1), 0)

    def scores(u):
        jb, g = units[u]
        kb = kfull[jb * WINDOW:(jb + 3) * WINDOW, g * HEAD_DIM:(g + 1) * HEAD_DIM]
        qt = jnp.concatenate(
            [qt_ref[0, h * HEAD_DIM:(h + 1) * HEAD_DIM, jb * WINDOW:(jb + 1) * WINDOW]
             for h in range(g * grp, (g + 1) * grp)], axis=1)
        s = _dot(kb, qt) + bias_ref[g]
        if jb == 0 or jb == n_qb - 1:
            kpos = t0 + (jb - 1) * WINDOW + key_i
            s = jnp.where((kpos >= 0) & (kpos < seq_len), s, NEG_INF)
        s_scr[u % 2] = s

    def values(u):
        jb, g = units[u]
        s = s_scr[u % 2]
        sink = sink_ref[g:g + 1, :]
        mx = jnp.maximum(jnp.max(s, axis=0, keepdims=True), sink)
        p = jnp.exp2(s - mx)
        den = jnp.sum(p, axis=0, keepdims=True) + jnp.exp2(sink - mx)
        vt = vtfull[g * HEAD_DIM:(g + 1) * HEAD_DIM, jb * WINDOW:(jb + 3) * WINDOW]
        ot = _dot(vt, p.astype(BF16)) / den
        for hh in range(grp):
            h = g * grp + hh
            ot_scr[h * HEAD_DIM:(h + 1) * HEAD_DIM, jb * WINDOW:(jb + 1) * WINDOW] = (
                ot[:, hh * WINDOW:(hh + 1) * WINDOW])

    scores(0)
    for u in range(len(units)):
        if u + 1 < len(units):
            scores(u + 1)
        values(u)
    yb = jnp.transpose(ot_scr[...]).astype(BF16)

    y = (_dot(ya_ref[0], wo_ref[0:A_WIDTH, :])
         + _dot(yb, wo_ref[A_WIDTH:A_WIDTH + B_WIDTH, :])
         + _dot(yc_ref[0], wo_ref[A_WIDTH + B_WIDTH:, :]) + bo_ref[...])
    x1 = _ln(ALPHA * x_ref[0] + (1.0 + m[2:3]) * y, lng_ref[...], lnb_ref[...])
    x1_ref[0] = x1

    h2 = (x1 * (1.0 + m[4:5]) + m[3:4]).astype(BF16)
    h2_ref[0] = h2
    scores = _sigmoid(_dot_nt(rw_ref[...], h2))
    sel = scores + rb_ref[...]
    iota_f = lax.broadcasted_iota(I32, (EXPERTS_PER_GROUP, TT), 0).astype(F32)
    best = None
    for g in range(N_GROUPS):
        sl = slice(g * EXPERTS_PER_GROUP, (g + 1) * EXPERTS_PER_GROUP)
        sg = sel[sl]
        m1, i1 = _first_argmax(sg, iota_f, EXPERTS_PER_GROUP)
        m2, i2 = _first_argmax(jnp.where(iota_f == i1, -jnp.inf, sg), iota_f, EXPERTS_PER_GROUP)
        sc = scores[sl]
        s1 = jnp.sum(jnp.where(iota_f == i1, sc, 0.0), axis=0, keepdims=True)
        s2 = jnp.sum(jnp.where(iota_f == i2, sc, 0.0), axis=0, keepdims=True)
        cand = (m1 + m2, i1 + g * EXPERTS_PER_GROUP, i2 + g * EXPERTS_PER_GROUP, s1, s2)
        if best is None:
            best = cand
        else:
            take = cand[0] > best[0]
            best = tuple(jnp.where(take, c, b) for c, b in zip(cand, best))
    _, e1, e2, s1, s2 = best
    eid = jnp.concatenate([e1, e2], axis=0).astype(I32)
    eid_ref[0] = eid
    gate_ref[0] = jnp.concatenate([s1, s2], axis=0) / (s1 + s2)
    iota_e = lax.broadcasted_iota(I32, (N_EXPERTS, TT), 0)
    member = jnp.where((iota_e == eid[0:1]) | (iota_e == eid[1:2]), 1.0, 0.0)
    cnt_ref[0] = jnp.broadcast_to(jnp.sum(member, axis=1, keepdims=True), (N_EXPERTS, LANES))


def _mixer_call(seq_len, qt, k, vt, bias, sink, yc, ya, x, mod, layer, w_out, b_out, ln_g, ln_b, rw, rbias):
    bsz, seq, d = x.shape
    nt = seq // TT
    kb = TT // WINDOW
    grp = B_HEADS // B_KV_HEADS
    const2 = lambda b, i: (0, 0)
    row = lambda b, i: (b, i, 0)
    colblk = lambda b, i: (b, 0, i)
    prev_k = lambda b, i: (b, jnp.maximum(i * kb - 1, 0), 0)
    next_k = lambda b, i: (b, jnp.minimum((i + 1) * kb, seq // WINDOW - 1), 0)
    prev_v = lambda b, i: (b, 0, jnp.maximum(i * kb - 1, 0))
    next_v = lambda b, i: (b, 0, jnp.minimum((i + 1) * kb, seq // WINDOW - 1))
    tile = lambda b, i: (b * nt + i, 0, 0)
    return pl.pallas_call(
        functools.partial(_mixer_kernel, seq_len),
        out_shape=[
            jax.ShapeDtypeStruct((bsz, seq, d), F32),
            jax.ShapeDtypeStruct((bsz, seq, d), BF16),
            jax.ShapeDtypeStruct((bsz * nt, TOP_K, TT), I32),
            jax.ShapeDtypeStruct((bsz * nt, TOP_K, TT), F32),
            jax.ShapeDtypeStruct((bsz * nt, N_EXPERTS, LANES), F32),
        ],
        grid=(bsz, nt),
        in_specs=[
            pl.BlockSpec((1, B_WIDTH, TT), colblk),
            pl.BlockSpec((1, WINDOW, KV_WIDTH), prev_k),
            pl.BlockSpec((1, TT, KV_WIDTH), row),
            pl.BlockSpec((1, WINDOW, KV_WIDTH), next_k),
            pl.BlockSpec((1, KV_WIDTH, WINDOW), prev_v),
            pl.BlockSpec((1, KV_WIDTH, TT), colblk),
            pl.BlockSpec((1, KV_WIDTH, WINDOW), next_v),
            pl.BlockSpec((B_KV_HEADS, 3 * WINDOW, grp * WINDOW), lambda b, i: (0, 0, 0)),
            pl.BlockSpec((B_KV_HEADS, grp * WINDOW), const2),
            pl.BlockSpec((1, TT, C_WIDTH), row),
            pl.BlockSpec((1, TT, A_WIDTH), row),
            pl.BlockSpec((1, TT, d), row),
            pl.BlockSpec((None, None, 6, d), lambda b, i: (layer, b, 0, 0)),
            pl.BlockSpec((d, d), const2),
            pl.BlockSpec((1, d), const2),
            pl.BlockSpec((1, d), const2),
            pl.BlockSpec((1, d), const2),
            pl.BlockSpec((N_EXPERTS, d), const2),
            pl.BlockSpec((N_EXPERTS, 1), const2),
        ],
        out_specs=[
            pl.BlockSpec((1, TT, d), row),
            pl.BlockSpec((1, TT, d), row),
            pl.BlockSpec((1, TOP_K, TT), tile),
            pl.BlockSpec((1, TOP_K, TT), tile),
            pl.BlockSpec((1, N_EXPERTS, LANES), tile),
        ],
        scratch_shapes=[
            pltpu.VMEM((B_WIDTH, TT), F32),
            pltpu.VMEM((2, 3 * WINDOW, grp * WINDOW), F32),
        ],
        compiler_params=pltpu.CompilerParams(
            dimension_semantics=("arbitrary", "arbitrary"), vmem_limit_bytes=VMEM_LIMIT),
        name="mixer_out_router",
    )(qt, k, k, k, vt, vt, vt, bias, sink, yc, ya, x, mod, w_out, b_out, ln_g, ln_b, rw, rbias)


def _sort_kernel(n_tiles, eid_ref, padoff_ref, h2_ref, xs_ref, pos_ref):
    n = pl.program_id(0)

    @pl.when(n < n_tiles)
    def _():
        eid = eid_ref[0]
        iota_e = lax.broadcasted_iota(I32, (N_EXPERTS, TT), 0)
        e0 = iota_e == eid[0:1]
        e1 = iota_e == eid[1:2]
        member = jnp.where(e0 | e1, 1.0, 0.0).astype(BF16)
        r_i = lax.broadcasted_iota(I32, (TT, TT), 0)
        c_i = lax.broadcasted_iota(I32, (TT, TT), 1)
        before = jnp.where(r_i < c_i, 1.0, 0.0).astype(BF16)
        rank = _dot(member, before)
        posf = padoff_ref[0] + rank
        pos0 = jnp.sum(jnp.where(e0, posf, 0.0), axis=0, keepdims=True).astype(I32)
        pos1 = jnp.sum(jnp.where(e1, posf, 0.0), axis=0, keepdims=True).astype(I32)
        pos_ref[0] = jnp.concatenate([pos0, pos1], axis=0)
        iota_r = lax.broadcasted_iota(I32, (R_TILE, TT), 0)
        onehot = jnp.where((iota_r == pos0) | (iota_r == pos1), 1.0, 0.0).astype(BF16)
        xs_ref[...] = _dot(onehot, h2_ref[...]).astype(BF16)

    @pl.when(n >= n_tiles)
    def _():
        xs_ref[...] = jnp.zeros_like(xs_ref)
        pos_ref[0] = jnp.zeros((TOP_K, TT), I32)


def _sort_call(eid, padoff, h2):
    n_tiles = eid.shape[0]
    d = h2.shape[-1]
    clamp = lambda n: jnp.minimum(n, n_tiles - 1)
    return pl.pallas_call(
        functools.partial(_sort_kernel, n_tiles),
        out_shape=[
            jax.ShapeDtypeStruct(((n_tiles + 1) * R_TILE, d), BF16),
            jax.ShapeDtypeStruct((n_tiles + 1, TOP_K, TT), I32),
        ],
        grid=(n_tiles + 1,),
        in_specs=[
            pl.BlockSpec((1, TOP_K, TT), lambda n: (clamp(n), 0, 0)),
            pl.BlockSpec((1, N_EXPERTS, 1), lambda n: (clamp(n), 0, 0)),
            pl.BlockSpec((TT, d), lambda n: (clamp(n), 0)),
        ],
        out_specs=[
            pl.BlockSpec((R_TILE, d), lambda n: (n, 0)),
            pl.BlockSpec((1, TOP_K, TT), lambda n: (n, 0, 0)),
        ],
        compiler_params=pltpu.CompilerParams(
            dimension_semantics=("arbitrary",), vmem_limit_bytes=VMEM_LIMIT),
        name="moe_sort",
    )(eid, padoff, h2)


def _expert_kernel(src_ref, dst_ref, bexp_ref, nused_ref, xs_hbm, wg_ref, wu_ref, wd_ref, ys_hbm,
                   xbuf, ybuf, wgb, wub, wdb, sem_in, sem_out):
    blk = pl.program_id(0)
    n_blk = pl.num_programs(0)
    nused = nused_ref[0]
    slot = blk % 2
    chunks_per_tile = R_TILE // CH
    zero_chunk = chunks_per_tile - 1
    scratch_chunk = ys_hbm.shape[0] // CH - chunks_per_tile
    de = wgb.shape[1]
    half = de // 2

    def in_copy(chunk, s, c):
        return pltpu.make_async_copy(
            xs_hbm.at[pl.ds(pl.multiple_of(chunk * CH, CH), CH)],
            xbuf.at[s, pl.ds(c * CH, CH)], sem_in.at[s])

    def out_copy(chunk, s, c):
        return pltpu.make_async_copy(
            ybuf.at[s, pl.ds(c * CH, CH)],
            ys_hbm.at[pl.ds(pl.multiple_of(chunk * CH, CH), CH)], sem_out.at[s])

    xslot = blk % N_XBUF

    @pl.when(blk == 0)
    def _():
        ybuf[...] = jnp.zeros_like(ybuf)
        for b in range(N_XBUF - 1):
            for c in range(NCH):
                in_copy(src_ref[b * NCH + c], b, c).start()
        for c in range(NCH):
            out_copy(scratch_chunk + c, 0, c).start()

    @pl.when(blk < nused)
    def _():
        expert = bexp_ref[blk]
        prev_expert = bexp_ref[jnp.maximum(blk - 1, 0)]

        @pl.when((blk == 0) | (expert != prev_expert))
        def _():
            wgb[...] = wg_ref[...].astype(BF16)
            wub[...] = wu_ref[...].astype(BF16)
            wdb[...] = wd_ref[...].astype(BF16)

        for c in range(NCH):
            in_copy(0, xslot, c).wait()
        ahead = blk + N_XBUF - 1
        live = ahead < nused
        nxt = jnp.minimum(ahead, n_blk - 1) * NCH
        prv = jnp.maximum(blk - 1, 0) * NCH

        def start_gather(lo, hi):
            for c in range(lo, hi):
                in_copy(jnp.where(live, src_ref[nxt + c], zero_chunk), ahead % N_XBUF, c).start(priority=1)

        def start_writeback(lo, hi):
            for c in range(lo, hi):
                out_copy(jnp.where(blk > 0, dst_ref[prv + c], scratch_chunk + NCH + c), 1 - slot, c).start()

        x = xbuf[xslot]
        q = NCH // 2
        hmid = []
        for j in range(2):
            g = _dot(x, wgb[:, j * half:(j + 1) * half])
            start_gather(j * q, (j + 1) * q)
            u = _dot(x, wub[:, j * half:(j + 1) * half])
            start_writeback(j * q, (j + 1) * q)
            hmid.append(((g * _sigmoid(g)) * u).astype(BF16))
        y = _dot(hmid[0], wdb[0:half, :]) + _dot(hmid[1], wdb[half:, :])
        for c in range(NCH):
            out_copy(0, slot, c).wait()
        ybuf[slot] = y.astype(BF16)

    @pl.when(blk == n_blk - 1)
    def _():
        @pl.when(nused >= 1)
        def _():
            last = (nused - 1) * NCH
            for c in range(NCH):
                out_copy(dst_ref[last + c], (nused - 1) % 2, c).start()
            for c in range(NCH):
                out_copy(0, nused % 2, c).wait()

        for c in range(NCH):
            out_copy(0, jnp.maximum(nused - 1, 0) % 2, c).wait()
        for b in range(N_XBUF - 1):
            for c in range(NCH):
                in_copy(0, (nused + b) % N_XBUF, c).wait()


def _expert_call(n_blocks, src, dst, bexp, nused, xs, layer, w_gate, w_up, w_down):
    d = xs.shape[-1]
    de = w_gate.shape[-1]
    grid_spec = pltpu.PrefetchScalarGridSpec(
        num_scalar_prefetch=4,
        grid=(n_blocks,),
        in_specs=[
            pl.BlockSpec(memory_space=pl.ANY),
            pl.BlockSpec((None, None, d, de), lambda b, s, t, e, n: (layer, e[b], 0, 0)),
            pl.BlockSpec((None, None, d, de), lambda b, s, t, e, n: (layer, e[b], 0, 0)),
            pl.BlockSpec((None, None, de, d), lambda b, s, t, e, n: (layer, e[b], 0, 0)),
        ],
        out_specs=pl.BlockSpec(memory_space=pl.ANY),
        scratch_shapes=[
            pltpu.VMEM((N_XBUF, BM, d), BF16),
            pltpu.VMEM((2, BM, d), BF16),
            pltpu.VMEM((d, de), BF16),
            pltpu.VMEM((d, de), BF16),
            pltpu.VMEM((de, d), BF16),
            pltpu.SemaphoreType.DMA((N_XBUF,)),
            pltpu.SemaphoreType.DMA((2,)),
        ],
    )
    return pl.pallas_call(
        _expert_kernel,
        out_shape=jax.ShapeDtypeStruct(xs.shape, xs.dtype),
        grid_spec=grid_spec,
        input_output_aliases={4: 0},
        compiler_params=pltpu.CompilerParams(
            dimension_semantics=("arbitrary",), vmem_limit_bytes=VMEM_LIMIT),
        name="moe_experts",
    )(src, dst, bexp, nused, xs, w_gate, w_up, w_down)


def _combine_kernel(ys_ref, pos_ref, gate_ref, x1_ref, mod_ref, lng_ref, lnb_ref, o_ref):
    pos = pos_ref[0]
    gate = gate_ref[0]
    m = mod_ref[...]
    iota_r = lax.broadcasted_iota(I32, (R_TILE, TT), 0)
    row_gate = (jnp.where(iota_r == pos[0:1], gate[0:1], 0.0)
                + jnp.where(iota_r == pos[1:2], gate[1:2], 0.0))
    gs = jnp.sum(row_gate, axis=1, keepdims=True)
    ysc = (ys_ref[...].astype(F32) * gs).astype(BF16)
    posc = jnp.transpose(pos.astype(F32))
    iota_c = lax.broadcasted_iota(I32, (TT, R_TILE), 1).astype(F32)
    pick = jnp.where((iota_c == posc[:, 0:1]) | (iota_c == posc[:, 1:2]), 1.0, 0.0).astype(BF16)
    y = _dot(pick, ysc)
    o_ref[...] = _ln(ALPHA * x1_ref[...] + (1.0 + m[5:6]) * y, lng_ref[...], lnb_ref[...])


def _combine_call(ys, pos, gate, x1, mod, layer, tiles_per_seq, ln_g, ln_b):
    t, d = x1.shape
    n_tiles = t // TT
    const2 = lambda n: (0, 0)
    return pl.pallas_call(
        _combine_kernel,
        out_shape=jax.ShapeDtypeStruct((t, d), F32),
        grid=(n_tiles,),
        in_specs=[
            pl.BlockSpec((R_TILE, d), lambda n: (n, 0)),
            pl.BlockSpec((1, TOP_K, TT), lambda n: (n, 0, 0)),
            pl.BlockSpec((1, TOP_K, TT), lambda n: (n, 0, 0)),
            pl.BlockSpec((TT, d), lambda n: (n, 0)),
            pl.BlockSpec((None, None, 6, d), lambda n: (layer, n // tiles_per_seq, 0, 0)),
            pl.BlockSpec((1, d), const2),
            pl.BlockSpec((1, d), const2),
        ],
        out_specs=pl.BlockSpec((TT, d), lambda n: (n, 0)),
        compiler_params=pltpu.CompilerParams(
            dimension_semantics=("arbitrary",), vmem_limit_bytes=VMEM_LIMIT),
        name="moe_combine",
    )(ys, pos, gate, x1, mod, ln_g, ln_b)


def _dispatch_plan(cnt, n_blocks):
    n_tiles = cnt.shape[0]
    chunks_per_tile = R_TILE // CH
    nch = (cnt + CH - 1) // CH
    padoff_ch = jnp.cumsum(nch, axis=1) - nch
    tot = jnp.sum(nch, axis=0)
    totpad = (tot + NCH - 1) // NCH * NCH
    eend = jnp.cumsum(totpad)
    ebase = eend - totpad
    tbase = jnp.cumsum(nch, axis=0) - nch
    start = (ebase[None, :] + tbase).T.reshape(-1)
    base = (jnp.arange(n_tiles, dtype=I32)[:, None] * chunks_per_tile + padoff_ch).T.reshape(-1)
    vals = jnp.stack([start, base, nch.T.reshape(-1)], axis=1)
    delta = vals - jnp.concatenate([jnp.zeros((1, 3), I32), vals[:-1]], axis=0)
    digits = jnp.concatenate([delta // LANES, delta % LANES], axis=1).astype(BF16)
    slot = jnp.arange(n_blocks * NCH, dtype=I32)
    started = (start[None, :] <= slot[:, None]).astype(BF16)
    got = jnp.dot(started, digits, preferred_element_type=F32).astype(I32)
    seg = got[:, :3] * LANES + got[:, 3:]
    j = slot - seg[:, 0]
    valid = (j < seg[:, 2]) & (slot < eend[-1])
    zero_chunk = chunks_per_tile - 1
    src = jnp.where(valid, seg[:, 1] + j, zero_chunk).astype(I32)
    blk = slot // NCH
    scratch = n_tiles * chunks_per_tile + (blk % 2) * NCH + slot % NCH
    dst = jnp.where(valid, src, scratch).astype(I32)
    first = jnp.arange(n_blocks, dtype=I32) * NCH
    bexp = jnp.minimum(jnp.sum(eend[None, :] <= first[:, None], axis=1), N_EXPERTS - 1).astype(I32)
    nused = (eend[-1] // NCH).astype(I32).reshape(1)
    padoff = (padoff_ch * CH).astype(F32)[:, :, None]
    return padoff, src, dst, bexp, nused


def _t5_bucket(rel):
    nb = N_BUCKETS // 2
    max_exact = nb // 2
    ret = jnp.where(rel > 0, nb, 0)
    n = jnp.abs(rel)
    nf = jnp.maximum(n, 1).astype(jnp.float32)
    large = max_exact + (jnp.log(nf / max_exact) / math.log(MAX_DISTANCE / max_exact)
                         * (nb - max_exact)).astype(jnp.int32)
    large = jnp.minimum(large, nb - 1)
    return ret + jnp.where(n < max_exact, n, large)


def _band_bias(rel_bias):
    qi = jnp.arange(WINDOW)
    kj = jnp.arange(3 * WINDOW)
    rel = kj[None, :] - WINDOW - qi[:, None]
    pick = _t5_bucket(rel)[:, :, None, None] == jnp.arange(N_BUCKETS)[None, None, :, None]
    bias = jnp.sum(jnp.where(pick, rel_bias.astype(F32)[None, None], 0.0), axis=2) * LOG2E
    bias = jnp.where((jnp.abs(rel) <= WINDOW)[:, :, None], bias, NEG_INF)
    grp = B_HEADS // B_KV_HEADS
    bias = jnp.transpose(bias, (2, 1, 0)).reshape(B_KV_HEADS, grp, 3 * WINDOW, WINDOW)
    return jnp.transpose(bias, (0, 2, 1, 3)).reshape(B_KV_HEADS, 3 * WINDOW, grp * WINDOW)


def kernel(x, c, ada_w, ada_b, w_in, b_in, gmlp_ln_g, gmlp_ln_b, gmlp_ws, gmlp_bs, attn_sink, conv_w,
           conv_b, conv_ln_g, conv_ln_b, w_out, b_out, ln_mix_g, ln_mix_b, w_gate, w_up, w_down,
           ln_ffn_g, ln_ffn_b, rel_bias, router_w, router_bias):
    bsz, seq, d = x.shape
    n_layers = ada_w.shape[0]
    t = bsz * seq
    nt = seq // TT
    n_tiles = t // TT
    max_chunks = n_tiles * (TOP_K * TT // CH + N_EXPERTS) + N_EXPERTS * (NCH - 1)
    n_blocks = -(-max_chunks // NCH)

    mod = _ada_call(c, ada_w, ada_b).reshape(n_layers, bsz, 6, d)
    bias = _band_bias(rel_bias)
    rw = router_w.T.astype(BF16)
    rbias = router_bias.astype(F32).reshape(N_EXPERTS, 1)
    row = lambda a: a.reshape(1, -1)

    for l in range(n_layers):
        bsb = jnp.repeat(gmlp_bs[l].T, HEAD_DIM, axis=1)
        ya, qt, k, vt, yc = _inproj_call(
            x, mod, l, w_in[l], b_in[l], row(gmlp_ln_g[l]), row(gmlp_ln_b[l]),
            gmlp_ws[l].astype(BF16), bsb, conv_w[l], row(conv_b[l]), row(conv_ln_g[l]), row(conv_ln_b[l]))
        sink = jnp.repeat(attn_sink[l].astype(F32) * LOG2E, WINDOW).reshape(B_KV_HEADS, -1)
        x1, h2, eid, gate, cntb = _mixer_call(
            seq, qt, k, vt, bias, sink, yc, ya, x, mod, l, w_out[l].astype(BF16),
            row(b_out[l]), row(ln_mix_g[l]), row(ln_mix_b[l]), rw, rbias)
        cnt = cntb[:, :, 0].astype(I32)
        padoff, src, dst, bexp, nused = _dispatch_plan(cnt, n_blocks)
        xs, pos = _sort_call(eid, padoff, h2.reshape(t, d))
        ys = _expert_call(n_blocks, src, dst, bexp, nused, xs, l, w_gate, w_up, w_down)
        x = _combine_call(ys, pos, gate, x1.reshape(t, d), mod, l, nt, row(ln_ffn_g[l]),
                          row(ln_ffn_b[l])).reshape(bsz, seq, d)
    return x
```

```python
import functools
import math

import jax
import jax.numpy as jnp
from jax import lax
from jax.experimental import pallas as pl
from jax.experimental.pallas import tpu as pltpu

F32 = jnp.float32
BF16 = jnp.bfloat16
I32 = jnp.int32

D_MODEL = 1024
DEPTH = 2
HEAD_DIM = 64
A_WIDTH = 256
A_HEADS = 4
CHUNK = 128
B_WIDTH = 512
B_HEADS = 8
B_KV_HEADS = 2
KV_WIDTH = B_KV_HEADS * HEAD_DIM
WINDOW = 128
N_BUCKETS = 32
MAX_DISTANCE = 128
C_WIDTH = 256
CONV_WIDTH = 31
CONV_PAD = CONV_WIDTH // 2
IN_WIDTH = 2 * A_WIDTH + B_WIDTH + 2 * KV_WIDTH + 2 * C_WIDTH
N_EXPERTS = 32
N_GROUPS = 4
EXPERTS_PER_GROUP = N_EXPERTS // N_GROUPS
TOP_K = 2
D_EXPERT = D_MODEL // 2
ALPHA = (2 * DEPTH) ** 0.25
LN_EPS = 1e-5
NEG_INF = -1e30
LOG2E = 1.4426950408889634
Q_SCALE = HEAD_DIM ** -0.5 * LOG2E

LANES = 128
SUBLANES = 8
BF16_SUBLANES = 16
VMEM_LIMIT = 48 * 1024 * 1024

ADA_TN = 1536
TS = 512
TT = 512
CH = BF16_SUBLANES
R_TILE = 1536
BM = 512
NCH = BM // CH
N_XBUF = 3
N_SORT_BLOCKS = 6
HALO = 16

assert R_TILE >= TOP_K * TT + N_EXPERTS * (CH - 1) + CH
assert R_TILE >= 2 * NCH * CH + CH


def _sigmoid(x):
    return 1.0 / (1.0 + jnp.exp(-x))


def _gelu_tanh(x):
    return x * (0.5 * (1.0 + jnp.tanh(0.7978845608028654 * (x + 0.044715 * (x * x * x)))))


def _ln(x, g, b):
    mu = jnp.mean(x, axis=-1, keepdims=True)
    xc = x - mu
    var = jnp.mean(xc * xc, axis=-1, keepdims=True)
    return xc * lax.rsqrt(var + LN_EPS) * g + b


def _dot(a, b):
    return jnp.dot(a, b, preferred_element_type=F32)


def _dot_nt(a, b):
    return lax.dot_general(a, b, (((1,), (1,)), ((), ())), preferred_element_type=F32)


def _ada_kernel(c_ref, w_ref, b_ref, o_ref):
    c = c_ref[...]
    s = (c * _sigmoid(c)).astype(BF16)
    o_ref[0] = _dot(s, w_ref[0].astype(BF16)) + b_ref[0]


def _ada_call(c, ada_w, ada_b):
    nl, d, n = ada_w.shape
    bsz = c.shape[0]
    return pl.pallas_call(
        _ada_kernel,
        out_shape=jax.ShapeDtypeStruct((nl, bsz, n), F32),
        grid=(nl, n // ADA_TN),
        in_specs=[
            pl.BlockSpec((bsz, d), lambda l, j: (0, 0)),
            pl.BlockSpec((1, d, ADA_TN), lambda l, j: (l, 0, j)),
            pl.BlockSpec((1, 1, ADA_TN), lambda l, j: (l, 0, j)),
        ],
        out_specs=pl.BlockSpec((1, bsz, ADA_TN), lambda l, j: (l, 0, j)),
        compiler_params=pltpu.CompilerParams(
            dimension_semantics=("arbitrary", "arbitrary"), vmem_limit_bytes=VMEM_LIMIT),
        name="ada_mod",
    )(c, ada_w, ada_b.reshape(nl, 1, n))


def _inproj_kernel(x_ref, xp_ref, xn_ref, mod_ref, w_ref, b_ref, wqt_ref, bq_ref, wvt_ref, bv_ref,
                   lng_ref, lnb_ref, ws_ref, bsb_ref, cw_ref, cb_ref, clg_ref, clb_ref,
                   ya_ref, qt_ref, k_ref, vt_ref, yc_ref, conv_scr, z_scr):
    i = pl.program_id(1)
    n_i = pl.num_programs(1)
    m = mod_ref[...]

    def modulate(xv):
        return (xv * (1.0 + m[1:2]) + m[0:1]).astype(BF16)

    hb = modulate(x_ref[0])
    col_u, col_v = 0, A_WIDTH
    col_k = 2 * A_WIDTH + B_WIDTH
    col_a = col_k + 2 * KV_WIDTH
    col_g = col_a + C_WIDTH

    def proj(lhs, c0, width):
        return _dot(lhs, w_ref[:, c0:c0 + width]) + b_ref[:, c0:c0 + width]

    hx = jnp.concatenate([modulate(xp_ref[0]), hb, modulate(xn_ref[0])], axis=0)
    yg = proj(hx, col_a, C_WIDTH) * _sigmoid(proj(hx, col_g, C_WIDTH))
    conv_scr[0:HALO, :] = yg[0:HALO] * jnp.where(i > 0, 1.0, 0.0)
    conv_scr[HALO:HALO + TS, :] = yg[HALO:HALO + TS]
    conv_scr[HALO + TS:, :] = yg[HALO + TS:] * jnp.where(i < n_i - 1, 1.0, 0.0)

    u = _gelu_tanh(proj(hb, col_u, A_WIDTH))
    v = _gelu_tanh(proj(hb, col_v, A_WIDTH))
    qt_ref[0] = ((_dot_nt(wqt_ref[...], hb) + bq_ref[...]) * Q_SCALE).astype(BF16)
    k_ref[0] = proj(hb, col_k, KV_WIDTH).astype(BF16)
    vt_ref[0] = (_dot_nt(wvt_ref[...], hb) + bv_ref[...]).astype(BF16)

    vb = _ln(v, lng_ref[...], lnb_ref[...]).astype(BF16)
    head_of_lane = lax.broadcasted_iota(I32, (CHUNK, A_WIDTH), 1) // HEAD_DIM
    for ch in range(TS // CHUNK):
        vc = vb[ch * CHUNK:(ch + 1) * CHUNK]
        acc = bsb_ref[...]
        for hh in range(A_HEADS):
            acc = acc + _dot(ws_ref[hh], jnp.where(head_of_lane == hh, vc, jnp.zeros_like(vc)))
        ya_ref[0, ch * CHUNK:(ch + 1) * CHUNK, :] = (u[ch * CHUNK:(ch + 1) * CHUNK] * acc).astype(BF16)

    first = HALO - CONV_PAD
    acc = jnp.zeros((TS, C_WIDTH), F32) + cb_ref[...]
    for r in range(SUBLANES):
        z = None
        for a in range(-(-(first + CONV_WIDTH) // SUBLANES)):
            w = a * SUBLANES + r - first
            if 0 <= w < CONV_WIDTH:
                term = conv_scr[a * SUBLANES:a * SUBLANES + TS + SUBLANES, :] * cw_ref[w:w + 1, :]
                z = term if z is None else z + term
        if r == 0:
            acc = acc + z[0:TS]
        else:
            z_scr[r] = z
            acc = acc + z_scr[r, r:r + TS, :]
    yc = _ln(acc, clg_ref[...], clb_ref[...])
    yc_ref[0] = (yc * _sigmoid(yc)).astype(BF16)


def _inproj_call(x, mod, layer, w_in, b_in, ln_g, ln_b, ws, bsb, conv_w, conv_b, conv_ln_g, conv_ln_b):
    bsz, seq, d = x.shape
    grid = (bsz, seq // TS)
    hb = TS // HALO
    const2 = lambda b, i: (0, 0)
    row = lambda b, i: (b, i, 0)
    colblk = lambda b, i: (b, 0, i)
    prev_h = lambda b, i: (b, jnp.maximum(i * hb - 1, 0), 0)
    next_h = lambda b, i: (b, jnp.minimum((i + 1) * hb, seq // HALO - 1), 0)
    q0 = 2 * A_WIDTH
    v0 = q0 + B_WIDTH + KV_WIDTH
    wb = w_in.astype(BF16)
    wqt = w_in[:, q0:q0 + B_WIDTH].T.astype(BF16)
    wvt = w_in[:, v0:v0 + KV_WIDTH].T.astype(BF16)
    bq = b_in[q0:q0 + B_WIDTH].reshape(B_WIDTH, 1)
    bv = b_in[v0:v0 + KV_WIDTH].reshape(KV_WIDTH, 1)

    def out(width):
        return jax.ShapeDtypeStruct((bsz, seq, width), BF16), pl.BlockSpec((1, TS, width), row)

    def out_t(width):
        return jax.ShapeDtypeStruct((bsz, width, seq), BF16), pl.BlockSpec((1, width, TS), colblk)

    outs = [out(A_WIDTH), out_t(B_WIDTH), out(KV_WIDTH), out_t(KV_WIDTH), out(C_WIDTH)]
    return pl.pallas_call(
        _inproj_kernel,
        out_shape=[o[0] for o in outs],
        grid=grid,
        in_specs=[
            pl.BlockSpec((1, TS, d), row),
            pl.BlockSpec((1, HALO, d), prev_h),
            pl.BlockSpec((1, HALO, d), next_h),
            pl.BlockSpec((None, None, 6, d), lambda b, i: (layer, b, 0, 0)),
            pl.BlockSpec((d, IN_WIDTH), const2),
            pl.BlockSpec((1, IN_WIDTH), const2),
            pl.BlockSpec((B_WIDTH, d), const2),
            pl.BlockSpec((B_WIDTH, 1), const2),
            pl.BlockSpec((KV_WIDTH, d), const2),
            pl.BlockSpec((KV_WIDTH, 1), const2),
            pl.BlockSpec((1, A_WIDTH), const2),
            pl.BlockSpec((1, A_WIDTH), const2),
            pl.BlockSpec((A_HEADS, CHUNK, CHUNK), lambda b, i: (0, 0, 0)),
            pl.BlockSpec((CHUNK, A_WIDTH), const2),
            pl.BlockSpec((CONV_WIDTH, C_WIDTH), const2),
            pl.BlockSpec((1, C_WIDTH), const2),
            pl.BlockSpec((1, C_WIDTH), const2),
            pl.BlockSpec((1, C_WIDTH), const2),
        ],
        out_specs=[o[1] for o in outs],
        scratch_shapes=[
            pltpu.VMEM((TS + 2 * HALO, C_WIDTH), F32),
            pltpu.VMEM((SUBLANES, TS + SUBLANES, C_WIDTH), F32),
        ],
        compiler_params=pltpu.CompilerParams(
            dimension_semantics=("arbitrary", "arbitrary"), vmem_limit_bytes=VMEM_LIMIT),
        name="inproj_gmlp_conv",
    )(x, x, x, mod, wb, b_in.reshape(1, -1), wqt, bq, wvt, bv, ln_g, ln_b, ws, bsb,
      conv_w, conv_b, conv_ln_g, conv_ln_b)


def _first_argmax(vals, iota_f, width):
    m = jnp.max(vals, axis=0, keepdims=True)
    idx = jnp.min(jnp.where(vals == m, iota_f, float(width)), axis=0, keepdims=True)
    return m, idx


def _mixer_kernel(seq_len, tiles_per_seq, qt_ref, kp_ref, kc_ref, kn_ref, vtp_ref, vtc_ref, vtn_ref,
                  bias_ref, sink_ref, yc_ref, ya_ref, x_ref, mod_ref, wo_ref, bo_ref, lng_ref, lnb_ref,
                  rw_ref, rb_ref, lt_ref, before_ref, x1_ref, xs_ref, pos_ref, gate_ref, cnt_ref,
                  ot_scr, s_scr, h2_scr, pos_scr):
    n = pl.program_id(0)
    last_tile = pl.num_programs(0) - 2
    i = jnp.minimum(n, last_tile) % tiles_per_seq
    t0 = i * TT
    m = mod_ref[...]

    @pl.when(n == 0)
    def _():
        h2_scr[...] = jnp.zeros_like(h2_scr)
        pos_scr[...] = jnp.zeros_like(pos_scr)

    pos_prev = pos_scr[...]
    h2_prev = h2_scr[...]
    sort_rows = R_TILE // N_SORT_BLOCKS

    def sort_block(j):
        iota_r = j * sort_rows + lax.broadcasted_iota(I32, (sort_rows, TT), 0)
        onehot = jnp.where((iota_r == pos_prev[0:1]) | (iota_r == pos_prev[1:2]), 1.0, 0.0).astype(BF16)
        xs_ref[j * sort_rows:(j + 1) * sort_rows, :] = _dot(onehot, h2_prev).astype(BF16)

    kfull = jnp.concatenate([kp_ref[0], kc_ref[0], kn_ref[0]], axis=0)
    vtfull = jnp.concatenate([vtp_ref[0], vtc_ref[0], vtn_ref[0]], axis=1)
    grp = B_HEADS // B_KV_HEADS
    n_qb = TT // WINDOW
    units = [(jb, g) for jb in range(n_qb) for g in range(B_KV_HEADS)]
    key_i = lax.broadcasted_iota(I32, (3 * WINDOW, 1), 0)

    def scores(u):
        jb, g = units[u]
        kb = kfull[jb * WINDOW:(jb + 3) * WINDOW, g * HEAD_DIM:(g + 1) * HEAD_DIM]
        qt = jnp.concatenate(
            [qt_ref[0, h * HEAD_DIM:(h + 1) * HEAD_DIM, jb * WINDOW:(jb + 1) * WINDOW]
             for h in range(g * grp, (g + 1) * grp)], axis=1)
        s = _dot(kb, qt) + bias_ref[g]
        if jb == 0 or jb == n_qb - 1:
            kpos = t0 + (jb - 1) * WINDOW + key_i
            s = jnp.where((kpos >= 0) & (kpos < seq_len), s, NEG_INF)
        s_scr[u % 2] = s

    def values(u):
        jb, g = units[u]
        s = s_scr[u % 2]
        sink = sink_ref[g:g + 1, :]
        mx = jnp.maximum(jnp.max(s, axis=0, keepdims=True), sink)
        p = jnp.exp2(s - mx)
        den = jnp.sum(p, axis=0, keepdims=True) + jnp.exp2(sink - mx)
        vt = vtfull[g * HEAD_DIM:(g + 1) * HEAD_DIM, jb * WINDOW:(jb + 3) * WINDOW]
        ot = _dot(vt, p.astype(BF16)) / den
        for hh in range(grp):
            h = g * grp + hh
            ot_scr[h * HEAD_DIM:(h + 1) * HEAD_DIM, jb * WINDOW:(jb + 1) * WINDOW] = (
                ot[:, hh * WINDOW:(hh + 1) * WINDOW])

    scores(0)
    for u in range(len(units)):
        if u + 1 < len(units):
            scores(u + 1)
        values(u)
    yb = jnp.transpose(ot_scr[...]).astype(BF16)

    y = (_dot(ya_ref[0], wo_ref[0:A_WIDTH, :])
         + _dot(yb, wo_ref[A_WIDTH:A_WIDTH + B_WIDTH, :])
         + _dot(yc_ref[0], wo_ref[A_WIDTH + B_WIDTH:, :]) + bo_ref[...])
    for j in range(N_SORT_BLOCKS):
        sort_block(j)
    x1 = _ln(ALPHA * x_ref[0] + (1.0 + m[2:3]) * y, lng_ref[...], lnb_ref[...])
    x1_ref[0] = x1

    h2 = (x1 * (1.0 + m[4:5]) + m[3:4]).astype(BF16)
    scores = _sigmoid(_dot_nt(rw_ref[...], h2))
    sel = scores + rb_ref[...]
    iota_f = lax.broadcasted_iota(I32, (EXPERTS_PER_GROUP, TT), 0).astype(F32)
    best = None
    for g in range(N_GROUPS):
        sl = slice(g * EXPERTS_PER_GROUP, (g + 1) * EXPERTS_PER_GROUP)
        sg = sel[sl]
        m1, i1 = _first_argmax(sg, iota_f, EXPERTS_PER_GROUP)
        m2, i2 = _first_argmax(jnp.where(iota_f == i1, -jnp.inf, sg), iota_f, EXPERTS_PER_GROUP)
        sc = scores[sl]
        s1 = jnp.sum(jnp.where(iota_f == i1, sc, 0.0), axis=0, keepdims=True)
        s2 = jnp.sum(jnp.where(iota_f == i2, sc, 0.0), axis=0, keepdims=True)
        cand = (m1 + m2, i1 + g * EXPERTS_PER_GROUP, i2 + g * EXPERTS_PER_GROUP, s1, s2)
        if best is None:
            best = cand
        else:
            take = cand[0] > best[0]
            best = tuple(jnp.where(take, c, b) for c, b in zip(cand, best))
    _, e1, e2, s1, s2 = best
    gate_ref[0] = jnp.concatenate([s1, s2], axis=0) / (s1 + s2)

    iota_e = lax.broadcasted_iota(I32, (N_EXPERTS, TT), 0).astype(F32)
    in0 = iota_e == e1
    in1 = iota_e == e2
    member = jnp.where(in0 | in1, 1.0, 0.0)
    cnt = jnp.sum(member, axis=1, keepdims=True)
    cnt_ref[0] = jnp.broadcast_to(cnt, (N_EXPERTS, LANES))
    nch = jnp.floor((cnt + (CH - 1)) * (1.0 / CH))
    nch_pad = jnp.concatenate([jnp.broadcast_to(nch, (N_EXPERTS, LANES)),
                               jnp.zeros((LANES - N_EXPERTS, LANES), F32)], axis=0).astype(BF16)
    padoff = _dot(lt_ref[...], nch_pad)[:, 0:1] * float(CH)
    rank = _dot(member.astype(BF16), before_ref[...])
    posf = padoff + rank
    pos0 = jnp.sum(jnp.where(in0, posf, 0.0), axis=0, keepdims=True).astype(I32)
    pos1 = jnp.sum(jnp.where(in1, posf, 0.0), axis=0, keepdims=True).astype(I32)
    pos = jnp.concatenate([pos0, pos1], axis=0)
    pos_ref[0] = pos
    pos_scr[...] = pos
    h2_scr[...] = h2


def _mixer_call(seq_len, qt, k, vt, bias, sink, yc, ya, x, mod, layer, w_out, b_out, ln_g, ln_b, rw, rbias):
    bsz, seq, d = x.shape
    nt = seq // TT
    n_tiles = bsz * nt
    kb = TT // WINDOW
    grp = B_HEADS // B_KV_HEADS
    const2 = lambda n: (0, 0)

    def at_tile(fn):
        def index_map(n):
            ta = jnp.minimum(n, n_tiles - 1)
            return fn(ta // nt, ta % nt)
        return index_map

    row = at_tile(lambda b, i: (b, i, 0))
    colblk = at_tile(lambda b, i: (b, 0, i))
    prev_k = at_tile(lambda b, i: (b, jnp.maximum(i * kb - 1, 0), 0))
    next_k = at_tile(lambda b, i: (b, jnp.minimum((i + 1) * kb, seq // WINDOW - 1), 0))
    prev_v = at_tile(lambda b, i: (b, 0, jnp.maximum(i * kb - 1, 0)))
    next_v = at_tile(lambda b, i: (b, 0, jnp.minimum((i + 1) * kb, seq // WINDOW - 1)))
    tile = at_tile(lambda b, i: (b * nt + i, 0, 0))
    lt = (jnp.arange(LANES)[None, :] < jnp.arange(N_EXPERTS)[:, None]).astype(BF16)
    before = (jnp.arange(TT)[:, None] < jnp.arange(TT)[None, :]).astype(BF16)
    return pl.pallas_call(
        functools.partial(_mixer_kernel, seq_len, nt),
        out_shape=[
            jax.ShapeDtypeStruct((bsz, seq, d), F32),
            jax.ShapeDtypeStruct(((n_tiles + 1) * R_TILE, d), BF16),
            jax.ShapeDtypeStruct((n_tiles, TOP_K, TT), I32),
            jax.ShapeDtypeStruct((n_tiles, TOP_K, TT), F32),
            jax.ShapeDtypeStruct((n_tiles, N_EXPERTS, LANES), F32),
        ],
        grid=(n_tiles + 1,),
        in_specs=[
            pl.BlockSpec((1, B_WIDTH, TT), colblk),
            pl.BlockSpec((1, WINDOW, KV_WIDTH), prev_k),
            pl.BlockSpec((1, TT, KV_WIDTH), row),
            pl.BlockSpec((1, WINDOW, KV_WIDTH), next_k),
            pl.BlockSpec((1, KV_WIDTH, WINDOW), prev_v),
            pl.BlockSpec((1, KV_WIDTH, TT), colblk),
            pl.BlockSpec((1, KV_WIDTH, WINDOW), next_v),
            pl.BlockSpec((B_KV_HEADS, 3 * WINDOW, grp * WINDOW), lambda n: (0, 0, 0)),
            pl.BlockSpec((B_KV_HEADS, grp * WINDOW), const2),
            pl.BlockSpec((1, TT, C_WIDTH), row),
            pl.BlockSpec((1, TT, A_WIDTH), row),
            pl.BlockSpec((1, TT, d), row),
            pl.BlockSpec((None, None, 6, d), at_tile(lambda b, i: (layer, b, 0, 0))),
            pl.BlockSpec((d, d), const2),
            pl.BlockSpec((1, d), const2),
            pl.BlockSpec((1, d), const2),
            pl.BlockSpec((1, d), const2),
            pl.BlockSpec((N_EXPERTS, d), const2),
            pl.BlockSpec((N_EXPERTS, 1), const2),
            pl.BlockSpec((N_EXPERTS, LANES), const2),
            pl.BlockSpec((TT, TT), const2),
        ],
        out_specs=[
            pl.BlockSpec((1, TT, d), row),
            pl.BlockSpec((R_TILE, d), lambda n: (n, 0)),
            pl.BlockSpec((1, TOP_K, TT), tile),
            pl.BlockSpec((1, TOP_K, TT), tile),
            pl.BlockSpec((1, N_EXPERTS, LANES), tile),
        ],
        scratch_shapes=[
            pltpu.VMEM((B_WIDTH, TT), F32),
            pltpu.VMEM((2, 3 * WINDOW, grp * WINDOW), F32),
            pltpu.VMEM((TT, d), BF16),
            pltpu.VMEM((TOP_K, TT), I32),
        ],
        compiler_params=pltpu.CompilerParams(
            dimension_semantics=("arbitrary",), vmem_limit_bytes=VMEM_LIMIT),
        name="mixer_router_sort",
    )(qt, k, k, k, vt, vt, vt, bias, sink, yc, ya, x, mod, w_out, b_out, ln_g, ln_b, rw, rbias, lt, before)


def _expert_kernel(src_ref, dst_ref, bexp_ref, nused_ref, xs_hbm, wg_ref, wu_ref, wd_ref, ys_hbm,
                   xbuf, ybuf, wgb, wub, wdb, sem_in, sem_out):
    blk = pl.program_id(0)
    n_blk = pl.num_programs(0)
    nused = nused_ref[0]
    slot = blk % 2
    chunks_per_tile = R_TILE // CH
    zero_chunk = chunks_per_tile - 1
    scratch_chunk = 0
    de = wgb.shape[1]
    half = de // 2

    def in_copy(chunk, s, c):
        return pltpu.make_async_copy(
            xs_hbm.at[pl.ds(pl.multiple_of(chunk * CH, CH), CH)],
            xbuf.at[s, pl.ds(c * CH, CH)], sem_in.at[s])

    def out_copy(chunk, s, c):
        return pltpu.make_async_copy(
            ybuf.at[s, pl.ds(c * CH, CH)],
            ys_hbm.at[pl.ds(pl.multiple_of(chunk * CH, CH), CH)], sem_out.at[s])

    xslot = blk % N_XBUF

    @pl.when(blk == 0)
    def _():
        ybuf[...] = jnp.zeros_like(ybuf)
        for b in range(N_XBUF - 1):
            for c in range(NCH):
                in_copy(src_ref[b * NCH + c], b, c).start()
        for c in range(NCH):
            out_copy(scratch_chunk + c, 0, c).start()

    @pl.when(blk < nused)
    def _():
        expert = bexp_ref[blk]
        prev_expert = bexp_ref[jnp.maximum(blk - 1, 0)]

        @pl.when((blk == 0) | (expert != prev_expert))
        def _():
            wgb[...] = wg_ref[...].astype(BF16)
            wub[...] = wu_ref[...].astype(BF16)
            wdb[...] = wd_ref[...].astype(BF16)

        for c in range(NCH):
            in_copy(0, xslot, c).wait()
        ahead = blk + N_XBUF - 1
        live = ahead < nused
        nxt = jnp.minimum(ahead, n_blk - 1) * NCH
        prv = jnp.maximum(blk - 1, 0) * NCH

        def start_gather(lo, hi):
            for c in range(lo, hi):
                in_copy(jnp.where(live, src_ref[nxt + c], zero_chunk), ahead % N_XBUF, c).start(priority=1)

        def start_writeback(lo, hi):
            for c in range(lo, hi):
                out_copy(jnp.where(blk > 0, dst_ref[prv + c], scratch_chunk + NCH + c), 1 - slot, c).start()

        x = xbuf[xslot]
        q = NCH // 2
        hmid = []
        for j in range(2):
            g = _dot(x, wgb[:, j * half:(j + 1) * half])
            start_gather(j * q, (j + 1) * q)
            u = _dot(x, wub[:, j * half:(j + 1) * half])
            start_writeback(j * q, (j + 1) * q)
            hmid.append(((g * _sigmoid(g)) * u).astype(BF16))
        y = _dot(hmid[0], wdb[0:half, :]) + _dot(hmid[1], wdb[half:, :])
        for c in range(NCH):
            out_copy(0, slot, c).wait()
        ybuf[slot] = y.astype(BF16)

    @pl.when(blk == n_blk - 1)
    def _():
        @pl.when(nused >= 1)
        def _():
            last = (nused - 1) * NCH
            for c in range(NCH):
                out_copy(dst_ref[last + c], (nused - 1) % 2, c).start()
            for c in range(NCH):
                out_copy(0, nused % 2, c).wait()

        for c in range(NCH):
            out_copy(0, jnp.maximum(nused - 1, 0) % 2, c).wait()
        for b in range(N_XBUF - 1):
            for c in range(NCH):
                in_copy(0, (nused + b) % N_XBUF, c).wait()


def _expert_call(n_blocks, src, dst, bexp, nused, xs, layer, w_gate, w_up, w_down):
    d = xs.shape[-1]
    de = w_gate.shape[-1]
    grid_spec = pltpu.PrefetchScalarGridSpec(
        num_scalar_prefetch=4,
        grid=(n_blocks,),
        in_specs=[
            pl.BlockSpec(memory_space=pl.ANY),
            pl.BlockSpec((None, None, d, de), lambda b, s, t, e, n: (layer, e[b], 0, 0)),
            pl.BlockSpec((None, None, d, de), lambda b, s, t, e, n: (layer, e[b], 0, 0)),
            pl.BlockSpec((None, None, de, d), lambda b, s, t, e, n: (layer, e[b], 0, 0)),
        ],
        out_specs=pl.BlockSpec(memory_space=pl.ANY),
        scratch_shapes=[
            pltpu.VMEM((N_XBUF, BM, d), BF16),
            pltpu.VMEM((2, BM, d), BF16),
            pltpu.VMEM((d, de), BF16),
            pltpu.VMEM((d, de), BF16),
            pltpu.VMEM((de, d), BF16),
            pltpu.SemaphoreType.DMA((N_XBUF,)),
            pltpu.SemaphoreType.DMA((2,)),
        ],
    )
    return pl.pallas_call(
        _expert_kernel,
        out_shape=jax.ShapeDtypeStruct(xs.shape, xs.dtype),
        grid_spec=grid_spec,
        input_output_aliases={4: 0},
        compiler_params=pltpu.CompilerParams(
            dimension_semantics=("arbitrary",), vmem_limit_bytes=VMEM_LIMIT),
        name="moe_experts",
    )(src, dst, bexp, nused, xs, w_gate, w_up, w_down)


def _combine_kernel(ys_ref, pos_ref, gate_ref, x1_ref, mod_ref, lng_ref, lnb_ref, o_ref):
    pos = pos_ref[0]
    gate = gate_ref[0]
    m = mod_ref[...]
    iota_r = lax.broadcasted_iota(I32, (R_TILE, TT), 0)
    row_gate = (jnp.where(iota_r == pos[0:1], gate[0:1], 0.0)
                + jnp.where(iota_r == pos[1:2], gate[1:2], 0.0))
    gs = jnp.sum(row_gate, axis=1, keepdims=True)
    ysc = (ys_ref[...].astype(F32) * gs).astype(BF16)
    posc = jnp.transpose(pos.astype(F32))
    iota_c = lax.broadcasted_iota(I32, (TT, R_TILE), 1).astype(F32)
    pick = jnp.where((iota_c == posc[:, 0:1]) | (iota_c == posc[:, 1:2]), 1.0, 0.0).astype(BF16)
    y = _dot(pick, ysc)
    o_ref[...] = _ln(ALPHA * x1_ref[...] + (1.0 + m[5:6]) * y, lng_ref[...], lnb_ref[...])


def _combine_call(ys, pos, gate, x1, mod, layer, tiles_per_seq, ln_g, ln_b):
    t, d = x1.shape
    n_tiles = t // TT
    const2 = lambda n: (0, 0)
    return pl.pallas_call(
        _combine_kernel,
        out_shape=jax.ShapeDtypeStruct((t, d), F32),
        grid=(n_tiles,),
        in_specs=[
            pl.BlockSpec((R_TILE, d), lambda n: (n + 1, 0)),
            pl.BlockSpec((1, TOP_K, TT), lambda n: (n, 0, 0)),
            pl.BlockSpec((1, TOP_K, TT), lambda n: (n, 0, 0)),
            pl.BlockSpec((TT, d), lambda n: (n, 0)),
            pl.BlockSpec((None, None, 6, d), lambda n: (layer, n // tiles_per_seq, 0, 0)),
            pl.BlockSpec((1, d), const2),
            pl.BlockSpec((1, d), const2),
        ],
        out_specs=pl.BlockSpec((TT, d), lambda n: (n, 0)),
        compiler_params=pltpu.CompilerParams(
            dimension_semantics=("arbitrary",), vmem_limit_bytes=VMEM_LIMIT),
        name="moe_combine",
    )(ys, pos, gate, x1, mod, ln_g, ln_b)


def _dispatch_plan(cnt, n_blocks):
    n_tiles = cnt.shape[0]
    chunks_per_tile = R_TILE // CH
    nch = (cnt + CH - 1) // CH
    padoff_ch = jnp.cumsum(nch, axis=1) - nch
    tot = jnp.sum(nch, axis=0)
    totpad = (tot + NCH - 1) // NCH * NCH
    eend = jnp.cumsum(totpad)
    ebase = eend - totpad
    tbase = jnp.cumsum(nch, axis=0) - nch
    start = (ebase[None, :] + tbase).T.reshape(-1)
    base = ((jnp.arange(n_tiles, dtype=I32)[:, None] + 1) * chunks_per_tile + padoff_ch).T.reshape(-1)
    vals = jnp.stack([start, base, nch.T.reshape(-1)], axis=1)
    delta = vals - jnp.concatenate([jnp.zeros((1, 3), I32), vals[:-1]], axis=0)
    digits = jnp.concatenate([delta // LANES, delta % LANES], axis=1).astype(BF16)
    slot = jnp.arange(n_blocks * NCH, dtype=I32)
    started = (start[None, :] <= slot[:, None]).astype(BF16)
    got = jnp.dot(started, digits, preferred_element_type=F32).astype(I32)
    seg = got[:, :3] * LANES + got[:, 3:]
    j = slot - seg[:, 0]
    valid = (j < seg[:, 2]) & (slot < eend[-1])
    zero_chunk = chunks_per_tile - 1
    src = jnp.where(valid, seg[:, 1] + j, zero_chunk).astype(I32)
    blk = slot // NCH
    scratch = (blk % 2) * NCH + slot % NCH
    dst = jnp.where(valid, src, scratch).astype(I32)
    first = jnp.arange(n_blocks, dtype=I32) * NCH
    bexp = jnp.minimum(jnp.sum(eend[None, :] <= first[:, None], axis=1), N_EXPERTS - 1).astype(I32)
    nused = (eend[-1] // NCH).astype(I32).reshape(1)
    return src, dst, bexp, nused


def _t5_bucket(rel):
    nb = N_BUCKETS // 2
    max_exact = nb // 2
    ret = jnp.where(rel > 0, nb, 0)
    n = jnp.abs(rel)
    nf = jnp.maximum(n, 1).astype(jnp.float32)
    large = max_exact + (jnp.log(nf / max_exact) / math.log(MAX_DISTANCE / max_exact)
                         * (nb - max_exact)).astype(jnp.int32)
    large = jnp.minimum(large, nb - 1)
    return ret + jnp.where(n < max_exact, n, large)


def _band_bias(rel_bias):
    qi = jnp.arange(WINDOW)
    kj = jnp.arange(3 * WINDOW)
    rel = kj[None, :] - WINDOW - qi[:, None]
    pick = _t5_bucket(rel)[:, :, None, None] == jnp.arange(N_BUCKETS)[None, None, :, None]
    bias = jnp.sum(jnp.where(pick, rel_bias.astype(F32)[None, None], 0.0), axis=2) * LOG2E
    bias = jnp.where((jnp.abs(rel) <= WINDOW)[:, :, None], bias, NEG_INF)
    grp = B_HEADS // B_KV_HEADS
    bias = jnp.transpose(bias, (2, 1, 0)).reshape(B_KV_HEADS, grp, 3 * WINDOW, WINDOW)
    return jnp.transpose(bias, (0, 2, 1, 3)).reshape(B_KV_HEADS, 3 * WINDOW, grp * WINDOW)


def kernel(x, c, ada_w, ada_b, w_in, b_in, gmlp_ln_g, gmlp_ln_b, gmlp_ws, gmlp_bs, attn_sink, conv_w,
           conv_b, conv_ln_g, conv_ln_b, w_out, b_out, ln_mix_g, ln_mix_b, w_gate, w_up, w_down,
           ln_ffn_g, ln_ffn_b, rel_bias, router_w, router_bias):
    bsz, seq, d = x.shape
    n_layers = ada_w.shape[0]
    t = bsz * seq
    nt = seq // TT
    n_tiles = t // TT
    max_chunks = n_tiles * (TOP_K * TT // CH + N_EXPERTS) + N_EXPERTS * (NCH - 1)
    n_blocks = -(-max_chunks // NCH)

    mod = _ada_call(c, ada_w, ada_b).reshape(n_layers, bsz, 6, d)
    bias = _band_bias(rel_bias)
    rw = router_w.T.astype(BF16)
    rbias = router_bias.astype(F32).reshape(N_EXPERTS, 1)
    row = lambda a: a.reshape(1, -1)

    for l in range(n_layers):
        bsb = jnp.repeat(gmlp_bs[l].T, HEAD_DIM, axis=1)
        ya, qt, k, vt, yc = _inproj_call(
            x, mod, l, w_in[l], b_in[l], row(gmlp_ln_g[l]), row(gmlp_ln_b[l]),
            gmlp_ws[l].astype(BF16), bsb, conv_w[l], row(conv_b[l]), row(conv_ln_g[l]), row(conv_ln_b[l]))
        sink = jnp.repeat(attn_sink[l].astype(F32) * LOG2E, WINDOW).reshape(B_KV_HEADS, -1)
        x1, xs, pos, gate, cntb = _mixer_call(
            seq, qt, k, vt, bias, sink, yc, ya, x, mod, l, w_out[l].astype(BF16),
            row(b_out[l]), row(ln_mix_g[l]), row(ln_mix_b[l]), rw, rbias)
        cnt = cntb[:, :, 0].astype(I32)
        src, dst, bexp, nused = _dispatch_plan(cnt, n_blocks)
        ys = _expert_call(n_blocks, src, dst, bexp, nused, xs, l, w_gate, w_up, w_down)
        x = _combine_call(ys, pos, gate, x1.reshape(t, d), mod, l, nt, row(ln_ffn_g[l]),
                          row(ln_ffn_b[l])).reshape(bsz, seq, d)
    return x
```

```python
import functools
import math

import jax
import jax.numpy as jnp
from jax import lax
from jax.experimental import pallas as pl
from jax.experimental.pallas import tpu as pltpu

F32 = jnp.float32
BF16 = jnp.bfloat16
I32 = jnp.int32

D_MODEL = 1024
DEPTH = 2
HEAD_DIM = 64
A_WIDTH = 256
A_HEADS = 4
CHUNK = 128
B_WIDTH = 512
B_HEADS = 8
B_KV_HEADS = 2
KV_WIDTH = B_KV_HEADS * HEAD_DIM
WINDOW = 128
N_BUCKETS = 32
MAX_DISTANCE = 128
C_WIDTH = 256
CONV_WIDTH = 31
CONV_PAD = CONV_WIDTH // 2
IN_WIDTH = 2 * A_WIDTH + B_WIDTH + 2 * KV_WIDTH + 2 * C_WIDTH
N_EXPERTS = 32
N_GROUPS = 4
EXPERTS_PER_GROUP = N_EXPERTS // N_GROUPS
TOP_K = 2
D_EXPERT = D_MODEL // 2
ALPHA = (2 * DEPTH) ** 0.25
LN_EPS = 1e-5
NEG_INF = -1e30
LOG2E = 1.4426950408889634
Q_SCALE = HEAD_DIM ** -0.5 * LOG2E

LANES = 128
SUBLANES = 8
BF16_SUBLANES = 16
VMEM_LIMIT = 48 * 1024 * 1024

ADA_TN = 1536
TS = 512
TT = 512
CH = BF16_SUBLANES
R_TILE = 1536
BM = 512
NCH = BM // CH
N_XBUF = 3
N_SORT_BLOCKS = 6
HALO = 16

assert R_TILE >= TOP_K * TT + N_EXPERTS * (CH - 1) + CH
assert R_TILE >= 2 * NCH * CH + CH


def _sigmoid(x):
    return 1.0 / (1.0 + jnp.exp(-x))


def _gelu_tanh(x):
    return x * (0.5 * (1.0 + jnp.tanh(0.7978845608028654 * (x + 0.044715 * (x * x * x)))))


def _ln(x, g, b):
    mu = jnp.mean(x, axis=-1, keepdims=True)
    xc = x - mu
    var = jnp.mean(xc * xc, axis=-1, keepdims=True)
    return xc * lax.rsqrt(var + LN_EPS) * g + b


def _dot(a, b):
    return jnp.dot(a, b, preferred_element_type=F32)


def _dot_nt(a, b):
    return lax.dot_general(a, b, (((1,), (1,)), ((), ())), preferred_element_type=F32)


def _ada_kernel(c_ref, w_ref, b_ref, o_ref):
    c = c_ref[...]
    s = (c * _sigmoid(c)).astype(BF16)
    o_ref[0] = _dot(s, w_ref[0].astype(BF16)) + b_ref[0]


def _ada_call(c, ada_w, ada_b):
    nl, d, n = ada_w.shape
    bsz = c.shape[0]
    return pl.pallas_call(
        _ada_kernel,
        out_shape=jax.ShapeDtypeStruct((nl, bsz, n), F32),
        grid=(nl, n // ADA_TN),
        in_specs=[
            pl.BlockSpec((bsz, d), lambda l, j: (0, 0)),
            pl.BlockSpec((1, d, ADA_TN), lambda l, j: (l, 0, j)),
            pl.BlockSpec((1, 1, ADA_TN), lambda l, j: (l, 0, j)),
        ],
        out_specs=pl.BlockSpec((1, bsz, ADA_TN), lambda l, j: (l, 0, j)),
        compiler_params=pltpu.CompilerParams(
            dimension_semantics=("arbitrary", "arbitrary"), vmem_limit_bytes=VMEM_LIMIT),
        name="ada_mod",
    )(c, ada_w, ada_b.reshape(nl, 1, n))


def _inproj_kernel(x_ref, xp_ref, xn_ref, mod_ref, w_ref, b_ref, wqt_ref, bq_ref, wvt_ref, bv_ref,
                   lng_ref, lnb_ref, ws_ref, bsb_ref, cw_ref, cb_ref, clg_ref, clb_ref,
                   ya_ref, qt_ref, k_ref, vt_ref, yc_ref, conv_scr, z_scr):
    i = pl.program_id(1)
    n_i = pl.num_programs(1)
    m = mod_ref[...]

    def modulate(xv):
        return (xv * (1.0 + m[1:2]) + m[0:1]).astype(BF16)

    hb = modulate(x_ref[0])
    col_u, col_v = 0, A_WIDTH
    col_k = 2 * A_WIDTH + B_WIDTH
    col_a = col_k + 2 * KV_WIDTH
    col_g = col_a + C_WIDTH

    def proj(lhs, c0, width):
        return _dot(lhs, w_ref[:, c0:c0 + width]) + b_ref[:, c0:c0 + width]

    hx = jnp.concatenate([modulate(xp_ref[0]), hb, modulate(xn_ref[0])], axis=0)
    yg = proj(hx, col_a, C_WIDTH) * _sigmoid(proj(hx, col_g, C_WIDTH))
    conv_scr[0:HALO, :] = yg[0:HALO] * jnp.where(i > 0, 1.0, 0.0)
    conv_scr[HALO:HALO + TS, :] = yg[HALO:HALO + TS]
    conv_scr[HALO + TS:, :] = yg[HALO + TS:] * jnp.where(i < n_i - 1, 1.0, 0.0)

    u = _gelu_tanh(proj(hb, col_u, A_WIDTH))
    v = _gelu_tanh(proj(hb, col_v, A_WIDTH))
    qt_ref[0] = ((_dot_nt(wqt_ref[...], hb) + bq_ref[...]) * Q_SCALE).astype(BF16)
    k_ref[0] = proj(hb, col_k, KV_WIDTH).astype(BF16)
    vt_ref[0] = (_dot_nt(wvt_ref[...], hb) + bv_ref[...]).astype(BF16)

    vb = _ln(v, lng_ref[...], lnb_ref[...]).astype(BF16)
    head_of_lane = lax.broadcasted_iota(I32, (CHUNK, A_WIDTH), 1) // HEAD_DIM
    for ch in range(TS // CHUNK):
        vc = vb[ch * CHUNK:(ch + 1) * CHUNK]
        acc = bsb_ref[...]
        for hh in range(A_HEADS):
            acc = acc + _dot(ws_ref[hh], jnp.where(head_of_lane == hh, vc, jnp.zeros_like(vc)))
        ya_ref[0, ch * CHUNK:(ch + 1) * CHUNK, :] = (u[ch * CHUNK:(ch + 1) * CHUNK] * acc).astype(BF16)

    first = HALO - CONV_PAD
    acc = jnp.zeros((TS, C_WIDTH), F32) + cb_ref[...]
    for r in range(SUBLANES):
        z = None
        for a in range(-(-(first + CONV_WIDTH) // SUBLANES)):
            w = a * SUBLANES + r - first
            if 0 <= w < CONV_WIDTH:
                term = conv_scr[a * SUBLANES:a * SUBLANES + TS + SUBLANES, :] * cw_ref[w:w + 1, :]
                z = term if z is None else z + term
        if r == 0:
            acc = acc + z[0:TS]
        else:
            z_scr[r] = z
            acc = acc + z_scr[r, r:r + TS, :]
    yc = _ln(acc, clg_ref[...], clb_ref[...])
    yc_ref[0] = (yc * _sigmoid(yc)).astype(BF16)


def _inproj_call(x, mod, layer, w_in, b_in, ln_g, ln_b, ws, bsb, conv_w, conv_b, conv_ln_g, conv_ln_b):
    bsz, seq, d = x.shape
    grid = (bsz, seq // TS)
    hb = TS // HALO
    const2 = lambda b, i: (0, 0)
    row = lambda b, i: (b, i, 0)
    colblk = lambda b, i: (b, 0, i)
    prev_h = lambda b, i: (b, jnp.maximum(i * hb - 1, 0), 0)
    next_h = lambda b, i: (b, jnp.minimum((i + 1) * hb, seq // HALO - 1), 0)
    q0 = 2 * A_WIDTH
    v0 = q0 + B_WIDTH + KV_WIDTH
    wb = w_in.astype(BF16)
    wqt = w_in[:, q0:q0 + B_WIDTH].T.astype(BF16)
    wvt = w_in[:, v0:v0 + KV_WIDTH].T.astype(BF16)
    bq = b_in[q0:q0 + B_WIDTH].reshape(B_WIDTH, 1)
    bv = b_in[v0:v0 + KV_WIDTH].reshape(KV_WIDTH, 1)

    def out(width):
        return jax.ShapeDtypeStruct((bsz, seq, width), BF16), pl.BlockSpec((1, TS, width), row)

    def out_t(width):
        return jax.ShapeDtypeStruct((bsz, width, seq), BF16), pl.BlockSpec((1, width, TS), colblk)

    outs = [out(A_WIDTH), out_t(B_WIDTH), out(KV_WIDTH), out_t(KV_WIDTH), out(C_WIDTH)]
    return pl.pallas_call(
        _inproj_kernel,
        out_shape=[o[0] for o in outs],
        grid=grid,
        in_specs=[
            pl.BlockSpec((1, TS, d), row),
            pl.BlockSpec((1, HALO, d), prev_h),
            pl.BlockSpec((1, HALO, d), next_h),
            pl.BlockSpec((None, None, 6, d), lambda b, i: (layer, b, 0, 0)),
            pl.BlockSpec((d, IN_WIDTH), const2),
            pl.BlockSpec((1, IN_WIDTH), const2),
            pl.BlockSpec((B_WIDTH, d), const2),
            pl.BlockSpec((B_WIDTH, 1), const2),
            pl.BlockSpec((KV_WIDTH, d), const2),
            pl.BlockSpec((KV_WIDTH, 1), const2),
            pl.BlockSpec((1, A_WIDTH), const2),
            pl.BlockSpec((1, A_WIDTH), const2),
            pl.BlockSpec((A_HEADS, CHUNK, CHUNK), lambda b, i: (0, 0, 0)),
            pl.BlockSpec((CHUNK, A_WIDTH), const2),
            pl.BlockSpec((CONV_WIDTH, C_WIDTH), const2),
            pl.BlockSpec((1, C_WIDTH), const2),
            pl.BlockSpec((1, C_WIDTH), const2),
            pl.BlockSpec((1, C_WIDTH), const2),
        ],
        out_specs=[o[1] for o in outs],
        scratch_shapes=[
            pltpu.VMEM((TS + 2 * HALO, C_WIDTH), F32),
            pltpu.VMEM((SUBLANES, TS + SUBLANES, C_WIDTH), F32),
        ],
        compiler_params=pltpu.CompilerParams(
            dimension_semantics=("arbitrary", "arbitrary"), vmem_limit_bytes=VMEM_LIMIT),
        name="inproj_gmlp_conv",
    )(x, x, x, mod, wb, b_in.reshape(1, -1), wqt, bq, wvt, bv, ln_g, ln_b, ws, bsb,
      conv_w, conv_b, conv_ln_g, conv_ln_b)


def _first_argmax(vals, iota_f, width):
    m = jnp.max(vals, axis=0, keepdims=True)
    idx = jnp.min(jnp.where(vals == m, iota_f, float(width)), axis=0, keepdims=True)
    return m, idx


def _mixer_kernel(seq_len, tiles_per_seq, qt_ref, kp_ref, kc_ref, kn_ref, vtp_ref, vtc_ref, vtn_ref,
                  bias_ref, sink_ref, yc_ref, ya_ref, x_ref, mod_ref, wo_ref, bo_ref, lng_ref, lnb_ref,
                  rw_ref, rb_ref, lt_ref, before_ref, x1_ref, xs_ref, pos_ref, gate_ref, cnt_ref,
                  ot_scr, s_scr, h2_scr, pos_scr):
    n = pl.program_id(0)
    last_tile = pl.num_programs(0) - 2
    i = jnp.minimum(n, last_tile) % tiles_per_seq
    t0 = i * TT
    m = mod_ref[...]

    @pl.when(n == 0)
    def _():
        h2_scr[...] = jnp.zeros_like(h2_scr)
        pos_scr[...] = jnp.zeros_like(pos_scr)

    pos_prev = pos_scr[...]
    h2_prev = h2_scr[...]
    sort_rows = R_TILE // N_SORT_BLOCKS

    def sort_block(j):
        iota_r = j * sort_rows + lax.broadcasted_iota(I32, (sort_rows, TT), 0)
        onehot = jnp.where((iota_r == pos_prev[0:1]) | (iota_r == pos_prev[1:2]), 1.0, 0.0).astype(BF16)
        xs_ref[j * sort_rows:(j + 1) * sort_rows, :] = _dot(onehot, h2_prev).astype(BF16)

    kfull = jnp.concatenate([kp_ref[0], kc_ref[0], kn_ref[0]], axis=0)
    vtfull = jnp.concatenate([vtp_ref[0], vtc_ref[0], vtn_ref[0]], axis=1)
    grp = B_HEADS // B_KV_HEADS
    n_qb = TT // WINDOW
    units = [(jb, g) for jb in range(n_qb) for g in range(B_KV_HEADS)]
    key_i = lax.broadcasted_iota(I32, (3 * WINDOW, 1), 0)

    def scores(u):
        jb, g = units[u]
        kb = kfull[jb * WINDOW:(jb + 3) * WINDOW, g * HEAD_DIM:(g + 1) * HEAD_DIM]
        qt = jnp.concatenate(
            [qt_ref[0, h * HEAD_DIM:(h + 1) * HEAD_DIM, jb * WINDOW:(jb + 1) * WINDOW]
             for h in range(g * grp, (g + 1) * grp)], axis=1)
        s = _dot(kb, qt) + bias_ref[g]
        if jb == 0 or jb == n_qb - 1:
            kpos = t0 + (jb - 1) * WINDOW + key_i
            s = jnp.where((kpos >= 0) & (kpos < seq_len), s, NEG_INF)
        s_scr[u % 2] = s

    def values(u):
        jb, g = units[u]
        s = s_scr[u % 2]
        sink = sink_ref[g:g + 1, :]
        mx = jnp.maximum(jnp.max(s, axis=0, keepdims=True), sink)
        p = jnp.exp2(s - mx)
        den = jnp.sum(p, axis=0, keepdims=True) + jnp.exp2(sink - mx)
        vt = vtfull[g * HEAD_DIM:(g + 1) * HEAD_DIM, jb * WINDOW:(jb + 3) * WINDOW]
        ot = _dot(vt, p.astype(BF16)) / den
        for hh in range(grp):
            h = g * grp + hh
            ot_scr[h * HEAD_DIM:(h + 1) * HEAD_DIM, jb * WINDOW:(jb + 1) * WINDOW] = (
                ot[:, hh * WINDOW:(hh + 1) * WINDOW])

    scores(0)
    for u in range(len(units)):
        if u + 1 < len(units):
            scores(u + 1)
        values(u)
    yb = jnp.transpose(ot_scr[...]).astype(BF16)

    y = (_dot(ya_ref[0], wo_ref[0:A_WIDTH, :])
         + _dot(yb, wo_ref[A_WIDTH:A_WIDTH + B_WIDTH, :])
         + _dot(yc_ref[0], wo_ref[A_WIDTH + B_WIDTH:, :]) + bo_ref[...])
    for j in range(N_SORT_BLOCKS):
        sort_block(j)
    x1 = _ln(ALPHA * x_ref[0] + (1.0 + m[2:3]) * y, lng_ref[...], lnb_ref[...])
    x1_ref[0] = x1

    h2 = (x1 * (1.0 + m[4:5]) + m[3:4]).astype(BF16)
    scores = _sigmoid(_dot_nt(rw_ref[...], h2))
    sel = scores + rb_ref[...]
    iota_f = lax.broadcasted_iota(I32, (EXPERTS_PER_GROUP, TT), 0).astype(F32)
    best = None
    for g in range(N_GROUPS):
        sl = slice(g * EXPERTS_PER_GROUP, (g + 1) * EXPERTS_PER_GROUP)
        sg = sel[sl]
        m1, i1 = _first_argmax(sg, iota_f, EXPERTS_PER_GROUP)
        m2, i2 = _first_argmax(jnp.where(iota_f == i1, -jnp.inf, sg), iota_f, EXPERTS_PER_GROUP)
        sc = scores[sl]
        s1 = jnp.sum(jnp.where(iota_f == i1, sc, 0.0), axis=0, keepdims=True)
        s2 = jnp.sum(jnp.where(iota_f == i2, sc, 0.0), axis=0, keepdims=True)
        cand = (m1 + m2, i1 + g * EXPERTS_PER_GROUP, i2 + g * EXPERTS_PER_GROUP, s1, s2)
        if best is None:
            best = cand
        else:
            take = cand[0] > best[0]
            best = tuple(jnp.where(take, c, b) for c, b in zip(cand, best))
    _, e1, e2, s1, s2 = best
    gate_ref[0] = jnp.concatenate([s1, s2], axis=0) / (s1 + s2)

    iota_e = lax.broadcasted_iota(I32, (N_EXPERTS, TT), 0).astype(F32)
    in0 = iota_e == e1
    in1 = iota_e == e2
    member = jnp.where(in0 | in1, 1.0, 0.0)
    cnt = jnp.sum(member, axis=1, keepdims=True)
    cnt_ref[0] = jnp.broadcast_to(cnt, (N_EXPERTS, LANES))
    nch = jnp.floor((cnt + (CH - 1)) * (1.0 / CH))
    nch_pad = jnp.concatenate([jnp.broadcast_to(nch, (N_EXPERTS, LANES)),
                               jnp.zeros((LANES - N_EXPERTS, LANES), F32)], axis=0).astype(BF16)
    padoff = _dot(lt_ref[...], nch_pad)[:, 0:1] * float(CH)
    rank = _dot(member.astype(BF16), before_ref[...])
    posf = padoff + rank
    pos0 = jnp.sum(jnp.where(in0, posf, 0.0), axis=0, keepdims=True).astype(I32)
    pos1 = jnp.sum(jnp.where(in1, posf, 0.0), axis=0, keepdims=True).astype(I32)
    pos = jnp.concatenate([pos0, pos1], axis=0)
    pos_ref[0] = pos
    pos_scr[...] = pos
    h2_scr[...] = h2


def _mixer_call(seq_len, qt, k, vt, bias, sink, yc, ya, x, mod, layer, w_out, b_out, ln_g, ln_b, rw, rbias):
    bsz, seq, d = x.shape
    nt = seq // TT
    n_tiles = bsz * nt
    kb = TT // WINDOW
    grp = B_HEADS // B_KV_HEADS
    const2 = lambda n: (0, 0)

    def at_tile(fn):
        def index_map(n):
            ta = jnp.minimum(n, n_tiles - 1)
            return fn(ta // nt, ta % nt)
        return index_map

    row = at_tile(lambda b, i: (b, i, 0))
    colblk = at_tile(lambda b, i: (b, 0, i))
    prev_k = at_tile(lambda b, i: (b, jnp.maximum(i * kb - 1, 0), 0))
    next_k = at_tile(lambda b, i: (b, jnp.minimum((i + 1) * kb, seq // WINDOW - 1), 0))
    prev_v = at_tile(lambda b, i: (b, 0, jnp.maximum(i * kb - 1, 0)))
    next_v = at_tile(lambda b, i: (b, 0, jnp.minimum((i + 1) * kb, seq // WINDOW - 1)))
    tile = at_tile(lambda b, i: (b * nt + i, 0, 0))
    lt = (jnp.arange(LANES)[None, :] < jnp.arange(N_EXPERTS)[:, None]).astype(BF16)
    before = (jnp.arange(TT)[:, None] < jnp.arange(TT)[None, :]).astype(BF16)
    return pl.pallas_call(
        functools.partial(_mixer_kernel, seq_len, nt),
        out_shape=[
            jax.ShapeDtypeStruct((bsz, seq, d), F32),
            jax.ShapeDtypeStruct(((n_tiles + 1) * R_TILE, d), BF16),
            jax.ShapeDtypeStruct((n_tiles, TOP_K, TT), I32),
            jax.ShapeDtypeStruct((n_tiles, TOP_K, TT), F32),
            jax.ShapeDtypeStruct((n_tiles, N_EXPERTS, LANES), F32),
        ],
        grid=(n_tiles + 1,),
        in_specs=[
            pl.BlockSpec((1, B_WIDTH, TT), colblk),
            pl.BlockSpec((1, WINDOW, KV_WIDTH), prev_k),
            pl.BlockSpec((1, TT, KV_WIDTH), row),
            pl.BlockSpec((1, WINDOW, KV_WIDTH), next_k),
            pl.BlockSpec((1, KV_WIDTH, WINDOW), prev_v),
            pl.BlockSpec((1, KV_WIDTH, TT), colblk),
            pl.BlockSpec((1, KV_WIDTH, WINDOW), next_v),
            pl.BlockSpec((B_KV_HEADS, 3 * WINDOW, grp * WINDOW), lambda n: (0, 0, 0)),
            pl.BlockSpec((B_KV_HEADS, grp * WINDOW), const2),
            pl.BlockSpec((1, TT, C_WIDTH), row),
            pl.BlockSpec((1, TT, A_WIDTH), row),
            pl.BlockSpec((1, TT, d), row),
            pl.BlockSpec((None, None, 6, d), at_tile(lambda b, i: (layer, b, 0, 0))),
            pl.BlockSpec((d, d), const2),
            pl.BlockSpec((1, d), const2),
            pl.BlockSpec((1, d), const2),
            pl.BlockSpec((1, d), const2),
            pl.BlockSpec((N_EXPERTS, d), const2),
            pl.BlockSpec((N_EXPERTS, 1), const2),
            pl.BlockSpec((N_EXPERTS, LANES), const2),
            pl.BlockSpec((TT, TT), const2),
        ],
        out_specs=[
            pl.BlockSpec((1, TT, d), row),
            pl.BlockSpec((R_TILE, d), lambda n: (n, 0)),
            pl.BlockSpec((1, TOP_K, TT), tile),
            pl.BlockSpec((1, TOP_K, TT), tile),
            pl.BlockSpec((1, N_EXPERTS, LANES), tile),
        ],
        scratch_shapes=[
            pltpu.VMEM((B_WIDTH, TT), F32),
            pltpu.VMEM((2, 3 * WINDOW, grp * WINDOW), F32),
            pltpu.VMEM((TT, d), BF16),
            pltpu.VMEM((TOP_K, TT), I32),
        ],
        compiler_params=pltpu.CompilerParams(
            dimension_semantics=("arbitrary",), vmem_limit_bytes=VMEM_LIMIT),
        name="mixer_router_sort",
    )(qt, k, k, k, vt, vt, vt, bias, sink, yc, ya, x, mod, w_out, b_out, ln_g, ln_b, rw, rbias, lt, before)


def _expert_kernel(layer, src_ref, dst_ref, bexp_ref, enext_ref, nused_ref, xs_hbm, wg_hbm, wu_hbm, wd_hbm,
                   ys_hbm, xbuf, ybuf, wg_st, wu_st, wd_st, wgb, wub, wdb, sem_in, sem_out, sem_w):
    n_blk = bexp_ref.shape[0]
    nused = nused_ref[0]
    chunks_per_tile = R_TILE // CH
    zero_chunk = chunks_per_tile - 1
    scratch_chunk = 0
    de = wgb.shape[1]
    half = de // 2

    def in_copy(chunk, s, c):
        return pltpu.make_async_copy(
            xs_hbm.at[pl.ds(pl.multiple_of(chunk * CH, CH), CH)],
            xbuf.at[s, pl.ds(c * CH, CH)], sem_in.at[s])

    def out_copy(chunk, s, c):
        return pltpu.make_async_copy(
            ybuf.at[s, pl.ds(c * CH, CH)],
            ys_hbm.at[pl.ds(pl.multiple_of(chunk * CH, CH), CH)], sem_out.at[s])

    def weight_copies(e, s):
        return [pltpu.make_async_copy(wg_hbm.at[layer, e], wg_st.at[s], sem_w.at[s]),
                pltpu.make_async_copy(wu_hbm.at[layer, e], wu_st.at[s], sem_w.at[s]),
                pltpu.make_async_copy(wd_hbm.at[layer, e], wd_st.at[s], sem_w.at[s])]

    ybuf[...] = jnp.zeros_like(ybuf)
    for b in range(N_XBUF - 1):
        for c in range(NCH):
            in_copy(src_ref[b * NCH + c], b, c).start()
    for c in range(NCH):
        out_copy(scratch_chunk + c, 0, c).start()

    @pl.when(nused > 0)
    def _():
        for cp in weight_copies(bexp_ref[0], 0):
            cp.start()

    def block(blk, n_changes):
        slot = blk % 2
        xslot = blk % N_XBUF
        expert = bexp_ref[blk]
        prev_expert = bexp_ref[jnp.maximum(blk - 1, 0)]
        change = (blk == 0) | (expert != prev_expert)

        @pl.when(change)
        def _():
            ws = n_changes % 2
            for cp in weight_copies(expert, ws):
                cp.wait()
            wgb[...] = wg_st[ws].astype(BF16)
            wub[...] = wu_st[ws].astype(BF16)
            wdb[...] = wd_st[ws].astype(BF16)
            upcoming = enext_ref[blk]

            @pl.when(upcoming != expert)
            def _():
                for cp in weight_copies(upcoming, 1 - ws):
                    cp.start()

        for c in range(NCH):
            in_copy(0, xslot, c).wait()
        ahead = blk + N_XBUF - 1
        live = ahead < nused
        nxt = jnp.minimum(ahead, n_blk - 1) * NCH
        prv = jnp.maximum(blk - 1, 0) * NCH

        def start_gather(lo, hi):
            for c in range(lo, hi):
                in_copy(jnp.where(live, src_ref[nxt + c], zero_chunk), ahead % N_XBUF, c).start(priority=1)

        def start_writeback(lo, hi):
            for c in range(lo, hi):
                out_copy(jnp.where(blk > 0, dst_ref[prv + c], scratch_chunk + NCH + c), 1 - slot, c).start()

        x = xbuf[xslot]
        q = NCH // 2
        hmid = []
        for j in range(2):
            g = _dot(x, wgb[:, j * half:(j + 1) * half])
            start_gather(j * q, (j + 1) * q)
            u = _dot(x, wub[:, j * half:(j + 1) * half])
            start_writeback(j * q, (j + 1) * q)
            hmid.append(((g * _sigmoid(g)) * u).astype(BF16))
        y = _dot(hmid[0], wdb[0:half, :]) + _dot(hmid[1], wdb[half:, :])
        for c in range(NCH):
            out_copy(0, slot, c).wait()
        ybuf[slot] = y.astype(BF16)
        return n_changes + change.astype(I32)

    lax.fori_loop(0, nused, block, jnp.int32(0))

    @pl.when(nused >= 1)
    def _():
        last = (nused - 1) * NCH
        for c in range(NCH):
            out_copy(dst_ref[last + c], (nused - 1) % 2, c).start()
        for c in range(NCH):
            out_copy(0, nused % 2, c).wait()

    for c in range(NCH):
        out_copy(0, jnp.maximum(nused - 1, 0) % 2, c).wait()
    for b in range(N_XBUF - 1):
        for c in range(NCH):
            in_copy(0, (nused + b) % N_XBUF, c).wait()


def _expert_call(src, dst, bexp, enext, nused, xs, layer, w_gate, w_up, w_down):
    d = xs.shape[-1]
    de = w_gate.shape[-1]
    grid_spec = pltpu.PrefetchScalarGridSpec(
        num_scalar_prefetch=5,
        grid=(1,),
        in_specs=[pl.BlockSpec(memory_space=pl.ANY)] * 4,
        out_specs=pl.BlockSpec(memory_space=pl.ANY),
        scratch_shapes=[
            pltpu.VMEM((N_XBUF, BM, d), BF16),
            pltpu.VMEM((2, BM, d), BF16),
            pltpu.VMEM((2, d, de), F32),
            pltpu.VMEM((2, d, de), F32),
            pltpu.VMEM((2, de, d), F32),
            pltpu.VMEM((d, de), BF16),
            pltpu.VMEM((d, de), BF16),
            pltpu.VMEM((de, d), BF16),
            pltpu.SemaphoreType.DMA((N_XBUF,)),
            pltpu.SemaphoreType.DMA((2,)),
            pltpu.SemaphoreType.DMA((2,)),
        ],
    )
    return pl.pallas_call(
        functools.partial(_expert_kernel, layer),
        out_shape=jax.ShapeDtypeStruct(xs.shape, xs.dtype),
        grid_spec=grid_spec,
        input_output_aliases={5: 0},
        compiler_params=pltpu.CompilerParams(
            dimension_semantics=("arbitrary",), vmem_limit_bytes=VMEM_LIMIT),
        name="moe_experts",
    )(src, dst, bexp, enext, nused, xs, w_gate, w_up, w_down)


def _combine_kernel(ys_ref, pos_ref, gate_ref, x1_ref, mod_ref, lng_ref, lnb_ref, o_ref):
    pos = pos_ref[0]
    gate = gate_ref[0]
    m = mod_ref[...]
    iota_r = lax.broadcasted_iota(I32, (R_TILE, TT), 0)
    row_gate = (jnp.where(iota_r == pos[0:1], gate[0:1], 0.0)
                + jnp.where(iota_r == pos[1:2], gate[1:2], 0.0))
    gs = jnp.sum(row_gate, axis=1, keepdims=True)
    ysc = (ys_ref[...].astype(F32) * gs).astype(BF16)
    posc = jnp.transpose(pos.astype(F32))
    iota_c = lax.broadcasted_iota(I32, (TT, R_TILE), 1).astype(F32)
    pick = jnp.where((iota_c == posc[:, 0:1]) | (iota_c == posc[:, 1:2]), 1.0, 0.0).astype(BF16)
    y = _dot(pick, ysc)
    o_ref[...] = _ln(ALPHA * x1_ref[...] + (1.0 + m[5:6]) * y, lng_ref[...], lnb_ref[...])


def _combine_call(ys, pos, gate, x1, mod, layer, tiles_per_seq, ln_g, ln_b):
    t, d = x1.shape
    n_tiles = t // TT
    const2 = lambda n: (0, 0)
    return pl.pallas_call(
        _combine_kernel,
        out_shape=jax.ShapeDtypeStruct((t, d), F32),
        grid=(n_tiles,),
        in_specs=[
            pl.BlockSpec((R_TILE, d), lambda n: (n + 1, 0)),
            pl.BlockSpec((1, TOP_K, TT), lambda n: (n, 0, 0)),
            pl.BlockSpec((1, TOP_K, TT), lambda n: (n, 0, 0)),
            pl.BlockSpec((TT, d), lambda n: (n, 0)),
            pl.BlockSpec((None, None, 6, d), lambda n: (layer, n // tiles_per_seq, 0, 0)),
            pl.BlockSpec((1, d), const2),
            pl.BlockSpec((1, d), const2),
        ],
        out_specs=pl.BlockSpec((TT, d), lambda n: (n, 0)),
        compiler_params=pltpu.CompilerParams(
            dimension_semantics=("arbitrary",), vmem_limit_bytes=VMEM_LIMIT),
        name="moe_combine",
    )(ys, pos, gate, x1, mod, ln_g, ln_b)


def _dispatch_plan(cnt, n_blocks):
    n_tiles = cnt.shape[0]
    chunks_per_tile = R_TILE // CH
    nch = (cnt + CH - 1) // CH
    padoff_ch = jnp.cumsum(nch, axis=1) - nch
    tot = jnp.sum(nch, axis=0)
    totpad = (tot + NCH - 1) // NCH * NCH
    eend = jnp.cumsum(totpad)
    ebase = eend - totpad
    tbase = jnp.cumsum(nch, axis=0) - nch
    start = (ebase[None, :] + tbase).T.reshape(-1)
    base = ((jnp.arange(n_tiles, dtype=I32)[:, None] + 1) * chunks_per_tile + padoff_ch).T.reshape(-1)
    vals = jnp.stack([start, base, nch.T.reshape(-1)], axis=1)
    delta = vals - jnp.concatenate([jnp.zeros((1, 3), I32), vals[:-1]], axis=0)
    digits = jnp.concatenate([delta // LANES, delta % LANES], axis=1).astype(BF16)
    slot = jnp.arange(n_blocks * NCH, dtype=I32)
    started = (start[None, :] <= slot[:, None]).astype(BF16)
    got = jnp.dot(started, digits, preferred_element_type=F32).astype(I32)
    seg = got[:, :3] * LANES + got[:, 3:]
    j = slot - seg[:, 0]
    valid = (j < seg[:, 2]) & (slot < eend[-1])
    zero_chunk = chunks_per_tile - 1
    src = jnp.where(valid, seg[:, 1] + j, zero_chunk).astype(I32)
    blk = slot // NCH
    scratch = (blk % 2) * NCH + slot % NCH
    dst = jnp.where(valid, src, scratch).astype(I32)
    first = jnp.arange(n_blocks, dtype=I32) * NCH
    bexp = jnp.minimum(jnp.sum(eend[None, :] <= first[:, None], axis=1), N_EXPERTS - 1).astype(I32)
    nused = (eend[-1] // NCH).astype(I32).reshape(1)
    ids = jnp.arange(N_EXPERTS, dtype=I32)
    later = jnp.where((ids[None, :] > ids[:, None]) & (totpad > 0)[None, :], ids[None, :], N_EXPERTS)
    next_of = jnp.min(later, axis=1)
    next_of = jnp.where(next_of == N_EXPERTS, ids, next_of)
    enext = jnp.sum(jnp.where(bexp[:, None] == ids[None, :], next_of[None, :], 0), axis=1).astype(I32)
    return src, dst, bexp, enext, nused


def _t5_bucket(rel):
    nb = N_BUCKETS // 2
    max_exact = nb // 2
    ret = jnp.where(rel > 0, nb, 0)
    n = jnp.abs(rel)
    nf = jnp.maximum(n, 1).astype(jnp.float32)
    large = max_exact + (jnp.log(nf / max_exact) / math.log(MAX_DISTANCE / max_exact)
                         * (nb - max_exact)).astype(jnp.int32)
    large = jnp.minimum(large, nb - 1)
    return ret + jnp.where(n < max_exact, n, large)


def _band_bias(rel_bias):
    qi = jnp.arange(WINDOW)
    kj = jnp.arange(3 * WINDOW)
    rel = kj[None, :] - WINDOW - qi[:, None]
    pick = _t5_bucket(rel)[:, :, None, None] == jnp.arange(N_BUCKETS)[None, None, :, None]
    bias = jnp.sum(jnp.where(pick, rel_bias.astype(F32)[None, None], 0.0), axis=2) * LOG2E
    bias = jnp.where((jnp.abs(rel) <= WINDOW)[:, :, None], bias, NEG_INF)
    grp = B_HEADS // B_KV_HEADS
    bias = jnp.transpose(bias, (2, 1, 0)).reshape(B_KV_HEADS, grp, 3 * WINDOW, WINDOW)
    return jnp.transpose(bias, (0, 2, 1, 3)).reshape(B_KV_HEADS, 3 * WINDOW, grp * WINDOW)


def kernel(x, c, ada_w, ada_b, w_in, b_in, gmlp_ln_g, gmlp_ln_b, gmlp_ws, gmlp_bs, attn_sink, conv_w,
           conv_b, conv_ln_g, conv_ln_b, w_out, b_out, ln_mix_g, ln_mix_b, w_gate, w_up, w_down,
           ln_ffn_g, ln_ffn_b, rel_bias, router_w, router_bias):
    bsz, seq, d = x.shape
    n_layers = ada_w.shape[0]
    t = bsz * seq
    nt = seq // TT
    n_tiles = t // TT
    max_chunks = n_tiles * (TOP_K * TT // CH + N_EXPERTS) + N_EXPERTS * (NCH - 1)
    n_blocks = -(-max_chunks // NCH)

    mod = _ada_call(c, ada_w, ada_b).reshape(n_layers, bsz, 6, d)
    bias = _band_bias(rel_bias)
    rw = router_w.T.astype(BF16)
    rbias = router_bias.astype(F32).reshape(N_EXPERTS, 1)
    row = lambda a: a.reshape(1, -1)

    for l in range(n_layers):
        bsb = jnp.repeat(gmlp_bs[l].T, HEAD_DIM, axis=1)
        ya, qt, k, vt, yc = _inproj_call(
            x, mod, l, w_in[l], b_in[l], row(gmlp_ln_g[l]), row(gmlp_ln_b[l]),
            gmlp_ws[l].astype(BF16), bsb, conv_w[l], row(conv_b[l]), row(conv_ln_g[l]), row(conv_ln_b[l]))
        sink = jnp.repeat(attn_sink[l].astype(F32) * LOG2E, WINDOW).reshape(B_KV_HEADS, -1)
        x1, xs, pos, gate, cntb = _mixer_call(
            seq, qt, k, vt, bias, sink, yc, ya, x, mod, l, w_out[l].astype(BF16),
            row(b_out[l]), row(ln_mix_g[l]), row(ln_mix_b[l]), rw, rbias)
        cnt = cntb[:, :, 0].astype(I32)
        src, dst, bexp, enext, nused = _dispatch_plan(cnt, n_blocks)
        ys = _expert_call(src, dst, bexp, enext, nused, xs, l, w_gate, w_up, w_down)
        x = _combine_call(ys, pos, gate, x1.reshape(t, d), mod, l, nt, row(ln_ffn_g[l]),
                          row(ln_ffn_b[l])).reshape(bsz, seq, d)
    return x
```

```python
import functools
import math

import jax
import jax.numpy as jnp
from jax import lax
from jax.experimental import pallas as pl
from jax.experimental.pallas import tpu as pltpu

F32 = jnp.float32
BF16 = jnp.bfloat16
I32 = jnp.int32

D_MODEL = 1024
DEPTH = 2
HEAD_DIM = 64
A_WIDTH = 256
A_HEADS = 4
CHUNK = 128
B_WIDTH = 512
B_HEADS = 8
B_KV_HEADS = 2
KV_WIDTH = B_KV_HEADS * HEAD_DIM
WINDOW = 128
N_BUCKETS = 32
MAX_DISTANCE = 128
C_WIDTH = 256
CONV_WIDTH = 31
CONV_PAD = CONV_WIDTH // 2
IN_WIDTH = 2 * A_WIDTH + B_WIDTH + 2 * KV_WIDTH + 2 * C_WIDTH
N_EXPERTS = 32
N_GROUPS = 4
EXPERTS_PER_GROUP = N_EXPERTS // N_GROUPS
TOP_K = 2
D_EXPERT = D_MODEL // 2
ALPHA = (2 * DEPTH) ** 0.25
LN_EPS = 1e-5
NEG_INF = -1e30
LOG2E = 1.4426950408889634
Q_SCALE = HEAD_DIM ** -0.5 * LOG2E

LANES = 128
SUBLANES = 8
BF16_SUBLANES = 16
VMEM_LIMIT = 48 * 1024 * 1024

ADA_TN = 1536
TS = 1024
TT = 512
CH = BF16_SUBLANES
R_TILE = 1536
BM = 512
NCH = BM // CH
N_XBUF = 3
N_SORT_BLOCKS = 6
N_SBUF = 2
HALO = 16

assert R_TILE >= TOP_K * TT + N_EXPERTS * (CH - 1) + CH
assert R_TILE >= 2 * NCH * CH + CH


def _sigmoid(x):
    return 1.0 / (1.0 + jnp.exp(-x))


def _gelu_tanh(x):
    return x * (0.5 * (1.0 + jnp.tanh(0.7978845608028654 * (x + 0.044715 * (x * x * x)))))


def _ln(x, g, b):
    mu = jnp.mean(x, axis=-1, keepdims=True)
    xc = x - mu
    var = jnp.mean(xc * xc, axis=-1, keepdims=True)
    return xc * lax.rsqrt(var + LN_EPS) * g + b


def _dot(a, b):
    return jnp.dot(a, b, preferred_element_type=F32)


def _dot_nt(a, b):
    return lax.dot_general(a, b, (((1,), (1,)), ((), ())), preferred_element_type=F32)


def _ada_kernel(c_ref, w_ref, b_ref, o_ref):
    c = c_ref[...]
    s = (c * _sigmoid(c)).astype(BF16)
    o_ref[0] = _dot(s, w_ref[0].astype(BF16)) + b_ref[0]


def _ada_call(c, ada_w, ada_b):
    nl, d, n = ada_w.shape
    bsz = c.shape[0]
    return pl.pallas_call(
        _ada_kernel,
        out_shape=jax.ShapeDtypeStruct((nl, bsz, n), F32),
        grid=(nl, n // ADA_TN),
        in_specs=[
            pl.BlockSpec((bsz, d), lambda l, j: (0, 0)),
            pl.BlockSpec((1, d, ADA_TN), lambda l, j: (l, 0, j)),
            pl.BlockSpec((1, 1, ADA_TN), lambda l, j: (l, 0, j)),
        ],
        out_specs=pl.BlockSpec((1, bsz, ADA_TN), lambda l, j: (l, 0, j)),
        compiler_params=pltpu.CompilerParams(
            dimension_semantics=("arbitrary", "arbitrary"), vmem_limit_bytes=VMEM_LIMIT),
        name="ada_mod",
    )(c, ada_w, ada_b.reshape(nl, 1, n))


def _inproj_kernel(x_ref, xp_ref, xn_ref, mod_ref, w_ref, b_ref, wqt_ref, bq_ref, wvt_ref, bv_ref,
                   lng_ref, lnb_ref, ws_ref, bsb_ref, cw_ref, cb_ref, clg_ref, clb_ref,
                   ya_ref, qt_ref, k_ref, vt_ref, yc_ref, conv_scr, z_scr):
    i = pl.program_id(1)
    n_i = pl.num_programs(1)
    m = mod_ref[...]

    def modulate(xv):
        return (xv * (1.0 + m[1:2]) + m[0:1]).astype(BF16)

    hb = modulate(x_ref[0])
    col_u, col_v = 0, A_WIDTH
    col_k = 2 * A_WIDTH + B_WIDTH
    col_a = col_k + 2 * KV_WIDTH
    col_g = col_a + C_WIDTH

    def proj(lhs, c0, width):
        return _dot(lhs, w_ref[:, c0:c0 + width]) + b_ref[:, c0:c0 + width]

    hx = jnp.concatenate([modulate(xp_ref[0]), hb, modulate(xn_ref[0])], axis=0)
    yg = proj(hx, col_a, C_WIDTH) * _sigmoid(proj(hx, col_g, C_WIDTH))
    conv_scr[0:HALO, :] = yg[0:HALO] * jnp.where(i > 0, 1.0, 0.0)
    conv_scr[HALO:HALO + TS, :] = yg[HALO:HALO + TS]
    conv_scr[HALO + TS:, :] = yg[HALO + TS:] * jnp.where(i < n_i - 1, 1.0, 0.0)
    first = HALO - CONV_PAD
    acc = jnp.zeros((TS, C_WIDTH), F32) + cb_ref[...]
    for r in range(SUBLANES):
        z = None
        for a in range(-(-(first + CONV_WIDTH) // SUBLANES)):
            w = a * SUBLANES + r - first
            if 0 <= w < CONV_WIDTH:
                term = conv_scr[a * SUBLANES:a * SUBLANES + TS + SUBLANES, :] * cw_ref[w:w + 1, :]
                z = term if z is None else z + term
        if r == 0:
            acc = acc + z[0:TS]
        else:
            z_scr[r] = z
            acc = acc + z_scr[r, r:r + TS, :]
    yc = _ln(acc, clg_ref[...], clb_ref[...])
    yc_ref[0] = (yc * _sigmoid(yc)).astype(BF16)

    u = _gelu_tanh(proj(hb, col_u, A_WIDTH))
    v = _gelu_tanh(proj(hb, col_v, A_WIDTH))
    qt_ref[0] = ((_dot_nt(wqt_ref[...], hb) + bq_ref[...]) * Q_SCALE).astype(BF16)
    k_ref[0] = proj(hb, col_k, KV_WIDTH).astype(BF16)
    vt_ref[0] = (_dot_nt(wvt_ref[...], hb) + bv_ref[...]).astype(BF16)

    vb = _ln(v, lng_ref[...], lnb_ref[...]).astype(BF16)
    head_of_lane = lax.broadcasted_iota(I32, (CHUNK, A_WIDTH), 1) // HEAD_DIM
    for ch in range(TS // CHUNK):
        vc = vb[ch * CHUNK:(ch + 1) * CHUNK]
        acc = bsb_ref[...]
        for hh in range(A_HEADS):
            acc = acc + _dot(ws_ref[hh], jnp.where(head_of_lane == hh, vc, jnp.zeros_like(vc)))
        ya_ref[0, ch * CHUNK:(ch + 1) * CHUNK, :] = (u[ch * CHUNK:(ch + 1) * CHUNK] * acc).astype(BF16)


def _inproj_call(x, mod, layer, w_in, b_in, ln_g, ln_b, ws, bsb, conv_w, conv_b, conv_ln_g, conv_ln_b):
    bsz, seq, d = x.shape
    grid = (bsz, seq // TS)
    hb = TS // HALO
    const2 = lambda b, i: (0, 0)
    row = lambda b, i: (b, i, 0)
    colblk = lambda b, i: (b, 0, i)
    prev_h = lambda b, i: (b, jnp.maximum(i * hb - 1, 0), 0)
    next_h = lambda b, i: (b, jnp.minimum((i + 1) * hb, seq // HALO - 1), 0)
    q0 = 2 * A_WIDTH
    v0 = q0 + B_WIDTH + KV_WIDTH
    wb = w_in.astype(BF16)
    wqt = w_in[:, q0:q0 + B_WIDTH].T.astype(BF16)
    wvt = w_in[:, v0:v0 + KV_WIDTH].T.astype(BF16)
    bq = b_in[q0:q0 + B_WIDTH].reshape(B_WIDTH, 1)
    bv = b_in[v0:v0 + KV_WIDTH].reshape(KV_WIDTH, 1)

    def out(width):
        return jax.ShapeDtypeStruct((bsz, seq, width), BF16), pl.BlockSpec((1, TS, width), row)

    def out_t(width):
        return jax.ShapeDtypeStruct((bsz, width, seq), BF16), pl.BlockSpec((1, width, TS), colblk)

    outs = [out(A_WIDTH), out_t(B_WIDTH), out(KV_WIDTH), out_t(KV_WIDTH), out(C_WIDTH)]
    return pl.pallas_call(
        _inproj_kernel,
        out_shape=[o[0] for o in outs],
        grid=grid,
        in_specs=[
            pl.BlockSpec((1, TS, d), row),
            pl.BlockSpec((1, HALO, d), prev_h),
            pl.BlockSpec((1, HALO, d), next_h),
            pl.BlockSpec((None, None, 6, d), lambda b, i: (layer, b, 0, 0)),
            pl.BlockSpec((d, IN_WIDTH), const2),
            pl.BlockSpec((1, IN_WIDTH), const2),
            pl.BlockSpec((B_WIDTH, d), const2),
            pl.BlockSpec((B_WIDTH, 1), const2),
            pl.BlockSpec((KV_WIDTH, d), const2),
            pl.BlockSpec((KV_WIDTH, 1), const2),
            pl.BlockSpec((1, A_WIDTH), const2),
            pl.BlockSpec((1, A_WIDTH), const2),
            pl.BlockSpec((A_HEADS, CHUNK, CHUNK), lambda b, i: (0, 0, 0)),
            pl.BlockSpec((CHUNK, A_WIDTH), const2),
            pl.BlockSpec((CONV_WIDTH, C_WIDTH), const2),
            pl.BlockSpec((1, C_WIDTH), const2),
            pl.BlockSpec((1, C_WIDTH), const2),
            pl.BlockSpec((1, C_WIDTH), const2),
        ],
        out_specs=[o[1] for o in outs],
        scratch_shapes=[
            pltpu.VMEM((TS + 2 * HALO, C_WIDTH), F32),
            pltpu.VMEM((SUBLANES, TS + SUBLANES, C_WIDTH), F32),
        ],
        compiler_params=pltpu.CompilerParams(
            dimension_semantics=("arbitrary", "arbitrary"), vmem_limit_bytes=VMEM_LIMIT),
        name="inproj_gmlp_conv",
    )(x, x, x, mod, wb, b_in.reshape(1, -1), wqt, bq, wvt, bv, ln_g, ln_b, ws, bsb,
      conv_w, conv_b, conv_ln_g, conv_ln_b)


def _first_argmax(vals, iota_f, width):
    m = jnp.max(vals, axis=0, keepdims=True)
    idx = jnp.min(jnp.where(vals == m, iota_f, float(width)), axis=0, keepdims=True)
    return m, idx


def _mixer_kernel(seq_len, tiles_per_seq, qt_ref, kp_ref, kc_ref, kn_ref, vtp_ref, vtc_ref, vtn_ref,
                  bias_ref, sink_ref, yc_ref, ya_ref, x_ref, mod_ref, wo_ref, bo_ref, lng_ref, lnb_ref,
                  rw_ref, rb_ref, lt_ref, before_ref, x1_ref, xs_ref, pos_ref, gate_ref, cnt_ref,
                  ot_scr, s_scr, h2_scr, pos_scr):
    n = pl.program_id(0)
    last_tile = pl.num_programs(0) - 2
    i = jnp.minimum(n, last_tile) % tiles_per_seq
    t0 = i * TT
    m = mod_ref[...]

    @pl.when(n == 0)
    def _():
        h2_scr[...] = jnp.zeros_like(h2_scr)
        pos_scr[...] = jnp.zeros_like(pos_scr)

    pos_prev = pos_scr[...]
    sort_rows = R_TILE // N_SORT_BLOCKS

    def sort_block(j):
        iota_r = j * sort_rows + lax.broadcasted_iota(I32, (sort_rows, TT), 0)
        onehot = jnp.where((iota_r == pos_prev[0:1]) | (iota_r == pos_prev[1:2]), 1.0, 0.0).astype(BF16)
        xs_ref[j * sort_rows:(j + 1) * sort_rows, :] = _dot(onehot, h2_scr[...]).astype(BF16)

    def out_proj(yb):
        return (_dot(ya_ref[0], wo_ref[0:A_WIDTH, :])
                + _dot(yb, wo_ref[A_WIDTH:A_WIDTH + B_WIDTH, :])
                + _dot(yc_ref[0], wo_ref[A_WIDTH + B_WIDTH:, :]) + bo_ref[...])

    def route(y):
        x1 = _ln(ALPHA * x_ref[0] + (1.0 + m[2:3]) * y, lng_ref[...], lnb_ref[...])
        x1_ref[0] = x1
        h2 = (x1 * (1.0 + m[4:5]) + m[3:4]).astype(BF16)
        scores = _sigmoid(_dot_nt(rw_ref[...], h2))
        sel = scores + rb_ref[...]
        iota_f = lax.broadcasted_iota(I32, (EXPERTS_PER_GROUP, TT), 0).astype(F32)
        best = None
        for g in range(N_GROUPS):
            sl = slice(g * EXPERTS_PER_GROUP, (g + 1) * EXPERTS_PER_GROUP)
            sg = sel[sl]
            m1, i1 = _first_argmax(sg, iota_f, EXPERTS_PER_GROUP)
            m2, i2 = _first_argmax(jnp.where(iota_f == i1, -jnp.inf, sg), iota_f, EXPERTS_PER_GROUP)
            sc = scores[sl]
            s1 = jnp.sum(jnp.where(iota_f == i1, sc, 0.0), axis=0, keepdims=True)
            s2 = jnp.sum(jnp.where(iota_f == i2, sc, 0.0), axis=0, keepdims=True)
            cand = (m1 + m2, i1 + g * EXPERTS_PER_GROUP, i2 + g * EXPERTS_PER_GROUP, s1, s2)
            if best is None:
                best = cand
            else:
                take = cand[0] > best[0]
                best = tuple(jnp.where(take, c, b) for c, b in zip(cand, best))
        _, e1, e2, s1, s2 = best
        gate_ref[0] = jnp.concatenate([s1, s2], axis=0) / (s1 + s2)

        iota_e = lax.broadcasted_iota(I32, (N_EXPERTS, TT), 0).astype(F32)
        in0 = iota_e == e1
        in1 = iota_e == e2
        member = jnp.where(in0 | in1, 1.0, 0.0)
        cnt = jnp.sum(member, axis=1, keepdims=True)
        cnt_ref[0] = jnp.broadcast_to(cnt, (N_EXPERTS, LANES))
        nch = jnp.floor((cnt + (CH - 1)) * (1.0 / CH))
        nch_pad = jnp.concatenate([jnp.broadcast_to(nch, (N_EXPERTS, LANES)),
                                   jnp.zeros((LANES - N_EXPERTS, LANES), F32)], axis=0).astype(BF16)
        padoff = _dot(lt_ref[...], nch_pad)[:, 0:1] * float(CH)
        rank = _dot(member.astype(BF16), before_ref[...])
        posf = padoff + rank
        pos0 = jnp.sum(jnp.where(in0, posf, 0.0), axis=0, keepdims=True).astype(I32)
        pos1 = jnp.sum(jnp.where(in1, posf, 0.0), axis=0, keepdims=True).astype(I32)
        pos = jnp.concatenate([pos0, pos1], axis=0)
        pos_ref[0] = pos
        return pos, h2

    kfull = jnp.concatenate([kp_ref[0], kc_ref[0], kn_ref[0]], axis=0)
    vtfull = jnp.concatenate([vtp_ref[0], vtc_ref[0], vtn_ref[0]], axis=1)
    grp = B_HEADS // B_KV_HEADS
    n_qb = TT // WINDOW
    units = [(jb, g) for jb in range(n_qb) for g in range(B_KV_HEADS)]
    key_i = lax.broadcasted_iota(I32, (3 * WINDOW, 1), 0)

    def scores(u):
        jb, g = units[u]
        kb = kfull[jb * WINDOW:(jb + 3) * WINDOW, g * HEAD_DIM:(g + 1) * HEAD_DIM]
        qt = jnp.concatenate(
            [qt_ref[0, h * HEAD_DIM:(h + 1) * HEAD_DIM, jb * WINDOW:(jb + 1) * WINDOW]
             for h in range(g * grp, (g + 1) * grp)], axis=1)
        s = _dot(kb, qt) + bias_ref[g]
        if jb == 0 or jb == n_qb - 1:
            kpos = t0 + (jb - 1) * WINDOW + key_i
            s = jnp.where((kpos >= 0) & (kpos < seq_len), s, NEG_INF)
        s_scr[u % N_SBUF] = s

    def values(u):
        jb, g = units[u]
        s = s_scr[u % N_SBUF]
        sink = sink_ref[g:g + 1, :]
        mx = jnp.maximum(jnp.max(s, axis=0, keepdims=True), sink)
        p = jnp.exp2(s - mx)
        den = jnp.sum(p, axis=0, keepdims=True) + jnp.exp2(sink - mx)
        vt = vtfull[g * HEAD_DIM:(g + 1) * HEAD_DIM, jb * WINDOW:(jb + 3) * WINDOW]
        ot = _dot(vt, p.astype(BF16)) / den
        for hh in range(grp):
            h = g * grp + hh
            ot_scr[h * HEAD_DIM:(h + 1) * HEAD_DIM, jb * WINDOW:(jb + 1) * WINDOW] = (
                ot[:, hh * WINDOW:(hh + 1) * WINDOW])

    for u in range(N_SBUF - 1):
        scores(u)
    for u in range(len(units)):
        if u + N_SBUF - 1 < len(units):
            scores(u + N_SBUF - 1)
        values(u)
    y = out_proj(jnp.transpose(ot_scr[...]).astype(BF16))
    for j in range(N_SORT_BLOCKS):
        sort_block(j)
    new_pos, new_h2 = route(y)
    pos_scr[...] = new_pos
    h2_scr[...] = new_h2


def _mixer_call(seq_len, qt, k, vt, bias, sink, yc, ya, x, mod, layer, w_out, b_out, ln_g, ln_b, rw, rbias):
    bsz, seq, d = x.shape
    nt = seq // TT
    n_tiles = bsz * nt
    kb = TT // WINDOW
    grp = B_HEADS // B_KV_HEADS
    const2 = lambda n: (0, 0)

    def at_tile(fn):
        def index_map(n):
            t = jnp.minimum(n, n_tiles - 1)
            return fn(t // nt, t % nt)
        return index_map

    row = at_tile(lambda b, i: (b, i, 0))
    colblk = at_tile(lambda b, i: (b, 0, i))
    prev_k = at_tile(lambda b, i: (b, jnp.maximum(i * kb - 1, 0), 0))
    next_k = at_tile(lambda b, i: (b, jnp.minimum((i + 1) * kb, seq // WINDOW - 1), 0))
    prev_v = at_tile(lambda b, i: (b, 0, jnp.maximum(i * kb - 1, 0)))
    next_v = at_tile(lambda b, i: (b, 0, jnp.minimum((i + 1) * kb, seq // WINDOW - 1)))
    tile = at_tile(lambda b, i: (b * nt + i, 0, 0))
    lt = (jnp.arange(LANES)[None, :] < jnp.arange(N_EXPERTS)[:, None]).astype(BF16)
    before = (jnp.arange(TT)[:, None] < jnp.arange(TT)[None, :]).astype(BF16)
    return pl.pallas_call(
        functools.partial(_mixer_kernel, seq_len, nt),
        out_shape=[
            jax.ShapeDtypeStruct((bsz, seq, d), F32),
            jax.ShapeDtypeStruct(((n_tiles + 1) * R_TILE, d), BF16),
            jax.ShapeDtypeStruct((n_tiles, TOP_K, TT), I32),
            jax.ShapeDtypeStruct((n_tiles, TOP_K, TT), F32),
            jax.ShapeDtypeStruct((n_tiles, N_EXPERTS, LANES), F32),
        ],
        grid=(n_tiles + 1,),
        in_specs=[
            pl.BlockSpec((1, B_WIDTH, TT), colblk),
            pl.BlockSpec((1, WINDOW, KV_WIDTH), prev_k),
            pl.BlockSpec((1, TT, KV_WIDTH), row),
            pl.BlockSpec((1, WINDOW, KV_WIDTH), next_k),
            pl.BlockSpec((1, KV_WIDTH, WINDOW), prev_v),
            pl.BlockSpec((1, KV_WIDTH, TT), colblk),
            pl.BlockSpec((1, KV_WIDTH, WINDOW), next_v),
            pl.BlockSpec((B_KV_HEADS, 3 * WINDOW, grp * WINDOW), lambda n: (0, 0, 0)),
            pl.BlockSpec((B_KV_HEADS, grp * WINDOW), const2),
            pl.BlockSpec((1, TT, C_WIDTH), row),
            pl.BlockSpec((1, TT, A_WIDTH), row),
            pl.BlockSpec((1, TT, d), row),
            pl.BlockSpec((None, None, 6, d), at_tile(lambda b, i: (layer, b, 0, 0))),
            pl.BlockSpec((d, d), const2),
            pl.BlockSpec((1, d), const2),
            pl.BlockSpec((1, d), const2),
            pl.BlockSpec((1, d), const2),
            pl.BlockSpec((N_EXPERTS, d), const2),
            pl.BlockSpec((N_EXPERTS, 1), const2),
            pl.BlockSpec((N_EXPERTS, LANES), const2),
            pl.BlockSpec((TT, TT), const2),
        ],
        out_specs=[
            pl.BlockSpec((1, TT, d), row),
            pl.BlockSpec((R_TILE, d), lambda n: (n, 0)),
            pl.BlockSpec((1, TOP_K, TT), tile),
            pl.BlockSpec((1, TOP_K, TT), tile),
            pl.BlockSpec((1, N_EXPERTS, LANES), tile),
        ],
        scratch_shapes=[
            pltpu.VMEM((B_WIDTH, TT), F32),
            pltpu.VMEM((N_SBUF, 3 * WINDOW, grp * WINDOW), F32),
            pltpu.VMEM((TT, d), BF16),
            pltpu.VMEM((TOP_K, TT), I32),
        ],
        compiler_params=pltpu.CompilerParams(
            dimension_semantics=("arbitrary",), vmem_limit_bytes=VMEM_LIMIT),
        name="mixer_router_sort",
    )(qt, k, k, k, vt, vt, vt, bias, sink, yc, ya, x, mod, w_out, b_out, ln_g, ln_b, rw, rbias, lt, before)


def _expert_kernel(layer, src_ref, dst_ref, bexp_ref, enext_ref, nused_ref, xs_hbm, wg_hbm, wu_hbm, wd_hbm,
                   ys_hbm, xbuf, ybuf, wg_st, wu_st, wd_st, wgb, wub, wdb, sem_in, sem_out, sem_w):
    n_blk = bexp_ref.shape[0]
    nused = nused_ref[0]
    chunks_per_tile = R_TILE // CH
    zero_chunk = chunks_per_tile - 1
    scratch_chunk = 0
    de = wgb.shape[1]
    half = de // 2

    def in_copy(chunk, s, c):
        return pltpu.make_async_copy(
            xs_hbm.at[pl.ds(pl.multiple_of(chunk * CH, CH), CH)],
            xbuf.at[s, pl.ds(c * CH, CH)], sem_in.at[s])

    def out_copy(chunk, s, c):
        return pltpu.make_async_copy(
            ybuf.at[s, pl.ds(c * CH, CH)],
            ys_hbm.at[pl.ds(pl.multiple_of(chunk * CH, CH), CH)], sem_out.at[s])

    def weight_copies(e, s):
        return [pltpu.make_async_copy(wg_hbm.at[layer, e], wg_st.at[s], sem_w.at[s]),
                pltpu.make_async_copy(wu_hbm.at[layer, e], wu_st.at[s], sem_w.at[s]),
                pltpu.make_async_copy(wd_hbm.at[layer, e], wd_st.at[s], sem_w.at[s])]

    ybuf[...] = jnp.zeros_like(ybuf)
    for b in range(N_XBUF - 1):
        for c in range(NCH):
            in_copy(src_ref[b * NCH + c], b, c).start()
    for c in range(NCH):
        out_copy(scratch_chunk + c, 0, c).start()

    @pl.when(nused > 0)
    def _():
        for cp in weight_copies(bexp_ref[0], 0):
            cp.start()

    def block(blk, n_changes):
        slot = blk % 2
        xslot = blk % N_XBUF
        expert = bexp_ref[blk]
        prev_expert = bexp_ref[jnp.maximum(blk - 1, 0)]
        change = (blk == 0) | (expert != prev_expert)

        @pl.when(change)
        def _():
            ws = n_changes % 2
            for cp in weight_copies(expert, ws):
                cp.wait()
            wgb[...] = wg_st[ws].astype(BF16)
            wub[...] = wu_st[ws].astype(BF16)
            wdb[...] = wd_st[ws].astype(BF16)
            upcoming = enext_ref[blk]

            @pl.when(upcoming != expert)
            def _():
                for cp in weight_copies(upcoming, 1 - ws):
                    cp.start()

        for c in range(NCH):
            in_copy(0, xslot, c).wait()
        ahead = blk + N_XBUF - 1
        live = ahead < nused
        nxt = jnp.minimum(ahead, n_blk - 1) * NCH
        prv = jnp.maximum(blk - 1, 0) * NCH

        def start_gather(lo, hi):
            for c in range(lo, hi):
                in_copy(jnp.where(live, src_ref[nxt + c], zero_chunk), ahead % N_XBUF, c).start(priority=1)

        def start_writeback(lo, hi):
            for c in range(lo, hi):
                out_copy(jnp.where(blk > 0, dst_ref[prv + c], scratch_chunk + NCH + c), 1 - slot, c).start()

        x = xbuf[xslot]
        q = NCH // 2
        hmid = []
        for j in range(2):
            g = _dot(x, wgb[:, j * half:(j + 1) * half])
            start_gather(j * q, (j + 1) * q)
            u = _dot(x, wub[:, j * half:(j + 1) * half])
            start_writeback(j * q, (j + 1) * q)
            hmid.append(((g * _sigmoid(g)) * u).astype(BF16))
        y = _dot(hmid[0], wdb[0:half, :]) + _dot(hmid[1], wdb[half:, :])
        for c in range(NCH):
            out_copy(0, slot, c).wait()
        ybuf[slot] = y.astype(BF16)
        return n_changes + change.astype(I32)

    lax.fori_loop(0, nused, block, jnp.int32(0))

    @pl.when(nused >= 1)
    def _():
        last = (nused - 1) * NCH
        for c in range(NCH):
            out_copy(dst_ref[last + c], (nused - 1) % 2, c).start()
        for c in range(NCH):
            out_copy(0, nused % 2, c).wait()

    for c in range(NCH):
        out_copy(0, jnp.maximum(nused - 1, 0) % 2, c).wait()
    for b in range(N_XBUF - 1):
        for c in range(NCH):
            in_copy(0, (nused + b) % N_XBUF, c).wait()


def _expert_call(src, dst, bexp, enext, nused, xs, layer, w_gate, w_up, w_down):
    d = xs.shape[-1]
    de = w_gate.shape[-1]
    grid_spec = pltpu.PrefetchScalarGridSpec(
        num_scalar_prefetch=5,
        grid=(1,),
        in_specs=[pl.BlockSpec(memory_space=pl.ANY)] * 4,
        out_specs=pl.BlockSpec(memory_space=pl.ANY),
        scratch_shapes=[
            pltpu.VMEM((N_XBUF, BM, d), BF16),
            pltpu.VMEM((2, BM, d), BF16),
            pltpu.VMEM((2, d, de), F32),
            pltpu.VMEM((2, d, de), F32),
            pltpu.VMEM((2, de, d), F32),
            pltpu.VMEM((d, de), BF16),
            pltpu.VMEM((d, de), BF16),
            pltpu.VMEM((de, d), BF16),
            pltpu.SemaphoreType.DMA((N_XBUF,)),
            pltpu.SemaphoreType.DMA((2,)),
            pltpu.SemaphoreType.DMA((2,)),
        ],
    )
    return pl.pallas_call(
        functools.partial(_expert_kernel, layer),
        out_shape=jax.ShapeDtypeStruct(xs.shape, xs.dtype),
        grid_spec=grid_spec,
        input_output_aliases={5: 0},
        compiler_params=pltpu.CompilerParams(
            dimension_semantics=("arbitrary",), vmem_limit_bytes=VMEM_LIMIT),
        name="moe_experts",
    )(src, dst, bexp, enext, nused, xs, w_gate, w_up, w_down)


def _combine_kernel(ys_ref, pos_ref, gate_ref, x1_ref, mod_ref, lng_ref, lnb_ref, o_ref):
    pos = pos_ref[0]
    gate = gate_ref[0]
    m = mod_ref[...]
    iota_r = lax.broadcasted_iota(I32, (R_TILE, TT), 0)
    row_gate = (jnp.where(iota_r == pos[0:1], gate[0:1], 0.0)
                + jnp.where(iota_r == pos[1:2], gate[1:2], 0.0))
    gs = jnp.sum(row_gate, axis=1, keepdims=True)
    ysc = (ys_ref[...].astype(F32) * gs).astype(BF16)
    posc = jnp.transpose(pos.astype(F32))
    iota_c = lax.broadcasted_iota(I32, (TT, R_TILE), 1).astype(F32)
    pick = jnp.where((iota_c == posc[:, 0:1]) | (iota_c == posc[:, 1:2]), 1.0, 0.0).astype(BF16)
    y = _dot(pick, ysc)
    o_ref[...] = _ln(ALPHA * x1_ref[...] + (1.0 + m[5:6]) * y, lng_ref[...], lnb_ref[...])


def _combine_call(ys, pos, gate, x1, mod, layer, tiles_per_seq, ln_g, ln_b):
    t, d = x1.shape
    n_tiles = t // TT
    const2 = lambda n: (0, 0)
    return pl.pallas_call(
        _combine_kernel,
        out_shape=jax.ShapeDtypeStruct((t, d), F32),
        grid=(n_tiles,),
        in_specs=[
            pl.BlockSpec((R_TILE, d), lambda n: (n + 1, 0)),
            pl.BlockSpec((1, TOP_K, TT), lambda n: (n, 0, 0)),
            pl.BlockSpec((1, TOP_K, TT), lambda n: (n, 0, 0)),
            pl.BlockSpec((TT, d), lambda n: (n, 0)),
            pl.BlockSpec((None, None, 6, d), lambda n: (layer, n // tiles_per_seq, 0, 0)),
            pl.BlockSpec((1, d), const2),
            pl.BlockSpec((1, d), const2),
        ],
        out_specs=pl.BlockSpec((TT, d), lambda n: (n, 0)),
        compiler_params=pltpu.CompilerParams(
            dimension_semantics=("arbitrary",), vmem_limit_bytes=VMEM_LIMIT),
        name="moe_combine",
    )(ys, pos, gate, x1, mod, ln_g, ln_b)


def _dispatch_plan(cnt, n_blocks):
    n_tiles = cnt.shape[0]
    chunks_per_tile = R_TILE // CH
    nch = (cnt + CH - 1) // CH
    padoff_ch = jnp.cumsum(nch, axis=1) - nch
    tot = jnp.sum(nch, axis=0)
    totpad = (tot + NCH - 1) // NCH * NCH
    eend = jnp.cumsum(totpad)
    ebase = eend - totpad
    tbase = jnp.cumsum(nch, axis=0) - nch
    start = (ebase[None, :] + tbase).T.reshape(-1)
    base = ((jnp.arange(n_tiles, dtype=I32)[:, None] + 1) * chunks_per_tile + padoff_ch).T.reshape(-1)
    vals = jnp.stack([start, base, nch.T.reshape(-1)], axis=1)
    delta = vals - jnp.concatenate([jnp.zeros((1, 3), I32), vals[:-1]], axis=0)
    digits = jnp.concatenate([delta // LANES, delta % LANES], axis=1).astype(BF16)
    slot = jnp.arange(n_blocks * NCH, dtype=I32)
    started = (start[None, :] <= slot[:, None]).astype(BF16)
    got = jnp.dot(started, digits, preferred_element_type=F32).astype(I32)
    seg = got[:, :3] * LANES + got[:, 3:]
    j = slot - seg[:, 0]
    valid = (j < seg[:, 2]) & (slot < eend[-1])
    zero_chunk = chunks_per_tile - 1
    src = jnp.where(valid, seg[:, 1] + j, zero_chunk).astype(I32)
    blk = slot // NCH
    scratch = (blk % 2) * NCH + slot % NCH
    dst = jnp.where(valid, src, scratch).astype(I32)
    first = jnp.arange(n_blocks, dtype=I32) * NCH
    bexp = jnp.minimum(jnp.sum(eend[None, :] <= first[:, None], axis=1), N_EXPERTS - 1).astype(I32)
    nused = (eend[-1] // NCH).astype(I32).reshape(1)
    ids = jnp.arange(N_EXPERTS, dtype=I32)
    later = jnp.where((ids[None, :] > ids[:, None]) & (totpad > 0)[None, :], ids[None, :], N_EXPERTS)
    next_of = jnp.min(later, axis=1)
    next_of = jnp.where(next_of == N_EXPERTS, ids, next_of)
    enext = jnp.sum(jnp.where(bexp[:, None] == ids[None, :], next_of[None, :], 0), axis=1).astype(I32)
    return src, dst, bexp, enext, nused


def _t5_bucket(rel):
    nb = N_BUCKETS // 2
    max_exact = nb // 2
    ret = jnp.where(rel > 0, nb, 0)
    n = jnp.abs(rel)
    nf = jnp.maximum(n, 1).astype(jnp.float32)
    large = max_exact + (jnp.log(nf / max_exact) / math.log(MAX_DISTANCE / max_exact)
                         * (nb - max_exact)).astype(jnp.int32)
    large = jnp.minimum(large, nb - 1)
    return ret + jnp.where(n < max_exact, n, large)


def _band_bias(rel_bias):
    qi = jnp.arange(WINDOW)
    kj = jnp.arange(3 * WINDOW)
    rel = kj[None, :] - WINDOW - qi[:, None]
    pick = _t5_bucket(rel)[:, :, None, None] == jnp.arange(N_BUCKETS)[None, None, :, None]
    bias = jnp.sum(jnp.where(pick, rel_bias.astype(F32)[None, None], 0.0), axis=2) * LOG2E
    bias = jnp.where((jnp.abs(rel) <= WINDOW)[:, :, None], bias, NEG_INF)
    grp = B_HEADS // B_KV_HEADS
    bias = jnp.transpose(bias, (2, 1, 0)).reshape(B_KV_HEADS, grp, 3 * WINDOW, WINDOW)
    return jnp.transpose(bias, (0, 2, 1, 3)).reshape(B_KV_HEADS, 3 * WINDOW, grp * WINDOW)


def kernel(x, c, ada_w, ada_b, w_in, b_in, gmlp_ln_g, gmlp_ln_b, gmlp_ws, gmlp_bs, attn_sink, conv_w,
           conv_b, conv_ln_g, conv_ln_b, w_out, b_out, ln_mix_g, ln_mix_b, w_gate, w_up, w_down,
           ln_ffn_g, ln_ffn_b, rel_bias, router_w, router_bias):
    bsz, seq, d = x.shape
    n_layers = ada_w.shape[0]
    t = bsz * seq
    nt = seq // TT
    n_tiles = t // TT
    max_chunks = n_tiles * (TOP_K * TT // CH + N_EXPERTS) + N_EXPERTS * (NCH - 1)
    n_blocks = -(-max_chunks // NCH)

    mod = _ada_call(c, ada_w, ada_b).reshape(n_layers, bsz, 6, d)
    bias = _band_bias(rel_bias)
    rw = router_w.T.astype(BF16)
    rbias = router_bias.astype(F32).reshape(N_EXPERTS, 1)
    row = lambda a: a.reshape(1, -1)

    for l in range(n_layers):
        bsb = jnp.repeat(gmlp_bs[l].T, HEAD_DIM, axis=1)
        ya, qt, k, vt, yc = _inproj_call(
            x, mod, l, w_in[l], b_in[l], row(gmlp_ln_g[l]), row(gmlp_ln_b[l]),
            gmlp_ws[l].astype(BF16), bsb, conv_w[l], row(conv_b[l]), row(conv_ln_g[l]), row(conv_ln_b[l]))
        sink = jnp.repeat(attn_sink[l].astype(F32) * LOG2E, WINDOW).reshape(B_KV_HEADS, -1)
        x1, xs, pos, gate, cntb = _mixer_call(
            seq, qt, k, vt, bias, sink, yc, ya, x, mod, l, w_out[l].astype(BF16),
            row(b_out[l]), row(ln_mix_g[l]), row(ln_mix_b[l]), rw, rbias)
        cnt = cntb[:, :, 0].astype(I32)
        src, dst, bexp, enext, nused = _dispatch_plan(cnt, n_blocks)
        ys = _expert_call(src, dst, bexp, enext, nused, xs, l, w_gate, w_up, w_down)
        x = _combine_call(ys, pos, gate, x1.reshape(t, d), mod, l, nt, row(ln_ffn_g[l]),
                          row(ln_ffn_b[l])).reshape(bsz, seq, d)
    return x
```

```python
import functools
import math

import jax
import jax.numpy as jnp
from jax import lax
from jax.experimental import pallas as pl
from jax.experimental.pallas import tpu as pltpu

F32 = jnp.float32
BF16 = jnp.bfloat16
I32 = jnp.int32

D_MODEL = 1024
DEPTH = 2
HEAD_DIM = 64
A_WIDTH = 256
A_HEADS = 4
CHUNK = 128
B_WIDTH = 512
B_HEADS = 8
B_KV_HEADS = 2
KV_WIDTH = B_KV_HEADS * HEAD_DIM
WINDOW = 128
N_BUCKETS = 32
MAX_DISTANCE = 128
C_WIDTH = 256
CONV_WIDTH = 31
CONV_PAD = CONV_WIDTH // 2
IN_WIDTH = 2 * A_WIDTH + B_WIDTH + 2 * KV_WIDTH + 2 * C_WIDTH
N_EXPERTS = 32
N_GROUPS = 4
EXPERTS_PER_GROUP = N_EXPERTS // N_GROUPS
TOP_K = 2
D_EXPERT = D_MODEL // 2
ALPHA = (2 * DEPTH) ** 0.25
LN_EPS = 1e-5
NEG_INF = -1e30
LOG2E = 1.4426950408889634
Q_SCALE = HEAD_DIM ** -0.5 * LOG2E

LANES = 128
SUBLANES = 8
BF16_SUBLANES = 16
VMEM_LIMIT = 48 * 1024 * 1024

ADA_TN = 1536
TS = 1024
TT = 512
CH = BF16_SUBLANES
R_TILE = 1536
BM = 512
NCH = BM // CH
N_RING = 4
N_SORT_BLOCKS = 6
N_SBUF = 2
HALO = 16

assert R_TILE >= TOP_K * TT + N_EXPERTS * (CH - 1) + CH
assert R_TILE >= 2 * NCH * CH + CH


def _sigmoid(x):
    return 1.0 / (1.0 + jnp.exp(-x))


def _gelu_tanh(x):
    return x * (0.5 * (1.0 + jnp.tanh(0.7978845608028654 * (x + 0.044715 * (x * x * x)))))


def _ln(x, g, b):
    mu = jnp.mean(x, axis=-1, keepdims=True)
    xc = x - mu
    var = jnp.mean(xc * xc, axis=-1, keepdims=True)
    return xc * lax.rsqrt(var + LN_EPS) * g + b


def _dot(a, b):
    return jnp.dot(a, b, preferred_element_type=F32)


def _dot_nt(a, b):
    return lax.dot_general(a, b, (((1,), (1,)), ((), ())), preferred_element_type=F32)


def _ada_kernel(c_ref, w_ref, b_ref, o_ref):
    c = c_ref[...]
    s = (c * _sigmoid(c)).astype(BF16)
    o_ref[0] = _dot(s, w_ref[0].astype(BF16)) + b_ref[0]


def _ada_call(c, ada_w, ada_b):
    nl, d, n = ada_w.shape
    bsz = c.shape[0]
    return pl.pallas_call(
        _ada_kernel,
        out_shape=jax.ShapeDtypeStruct((nl, bsz, n), F32),
        grid=(nl, n // ADA_TN),
        in_specs=[
            pl.BlockSpec((bsz, d), lambda l, j: (0, 0)),
            pl.BlockSpec((1, d, ADA_TN), lambda l, j: (l, 0, j)),
            pl.BlockSpec((1, 1, ADA_TN), lambda l, j: (l, 0, j)),
        ],
        out_specs=pl.BlockSpec((1, bsz, ADA_TN), lambda l, j: (l, 0, j)),
        compiler_params=pltpu.CompilerParams(
            dimension_semantics=("arbitrary", "arbitrary"), vmem_limit_bytes=VMEM_LIMIT),
        name="ada_mod",
    )(c, ada_w, ada_b.reshape(nl, 1, n))


def _inproj_kernel(x_ref, xp_ref, xn_ref, mod_ref, w_ref, b_ref, wqt_ref, bq_ref, wvt_ref, bv_ref,
                   lng_ref, lnb_ref, ws_ref, bsb_ref, cw_ref, cb_ref, clg_ref, clb_ref,
                   ya_ref, qt_ref, k_ref, vt_ref, yc_ref, conv_scr, z_scr):
    i = pl.program_id(1)
    n_i = pl.num_programs(1)
    m = mod_ref[...]

    def modulate(xv):
        return (xv * (1.0 + m[1:2]) + m[0:1]).astype(BF16)

    hb = modulate(x_ref[0])
    col_u, col_v = 0, A_WIDTH
    col_k = 2 * A_WIDTH + B_WIDTH
    col_a = col_k + 2 * KV_WIDTH
    col_g = col_a + C_WIDTH

    def proj(lhs, c0, width):
        return _dot(lhs, w_ref[:, c0:c0 + width]) + b_ref[:, c0:c0 + width]

    hx = jnp.concatenate([modulate(xp_ref[0]), hb, modulate(xn_ref[0])], axis=0)
    yg = proj(hx, col_a, C_WIDTH) * _sigmoid(proj(hx, col_g, C_WIDTH))
    conv_scr[0:HALO, :] = yg[0:HALO] * jnp.where(i > 0, 1.0, 0.0)
    conv_scr[HALO:HALO + TS, :] = yg[HALO:HALO + TS]
    conv_scr[HALO + TS:, :] = yg[HALO + TS:] * jnp.where(i < n_i - 1, 1.0, 0.0)
    first = HALO - CONV_PAD
    acc = jnp.zeros((TS, C_WIDTH), F32) + cb_ref[...]
    for r in range(SUBLANES):
        z = None
        for a in range(-(-(first + CONV_WIDTH) // SUBLANES)):
            w = a * SUBLANES + r - first
            if 0 <= w < CONV_WIDTH:
                term = conv_scr[a * SUBLANES:a * SUBLANES + TS + SUBLANES, :] * cw_ref[w:w + 1, :]
                z = term if z is None else z + term
        if r == 0:
            acc = acc + z[0:TS]
        else:
            z_scr[r] = z
            acc = acc + z_scr[r, r:r + TS, :]
    yc = _ln(acc, clg_ref[...], clb_ref[...])
    yc_ref[0] = (yc * _sigmoid(yc)).astype(BF16)

    u = _gelu_tanh(proj(hb, col_u, A_WIDTH))
    v = _gelu_tanh(proj(hb, col_v, A_WIDTH))
    qt_ref[0] = ((_dot_nt(wqt_ref[...], hb) + bq_ref[...]) * Q_SCALE).astype(BF16)
    k_ref[0] = proj(hb, col_k, KV_WIDTH).astype(BF16)
    vt_ref[0] = (_dot_nt(wvt_ref[...], hb) + bv_ref[...]).astype(BF16)

    vb = _ln(v, lng_ref[...], lnb_ref[...]).astype(BF16)
    head_of_lane = lax.broadcasted_iota(I32, (CHUNK, A_WIDTH), 1) // HEAD_DIM
    for ch in range(TS // CHUNK):
        vc = vb[ch * CHUNK:(ch + 1) * CHUNK]
        acc = bsb_ref[...]
        for hh in range(A_HEADS):
            acc = acc + _dot(ws_ref[hh], jnp.where(head_of_lane == hh, vc, jnp.zeros_like(vc)))
        ya_ref[0, ch * CHUNK:(ch + 1) * CHUNK, :] = (u[ch * CHUNK:(ch + 1) * CHUNK] * acc).astype(BF16)


def _inproj_call(x, mod, layer, w_in, b_in, ln_g, ln_b, ws, bsb, conv_w, conv_b, conv_ln_g, conv_ln_b):
    bsz, seq, d = x.shape
    grid = (bsz, seq // TS)
    hb = TS // HALO
    const2 = lambda b, i: (0, 0)
    row = lambda b, i: (b, i, 0)
    colblk = lambda b, i: (b, 0, i)
    prev_h = lambda b, i: (b, jnp.maximum(i * hb - 1, 0), 0)
    next_h = lambda b, i: (b, jnp.minimum((i + 1) * hb, seq // HALO - 1), 0)
    q0 = 2 * A_WIDTH
    v0 = q0 + B_WIDTH + KV_WIDTH
    wb = w_in.astype(BF16)
    wqt = w_in[:, q0:q0 + B_WIDTH].T.astype(BF16)
    wvt = w_in[:, v0:v0 + KV_WIDTH].T.astype(BF16)
    bq = b_in[q0:q0 + B_WIDTH].reshape(B_WIDTH, 1)
    bv = b_in[v0:v0 + KV_WIDTH].reshape(KV_WIDTH, 1)

    def out(width):
        return jax.ShapeDtypeStruct((bsz, seq, width), BF16), pl.BlockSpec((1, TS, width), row)

    def out_t(width):
        return jax.ShapeDtypeStruct((bsz, width, seq), BF16), pl.BlockSpec((1, width, TS), colblk)

    outs = [out(A_WIDTH), out_t(B_WIDTH), out(KV_WIDTH), out_t(KV_WIDTH), out(C_WIDTH)]
    return pl.pallas_call(
        _inproj_kernel,
        out_shape=[o[0] for o in outs],
        grid=grid,
        in_specs=[
            pl.BlockSpec((1, TS, d), row),
            pl.BlockSpec((1, HALO, d), prev_h),
            pl.BlockSpec((1, HALO, d), next_h),
            pl.BlockSpec((None, None, 6, d), lambda b, i: (layer, b, 0, 0)),
            pl.BlockSpec((d, IN_WIDTH), const2),
            pl.BlockSpec((1, IN_WIDTH), const2),
            pl.BlockSpec((B_WIDTH, d), const2),
            pl.BlockSpec((B_WIDTH, 1), const2),
            pl.BlockSpec((KV_WIDTH, d), const2),
            pl.BlockSpec((KV_WIDTH, 1), const2),
            pl.BlockSpec((1, A_WIDTH), const2),
            pl.BlockSpec((1, A_WIDTH), const2),
            pl.BlockSpec((A_HEADS, CHUNK, CHUNK), lambda b, i: (0, 0, 0)),
            pl.BlockSpec((CHUNK, A_WIDTH), const2),
            pl.BlockSpec((CONV_WIDTH, C_WIDTH), const2),
            pl.BlockSpec((1, C_WIDTH), const2),
            pl.BlockSpec((1, C_WIDTH), const2),
            pl.BlockSpec((1, C_WIDTH), const2),
        ],
        out_specs=[o[1] for o in outs],
        scratch_shapes=[
            pltpu.VMEM((TS + 2 * HALO, C_WIDTH), F32),
            pltpu.VMEM((SUBLANES, TS + SUBLANES, C_WIDTH), F32),
        ],
        compiler_params=pltpu.CompilerParams(
            dimension_semantics=("arbitrary", "arbitrary"), vmem_limit_bytes=VMEM_LIMIT),
        name="inproj_gmlp_conv",
    )(x, x, x, mod, wb, b_in.reshape(1, -1), wqt, bq, wvt, bv, ln_g, ln_b, ws, bsb,
      conv_w, conv_b, conv_ln_g, conv_ln_b)


def _first_argmax(vals, iota_f, width):
    m = jnp.max(vals, axis=0, keepdims=True)
    idx = jnp.min(jnp.where(vals == m, iota_f, float(width)), axis=0, keepdims=True)
    return m, idx


def _mixer_kernel(seq_len, tiles_per_seq, qt_ref, kp_ref, kc_ref, kn_ref, vtp_ref, vtc_ref, vtn_ref,
                  bias_ref, sink_ref, yc_ref, ya_ref, x_ref, mod_ref, wo_ref, bo_ref, lng_ref, lnb_ref,
                  rw_ref, rb_ref, lt_ref, before_ref, x1_ref, xs_ref, pos_ref, gate_ref, cnt_ref,
                  ot_scr, s_scr, h2_scr, pos_scr):
    n = pl.program_id(0)
    last_tile = pl.num_programs(0) - 2
    i = jnp.minimum(n, last_tile) % tiles_per_seq
    t0 = i * TT
    m = mod_ref[...]

    @pl.when(n == 0)
    def _():
        h2_scr[...] = jnp.zeros_like(h2_scr)
        pos_scr[...] = jnp.zeros_like(pos_scr)

    pos_prev = pos_scr[...]
    sort_rows = R_TILE // N_SORT_BLOCKS

    def sort_block(j):
        iota_r = j * sort_rows + lax.broadcasted_iota(I32, (sort_rows, TT), 0)
        onehot = jnp.where((iota_r == pos_prev[0:1]) | (iota_r == pos_prev[1:2]), 1.0, 0.0).astype(BF16)
        xs_ref[j * sort_rows:(j + 1) * sort_rows, :] = _dot(onehot, h2_scr[...]).astype(BF16)

    def out_proj(yb):
        return (_dot(ya_ref[0], wo_ref[0:A_WIDTH, :])
                + _dot(yb, wo_ref[A_WIDTH:A_WIDTH + B_WIDTH, :])
                + _dot(yc_ref[0], wo_ref[A_WIDTH + B_WIDTH:, :]) + bo_ref[...])

    def route(y):
        x1 = _ln(ALPHA * x_ref[0] + (1.0 + m[2:3]) * y, lng_ref[...], lnb_ref[...])
        x1_ref[0] = x1
        h2 = (x1 * (1.0 + m[4:5]) + m[3:4]).astype(BF16)
        scores = _sigmoid(_dot_nt(rw_ref[...], h2))
        sel = scores + rb_ref[...]
        iota_f = lax.broadcasted_iota(I32, (EXPERTS_PER_GROUP, TT), 0).astype(F32)
        best = None
        for g in range(N_GROUPS):
            sl = slice(g * EXPERTS_PER_GROUP, (g + 1) * EXPERTS_PER_GROUP)
            sg = sel[sl]
            m1, i1 = _first_argmax(sg, iota_f, EXPERTS_PER_GROUP)
            m2, i2 = _first_argmax(jnp.where(iota_f == i1, -jnp.inf, sg), iota_f, EXPERTS_PER_GROUP)
            sc = scores[sl]
            s1 = jnp.sum(jnp.where(iota_f == i1, sc, 0.0), axis=0, keepdims=True)
            s2 = jnp.sum(jnp.where(iota_f == i2, sc, 0.0), axis=0, keepdims=True)
            cand = (m1 + m2, i1 + g * EXPERTS_PER_GROUP, i2 + g * EXPERTS_PER_GROUP, s1, s2)
            if best is None:
                best = cand
            else:
                take = cand[0] > best[0]
                best = tuple(jnp.where(take, c, b) for c, b in zip(cand, best))
        _, e1, e2, s1, s2 = best
        gate_ref[0] = jnp.concatenate([s1, s2], axis=0) / (s1 + s2)

        iota_e = lax.broadcasted_iota(I32, (N_EXPERTS, TT), 0).astype(F32)
        in0 = iota_e == e1
        in1 = iota_e == e2
        member = jnp.where(in0 | in1, 1.0, 0.0)
        cnt = jnp.sum(member, axis=1, keepdims=True)
        cnt_ref[0] = jnp.broadcast_to(cnt, (N_EXPERTS, LANES))
        nch = jnp.floor((cnt + (CH - 1)) * (1.0 / CH))
        nch_pad = jnp.concatenate([jnp.broadcast_to(nch, (N_EXPERTS, LANES)),
                                   jnp.zeros((LANES - N_EXPERTS, LANES), F32)], axis=0).astype(BF16)
        padoff = _dot(lt_ref[...], nch_pad)[:, 0:1] * float(CH)
        rank = _dot(member.astype(BF16), before_ref[...])
        posf = padoff + rank
        pos0 = jnp.sum(jnp.where(in0, posf, 0.0), axis=0, keepdims=True).astype(I32)
        pos1 = jnp.sum(jnp.where(in1, posf, 0.0), axis=0, keepdims=True).astype(I32)
        pos = jnp.concatenate([pos0, pos1], axis=0)
        pos_ref[0] = pos
        return pos, h2

    kfull = jnp.concatenate([kp_ref[0], kc_ref[0], kn_ref[0]], axis=0)
    vtfull = jnp.concatenate([vtp_ref[0], vtc_ref[0], vtn_ref[0]], axis=1)
    grp = B_HEADS // B_KV_HEADS
    n_qb = TT // WINDOW
    units = [(jb, g) for jb in range(n_qb) for g in range(B_KV_HEADS)]
    key_i = lax.broadcasted_iota(I32, (3 * WINDOW, 1), 0)

    def scores(u):
        jb, g = units[u]
        kb = kfull[jb * WINDOW:(jb + 3) * WINDOW, g * HEAD_DIM:(g + 1) * HEAD_DIM]
        qt = jnp.concatenate(
            [qt_ref[0, h * HEAD_DIM:(h + 1) * HEAD_DIM, jb * WINDOW:(jb + 1) * WINDOW]
             for h in range(g * grp, (g + 1) * grp)], axis=1)
        s = _dot(kb, qt) + bias_ref[g]
        if jb == 0 or jb == n_qb - 1:
            kpos = t0 + (jb - 1) * WINDOW + key_i
            s = jnp.where((kpos >= 0) & (kpos < seq_len), s, NEG_INF)
        s_scr[u % N_SBUF] = s

    def values(u):
        jb, g = units[u]
        s = s_scr[u % N_SBUF]
        sink = sink_ref[g:g + 1, :]
        mx = jnp.maximum(jnp.max(s, axis=0, keepdims=True), sink)
        p = jnp.exp2(s - mx)
        den = jnp.sum(p, axis=0, keepdims=True) + jnp.exp2(sink - mx)
        vt = vtfull[g * HEAD_DIM:(g + 1) * HEAD_DIM, jb * WINDOW:(jb + 3) * WINDOW]
        ot = _dot(vt, p.astype(BF16)) / den
        for hh in range(grp):
            h = g * grp + hh
            ot_scr[h * HEAD_DIM:(h + 1) * HEAD_DIM, jb * WINDOW:(jb + 1) * WINDOW] = (
                ot[:, hh * WINDOW:(hh + 1) * WINDOW])

    for u in range(N_SBUF - 1):
        scores(u)
    for u in range(len(units)):
        if u + N_SBUF - 1 < len(units):
            scores(u + N_SBUF - 1)
        values(u)
    y = out_proj(jnp.transpose(ot_scr[...]).astype(BF16))
    for j in range(N_SORT_BLOCKS):
        sort_block(j)
    new_pos, new_h2 = route(y)
    pos_scr[...] = new_pos
    h2_scr[...] = new_h2


def _mixer_call(seq_len, qt, k, vt, bias, sink, yc, ya, x, mod, layer, w_out, b_out, ln_g, ln_b, rw, rbias):
    bsz, seq, d = x.shape
    nt = seq // TT
    n_tiles = bsz * nt
    kb = TT // WINDOW
    grp = B_HEADS // B_KV_HEADS
    const2 = lambda n: (0, 0)

    def at_tile(fn):
        def index_map(n):
            t = jnp.minimum(n, n_tiles - 1)
            return fn(t // nt, t % nt)
        return index_map

    row = at_tile(lambda b, i: (b, i, 0))
    colblk = at_tile(lambda b, i: (b, 0, i))
    prev_k = at_tile(lambda b, i: (b, jnp.maximum(i * kb - 1, 0), 0))
    next_k = at_tile(lambda b, i: (b, jnp.minimum((i + 1) * kb, seq // WINDOW - 1), 0))
    prev_v = at_tile(lambda b, i: (b, 0, jnp.maximum(i * kb - 1, 0)))
    next_v = at_tile(lambda b, i: (b, 0, jnp.minimum((i + 1) * kb, seq // WINDOW - 1)))
    tile = at_tile(lambda b, i: (b * nt + i, 0, 0))
    lt = (jnp.arange(LANES)[None, :] < jnp.arange(N_EXPERTS)[:, None]).astype(BF16)
    before = (jnp.arange(TT)[:, None] < jnp.arange(TT)[None, :]).astype(BF16)
    return pl.pallas_call(
        functools.partial(_mixer_kernel, seq_len, nt),
        out_shape=[
            jax.ShapeDtypeStruct((bsz, seq, d), F32),
            jax.ShapeDtypeStruct(((n_tiles + 1) * R_TILE, d), BF16),
            jax.ShapeDtypeStruct((n_tiles, TOP_K, TT), I32),
            jax.ShapeDtypeStruct((n_tiles, TOP_K, TT), F32),
            jax.ShapeDtypeStruct((n_tiles, N_EXPERTS, LANES), F32),
        ],
        grid=(n_tiles + 1,),
        in_specs=[
            pl.BlockSpec((1, B_WIDTH, TT), colblk),
            pl.BlockSpec((1, WINDOW, KV_WIDTH), prev_k),
            pl.BlockSpec((1, TT, KV_WIDTH), row),
            pl.BlockSpec((1, WINDOW, KV_WIDTH), next_k),
            pl.BlockSpec((1, KV_WIDTH, WINDOW), prev_v),
            pl.BlockSpec((1, KV_WIDTH, TT), colblk),
            pl.BlockSpec((1, KV_WIDTH, WINDOW), next_v),
            pl.BlockSpec((B_KV_HEADS, 3 * WINDOW, grp * WINDOW), lambda n: (0, 0, 0)),
            pl.BlockSpec((B_KV_HEADS, grp * WINDOW), const2),
            pl.BlockSpec((1, TT, C_WIDTH), row),
            pl.BlockSpec((1, TT, A_WIDTH), row),
            pl.BlockSpec((1, TT, d), row),
            pl.BlockSpec((None, None, 6, d), at_tile(lambda b, i: (layer, b, 0, 0))),
            pl.BlockSpec((d, d), const2),
            pl.BlockSpec((1, d), const2),
            pl.BlockSpec((1, d), const2),
            pl.BlockSpec((1, d), const2),
            pl.BlockSpec((N_EXPERTS, d), const2),
            pl.BlockSpec((N_EXPERTS, 1), const2),
            pl.BlockSpec((N_EXPERTS, LANES), const2),
            pl.BlockSpec((TT, TT), const2),
        ],
        out_specs=[
            pl.BlockSpec((1, TT, d), row),
            pl.BlockSpec((R_TILE, d), lambda n: (n, 0)),
            pl.BlockSpec((1, TOP_K, TT), tile),
            pl.BlockSpec((1, TOP_K, TT), tile),
            pl.BlockSpec((1, N_EXPERTS, LANES), tile),
        ],
        scratch_shapes=[
            pltpu.VMEM((B_WIDTH, TT), F32),
            pltpu.VMEM((N_SBUF, 3 * WINDOW, grp * WINDOW), F32),
            pltpu.VMEM((TT, d), BF16),
            pltpu.VMEM((TOP_K, TT), I32),
        ],
        compiler_params=pltpu.CompilerParams(
            dimension_semantics=("arbitrary",), vmem_limit_bytes=VMEM_LIMIT),
        name="mixer_router_sort",
    )(qt, k, k, k, vt, vt, vt, bias, sink, yc, ya, x, mod, w_out, b_out, ln_g, ln_b, rw, rbias, lt, before)


def _expert_kernel(layer, src_ref, dst_ref, bexp_ref, enext_ref, nused_ref, xs_hbm, wg_hbm, wu_hbm, wd_hbm,
                   ys_hbm, xbuf, ybuf, wg_st, wu_st, wd_st, wgb, wub, wdb, sem_in, sem_out, sem_w):
    n_blk = bexp_ref.shape[0]
    nused = nused_ref[0]
    chunks_per_tile = R_TILE // CH
    zero_chunk = chunks_per_tile - 1
    scratch_chunk = 0
    de = wgb.shape[2]
    half = de // 2

    def in_copy(chunk, s, c):
        return pltpu.make_async_copy(
            xs_hbm.at[pl.ds(pl.multiple_of(chunk * CH, CH), CH)],
            xbuf.at[s, pl.ds(c * CH, CH)], sem_in.at[s])

    def out_copy(chunk, s, c):
        return pltpu.make_async_copy(
            ybuf.at[s, pl.ds(c * CH, CH)],
            ys_hbm.at[pl.ds(pl.multiple_of(chunk * CH, CH), CH)], sem_out.at[s])

    def weight_copies(e, s):
        return [pltpu.make_async_copy(wg_hbm.at[layer, e], wg_st.at[s], sem_w.at[s]),
                pltpu.make_async_copy(wu_hbm.at[layer, e], wu_st.at[s], sem_w.at[s]),
                pltpu.make_async_copy(wd_hbm.at[layer, e], wd_st.at[s], sem_w.at[s])]

    def gather_chunk(b, c):
        return jnp.where(b < nused, src_ref[jnp.minimum(b, n_blk - 1) * NCH + c], zero_chunk)

    def write_back_chunk(b, c):
        real = (b >= 0) & (b < nused)
        return jnp.where(real, dst_ref[jnp.clip(b, 0, n_blk - 1) * NCH + c], scratch_chunk + (b % 2) * NCH + c)

    ybuf[...] = jnp.zeros_like(ybuf)
    wgb[...] = jnp.zeros_like(wgb)
    wub[...] = jnp.zeros_like(wub)
    wdb[...] = jnp.zeros_like(wdb)
    for b in range(2):
        for c in range(NCH):
            in_copy(gather_chunk(b, c), b, c).start()
        for c in range(NCH):
            out_copy(scratch_chunk + b * NCH + c, b, c).start()

    @pl.when(nused > 0)
    def _():
        for cp in weight_copies(bexp_ref[0], 0):
            cp.start()

    def pair(i, n_changes):
        blocks = (2 * i, 2 * i + 1)
        wslot = []
        for blk in blocks:
            jb = jnp.minimum(blk, n_blk - 1)
            expert = bexp_ref[jb]
            prev_expert = bexp_ref[jnp.maximum(jb - 1, 0)]
            change = (blk < nused) & ((blk == 0) | (expert != prev_expert))

            @pl.when(change)
            def _(expert=expert, jb=jb, n_changes=n_changes):
                ws = n_changes % 2
                for cp in weight_copies(expert, ws):
                    cp.wait()
                wgb[ws] = wg_st[ws].astype(BF16)
                wub[ws] = wu_st[ws].astype(BF16)
                wdb[ws] = wd_st[ws].astype(BF16)
                upcoming = enext_ref[jb]

                @pl.when(upcoming != expert)
                def _():
                    for cp in weight_copies(upcoming, 1 - ws):
                        cp.start()

            n_changes = n_changes + change.astype(I32)
            wslot.append(jnp.maximum(n_changes - 1, 0) % 2)

        for blk in blocks:
            for c in range(NCH):
                in_copy(0, blk % N_RING, c).wait()
            for c in range(NCH):
                out_copy(0, blk % N_RING, c).wait()

        dmas = []
        for blk in blocks:
            for c in range(NCH):
                dmas.append(functools.partial(
                    lambda b, c: in_copy(gather_chunk(b, c), b % N_RING, c).start(priority=1), blk + 2, c))
            for c in range(NCH):
                dmas.append(functools.partial(
                    lambda b, c: out_copy(write_back_chunk(b, c), b % N_RING, c).start(), blk - 2, c))
        n_groups = 8
        per_group = len(dmas) // n_groups

        def issue(gidx):
            for start in dmas[gidx * per_group:(gidx + 1) * per_group]:
                start()

        gidx = 0
        for k, blk in enumerate(blocks):
            x = xbuf[blk % N_RING]
            ws = wslot[k]
            hmid = []
            for h in range(2):
                cols = slice(h * half, (h + 1) * half)
                g = _dot(x, wgb[ws, :, cols])
                issue(gidx)
                u = _dot(x, wub[ws, :, cols])
                issue(gidx + 1)
                gidx += 2
                hmid.append(((g * _sigmoid(g)) * u).astype(BF16))
            y = _dot(hmid[0], wdb[ws, 0:half, :]) + _dot(hmid[1], wdb[ws, half:, :])
            ybuf[blk % N_RING] = y.astype(BF16)
        return n_changes

    n_pairs = (nused + 1) // 2
    lax.fori_loop(0, n_pairs, pair, jnp.int32(0))

    last = 2 * n_pairs
    for blk in (last, last + 1):
        for c in range(NCH):
            in_copy(0, blk % N_RING, c).wait()
        for c in range(NCH):
            out_copy(0, blk % N_RING, c).wait()
    for blk in (last - 2, last - 1):
        for c in range(NCH):
            out_copy(write_back_chunk(blk, c), blk % N_RING, c).start()
    for blk in (last - 2, last - 1):
        for c in range(NCH):
            out_copy(0, blk % N_RING, c).wait()


def _expert_call(src, dst, bexp, enext, nused, xs, layer, w_gate, w_up, w_down):
    d = xs.shape[-1]
    de = w_gate.shape[-1]
    grid_spec = pltpu.PrefetchScalarGridSpec(
        num_scalar_prefetch=5,
        grid=(1,),
        in_specs=[pl.BlockSpec(memory_space=pl.ANY)] * 4,
        out_specs=pl.BlockSpec(memory_space=pl.ANY),
        scratch_shapes=[
            pltpu.VMEM((N_RING, BM, d), BF16),
            pltpu.VMEM((N_RING, BM, d), BF16),
            pltpu.VMEM((2, d, de), F32),
            pltpu.VMEM((2, d, de), F32),
            pltpu.VMEM((2, de, d), F32),
            pltpu.VMEM((2, d, de), BF16),
            pltpu.VMEM((2, d, de), BF16),
            pltpu.VMEM((2, de, d), BF16),
            pltpu.SemaphoreType.DMA((N_RING,)),
            pltpu.SemaphoreType.DMA((N_RING,)),
            pltpu.SemaphoreType.DMA((2,)),
        ],
    )
    return pl.pallas_call(
        functools.partial(_expert_kernel, layer),
        out_shape=jax.ShapeDtypeStruct(xs.shape, xs.dtype),
        grid_spec=grid_spec,
        input_output_aliases={5: 0},
        compiler_params=pltpu.CompilerParams(
            dimension_semantics=("arbitrary",), vmem_limit_bytes=VMEM_LIMIT),
        name="moe_experts",
    )(src, dst, bexp, enext, nused, xs, w_gate, w_up, w_down)


def _combine_kernel(ys_ref, pos_ref, gate_ref, x1_ref, mod_ref, lng_ref, lnb_ref, o_ref):
    pos = pos_ref[0]
    gate = gate_ref[0]
    m = mod_ref[...]
    iota_r = lax.broadcasted_iota(I32, (R_TILE, TT), 0)
    row_gate = (jnp.where(iota_r == pos[0:1], gate[0:1], 0.0)
                + jnp.where(iota_r == pos[1:2], gate[1:2], 0.0))
    gs = jnp.sum(row_gate, axis=1, keepdims=True)
    ysc = (ys_ref[...].astype(F32) * gs).astype(BF16)
    posc = jnp.transpose(pos.astype(F32))
    iota_c = lax.broadcasted_iota(I32, (TT, R_TILE), 1).astype(F32)
    pick = jnp.where((iota_c == posc[:, 0:1]) | (iota_c == posc[:, 1:2]), 1.0, 0.0).astype(BF16)
    y = _dot(pick, ysc)
    o_ref[...] = _ln(ALPHA * x1_ref[...] + (1.0 + m[5:6]) * y, lng_ref[...], lnb_ref[...])


def _combine_call(ys, pos, gate, x1, mod, layer, tiles_per_seq, ln_g, ln_b):
    t, d = x1.shape
    n_tiles = t // TT
    const2 = lambda n: (0, 0)
    return pl.pallas_call(
        _combine_kernel,
        out_shape=jax.ShapeDtypeStruct((t, d), F32),
        grid=(n_tiles,),
        in_specs=[
            pl.BlockSpec((R_TILE, d), lambda n: (n + 1, 0)),
            pl.BlockSpec((1, TOP_K, TT), lambda n: (n, 0, 0)),
            pl.BlockSpec((1, TOP_K, TT), lambda n: (n, 0, 0)),
            pl.BlockSpec((TT, d), lambda n: (n, 0)),
            pl.BlockSpec((None, None, 6, d), lambda n: (layer, n // tiles_per_seq, 0, 0)),
            pl.BlockSpec((1, d), const2),
            pl.BlockSpec((1, d), const2),
        ],
        out_specs=pl.BlockSpec((TT, d), lambda n: (n, 0)),
        compiler_params=pltpu.CompilerParams(
            dimension_semantics=("arbitrary",), vmem_limit_bytes=VMEM_LIMIT),
        name="moe_combine",
    )(ys, pos, gate, x1, mod, ln_g, ln_b)


def _dispatch_plan(cnt, n_blocks):
    n_tiles = cnt.shape[0]
    chunks_per_tile = R_TILE // CH
    nch = (cnt + CH - 1) // CH
    padoff_ch = jnp.cumsum(nch, axis=1) - nch
    tot = jnp.sum(nch, axis=0)
    totpad = (tot + NCH - 1) // NCH * NCH
    eend = jnp.cumsum(totpad)
    ebase = eend - totpad
    tbase = jnp.cumsum(nch, axis=0) - nch
    start = (ebase[None, :] + tbase).T.reshape(-1)
    base = ((jnp.arange(n_tiles, dtype=I32)[:, None] + 1) * chunks_per_tile + padoff_ch).T.reshape(-1)
    vals = jnp.stack([start, base, nch.T.reshape(-1)], axis=1)
    delta = vals - jnp.concatenate([jnp.zeros((1, 3), I32), vals[:-1]], axis=0)
    digits = jnp.concatenate([delta // LANES, delta % LANES], axis=1).astype(BF16)
    slot = jnp.arange(n_blocks * NCH, dtype=I32)
    started = (start[None, :] <= slot[:, None]).astype(BF16)
    got = jnp.dot(started, digits, preferred_element_type=F32).astype(I32)
    seg = got[:, :3] * LANES + got[:, 3:]
    j = slot - seg[:, 0]
    valid = (j < seg[:, 2]) & (slot < eend[-1])
    zero_chunk = chunks_per_tile - 1
    src = jnp.where(valid, seg[:, 1] + j, zero_chunk).astype(I32)
    blk = slot // NCH
    scratch = (blk % 2) * NCH + slot % NCH
    dst = jnp.where(valid, src, scratch).astype(I32)
    first = jnp.arange(n_blocks, dtype=I32) * NCH
    bexp = jnp.minimum(jnp.sum(eend[None, :] <= first[:, None], axis=1), N_EXPERTS - 1).astype(I32)
    nused = (eend[-1] // NCH).astype(I32).reshape(1)
    ids = jnp.arange(N_EXPERTS, dtype=I32)
    later = jnp.where((ids[None, :] > ids[:, None]) & (totpad > 0)[None, :], ids[None, :], N_EXPERTS)
    next_of = jnp.min(later, axis=1)
    next_of = jnp.where(next_of == N_EXPERTS, ids, next_of)
    enext = jnp.sum(jnp.where(bexp[:, None] == ids[None, :], next_of[None, :], 0), axis=1).astype(I32)
    return src, dst, bexp, enext, nused


def _t5_bucket(rel):
    nb = N_BUCKETS // 2
    max_exact = nb // 2
    ret = jnp.where(rel > 0, nb, 0)
    n = jnp.abs(rel)
    nf = jnp.maximum(n, 1).astype(jnp.float32)
    large = max_exact + (jnp.log(nf / max_exact) / math.log(MAX_DISTANCE / max_exact)
                         * (nb - max_exact)).astype(jnp.int32)
    large = jnp.minimum(large, nb - 1)
    return ret + jnp.where(n < max_exact, n, large)


def _band_bias(rel_bias):
    qi = jnp.arange(WINDOW)
    kj = jnp.arange(3 * WINDOW)
    rel = kj[None, :] - WINDOW - qi[:, None]
    pick = _t5_bucket(rel)[:, :, None, None] == jnp.arange(N_BUCKETS)[None, None, :, None]
    bias = jnp.sum(jnp.where(pick, rel_bias.astype(F32)[None, None], 0.0), axis=2) * LOG2E
    bias = jnp.where((jnp.abs(rel) <= WINDOW)[:, :, None], bias, NEG_INF)
    grp = B_HEADS // B_KV_HEADS
    bias = jnp.transpose(bias, (2, 1, 0)).reshape(B_KV_HEADS, grp, 3 * WINDOW, WINDOW)
    return jnp.transpose(bias, (0, 2, 1, 3)).reshape(B_KV_HEADS, 3 * WINDOW, grp * WINDOW)


def kernel(x, c, ada_w, ada_b, w_in, b_in, gmlp_ln_g, gmlp_ln_b, gmlp_ws, gmlp_bs, attn_sink, conv_w,
           conv_b, conv_ln_g, conv_ln_b, w_out, b_out, ln_mix_g, ln_mix_b, w_gate, w_up, w_down,
           ln_ffn_g, ln_ffn_b, rel_bias, router_w, router_bias):
    bsz, seq, d = x.shape
    n_layers = ada_w.shape[0]
    t = bsz * seq
    nt = seq // TT
    n_tiles = t // TT
    max_chunks = n_tiles * (TOP_K * TT // CH + N_EXPERTS) + N_EXPERTS * (NCH - 1)
    n_blocks = -(-max_chunks // NCH)

    mod = _ada_call(c, ada_w, ada_b).reshape(n_layers, bsz, 6, d)
    bias = _band_bias(rel_bias)
    rw = router_w.T.astype(BF16)
    rbias = router_bias.astype(F32).reshape(N_EXPERTS, 1)
    row = lambda a: a.reshape(1, -1)

    for l in range(n_layers):
        bsb = jnp.repeat(gmlp_bs[l].T, HEAD_DIM, axis=1)
        ya, qt, k, vt, yc = _inproj_call(
            x, mod, l, w_in[l], b_in[l], row(gmlp_ln_g[l]), row(gmlp_ln_b[l]),
            gmlp_ws[l].astype(BF16), bsb, conv_w[l], row(conv_b[l]), row(conv_ln_g[l]), row(conv_ln_b[l]))
        sink = jnp.repeat(attn_sink[l].astype(F32) * LOG2E, WINDOW).reshape(B_KV_HEADS, -1)
        x1, xs, pos, gate, cntb = _mixer_call(
            seq, qt, k, vt, bias, sink, yc, ya, x, mod, l, w_out[l].astype(BF16),
            row(b_out[l]), row(ln_mix_g[l]), row(ln_mix_b[l]), rw, rbias)
        cnt = cntb[:, :, 0].astype(I32)
        src, dst, bexp, enext, nused = _dispatch_plan(cnt, n_blocks)
        ys = _expert_call(src, dst, bexp, enext, nused, xs, l, w_gate, w_up, w_down)
        x = _combine_call(ys, pos, gate, x1.reshape(t, d), mod, l, nt, row(ln_ffn_g[l]),
                          row(ln_ffn_b[l])).reshape(bsz, seq, d)
    return x
```

```python
import functools
import math

import jax
import jax.numpy as jnp
from jax import lax
from jax.experimental import pallas as pl
from jax.experimental.pallas import tpu as pltpu

F32 = jnp.float32
BF16 = jnp.bfloat16
I32 = jnp.int32

D_MODEL = 1024
DEPTH = 2
HEAD_DIM = 64
A_WIDTH = 256
A_HEADS = 4
CHUNK = 128
B_WIDTH = 512
B_HEADS = 8
B_KV_HEADS = 2
KV_WIDTH = B_KV_HEADS * HEAD_DIM
WINDOW = 128
N_BUCKETS = 32
MAX_DISTANCE = 128
C_WIDTH = 256
CONV_WIDTH = 31
CONV_PAD = CONV_WIDTH // 2
IN_WIDTH = 2 * A_WIDTH + B_WIDTH + 2 * KV_WIDTH + 2 * C_WIDTH
N_EXPERTS = 32
N_GROUPS = 4
EXPERTS_PER_GROUP = N_EXPERTS // N_GROUPS
TOP_K = 2
D_EXPERT = D_MODEL // 2
ALPHA = (2 * DEPTH) ** 0.25
LN_EPS = 1e-5
NEG_INF = -1e30
LOG2E = 1.4426950408889634
Q_SCALE = HEAD_DIM ** -0.5 * LOG2E

LANES = 128
SUBLANES = 8
BF16_SUBLANES = 16
VMEM_LIMIT = 48 * 1024 * 1024

ADA_TN = 1536
TS = 1024
TT = 512
CH = BF16_SUBLANES
R_TILE = 1536
BM = 512
NCH = BM // CH
N_XRING = 6
N_YRING = 4
N_SORT_BLOCKS = 6
N_SBUF = 2
HALO = 16

assert R_TILE >= TOP_K * TT + N_EXPERTS * (CH - 1) + CH
assert R_TILE >= 2 * NCH * CH + CH


def _sigmoid(x):
    return 1.0 / (1.0 + jnp.exp(-x))


def _gelu_tanh(x):
    return x * (0.5 * (1.0 + jnp.tanh(0.7978845608028654 * (x + 0.044715 * (x * x * x)))))


def _ln(x, g, b):
    mu = jnp.mean(x, axis=-1, keepdims=True)
    xc = x - mu
    var = jnp.mean(xc * xc, axis=-1, keepdims=True)
    return xc * lax.rsqrt(var + LN_EPS) * g + b


def _dot(a, b):
    return jnp.dot(a, b, preferred_element_type=F32)


def _dot_nt(a, b):
    return lax.dot_general(a, b, (((1,), (1,)), ((), ())), preferred_element_type=F32)


def _ada_kernel(c_ref, w_ref, b_ref, o_ref):
    c = c_ref[...]
    s = (c * _sigmoid(c)).astype(BF16)
    o_ref[0] = _dot(s, w_ref[0].astype(BF16)) + b_ref[0]


def _ada_call(c, ada_w, ada_b):
    nl, d, n = ada_w.shape
    bsz = c.shape[0]
    return pl.pallas_call(
        _ada_kernel,
        out_shape=jax.ShapeDtypeStruct((nl, bsz, n), F32),
        grid=(nl, n // ADA_TN),
        in_specs=[
            pl.BlockSpec((bsz, d), lambda l, j: (0, 0)),
            pl.BlockSpec((1, d, ADA_TN), lambda l, j: (l, 0, j)),
            pl.BlockSpec((1, 1, ADA_TN), lambda l, j: (l, 0, j)),
        ],
        out_specs=pl.BlockSpec((1, bsz, ADA_TN), lambda l, j: (l, 0, j)),
        compiler_params=pltpu.CompilerParams(
            dimension_semantics=("arbitrary", "arbitrary"), vmem_limit_bytes=VMEM_LIMIT),
        name="ada_mod",
    )(c, ada_w, ada_b.reshape(nl, 1, n))


def _inproj_kernel(x_ref, xp_ref, xn_ref, mod_ref, w_ref, b_ref, wqt_ref, bq_ref, wvt_ref, bv_ref,
                   lng_ref, lnb_ref, ws_ref, bsb_ref, cw_ref, cb_ref, clg_ref, clb_ref,
                   ya_ref, qt_ref, k_ref, vt_ref, yc_ref, conv_scr, z_scr):
    i = pl.program_id(1)
    n_i = pl.num_programs(1)
    m = mod_ref[...]

    def modulate(xv):
        return (xv * (1.0 + m[1:2]) + m[0:1]).astype(BF16)

    hb = modulate(x_ref[0])
    col_u, col_v = 0, A_WIDTH
    col_k = 2 * A_WIDTH + B_WIDTH
    col_a = col_k + 2 * KV_WIDTH
    col_g = col_a + C_WIDTH

    def proj(lhs, c0, width):
        return _dot(lhs, w_ref[:, c0:c0 + width]) + b_ref[:, c0:c0 + width]

    hx = jnp.concatenate([modulate(xp_ref[0]), hb, modulate(xn_ref[0])], axis=0)
    yg = proj(hx, col_a, C_WIDTH) * _sigmoid(proj(hx, col_g, C_WIDTH))
    conv_scr[0:HALO, :] = yg[0:HALO] * jnp.where(i > 0, 1.0, 0.0)
    conv_scr[HALO:HALO + TS, :] = yg[HALO:HALO + TS]
    conv_scr[HALO + TS:, :] = yg[HALO + TS:] * jnp.where(i < n_i - 1, 1.0, 0.0)
    first = HALO - CONV_PAD
    acc = jnp.zeros((TS, C_WIDTH), F32) + cb_ref[...]
    for r in range(SUBLANES):
        z = None
        for a in range(-(-(first + CONV_WIDTH) // SUBLANES)):
            w = a * SUBLANES + r - first
            if 0 <= w < CONV_WIDTH:
                term = conv_scr[a * SUBLANES:a * SUBLANES + TS + SUBLANES, :] * cw_ref[w:w + 1, :]
                z = term if z is None else z + term
        if r == 0:
            acc = acc + z[0:TS]
        else:
            z_scr[r] = z
            acc = acc + z_scr[r, r:r + TS, :]
    yc = _ln(acc, clg_ref[...], clb_ref[...])
    yc_ref[0] = (yc * _sigmoid(yc)).astype(BF16)

    u = _gelu_tanh(proj(hb, col_u, A_WIDTH))
    v = _gelu_tanh(proj(hb, col_v, A_WIDTH))
    qt_ref[0] = ((_dot_nt(wqt_ref[...], hb) + bq_ref[...]) * Q_SCALE).astype(BF16)
    k_ref[0] = proj(hb, col_k, KV_WIDTH).astype(BF16)
    vt_ref[0] = (_dot_nt(wvt_ref[...], hb) + bv_ref[...]).astype(BF16)

    vb = _ln(v, lng_ref[...], lnb_ref[...]).astype(BF16)
    head_of_lane = lax.broadcasted_iota(I32, (CHUNK, A_WIDTH), 1) // HEAD_DIM
    for ch in range(TS // CHUNK):
        vc = vb[ch * CHUNK:(ch + 1) * CHUNK]
        acc = bsb_ref[...]
        for hh in range(A_HEADS):
            acc = acc + _dot(ws_ref[hh], jnp.where(head_of_lane == hh, vc, jnp.zeros_like(vc)))
        ya_ref[0, ch * CHUNK:(ch + 1) * CHUNK, :] = (u[ch * CHUNK:(ch + 1) * CHUNK] * acc).astype(BF16)


def _inproj_call(x, mod, layer, w_in, b_in, ln_g, ln_b, ws, bsb, conv_w, conv_b, conv_ln_g, conv_ln_b):
    bsz, seq, d = x.shape
    grid = (bsz, seq // TS)
    hb = TS // HALO
    const2 = lambda b, i: (0, 0)
    row = lambda b, i: (b, i, 0)
    colblk = lambda b, i: (b, 0, i)
    prev_h = lambda b, i: (b, jnp.maximum(i * hb - 1, 0), 0)
    next_h = lambda b, i: (b, jnp.minimum((i + 1) * hb, seq // HALO - 1), 0)
    q0 = 2 * A_WIDTH
    v0 = q0 + B_WIDTH + KV_WIDTH
    wb = w_in.astype(BF16)
    wqt = w_in[:, q0:q0 + B_WIDTH].T.astype(BF16)
    wvt = w_in[:, v0:v0 + KV_WIDTH].T.astype(BF16)
    bq = b_in[q0:q0 + B_WIDTH].reshape(B_WIDTH, 1)
    bv = b_in[v0:v0 + KV_WIDTH].reshape(KV_WIDTH, 1)

    def out(width):
        return jax.ShapeDtypeStruct((bsz, seq, width), BF16), pl.BlockSpec((1, TS, width), row)

    def out_t(width):
        return jax.ShapeDtypeStruct((bsz, width, seq), BF16), pl.BlockSpec((1, width, TS), colblk)

    outs = [out(A_WIDTH), out_t(B_WIDTH), out(KV_WIDTH), out_t(KV_WIDTH), out(C_WIDTH)]
    return pl.pallas_call(
        _inproj_kernel,
        out_shape=[o[0] for o in outs],
        grid=grid,
        in_specs=[
            pl.BlockSpec((1, TS, d), row),
            pl.BlockSpec((1, HALO, d), prev_h),
            pl.BlockSpec((1, HALO, d), next_h),
            pl.BlockSpec((None, None, 6, d), lambda b, i: (layer, b, 0, 0)),
            pl.BlockSpec((d, IN_WIDTH), const2),
            pl.BlockSpec((1, IN_WIDTH), const2),
            pl.BlockSpec((B_WIDTH, d), const2),
            pl.BlockSpec((B_WIDTH, 1), const2),
            pl.BlockSpec((KV_WIDTH, d), const2),
            pl.BlockSpec((KV_WIDTH, 1), const2),
            pl.BlockSpec((1, A_WIDTH), const2),
            pl.BlockSpec((1, A_WIDTH), const2),
            pl.BlockSpec((A_HEADS, CHUNK, CHUNK), lambda b, i: (0, 0, 0)),
            pl.BlockSpec((CHUNK, A_WIDTH), const2),
            pl.BlockSpec((CONV_WIDTH, C_WIDTH), const2),
            pl.BlockSpec((1, C_WIDTH), const2),
            pl.BlockSpec((1, C_WIDTH), const2),
            pl.BlockSpec((1, C_WIDTH), const2),
        ],
        out_specs=[o[1] for o in outs],
        scratch_shapes=[
            pltpu.VMEM((TS + 2 * HALO, C_WIDTH), F32),
            pltpu.VMEM((SUBLANES, TS + SUBLANES, C_WIDTH), F32),
        ],
        compiler_params=pltpu.CompilerParams(
            dimension_semantics=("arbitrary", "arbitrary"), vmem_limit_bytes=VMEM_LIMIT),
        name="inproj_gmlp_conv",
    )(x, x, x, mod, wb, b_in.reshape(1, -1), wqt, bq, wvt, bv, ln_g, ln_b, ws, bsb,
      conv_w, conv_b, conv_ln_g, conv_ln_b)


def _first_argmax(vals, iota_f, width):
    m = jnp.max(vals, axis=0, keepdims=True)
    idx = jnp.min(jnp.where(vals == m, iota_f, float(width)), axis=0, keepdims=True)
    return m, idx


def _mixer_kernel(seq_len, tiles_per_seq, qt_ref, kp_ref, kc_ref, kn_ref, vtp_ref, vtc_ref, vtn_ref,
                  bias_ref, sink_ref, yc_ref, ya_ref, x_ref, mod_ref, wo_ref, bo_ref, lng_ref, lnb_ref,
                  rw_ref, rb_ref, lt_ref, before_ref, x1_ref, xs_ref, pos_ref, gate_ref, cnt_ref,
                  ot_scr, s_scr, h2_scr, pos_scr):
    n = pl.program_id(0)
    last_tile = pl.num_programs(0) - 2
    i = jnp.minimum(n, last_tile) % tiles_per_seq
    t0 = i * TT
    m = mod_ref[...]

    @pl.when(n == 0)
    def _():
        h2_scr[...] = jnp.zeros_like(h2_scr)
        pos_scr[...] = jnp.zeros_like(pos_scr)

    pos_prev = pos_scr[...]
    sort_rows = R_TILE // N_SORT_BLOCKS

    def sort_block(j):
        iota_r = j * sort_rows + lax.broadcasted_iota(I32, (sort_rows, TT), 0)
        onehot = jnp.where((iota_r == pos_prev[0:1]) | (iota_r == pos_prev[1:2]), 1.0, 0.0).astype(BF16)
        xs_ref[j * sort_rows:(j + 1) * sort_rows, :] = _dot(onehot, h2_scr[...]).astype(BF16)

    def out_proj(yb):
        return (_dot(ya_ref[0], wo_ref[0:A_WIDTH, :])
                + _dot(yb, wo_ref[A_WIDTH:A_WIDTH + B_WIDTH, :])
                + _dot(yc_ref[0], wo_ref[A_WIDTH + B_WIDTH:, :]) + bo_ref[...])

    def route(y):
        x1 = _ln(ALPHA * x_ref[0] + (1.0 + m[2:3]) * y, lng_ref[...], lnb_ref[...])
        x1_ref[0] = x1
        h2 = (x1 * (1.0 + m[4:5]) + m[3:4]).astype(BF16)
        scores = _sigmoid(_dot_nt(rw_ref[...], h2))
        sel = scores + rb_ref[...]
        iota_f = lax.broadcasted_iota(I32, (EXPERTS_PER_GROUP, TT), 0).astype(F32)
        best = None
        for g in range(N_GROUPS):
            sl = slice(g * EXPERTS_PER_GROUP, (g + 1) * EXPERTS_PER_GROUP)
            sg = sel[sl]
            m1, i1 = _first_argmax(sg, iota_f, EXPERTS_PER_GROUP)
            m2, i2 = _first_argmax(jnp.where(iota_f == i1, -jnp.inf, sg), iota_f, EXPERTS_PER_GROUP)
            sc = scores[sl]
            s1 = jnp.sum(jnp.where(iota_f == i1, sc, 0.0), axis=0, keepdims=True)
            s2 = jnp.sum(jnp.where(iota_f == i2, sc, 0.0), axis=0, keepdims=True)
            cand = (m1 + m2, i1 + g * EXPERTS_PER_GROUP, i2 + g * EXPERTS_PER_GROUP, s1, s2)
            if best is None:
                best = cand
            else:
                take = cand[0] > best[0]
                best = tuple(jnp.where(take, c, b) for c, b in zip(cand, best))
        _, e1, e2, s1, s2 = best
        gate_ref[0] = jnp.concatenate([s1, s2], axis=0) / (s1 + s2)

        iota_e = lax.broadcasted_iota(I32, (N_EXPERTS, TT), 0).astype(F32)
        in0 = iota_e == e1
        in1 = iota_e == e2
        member = jnp.where(in0 | in1, 1.0, 0.0)
        cnt = jnp.sum(member, axis=1, keepdims=True)
        cnt_ref[0] = jnp.broadcast_to(cnt, (N_EXPERTS, LANES))
        nch = jnp.floor((cnt + (CH - 1)) * (1.0 / CH))
        nch_pad = jnp.concatenate([jnp.broadcast_to(nch, (N_EXPERTS, LANES)),
                                   jnp.zeros((LANES - N_EXPERTS, LANES), F32)], axis=0).astype(BF16)
        padoff = _dot(lt_ref[...], nch_pad)[:, 0:1] * float(CH)
        rank = _dot(member.astype(BF16), before_ref[...])
        posf = padoff + rank
        pos0 = jnp.sum(jnp.where(in0, posf, 0.0), axis=0, keepdims=True).astype(I32)
        pos1 = jnp.sum(jnp.where(in1, posf, 0.0), axis=0, keepdims=True).astype(I32)
        pos = jnp.concatenate([pos0, pos1], axis=0)
        pos_ref[0] = pos
        return pos, h2

    kfull = jnp.concatenate([kp_ref[0], kc_ref[0], kn_ref[0]], axis=0)
    vtfull = jnp.concatenate([vtp_ref[0], vtc_ref[0], vtn_ref[0]], axis=1)
    grp = B_HEADS // B_KV_HEADS
    n_qb = TT // WINDOW
    units = [(jb, g) for jb in range(n_qb) for g in range(B_KV_HEADS)]
    key_i = lax.broadcasted_iota(I32, (3 * WINDOW, 1), 0)

    def scores(u):
        jb, g = units[u]
        kb = kfull[jb * WINDOW:(jb + 3) * WINDOW, g * HEAD_DIM:(g + 1) * HEAD_DIM]
        qt = jnp.concatenate(
            [qt_ref[0, h * HEAD_DIM:(h + 1) * HEAD_DIM, jb * WINDOW:(jb + 1) * WINDOW]
             for h in range(g * grp, (g + 1) * grp)], axis=1)
        s = _dot(kb, qt) + bias_ref[g]
        if jb == 0 or jb == n_qb - 1:
            kpos = t0 + (jb - 1) * WINDOW + key_i
            s = jnp.where((kpos >= 0) & (kpos < seq_len), s, NEG_INF)
        s_scr[u % N_SBUF] = s

    def values(u):
        jb, g = units[u]
        s = s_scr[u % N_SBUF]
        sink = sink_ref[g:g + 1, :]
        mx = jnp.maximum(jnp.max(s, axis=0, keepdims=True), sink)
        p = jnp.exp2(s - mx)
        den = jnp.sum(p, axis=0, keepdims=True) + jnp.exp2(sink - mx)
        vt = vtfull[g * HEAD_DIM:(g + 1) * HEAD_DIM, jb * WINDOW:(jb + 3) * WINDOW]
        ot = _dot(vt, p.astype(BF16)) / den
        for hh in range(grp):
            h = g * grp + hh
            ot_scr[h * HEAD_DIM:(h + 1) * HEAD_DIM, jb * WINDOW:(jb + 1) * WINDOW] = (
                ot[:, hh * WINDOW:(hh + 1) * WINDOW])

    for u in range(N_SBUF - 1):
        scores(u)
    for u in range(len(units)):
        if u + N_SBUF - 1 < len(units):
            scores(u + N_SBUF - 1)
        values(u)
    y = out_proj(jnp.transpose(ot_scr[...]).astype(BF16))
    for j in range(N_SORT_BLOCKS):
        sort_block(j)
    new_pos, new_h2 = route(y)
    pos_scr[...] = new_pos
    h2_scr[...] = new_h2


def _mixer_call(seq_len, qt, k, vt, bias, sink, yc, ya, x, mod, layer, w_out, b_out, ln_g, ln_b, rw, rbias):
    bsz, seq, d = x.shape
    nt = seq // TT
    n_tiles = bsz * nt
    kb = TT // WINDOW
    grp = B_HEADS // B_KV_HEADS
    const2 = lambda n: (0, 0)

    def at_tile(fn):
        def index_map(n):
            t = jnp.minimum(n, n_tiles - 1)
            return fn(t // nt, t % nt)
        return index_map

    row = at_tile(lambda b, i: (b, i, 0))
    colblk = at_tile(lambda b, i: (b, 0, i))
    prev_k = at_tile(lambda b, i: (b, jnp.maximum(i * kb - 1, 0), 0))
    next_k = at_tile(lambda b, i: (b, jnp.minimum((i + 1) * kb, seq // WINDOW - 1), 0))
    prev_v = at_tile(lambda b, i: (b, 0, jnp.maximum(i * kb - 1, 0)))
    next_v = at_tile(lambda b, i: (b, 0, jnp.minimum((i + 1) * kb, seq // WINDOW - 1)))
    tile = at_tile(lambda b, i: (b * nt + i, 0, 0))
    lt = (jnp.arange(LANES)[None, :] < jnp.arange(N_EXPERTS)[:, None]).astype(BF16)
    before = (jnp.arange(TT)[:, None] < jnp.arange(TT)[None, :]).astype(BF16)
    return pl.pallas_call(
        functools.partial(_mixer_kernel, seq_len, nt),
        out_shape=[
            jax.ShapeDtypeStruct((bsz, seq, d), F32),
            jax.ShapeDtypeStruct(((n_tiles + 1) * R_TILE, d), BF16),
            jax.ShapeDtypeStruct((n_tiles, TOP_K, TT), I32),
            jax.ShapeDtypeStruct((n_tiles, TOP_K, TT), F32),
            jax.ShapeDtypeStruct((n_tiles, N_EXPERTS, LANES), F32),
        ],
        grid=(n_tiles + 1,),
        in_specs=[
            pl.BlockSpec((1, B_WIDTH, TT), colblk),
            pl.BlockSpec((1, WINDOW, KV_WIDTH), prev_k),
            pl.BlockSpec((1, TT, KV_WIDTH), row),
            pl.BlockSpec((1, WINDOW, KV_WIDTH), next_k),
            pl.BlockSpec((1, KV_WIDTH, WINDOW), prev_v),
            pl.BlockSpec((1, KV_WIDTH, TT), colblk),
            pl.BlockSpec((1, KV_WIDTH, WINDOW), next_v),
            pl.BlockSpec((B_KV_HEADS, 3 * WINDOW, grp * WINDOW), lambda n: (0, 0, 0)),
            pl.BlockSpec((B_KV_HEADS, grp * WINDOW), const2),
            pl.BlockSpec((1, TT, C_WIDTH), row),
            pl.BlockSpec((1, TT, A_WIDTH), row),
            pl.BlockSpec((1, TT, d), row),
            pl.BlockSpec((None, None, 6, d), at_tile(lambda b, i: (layer, b, 0, 0))),
            pl.BlockSpec((d, d), const2),
            pl.BlockSpec((1, d), const2),
            pl.BlockSpec((1, d), const2),
            pl.BlockSpec((1, d), const2),
            pl.BlockSpec((N_EXPERTS, d), const2),
            pl.BlockSpec((N_EXPERTS, 1), const2),
            pl.BlockSpec((N_EXPERTS, LANES), const2),
            pl.BlockSpec((TT, TT), const2),
        ],
        out_specs=[
            pl.BlockSpec((1, TT, d), row),
            pl.BlockSpec((R_TILE, d), lambda n: (n, 0)),
            pl.BlockSpec((1, TOP_K, TT), tile),
            pl.BlockSpec((1, TOP_K, TT), tile),
            pl.BlockSpec((1, N_EXPERTS, LANES), tile),
        ],
        scratch_shapes=[
            pltpu.VMEM((B_WIDTH, TT), F32),
            pltpu.VMEM((N_SBUF, 3 * WINDOW, grp * WINDOW), F32),
            pltpu.VMEM((TT, d), BF16),
            pltpu.VMEM((TOP_K, TT), I32),
        ],
        compiler_params=pltpu.CompilerParams(
            dimension_semantics=("arbitrary",), vmem_limit_bytes=VMEM_LIMIT),
        name="mixer_router_sort",
    )(qt, k, k, k, vt, vt, vt, bias, sink, yc, ya, x, mod, w_out, b_out, ln_g, ln_b, rw, rbias, lt, before)


def _expert_kernel(layer, src_ref, dst_ref, bexp_ref, enext_ref, nused_ref, xs_hbm, wg_hbm, wu_hbm, wd_hbm,
                   ys_hbm, xbuf, ybuf, wg_st, wu_st, wd_st, wgb, wub, wdb, sem_in, sem_out, sem_w):
    n_blk = bexp_ref.shape[0]
    nused = nused_ref[0]
    chunks_per_tile = R_TILE // CH
    zero_chunk = chunks_per_tile - 1
    scratch_chunk = 0
    de = wgb.shape[2]
    half = de // 2

    def in_copy(chunk, s, c):
        return pltpu.make_async_copy(
            xs_hbm.at[pl.ds(pl.multiple_of(chunk * CH, CH), CH)],
            xbuf.at[s, pl.ds(c * CH, CH)], sem_in.at[s])

    def out_copy(chunk, s, c):
        return pltpu.make_async_copy(
            ybuf.at[s, pl.ds(c * CH, CH)],
            ys_hbm.at[pl.ds(pl.multiple_of(chunk * CH, CH), CH)], sem_out.at[s])

    def weight_copies(e, s):
        return [pltpu.make_async_copy(wg_hbm.at[layer, e], wg_st.at[s], sem_w.at[s]),
                pltpu.make_async_copy(wu_hbm.at[layer, e], wu_st.at[s], sem_w.at[s]),
                pltpu.make_async_copy(wd_hbm.at[layer, e], wd_st.at[s], sem_w.at[s])]

    def gather_chunk(b, c):
        return jnp.where(b < nused, src_ref[jnp.minimum(b, n_blk - 1) * NCH + c], zero_chunk)

    def write_back_chunk(b, c):
        real = (b >= 0) & (b < nused)
        return jnp.where(real, dst_ref[jnp.clip(b, 0, n_blk - 1) * NCH + c], scratch_chunk + (b % 2) * NCH + c)

    ybuf[...] = jnp.zeros_like(ybuf)
    wgb[...] = jnp.zeros_like(wgb)
    wub[...] = jnp.zeros_like(wub)
    wdb[...] = jnp.zeros_like(wdb)
    for b in range(4):
        for c in range(NCH):
            in_copy(gather_chunk(b, c), b, c).start()
    for b in range(2):
        for c in range(NCH):
            out_copy(scratch_chunk + b * NCH + c, b, c).start()

    @pl.when(nused > 0)
    def _():
        for cp in weight_copies(bexp_ref[0], 0):
            cp.start()

    def pair(i, n_changes):
        blocks = (2 * i, 2 * i + 1)
        wslot = []
        for blk in blocks:
            jb = jnp.minimum(blk, n_blk - 1)
            expert = bexp_ref[jb]
            prev_expert = bexp_ref[jnp.maximum(jb - 1, 0)]
            change = (blk < nused) & ((blk == 0) | (expert != prev_expert))

            @pl.when(change)
            def _(expert=expert, jb=jb, n_changes=n_changes):
                ws = n_changes % 2
                for cp in weight_copies(expert, ws):
                    cp.wait()
                wgb[ws] = wg_st[ws].astype(BF16)
                wub[ws] = wu_st[ws].astype(BF16)
                wdb[ws] = wd_st[ws].astype(BF16)
                upcoming = enext_ref[jb]

                @pl.when(upcoming != expert)
                def _():
                    for cp in weight_copies(upcoming, 1 - ws):
                        cp.start()

            n_changes = n_changes + change.astype(I32)
            wslot.append(jnp.maximum(n_changes - 1, 0) % 2)

        for blk in blocks:
            for c in range(NCH):
                in_copy(0, blk % N_XRING, c).wait()
            for c in range(NCH):
                out_copy(0, blk % N_YRING, c).wait()

        dmas = []
        for blk in blocks:
            for c in range(NCH):
                dmas.append(functools.partial(
                    lambda b, c: in_copy(gather_chunk(b, c), b % N_XRING, c).start(priority=1), blk + 4, c))
        for blk in blocks:
            for c in range(NCH):
                dmas.append(functools.partial(
                    lambda b, c: out_copy(write_back_chunk(b, c), b % N_YRING, c).start(), blk - 2, c))
        n_groups = 8
        per_group = len(dmas) // n_groups

        def issue(gidx):
            for start in dmas[gidx * per_group:(gidx + 1) * per_group]:
                start()

        gidx = 0
        for k, blk in enumerate(blocks):
            x = xbuf[blk % N_XRING]
            ws = wslot[k]
            hmid = []
            for h in range(2):
                cols = slice(h * half, (h + 1) * half)
                g = _dot(x, wgb[ws, :, cols])
                issue(gidx)
                u = _dot(x, wub[ws, :, cols])
                issue(gidx + 1)
                gidx += 2
                hmid.append(((g * _sigmoid(g)) * u).astype(BF16))
            y = _dot(hmid[0], wdb[ws, 0:half, :]) + _dot(hmid[1], wdb[ws, half:, :])
            ybuf[blk % N_YRING] = y.astype(BF16)
        return n_changes

    n_pairs = (nused + 1) // 2
    lax.fori_loop(0, n_pairs, pair, jnp.int32(0))

    last = 2 * n_pairs
    for blk in range(4):
        for c in range(NCH):
            in_copy(0, (last + blk) % N_XRING, c).wait()
    for blk in (last, last + 1):
        for c in range(NCH):
            out_copy(0, blk % N_YRING, c).wait()
    for blk in (last - 2, last - 1):
        for c in range(NCH):
            out_copy(write_back_chunk(blk, c), blk % N_YRING, c).start()
    for blk in (last - 2, last - 1):
        for c in range(NCH):
            out_copy(0, blk % N_YRING, c).wait()


def _expert_call(src, dst, bexp, enext, nused, xs, layer, w_gate, w_up, w_down):
    d = xs.shape[-1]
    de = w_gate.shape[-1]
    grid_spec = pltpu.PrefetchScalarGridSpec(
        num_scalar_prefetch=5,
        grid=(1,),
        in_specs=[pl.BlockSpec(memory_space=pl.ANY)] * 4,
        out_specs=pl.BlockSpec(memory_space=pl.ANY),
        scratch_shapes=[
            pltpu.VMEM((N_XRING, BM, d), BF16),
            pltpu.VMEM((N_YRING, BM, d), BF16),
            pltpu.VMEM((2, d, de), F32),
            pltpu.VMEM((2, d, de), F32),
            pltpu.VMEM((2, de, d), F32),
            pltpu.VMEM((2, d, de), BF16),
            pltpu.VMEM((2, d, de), BF16),
            pltpu.VMEM((2, de, d), BF16),
            pltpu.SemaphoreType.DMA((N_XRING,)),
            pltpu.SemaphoreType.DMA((N_YRING,)),
            pltpu.SemaphoreType.DMA((2,)),
        ],
    )
    return pl.pallas_call(
        functools.partial(_expert_kernel, layer),
        out_shape=jax.ShapeDtypeStruct(xs.shape, xs.dtype),
        grid_spec=grid_spec,
        input_output_aliases={5: 0},
        compiler_params=pltpu.CompilerParams(
            dimension_semantics=("arbitrary",), vmem_limit_bytes=VMEM_LIMIT),
        name="moe_experts",
    )(src, dst, bexp, enext, nused, xs, w_gate, w_up, w_down)


def _combine_kernel(ys_ref, pos_ref, gate_ref, x1_ref, mod_ref, lng_ref, lnb_ref, o_ref):
    pos = pos_ref[0]
    gate = gate_ref[0]
    m = mod_ref[...]
    iota_r = lax.broadcasted_iota(I32, (R_TILE, TT), 0)
    row_gate = (jnp.where(iota_r == pos[0:1], gate[0:1], 0.0)
                + jnp.where(iota_r == pos[1:2], gate[1:2], 0.0))
    gs = jnp.sum(row_gate, axis=1, keepdims=True)
    ysc = (ys_ref[...].astype(F32) * gs).astype(BF16)
    posc = jnp.transpose(pos.astype(F32))
    iota_c = lax.broadcasted_iota(I32, (TT, R_TILE), 1).astype(F32)
    pick = jnp.where((iota_c == posc[:, 0:1]) | (iota_c == posc[:, 1:2]), 1.0, 0.0).astype(BF16)
    y = _dot(pick, ysc)
    o_ref[...] = _ln(ALPHA * x1_ref[...] + (1.0 + m[5:6]) * y, lng_ref[...], lnb_ref[...])


def _combine_call(ys, pos, gate, x1, mod, layer, tiles_per_seq, ln_g, ln_b):
    t, d = x1.shape
    n_tiles = t // TT
    const2 = lambda n: (0, 0)
    return pl.pallas_call(
        _combine_kernel,
        out_shape=jax.ShapeDtypeStruct((t, d), F32),
        grid=(n_tiles,),
        in_specs=[
            pl.BlockSpec((R_TILE, d), lambda n: (n + 1, 0)),
            pl.BlockSpec((1, TOP_K, TT), lambda n: (n, 0, 0)),
            pl.BlockSpec((1, TOP_K, TT), lambda n: (n, 0, 0)),
            pl.BlockSpec((TT, d), lambda n: (n, 0)),
            pl.BlockSpec((None, None, 6, d), lambda n: (layer, n // tiles_per_seq, 0, 0)),
            pl.BlockSpec((1, d), const2),
            pl.BlockSpec((1, d), const2),
        ],
        out_specs=pl.BlockSpec((TT, d), lambda n: (n, 0)),
        compiler_params=pltpu.CompilerParams(
            dimension_semantics=("arbitrary",), vmem_limit_bytes=VMEM_LIMIT),
        name="moe_combine",
    )(ys, pos, gate, x1, mod, ln_g, ln_b)


def _dispatch_plan(cnt, n_blocks):
    n_tiles = cnt.shape[0]
    chunks_per_tile = R_TILE // CH
    nch = (cnt + CH - 1) // CH
    padoff_ch = jnp.cumsum(nch, axis=1) - nch
    tot = jnp.sum(nch, axis=0)
    totpad = (tot + NCH - 1) // NCH * NCH
    eend = jnp.cumsum(totpad)
    ebase = eend - totpad
    tbase = jnp.cumsum(nch, axis=0) - nch
    start = (ebase[None, :] + tbase).T.reshape(-1)
    base = ((jnp.arange(n_tiles, dtype=I32)[:, None] + 1) * chunks_per_tile + padoff_ch).T.reshape(-1)
    vals = jnp.stack([start, base, nch.T.reshape(-1)], axis=1)
    delta = vals - jnp.concatenate([jnp.zeros((1, 3), I32), vals[:-1]], axis=0)
    digits = jnp.concatenate([delta // LANES, delta % LANES], axis=1).astype(BF16)
    slot = jnp.arange(n_blocks * NCH, dtype=I32)
    started = (start[None, :] <= slot[:, None]).astype(BF16)
    got = jnp.dot(started, digits, preferred_element_type=F32).astype(I32)
    seg = got[:, :3] * LANES + got[:, 3:]
    j = slot - seg[:, 0]
    valid = (j < seg[:, 2]) & (slot < eend[-1])
    zero_chunk = chunks_per_tile - 1
    src = jnp.where(valid, seg[:, 1] + j, zero_chunk).astype(I32)
    blk = slot // NCH
    scratch = (blk % 2) * NCH + slot % NCH
    dst = jnp.where(valid, src, scratch).astype(I32)
    first = jnp.arange(n_blocks, dtype=I32) * NCH
    bexp = jnp.minimum(jnp.sum(eend[None, :] <= first[:, None], axis=1), N_EXPERTS - 1).astype(I32)
    nused = (eend[-1] // NCH).astype(I32).reshape(1)
    ids = jnp.arange(N_EXPERTS, dtype=I32)
    later = jnp.where((ids[None, :] > ids[:, None]) & (totpad > 0)[None, :], ids[None, :], N_EXPERTS)
    next_of = jnp.min(later, axis=1)
    next_of = jnp.where(next_of == N_EXPERTS, ids, next_of)
    enext = jnp.sum(jnp.where(bexp[:, None] == ids[None, :], next_of[None, :], 0), axis=1).astype(I32)
    return src, dst, bexp, enext, nused


def _t5_bucket(rel):
    nb = N_BUCKETS // 2
    max_exact = nb // 2
    ret = jnp.where(rel > 0, nb, 0)
    n = jnp.abs(rel)
    nf = jnp.maximum(n, 1).astype(jnp.float32)
    large = max_exact + (jnp.log(nf / max_exact) / math.log(MAX_DISTANCE / max_exact)
                         * (nb - max_exact)).astype(jnp.int32)
    large = jnp.minimum(large, nb - 1)
    return ret + jnp.where(n < max_exact, n, large)


def _band_bias(rel_bias):
    qi = jnp.arange(WINDOW)
    kj = jnp.arange(3 * WINDOW)
    rel = kj[None, :] - WINDOW - qi[:, None]
    pick = _t5_bucket(rel)[:, :, None, None] == jnp.arange(N_BUCKETS)[None, None, :, None]
    bias = jnp.sum(jnp.where(pick, rel_bias.astype(F32)[None, None], 0.0), axis=2) * LOG2E
    bias = jnp.where((jnp.abs(rel) <= WINDOW)[:, :, None], bias, NEG_INF)
    grp = B_HEADS // B_KV_HEADS
    bias = jnp.transpose(bias, (2, 1, 0)).reshape(B_KV_HEADS, grp, 3 * WINDOW, WINDOW)
    return jnp.transpose(bias, (0, 2, 1, 3)).reshape(B_KV_HEADS, 3 * WINDOW, grp * WINDOW)


def kernel(x, c, ada_w, ada_b, w_in, b_in, gmlp_ln_g, gmlp_ln_b, gmlp_ws, gmlp_bs, attn_sink, conv_w,
           conv_b, conv_ln_g, conv_ln_b, w_out, b_out, ln_mix_g, ln_mix_b, w_gate, w_up, w_down,
           ln_ffn_g, ln_ffn_b, rel_bias, router_w, router_bias):
    bsz, seq, d = x.shape
    n_layers = ada_w.shape[0]
    t = bsz * seq
    nt = seq // TT
    n_tiles = t // TT
    max_chunks = n_tiles * (TOP_K * TT // CH + N_EXPERTS) + N_EXPERTS * (NCH - 1)
    n_blocks = -(-max_chunks // NCH)

    mod = _ada_call(c, ada_w, ada_b).reshape(n_layers, bsz, 6, d)
    bias = _band_bias(rel_bias)
    rw = router_w.T.astype(BF16)
    rbias = router_bias.astype(F32).reshape(N_EXPERTS, 1)
    row = lambda a: a.reshape(1, -1)

    for l in range(n_layers):
        bsb = jnp.repeat(gmlp_bs[l].T, HEAD_DIM, axis=1)
        ya, qt, k, vt, yc = _inproj_call(
            x, mod, l, w_in[l], b_in[l], row(gmlp_ln_g[l]), row(gmlp_ln_b[l]),
            gmlp_ws[l].astype(BF16), bsb, conv_w[l], row(conv_b[l]), row(conv_ln_g[l]), row(conv_ln_b[l]))
        sink = jnp.repeat(attn_sink[l].astype(F32) * LOG2E, WINDOW).reshape(B_KV_HEADS, -1)
        x1, xs, pos, gate, cntb = _mixer_call(
            seq, qt, k, vt, bias, sink, yc, ya, x, mod, l, w_out[l].astype(BF16),
            row(b_out[l]), row(ln_mix_g[l]), row(ln_mix_b[l]), rw, rbias)
        cnt = cntb[:, :, 0].astype(I32)
        src, dst, bexp, enext, nused = _dispatch_plan(cnt, n_blocks)
        ys = _expert_call(src, dst, bexp, enext, nused, xs, l, w_gate, w_up, w_down)
        x = _combine_call(ys, pos, gate, x1.reshape(t, d), mod, l, nt, row(ln_ffn_g[l]),
                          row(ln_ffn_b[l])).reshape(bsz, seq, d)
    return x
```

```python
import functools
import math

import jax
import jax.numpy as jnp
from jax import lax
from jax.experimental import pallas as pl
from jax.experimental.pallas import tpu as pltpu

F32 = jnp.float32
BF16 = jnp.bfloat16
I32 = jnp.int32

D_MODEL = 1024
DEPTH = 2
HEAD_DIM = 64
A_WIDTH = 256
A_HEADS = 4
CHUNK = 128
B_WIDTH = 512
B_HEADS = 8
B_KV_HEADS = 2
KV_WIDTH = B_KV_HEADS * HEAD_DIM
WINDOW = 128
N_BUCKETS = 32
MAX_DISTANCE = 128
C_WIDTH = 256
CONV_WIDTH = 31
CONV_PAD = CONV_WIDTH // 2
IN_WIDTH = 2 * A_WIDTH + B_WIDTH + 2 * KV_WIDTH + 2 * C_WIDTH
N_EXPERTS = 32
N_GROUPS = 4
EXPERTS_PER_GROUP = N_EXPERTS // N_GROUPS
TOP_K = 2
D_EXPERT = D_MODEL // 2
ALPHA = (2 * DEPTH) ** 0.25
LN_EPS = 1e-5
NEG_INF = -1e30
LOG2E = 1.4426950408889634
Q_SCALE = HEAD_DIM ** -0.5 * LOG2E

LANES = 128
SUBLANES = 8
BF16_SUBLANES = 16
VMEM_LIMIT = 48 * 1024 * 1024

ADA_TN = 1536
TS = 1024
TT = 512
CH = BF16_SUBLANES
R_TILE = 1536
BM = 512
NCH = BM // CH
N_XRING = 6
N_YRING = 4
N_SORT_BLOCKS = 6
N_SBUF = 2
HALO = 16

assert R_TILE >= TOP_K * TT + N_EXPERTS * (CH - 1) + CH
assert R_TILE >= 2 * NCH * CH + CH


def _sigmoid(x):
    return 1.0 / (1.0 + jnp.exp(-x))


def _gelu_tanh(x):
    return x * (0.5 * (1.0 + jnp.tanh(0.7978845608028654 * (x + 0.044715 * (x * x * x)))))


def _ln(x, g, b):
    mu = jnp.mean(x, axis=-1, keepdims=True)
    xc = x - mu
    var = jnp.mean(xc * xc, axis=-1, keepdims=True)
    return xc * lax.rsqrt(var + LN_EPS) * g + b


def _dot(a, b):
    return jnp.dot(a, b, preferred_element_type=F32)


def _dot_nt(a, b):
    return lax.dot_general(a, b, (((1,), (1,)), ((), ())), preferred_element_type=F32)


def _ada_kernel(c_ref, w_ref, b_ref, o_ref):
    c = c_ref[...]
    s = (c * _sigmoid(c)).astype(BF16)
    o_ref[0] = _dot(s, w_ref[0].astype(BF16)) + b_ref[0]


def _ada_call(c, ada_w, ada_b):
    nl, d, n = ada_w.shape
    bsz = c.shape[0]
    return pl.pallas_call(
        _ada_kernel,
        out_shape=jax.ShapeDtypeStruct((nl, bsz, n), F32),
        grid=(nl, n // ADA_TN),
        in_specs=[
            pl.BlockSpec((bsz, d), lambda l, j: (0, 0)),
            pl.BlockSpec((1, d, ADA_TN), lambda l, j: (l, 0, j)),
            pl.BlockSpec((1, 1, ADA_TN), lambda l, j: (l, 0, j)),
        ],
        out_specs=pl.BlockSpec((1, bsz, ADA_TN), lambda l, j: (l, 0, j)),
        compiler_params=pltpu.CompilerParams(
            dimension_semantics=("arbitrary", "arbitrary"), vmem_limit_bytes=VMEM_LIMIT),
        name="ada_mod",
    )(c, ada_w, ada_b.reshape(nl, 1, n))


def _inproj_kernel(x_ref, xp_ref, xn_ref, mod_ref, w_ref, b_ref, wqt_ref, bq_ref, wvt_ref, bv_ref,
                   lng_ref, lnb_ref, ws_ref, bsb_ref, cw_ref, cb_ref, clg_ref, clb_ref,
                   ya_ref, qt_ref, k_ref, vt_ref, yc_ref, conv_scr, z_scr):
    i = pl.program_id(1)
    n_i = pl.num_programs(1)
    m = mod_ref[...]

    def modulate(xv):
        return (xv * (1.0 + m[1:2]) + m[0:1]).astype(BF16)

    hb = modulate(x_ref[0])
    col_u, col_v = 0, A_WIDTH
    col_k = 2 * A_WIDTH + B_WIDTH
    col_a = col_k + 2 * KV_WIDTH
    col_g = col_a + C_WIDTH

    def proj(lhs, c0, width):
        return _dot(lhs, w_ref[:, c0:c0 + width]) + b_ref[:, c0:c0 + width]

    hx = jnp.concatenate([modulate(xp_ref[0]), hb, modulate(xn_ref[0])], axis=0)
    yg = proj(hx, col_a, C_WIDTH) * _sigmoid(proj(hx, col_g, C_WIDTH))
    conv_scr[0:HALO, :] = yg[0:HALO] * jnp.where(i > 0, 1.0, 0.0)
    conv_scr[HALO:HALO + TS, :] = yg[HALO:HALO + TS]
    conv_scr[HALO + TS:, :] = yg[HALO + TS:] * jnp.where(i < n_i - 1, 1.0, 0.0)
    first = HALO - CONV_PAD
    acc = jnp.zeros((TS, C_WIDTH), F32) + cb_ref[...]
    for r in range(SUBLANES):
        z = None
        for a in range(-(-(first + CONV_WIDTH) // SUBLANES)):
            w = a * SUBLANES + r - first
            if 0 <= w < CONV_WIDTH:
                term = conv_scr[a * SUBLANES:a * SUBLANES + TS + SUBLANES, :] * cw_ref[w:w + 1, :]
                z = term if z is None else z + term
        if r == 0:
            acc = acc + z[0:TS]
        else:
            z_scr[r] = z
            acc = acc + z_scr[r, r:r + TS, :]
    yc = _ln(acc, clg_ref[...], clb_ref[...])
    yc_ref[0] = (yc * _sigmoid(yc)).astype(BF16)

    u = _gelu_tanh(proj(hb, col_u, A_WIDTH))
    v = _gelu_tanh(proj(hb, col_v, A_WIDTH))
    qt_ref[0] = ((_dot_nt(wqt_ref[...], hb) + bq_ref[...]) * Q_SCALE).astype(BF16)
    k_ref[0] = proj(hb, col_k, KV_WIDTH).astype(BF16)
    vt_ref[0] = (_dot_nt(wvt_ref[...], hb) + bv_ref[...]).astype(BF16)

    vb = _ln(v, lng_ref[...], lnb_ref[...]).astype(BF16)
    head_of_lane = lax.broadcasted_iota(I32, (CHUNK, A_WIDTH), 1) // HEAD_DIM
    for ch in range(TS // CHUNK):
        vc = vb[ch * CHUNK:(ch + 1) * CHUNK]
        acc = bsb_ref[...]
        for hh in range(A_HEADS):
            acc = acc + _dot(ws_ref[hh], jnp.where(head_of_lane == hh, vc, jnp.zeros_like(vc)))
        ya_ref[0, ch * CHUNK:(ch + 1) * CHUNK, :] = (u[ch * CHUNK:(ch + 1) * CHUNK] * acc).astype(BF16)


def _inproj_call(x, mod, layer, w_in, b_in, ln_g, ln_b, ws, bsb, conv_w, conv_b, conv_ln_g, conv_ln_b):
    bsz, seq, d = x.shape
    grid = (bsz, seq // TS)
    hb = TS // HALO
    const2 = lambda b, i: (0, 0)
    row = lambda b, i: (b, i, 0)
    colblk = lambda b, i: (b, 0, i)
    prev_h = lambda b, i: (b, jnp.maximum(i * hb - 1, 0), 0)
    next_h = lambda b, i: (b, jnp.minimum((i + 1) * hb, seq // HALO - 1), 0)
    q0 = 2 * A_WIDTH
    v0 = q0 + B_WIDTH + KV_WIDTH
    wb = w_in.astype(BF16)
    wqt = w_in[:, q0:q0 + B_WIDTH].T.astype(BF16)
    wvt = w_in[:, v0:v0 + KV_WIDTH].T.astype(BF16)
    bq = b_in[q0:q0 + B_WIDTH].reshape(B_WIDTH, 1)
    bv = b_in[v0:v0 + KV_WIDTH].reshape(KV_WIDTH, 1)

    def out(width):
        return jax.ShapeDtypeStruct((bsz, seq, width), BF16), pl.BlockSpec((1, TS, width), row)

    def out_t(width):
        return jax.ShapeDtypeStruct((bsz, width, seq), BF16), pl.BlockSpec((1, width, TS), colblk)

    outs = [out(A_WIDTH), out_t(B_WIDTH), out(KV_WIDTH), out_t(KV_WIDTH), out(C_WIDTH)]
    return pl.pallas_call(
        _inproj_kernel,
        out_shape=[o[0] for o in outs],
        grid=grid,
        in_specs=[
            pl.BlockSpec((1, TS, d), row),
            pl.BlockSpec((1, HALO, d), prev_h),
            pl.BlockSpec((1, HALO, d), next_h),
            pl.BlockSpec((None, None, 6, d), lambda b, i: (layer, b, 0, 0)),
            pl.BlockSpec((d, IN_WIDTH), const2),
            pl.BlockSpec((1, IN_WIDTH), const2),
            pl.BlockSpec((B_WIDTH, d), const2),
            pl.BlockSpec((B_WIDTH, 1), const2),
            pl.BlockSpec((KV_WIDTH, d), const2),
            pl.BlockSpec((KV_WIDTH, 1), const2),
            pl.BlockSpec((1, A_WIDTH), const2),
            pl.BlockSpec((1, A_WIDTH), const2),
            pl.BlockSpec((A_HEADS, CHUNK, CHUNK), lambda b, i: (0, 0, 0)),
            pl.BlockSpec((CHUNK, A_WIDTH), const2),
            pl.BlockSpec((CONV_WIDTH, C_WIDTH), const2),
            pl.BlockSpec((1, C_WIDTH), const2),
            pl.BlockSpec((1, C_WIDTH), const2),
            pl.BlockSpec((1, C_WIDTH), const2),
        ],
        out_specs=[o[1] for o in outs],
        scratch_shapes=[
            pltpu.VMEM((TS + 2 * HALO, C_WIDTH), F32),
            pltpu.VMEM((SUBLANES, TS + SUBLANES, C_WIDTH), F32),
        ],
        compiler_params=pltpu.CompilerParams(
            dimension_semantics=("arbitrary", "arbitrary"), vmem_limit_bytes=VMEM_LIMIT),
        name="inproj_gmlp_conv",
    )(x, x, x, mod, wb, b_in.reshape(1, -1), wqt, bq, wvt, bv, ln_g, ln_b, ws, bsb,
      conv_w, conv_b, conv_ln_g, conv_ln_b)


def _first_argmax(vals, iota_f, width):
    m = jnp.max(vals, axis=0, keepdims=True)
    idx = jnp.min(jnp.where(vals == m, iota_f, float(width)), axis=0, keepdims=True)
    return m, idx


def _mixer_kernel(seq_len, tiles_per_seq, qt_ref, kp_ref, kc_ref, kn_ref, vtp_ref, vtc_ref, vtn_ref,
                  bias_ref, sink_ref, yc_ref, ya_ref, x_ref, mod_ref, wo_ref, bo_ref, lng_ref, lnb_ref,
                  rw_ref, rb_ref, lt_ref, before_ref, x1_ref, xs_ref, pos_ref, gate_ref, cnt_ref,
                  ot_scr, s_scr, h2_scr, pos_scr):
    n = pl.program_id(0)
    last_tile = pl.num_programs(0) - 2
    i = jnp.minimum(n, last_tile) % tiles_per_seq
    t0 = i * TT
    m = mod_ref[...]

    @pl.when(n == 0)
    def _():
        h2_scr[...] = jnp.zeros_like(h2_scr)
        pos_scr[...] = jnp.zeros_like(pos_scr)

    pos_prev = pos_scr[...]
    sort_rows = R_TILE // N_SORT_BLOCKS

    def sort_block(j):
        iota_r = j * sort_rows + lax.broadcasted_iota(I32, (sort_rows, TT), 0)
        onehot = jnp.where((iota_r == pos_prev[0:1]) | (iota_r == pos_prev[1:2]), 1.0, 0.0).astype(BF16)
        xs_ref[j * sort_rows:(j + 1) * sort_rows, :] = _dot(onehot, h2_scr[...]).astype(BF16)

    def out_proj(yb):
        return (_dot(ya_ref[0], wo_ref[0:A_WIDTH, :])
                + _dot(yb, wo_ref[A_WIDTH:A_WIDTH + B_WIDTH, :])
                + _dot(yc_ref[0], wo_ref[A_WIDTH + B_WIDTH:, :]) + bo_ref[...])

    def route(y):
        x1 = _ln(ALPHA * x_ref[0] + (1.0 + m[2:3]) * y, lng_ref[...], lnb_ref[...])
        x1_ref[0] = x1
        h2 = (x1 * (1.0 + m[4:5]) + m[3:4]).astype(BF16)
        scores = _sigmoid(_dot_nt(rw_ref[...], h2))
        sel = scores + rb_ref[...]
        iota_f = lax.broadcasted_iota(I32, (EXPERTS_PER_GROUP, TT), 0).astype(F32)
        best = None
        for g in range(N_GROUPS):
            sl = slice(g * EXPERTS_PER_GROUP, (g + 1) * EXPERTS_PER_GROUP)
            sg = sel[sl]
            m1, i1 = _first_argmax(sg, iota_f, EXPERTS_PER_GROUP)
            m2, i2 = _first_argmax(jnp.where(iota_f == i1, -jnp.inf, sg), iota_f, EXPERTS_PER_GROUP)
            sc = scores[sl]
            s1 = jnp.sum(jnp.where(iota_f == i1, sc, 0.0), axis=0, keepdims=True)
            s2 = jnp.sum(jnp.where(iota_f == i2, sc, 0.0), axis=0, keepdims=True)
            cand = (m1 + m2, i1 + g * EXPERTS_PER_GROUP, i2 + g * EXPERTS_PER_GROUP, s1, s2)
            if best is None:
                best = cand
            else:
                take = cand[0] > best[0]
                best = tuple(jnp.where(take, c, b) for c, b in zip(cand, best))
        _, e1, e2, s1, s2 = best
        gate_ref[0] = jnp.concatenate([s1, s2], axis=0) / (s1 + s2)

        iota_e = lax.broadcasted_iota(I32, (N_EXPERTS, TT), 0).astype(F32)
        in0 = iota_e == e1
        in1 = iota_e == e2
        member = jnp.where(in0 | in1, 1.0, 0.0)
        cnt = jnp.sum(member, axis=1, keepdims=True)
        cnt_ref[0] = jnp.broadcast_to(cnt, (N_EXPERTS, LANES))
        nch = jnp.floor((cnt + (CH - 1)) * (1.0 / CH))
        nch_pad = jnp.concatenate([jnp.broadcast_to(nch, (N_EXPERTS, LANES)),
                                   jnp.zeros((LANES - N_EXPERTS, LANES), F32)], axis=0).astype(BF16)
        padoff = _dot(lt_ref[...], nch_pad)[:, 0:1] * float(CH)
        rank = _dot(member.astype(BF16), before_ref[...])
        posf = padoff + rank
        pos0 = jnp.sum(jnp.where(in0, posf, 0.0), axis=0, keepdims=True).astype(I32)
        pos1 = jnp.sum(jnp.where(in1, posf, 0.0), axis=0, keepdims=True).astype(I32)
        pos = jnp.concatenate([pos0, pos1], axis=0)
        pos_ref[0] = pos
        return pos, h2

    kfull = jnp.concatenate([kp_ref[0], kc_ref[0], kn_ref[0]], axis=0)
    vtfull = jnp.concatenate([vtp_ref[0], vtc_ref[0], vtn_ref[0]], axis=1)
    grp = B_HEADS // B_KV_HEADS
    n_qb = TT // WINDOW
    units = [(jb, g) for jb in range(n_qb) for g in range(B_KV_HEADS)]
    key_i = lax.broadcasted_iota(I32, (3 * WINDOW, 1), 0)

    def scores(u):
        jb, g = units[u]
        kb = kfull[jb * WINDOW:(jb + 3) * WINDOW, g * HEAD_DIM:(g + 1) * HEAD_DIM]
        qt = jnp.concatenate(
            [qt_ref[0, h * HEAD_DIM:(h + 1) * HEAD_DIM, jb * WINDOW:(jb + 1) * WINDOW]
             for h in range(g * grp, (g + 1) * grp)], axis=1)
        s = _dot(kb, qt) + bias_ref[g]
        if jb == 0 or jb == n_qb - 1:
            kpos = t0 + (jb - 1) * WINDOW + key_i
            s = jnp.where((kpos >= 0) & (kpos < seq_len), s, NEG_INF)
        s_scr[u % N_SBUF] = s

    def values(u):
        jb, g = units[u]
        s = s_scr[u % N_SBUF]
        sink = sink_ref[g:g + 1, :]
        mx = jnp.maximum(jnp.max(s, axis=0, keepdims=True), sink)
        p = jnp.exp2(s - mx)
        den = jnp.sum(p, axis=0, keepdims=True) + jnp.exp2(sink - mx)
        vt = vtfull[g * HEAD_DIM:(g + 1) * HEAD_DIM, jb * WINDOW:(jb + 3) * WINDOW]
        ot = _dot(vt, p.astype(BF16)) / den
        for hh in range(grp):
            h = g * grp + hh
            ot_scr[h * HEAD_DIM:(h + 1) * HEAD_DIM, jb * WINDOW:(jb + 1) * WINDOW] = (
                ot[:, hh * WINDOW:(hh + 1) * WINDOW])

    for u in range(N_SBUF - 1):
        scores(u)
    for u in range(len(units)):
        if u + N_SBUF - 1 < len(units):
            scores(u + N_SBUF - 1)
        values(u)
    y = out_proj(jnp.transpose(ot_scr[...]).astype(BF16))
    for j in range(N_SORT_BLOCKS):
        sort_block(j)
    new_pos, new_h2 = route(y)
    pos_scr[...] = new_pos
    h2_scr[...] = new_h2


def _mixer_call(seq_len, qt, k, vt, bias, sink, yc, ya, x, mod, layer, w_out, b_out, ln_g, ln_b, rw, rbias):
    bsz, seq, d = x.shape
    nt = seq // TT
    n_tiles = bsz * nt
    kb = TT // WINDOW
    grp = B_HEADS // B_KV_HEADS
    const2 = lambda n: (0, 0)

    def at_tile(fn):
        def index_map(n):
            t = jnp.minimum(n, n_tiles - 1)
            return fn(t // nt, t % nt)
        return index_map

    row = at_tile(lambda b, i: (b, i, 0))
    colblk = at_tile(lambda b, i: (b, 0, i))
    prev_k = at_tile(lambda b, i: (b, jnp.maximum(i * kb - 1, 0), 0))
    next_k = at_tile(lambda b, i: (b, jnp.minimum((i + 1) * kb, seq // WINDOW - 1), 0))
    prev_v = at_tile(lambda b, i: (b, 0, jnp.maximum(i * kb - 1, 0)))
    next_v = at_tile(lambda b, i: (b, 0, jnp.minimum((i + 1) * kb, seq // WINDOW - 1)))
    tile = at_tile(lambda b, i: (b * nt + i, 0, 0))
    lt = (jnp.arange(LANES)[None, :] < jnp.arange(N_EXPERTS)[:, None]).astype(BF16)
    before = (jnp.arange(TT)[:, None] < jnp.arange(TT)[None, :]).astype(BF16)
    return pl.pallas_call(
        functools.partial(_mixer_kernel, seq_len, nt),
        out_shape=[
            jax.ShapeDtypeStruct((bsz, seq, d), F32),
            jax.ShapeDtypeStruct(((n_tiles + 1) * R_TILE, d), BF16),
            jax.ShapeDtypeStruct((n_tiles, TOP_K, TT), I32),
            jax.ShapeDtypeStruct((n_tiles, TOP_K, TT), F32),
            jax.ShapeDtypeStruct((n_tiles, N_EXPERTS, LANES), F32),
        ],
        grid=(n_tiles + 1,),
        in_specs=[
            pl.BlockSpec((1, B_WIDTH, TT), colblk),
            pl.BlockSpec((1, WINDOW, KV_WIDTH), prev_k),
            pl.BlockSpec((1, TT, KV_WIDTH), row),
            pl.BlockSpec((1, WINDOW, KV_WIDTH), next_k),
            pl.BlockSpec((1, KV_WIDTH, WINDOW), prev_v),
            pl.BlockSpec((1, KV_WIDTH, TT), colblk),
            pl.BlockSpec((1, KV_WIDTH, WINDOW), next_v),
            pl.BlockSpec((B_KV_HEADS, 3 * WINDOW, grp * WINDOW), lambda n: (0, 0, 0)),
            pl.BlockSpec((B_KV_HEADS, grp * WINDOW), const2),
            pl.BlockSpec((1, TT, C_WIDTH), row),
            pl.BlockSpec((1, TT, A_WIDTH), row),
            pl.BlockSpec((1, TT, d), row),
            pl.BlockSpec((None, None, 6, d), at_tile(lambda b, i: (layer, b, 0, 0))),
            pl.BlockSpec((d, d), const2),
            pl.BlockSpec((1, d), const2),
            pl.BlockSpec((1, d), const2),
            pl.BlockSpec((1, d), const2),
            pl.BlockSpec((N_EXPERTS, d), const2),
            pl.BlockSpec((N_EXPERTS, 1), const2),
            pl.BlockSpec((N_EXPERTS, LANES), const2),
            pl.BlockSpec((TT, TT), const2),
        ],
        out_specs=[
            pl.BlockSpec((1, TT, d), row),
            pl.BlockSpec((R_TILE, d), lambda n: (n, 0)),
            pl.BlockSpec((1, TOP_K, TT), tile),
            pl.BlockSpec((1, TOP_K, TT), tile),
            pl.BlockSpec((1, N_EXPERTS, LANES), tile),
        ],
        scratch_shapes=[
            pltpu.VMEM((B_WIDTH, TT), F32),
            pltpu.VMEM((N_SBUF, 3 * WINDOW, grp * WINDOW), F32),
            pltpu.VMEM((TT, d), BF16),
            pltpu.VMEM((TOP_K, TT), I32),
        ],
        compiler_params=pltpu.CompilerParams(
            dimension_semantics=("arbitrary",), vmem_limit_bytes=VMEM_LIMIT),
        name="mixer_router_sort",
    )(qt, k, k, k, vt, vt, vt, bias, sink, yc, ya, x, mod, w_out, b_out, ln_g, ln_b, rw, rbias, lt, before)


def _expert_kernel(layer, src_ref, dst_ref, bexp_ref, enext_ref, nused_ref, xs_hbm, wg_hbm, wu_hbm, wd_hbm,
                   ys_hbm, xbuf, ybuf, wg_st, wu_st, wd_st, wgb, wub, wdb, sem_in, sem_out, sem_w):
    n_blk = bexp_ref.shape[0]
    nused = nused_ref[0]
    chunks_per_tile = R_TILE // CH
    zero_chunk = chunks_per_tile - 1
    scratch_chunk = 0
    de = wgb.shape[2]
    half = de // 2

    def in_copy(chunk, s, c):
        return pltpu.make_async_copy(
            xs_hbm.at[pl.ds(pl.multiple_of(chunk * CH, CH), CH)],
            xbuf.at[s, pl.ds(c * CH, CH)], sem_in.at[s])

    def out_copy(chunk, s, c):
        return pltpu.make_async_copy(
            ybuf.at[s, pl.ds(c * CH, CH)],
            ys_hbm.at[pl.ds(pl.multiple_of(chunk * CH, CH), CH)], sem_out.at[s])

    def weight_copies(e, s):
        return [pltpu.make_async_copy(wg_hbm.at[layer, e], wg_st.at[s], sem_w.at[s]),
                pltpu.make_async_copy(wu_hbm.at[layer, e], wu_st.at[s], sem_w.at[s]),
                pltpu.make_async_copy(wd_hbm.at[layer, e], wd_st.at[s], sem_w.at[s])]

    def gather_chunk(b, c):
        return jnp.where(b < nused, src_ref[jnp.minimum(b, n_blk - 1) * NCH + c], zero_chunk)

    def write_back_chunk(b, c):
        real = (b >= 0) & (b < nused)
        return jnp.where(real, dst_ref[jnp.clip(b, 0, n_blk - 1) * NCH + c], scratch_chunk + (b % 2) * NCH + c)

    ybuf[...] = jnp.zeros_like(ybuf)
    wgb[...] = jnp.zeros_like(wgb)
    wub[...] = jnp.zeros_like(wub)
    wdb[...] = jnp.zeros_like(wdb)
    for b in range(4):
        for c in range(NCH):
            in_copy(gather_chunk(b, c), b, c).start()
    for b in range(2):
        for c in range(NCH):
            out_copy(scratch_chunk + b * NCH + c, b, c).start()

    @pl.when(nused > 0)
    def _():
        for cp in weight_copies(bexp_ref[0], 0):
            cp.start()

    def pair(i, n_changes):
        blocks = (2 * i, 2 * i + 1)
        wslot = []
        for blk in blocks:
            jb = jnp.minimum(blk, n_blk - 1)
            expert = bexp_ref[jb]
            prev_expert = bexp_ref[jnp.maximum(jb - 1, 0)]
            change = (blk < nused) & ((blk == 0) | (expert != prev_expert))

            @pl.when(change)
            def _(expert=expert, jb=jb, n_changes=n_changes):
                ws = n_changes % 2
                for cp in weight_copies(expert, ws):
                    cp.wait()
                wgb[ws] = wg_st[ws].astype(BF16)
                wub[ws] = wu_st[ws].astype(BF16)
                wdb[ws] = wd_st[ws].astype(BF16)
                upcoming = enext_ref[jb]

                @pl.when(upcoming != expert)
                def _():
                    for cp in weight_copies(upcoming, 1 - ws):
                        cp.start()

            n_changes = n_changes + change.astype(I32)
            wslot.append(jnp.maximum(n_changes - 1, 0) % 2)

        for blk in blocks:
            for c in range(NCH):
                in_copy(0, blk % N_XRING, c).wait()
            for c in range(NCH):
                out_copy(0, blk % N_YRING, c).wait()

        dmas = []
        for blk in blocks:
            for c in range(NCH):
                dmas.append(functools.partial(
                    lambda b, c: out_copy(write_back_chunk(b, c), b % N_YRING, c).start(), blk - 2, c))
        for blk in blocks:
            for c in range(NCH):
                dmas.append(functools.partial(
                    lambda b, c: in_copy(gather_chunk(b, c), b % N_XRING, c).start(priority=1), blk + 4, c))
        n_groups = 8
        per_group = len(dmas) // n_groups

        def issue(gidx):
            for start in dmas[gidx * per_group:(gidx + 1) * per_group]:
                start()

        gidx = 0
        for k, blk in enumerate(blocks):
            x = xbuf[blk % N_XRING]
            ws = wslot[k]
            hmid = []
            for h in range(2):
                cols = slice(h * half, (h + 1) * half)
                g = _dot(x, wgb[ws, :, cols])
                issue(gidx)
                u = _dot(x, wub[ws, :, cols])
                issue(gidx + 1)
                gidx += 2
                hmid.append(((g * _sigmoid(g)) * u).astype(BF16))
            y = _dot(hmid[0], wdb[ws, 0:half, :]) + _dot(hmid[1], wdb[ws, half:, :])
            ybuf[blk % N_YRING] = y.astype(BF16)
        return n_changes

    n_pairs = (nused + 1) // 2
    lax.fori_loop(0, n_pairs, pair, jnp.int32(0))

    last = 2 * n_pairs
    for blk in range(4):
        for c in range(NCH):
            in_copy(0, (last + blk) % N_XRING, c).wait()
    for blk in (last, last + 1):
        for c in range(NCH):
            out_copy(0, blk % N_YRING, c).wait()
    for blk in (last - 2, last - 1):
        for c in range(NCH):
            out_copy(write_back_chunk(blk, c), blk % N_YRING, c).start()
    for blk in (last - 2, last - 1):
        for c in range(NCH):
            out_copy(0, blk % N_YRING, c).wait()


def _expert_call(src, dst, bexp, enext, nused, xs, layer, w_gate, w_up, w_down):
    d = xs.shape[-1]
    de = w_gate.shape[-1]
    grid_spec = pltpu.PrefetchScalarGridSpec(
        num_scalar_prefetch=5,
        grid=(1,),
        in_specs=[pl.BlockSpec(memory_space=pl.ANY)] * 4,
        out_specs=pl.BlockSpec(memory_space=pl.ANY),
        scratch_shapes=[
            pltpu.VMEM((N_XRING, BM, d), BF16),
            pltpu.VMEM((N_YRING, BM, d), BF16),
            pltpu.VMEM((2, d, de), F32),
            pltpu.VMEM((2, d, de), F32),
            pltpu.VMEM((2, de, d), F32),
            pltpu.VMEM((2, d, de), BF16),
            pltpu.VMEM((2, d, de), BF16),
            pltpu.VMEM((2, de, d), BF16),
            pltpu.SemaphoreType.DMA((N_XRING,)),
            pltpu.SemaphoreType.DMA((N_YRING,)),
            pltpu.SemaphoreType.DMA((2,)),
        ],
    )
    return pl.pallas_call(
        functools.partial(_expert_kernel, layer),
        out_shape=jax.ShapeDtypeStruct(xs.shape, xs.dtype),
        grid_spec=grid_spec,
        input_output_aliases={5: 0},
        compiler_params=pltpu.CompilerParams(
            dimension_semantics=("arbitrary",), vmem_limit_bytes=VMEM_LIMIT),
        name="moe_experts",
    )(src, dst, bexp, enext, nused, xs, w_gate, w_up, w_down)


def _combine_kernel(ysa_ref, ysb_ref, pos_ref, gate_ref, x1_ref, mod_ref, lng_ref, lnb_ref, o_ref):
    m = mod_ref[...]
    for k, ys_ref in enumerate((ysa_ref, ysb_ref)):
        pos = pos_ref[k]
        gate = gate_ref[k]
        rows = slice(k * TT, (k + 1) * TT)
        iota_r = lax.broadcasted_iota(I32, (R_TILE, TT), 0)
        row_gate = (jnp.where(iota_r == pos[0:1], gate[0:1], 0.0)
                    + jnp.where(iota_r == pos[1:2], gate[1:2], 0.0))
        gs = jnp.sum(row_gate, axis=1, keepdims=True)
        ysc = (ys_ref[...].astype(F32) * gs).astype(BF16)
        posc = jnp.transpose(pos.astype(F32))
        iota_c = lax.broadcasted_iota(I32, (TT, R_TILE), 1).astype(F32)
        pick = jnp.where((iota_c == posc[:, 0:1]) | (iota_c == posc[:, 1:2]), 1.0, 0.0).astype(BF16)
        y = _dot(pick, ysc)
        o_ref[rows, :] = _ln(ALPHA * x1_ref[rows, :] + (1.0 + m[5:6]) * y, lng_ref[...], lnb_ref[...])


def _combine_call(ys, pos, gate, x1, mod, layer, tiles_per_seq, ln_g, ln_b):
    t, d = x1.shape
    n_tiles = t // TT
    assert tiles_per_seq % 2 == 0
    const2 = lambda n: (0, 0)
    return pl.pallas_call(
        _combine_kernel,
        out_shape=jax.ShapeDtypeStruct((t, d), F32),
        grid=(n_tiles // 2,),
        in_specs=[
            pl.BlockSpec((R_TILE, d), lambda n: (2 * n + 1, 0)),
            pl.BlockSpec((R_TILE, d), lambda n: (2 * n + 2, 0)),
            pl.BlockSpec((2, TOP_K, TT), lambda n: (n, 0, 0)),
            pl.BlockSpec((2, TOP_K, TT), lambda n: (n, 0, 0)),
            pl.BlockSpec((2 * TT, d), lambda n: (n, 0)),
            pl.BlockSpec((None, None, 6, d), lambda n: (layer, 2 * n // tiles_per_seq, 0, 0)),
            pl.BlockSpec((1, d), const2),
            pl.BlockSpec((1, d), const2),
        ],
        out_specs=pl.BlockSpec((2 * TT, d), lambda n: (n, 0)),
        compiler_params=pltpu.CompilerParams(
            dimension_semantics=("arbitrary",), vmem_limit_bytes=VMEM_LIMIT),
        name="moe_combine",
    )(ys, ys, pos, gate, x1, mod, ln_g, ln_b)


def _dispatch_plan(cnt, n_blocks):
    n_tiles = cnt.shape[0]
    chunks_per_tile = R_TILE // CH
    nch = (cnt + CH - 1) // CH
    padoff_ch = jnp.cumsum(nch, axis=1) - nch
    tot = jnp.sum(nch, axis=0)
    totpad = (tot + NCH - 1) // NCH * NCH
    eend = jnp.cumsum(totpad)
    ebase = eend - totpad
    tbase = jnp.cumsum(nch, axis=0) - nch
    start = (ebase[None, :] + tbase).T.reshape(-1)
    base = ((jnp.arange(n_tiles, dtype=I32)[:, None] + 1) * chunks_per_tile + padoff_ch).T.reshape(-1)
    vals = jnp.stack([start, base, nch.T.reshape(-1)], axis=1)
    delta = vals - jnp.concatenate([jnp.zeros((1, 3), I32), vals[:-1]], axis=0)
    digits = jnp.concatenate([delta // LANES, delta % LANES], axis=1).astype(BF16)
    slot = jnp.arange(n_blocks * NCH, dtype=I32)
    started = (start[None, :] <= slot[:, None]).astype(BF16)
    got = jnp.dot(started, digits, preferred_element_type=F32).astype(I32)
    seg = got[:, :3] * LANES + got[:, 3:]
    j = slot - seg[:, 0]
    valid = (j < seg[:, 2]) & (slot < eend[-1])
    zero_chunk = chunks_per_tile - 1
    src = jnp.where(valid, seg[:, 1] + j, zero_chunk).astype(I32)
    blk = slot // NCH
    scratch = (blk % 2) * NCH + slot % NCH
    dst = jnp.where(valid, src, scratch).astype(I32)
    first = jnp.arange(n_blocks, dtype=I32) * NCH
    bexp = jnp.minimum(jnp.sum(eend[None, :] <= first[:, None], axis=1), N_EXPERTS - 1).astype(I32)
    nused = (eend[-1] // NCH).astype(I32).reshape(1)
    ids = jnp.arange(N_EXPERTS, dtype=I32)
    later = jnp.where((ids[None, :] > ids[:, None]) & (totpad > 0)[None, :], ids[None, :], N_EXPERTS)
    next_of = jnp.min(later, axis=1)
    next_of = jnp.where(next_of == N_EXPERTS, ids, next_of)
    enext = jnp.sum(jnp.where(bexp[:, None] == ids[None, :], next_of[None, :], 0), axis=1).astype(I32)
    return src, dst, bexp, enext, nused


def _t5_bucket(rel):
    nb = N_BUCKETS // 2
    max_exact = nb // 2
    ret = jnp.where(rel > 0, nb, 0)
    n = jnp.abs(rel)
    nf = jnp.maximum(n, 1).astype(jnp.float32)
    large = max_exact + (jnp.log(nf / max_exact) / math.log(MAX_DISTANCE / max_exact)
                         * (nb - max_exact)).astype(jnp.int32)
    large = jnp.minimum(large, nb - 1)
    return ret + jnp.where(n < max_exact, n, large)


def _band_bias(rel_bias):
    qi = jnp.arange(WINDOW)
    kj = jnp.arange(3 * WINDOW)
    rel = kj[None, :] - WINDOW - qi[:, None]
    pick = _t5_bucket(rel)[:, :, None, None] == jnp.arange(N_BUCKETS)[None, None, :, None]
    bias = jnp.sum(jnp.where(pick, rel_bias.astype(F32)[None, None], 0.0), axis=2) * LOG2E
    bias = jnp.where((jnp.abs(rel) <= WINDOW)[:, :, None], bias, NEG_INF)
    grp = B_HEADS // B_KV_HEADS
    bias = jnp.transpose(bias, (2, 1, 0)).reshape(B_KV_HEADS, grp, 3 * WINDOW, WINDOW)
    return jnp.transpose(bias, (0, 2, 1, 3)).reshape(B_KV_HEADS, 3 * WINDOW, grp * WINDOW)


def kernel(x, c, ada_w, ada_b, w_in, b_in, gmlp_ln_g, gmlp_ln_b, gmlp_ws, gmlp_bs, attn_sink, conv_w,
           conv_b, conv_ln_g, conv_ln_b, w_out, b_out, ln_mix_g, ln_mix_b, w_gate, w_up, w_down,
           ln_ffn_g, ln_ffn_b, rel_bias, router_w, router_bias):
    bsz, seq, d = x.shape
    n_layers = ada_w.shape[0]
    t = bsz * seq
    nt = seq // TT
    n_tiles = t // TT
    max_chunks = n_tiles * (TOP_K * TT // CH + N_EXPERTS) + N_EXPERTS * (NCH - 1)
    n_blocks = -(-max_chunks // NCH)

    mod = _ada_call(c, ada_w, ada_b).reshape(n_layers, bsz, 6, d)
    bias = _band_bias(rel_bias)
    rw = router_w.T.astype(BF16)
    rbias = router_bias.astype(F32).reshape(N_EXPERTS, 1)
    row = lambda a: a.reshape(1, -1)

    for l in range(n_layers):
        bsb = jnp.repeat(gmlp_bs[l].T, HEAD_DIM, axis=1)
        ya, qt, k, vt, yc = _inproj_call(
            x, mod, l, w_in[l], b_in[l], row(gmlp_ln_g[l]), row(gmlp_ln_b[l]),
            gmlp_ws[l].astype(BF16), bsb, conv_w[l], row(conv_b[l]), row(conv_ln_g[l]), row(conv_ln_b[l]))
        sink = jnp.repeat(attn_sink[l].astype(F32) * LOG2E, WINDOW).reshape(B_KV_HEADS, -1)
        x1, xs, pos, gate, cntb = _mixer_call(
            seq, qt, k, vt, bias, sink, yc, ya, x, mod, l, w_out[l].astype(BF16),
            row(b_out[l]), row(ln_mix_g[l]), row(ln_mix_b[l]), rw, rbias)
        cnt = cntb[:, :, 0].astype(I32)
        src, dst, bexp, enext, nused = _dispatch_plan(cnt, n_blocks)
        ys = _expert_call(src, dst, bexp, enext, nused, xs, l, w_gate, w_up, w_down)
        x = _combine_call(ys, pos, gate, x1.reshape(t, d), mod, l, nt, row(ln_ffn_g[l]),
                          row(ln_ffn_b[l])).reshape(bsz, seq, d)
    return x
```

```python
import functools
import math

import jax
import jax.numpy as jnp
from jax import lax
from jax.experimental import pallas as pl
from jax.experimental.pallas import tpu as pltpu

F32 = jnp.float32
BF16 = jnp.bfloat16
I32 = jnp.int32

D_MODEL = 1024
DEPTH = 2
HEAD_DIM = 64
A_WIDTH = 256
A_HEADS = 4
CHUNK = 128
B_WIDTH = 512
B_HEADS = 8
B_KV_HEADS = 2
KV_WIDTH = B_KV_HEADS * HEAD_DIM
WINDOW = 128
N_BUCKETS = 32
MAX_DISTANCE = 128
C_WIDTH = 256
CONV_WIDTH = 31
CONV_PAD = CONV_WIDTH // 2
IN_WIDTH = 2 * A_WIDTH + B_WIDTH + 2 * KV_WIDTH + 2 * C_WIDTH
N_EXPERTS = 32
N_GROUPS = 4
EXPERTS_PER_GROUP = N_EXPERTS // N_GROUPS
TOP_K = 2
D_EXPERT = D_MODEL // 2
ALPHA = (2 * DEPTH) ** 0.25
LN_EPS = 1e-5
NEG_INF = -1e30
LOG2E = 1.4426950408889634
Q_SCALE = HEAD_DIM ** -0.5 * LOG2E

LANES = 128
SUBLANES = 8
BF16_SUBLANES = 16
VMEM_LIMIT = 48 * 1024 * 1024

ADA_TN = 1536
TS = 1024
TT = 512
CH = BF16_SUBLANES
R_TILE = 1536
BM = 512
NCH = BM // CH
N_XRING = 6
N_YRING = 4
N_SORT_BLOCKS = 6
N_SBUF = 2
HALO = 16

assert R_TILE >= TOP_K * TT + N_EXPERTS * (CH - 1) + CH
assert R_TILE >= 2 * NCH * CH + CH


def _sigmoid(x):
    return 1.0 / (1.0 + jnp.exp(-x))


def _gelu_tanh(x):
    return x * (0.5 * (1.0 + jnp.tanh(0.7978845608028654 * (x + 0.044715 * (x * x * x)))))


def _ln(x, g, b):
    mu = jnp.mean(x, axis=-1, keepdims=True)
    xc = x - mu
    var = jnp.mean(xc * xc, axis=-1, keepdims=True)
    return xc * lax.rsqrt(var + LN_EPS) * g + b


def _dot(a, b):
    return jnp.dot(a, b, preferred_element_type=F32)


def _dot_nt(a, b):
    return lax.dot_general(a, b, (((1,), (1,)), ((), ())), preferred_element_type=F32)


def _ada_kernel(c_ref, w_ref, b_ref, o_ref):
    c = c_ref[...]
    s = (c * _sigmoid(c)).astype(BF16)
    o_ref[0] = _dot(s, w_ref[0].astype(BF16)) + b_ref[0]


def _ada_call(c, ada_w, ada_b):
    nl, d, n = ada_w.shape
    bsz = c.shape[0]
    return pl.pallas_call(
        _ada_kernel,
        out_shape=jax.ShapeDtypeStruct((nl, bsz, n), F32),
        grid=(nl, n // ADA_TN),
        in_specs=[
            pl.BlockSpec((bsz, d), lambda l, j: (0, 0)),
            pl.BlockSpec((1, d, ADA_TN), lambda l, j: (l, 0, j)),
            pl.BlockSpec((1, 1, ADA_TN), lambda l, j: (l, 0, j)),
        ],
        out_specs=pl.BlockSpec((1, bsz, ADA_TN), lambda l, j: (l, 0, j)),
        compiler_params=pltpu.CompilerParams(
            dimension_semantics=("arbitrary", "arbitrary"), vmem_limit_bytes=VMEM_LIMIT),
        name="ada_mod",
    )(c, ada_w, ada_b.reshape(nl, 1, n))


def _inproj_kernel(x_ref, xp_ref, xn_ref, mod_ref, w_ref, b_ref, wqt_ref, bq_ref, wvt_ref, bv_ref,
                   lng_ref, lnb_ref, ws_ref, bsb_ref, cw_ref, cb_ref, clg_ref, clb_ref,
                   ya_ref, qt_ref, k_ref, vt_ref, yc_ref, conv_scr, z_scr):
    i = pl.program_id(1)
    n_i = pl.num_programs(1)
    m = mod_ref[...]

    def modulate(xv):
        return (xv * (1.0 + m[1:2]) + m[0:1]).astype(BF16)

    hb = modulate(x_ref[0])
    col_u, col_v = 0, A_WIDTH
    col_k = 2 * A_WIDTH + B_WIDTH
    col_a = col_k + 2 * KV_WIDTH
    col_g = col_a + C_WIDTH

    def proj(lhs, c0, width):
        return _dot(lhs, w_ref[:, c0:c0 + width]) + b_ref[:, c0:c0 + width]

    hx = jnp.concatenate([modulate(xp_ref[0]), hb, modulate(xn_ref[0])], axis=0)
    yg = proj(hx, col_a, C_WIDTH) * _sigmoid(proj(hx, col_g, C_WIDTH))
    conv_scr[0:HALO, :] = yg[0:HALO] * jnp.where(i > 0, 1.0, 0.0)
    conv_scr[HALO:HALO + TS, :] = yg[HALO:HALO + TS]
    conv_scr[HALO + TS:, :] = yg[HALO + TS:] * jnp.where(i < n_i - 1, 1.0, 0.0)
    first = HALO - CONV_PAD
    acc = jnp.zeros((TS, C_WIDTH), F32) + cb_ref[...]
    for r in range(SUBLANES):
        z = None
        for a in range(-(-(first + CONV_WIDTH) // SUBLANES)):
            w = a * SUBLANES + r - first
            if 0 <= w < CONV_WIDTH:
                term = conv_scr[a * SUBLANES:a * SUBLANES + TS + SUBLANES, :] * cw_ref[w:w + 1, :]
                z = term if z is None else z + term
        if r == 0:
            acc = acc + z[0:TS]
        else:
            z_scr[r] = z
            acc = acc + z_scr[r, r:r + TS, :]
    yc = _ln(acc, clg_ref[...], clb_ref[...])
    yc_ref[0] = (yc * _sigmoid(yc)).astype(BF16)

    u = _gelu_tanh(proj(hb, col_u, A_WIDTH))
    v = _gelu_tanh(proj(hb, col_v, A_WIDTH))
    qt_ref[0] = ((_dot_nt(wqt_ref[...], hb) + bq_ref[...]) * Q_SCALE).astype(BF16)
    k_ref[0] = proj(hb, col_k, KV_WIDTH).astype(BF16)
    vt_ref[0] = (_dot_nt(wvt_ref[...], hb) + bv_ref[...]).astype(BF16)

    vb = _ln(v, lng_ref[...], lnb_ref[...]).astype(BF16)
    head_of_lane = lax.broadcasted_iota(I32, (CHUNK, A_WIDTH), 1) // HEAD_DIM
    for ch in range(TS // CHUNK):
        vc = vb[ch * CHUNK:(ch + 1) * CHUNK]
        acc = bsb_ref[...]
        for hh in range(A_HEADS):
            acc = acc + _dot(ws_ref[hh], jnp.where(head_of_lane == hh, vc, jnp.zeros_like(vc)))
        ya_ref[0, ch * CHUNK:(ch + 1) * CHUNK, :] = (u[ch * CHUNK:(ch + 1) * CHUNK] * acc).astype(BF16)


def _inproj_call(x, mod, layer, w_in, b_in, ln_g, ln_b, ws, bsb, conv_w, conv_b, conv_ln_g, conv_ln_b):
    bsz, seq, d = x.shape
    grid = (bsz, seq // TS)
    hb = TS // HALO
    const2 = lambda b, i: (0, 0)
    row = lambda b, i: (b, i, 0)
    colblk = lambda b, i: (b, 0, i)
    prev_h = lambda b, i: (b, jnp.maximum(i * hb - 1, 0), 0)
    next_h = lambda b, i: (b, jnp.minimum((i + 1) * hb, seq // HALO - 1), 0)
    q0 = 2 * A_WIDTH
    v0 = q0 + B_WIDTH + KV_WIDTH
    wb = w_in.astype(BF16)
    wqt = w_in[:, q0:q0 + B_WIDTH].T.astype(BF16)
    wvt = w_in[:, v0:v0 + KV_WIDTH].T.astype(BF16)
    bq = b_in[q0:q0 + B_WIDTH].reshape(B_WIDTH, 1)
    bv = b_in[v0:v0 + KV_WIDTH].reshape(KV_WIDTH, 1)

    def out(width):
        return jax.ShapeDtypeStruct((bsz, seq, width), BF16), pl.BlockSpec((1, TS, width), row)

    def out_t(width):
        return jax.ShapeDtypeStruct((bsz, width, seq), BF16), pl.BlockSpec((1, width, TS), colblk)

    outs = [out(A_WIDTH), out_t(B_WIDTH), out(KV_WIDTH), out_t(KV_WIDTH), out(C_WIDTH)]
    return pl.pallas_call(
        _inproj_kernel,
        out_shape=[o[0] for o in outs],
        grid=grid,
        in_specs=[
            pl.BlockSpec((1, TS, d), row),
            pl.BlockSpec((1, HALO, d), prev_h),
            pl.BlockSpec((1, HALO, d), next_h),
            pl.BlockSpec((None, None, 6, d), lambda b, i: (layer, b, 0, 0)),
            pl.BlockSpec((d, IN_WIDTH), const2),
            pl.BlockSpec((1, IN_WIDTH), const2),
            pl.BlockSpec((B_WIDTH, d), const2),
            pl.BlockSpec((B_WIDTH, 1), const2),
            pl.BlockSpec((KV_WIDTH, d), const2),
            pl.BlockSpec((KV_WIDTH, 1), const2),
            pl.BlockSpec((1, A_WIDTH), const2),
            pl.BlockSpec((1, A_WIDTH), const2),
            pl.BlockSpec((A_HEADS, CHUNK, CHUNK), lambda b, i: (0, 0, 0)),
            pl.BlockSpec((CHUNK, A_WIDTH), const2),
            pl.BlockSpec((CONV_WIDTH, C_WIDTH), const2),
            pl.BlockSpec((1, C_WIDTH), const2),
            pl.BlockSpec((1, C_WIDTH), const2),
            pl.BlockSpec((1, C_WIDTH), const2),
        ],
        out_specs=[o[1] for o in outs],
        scratch_shapes=[
            pltpu.VMEM((TS + 2 * HALO, C_WIDTH), F32),
            pltpu.VMEM((SUBLANES, TS + SUBLANES, C_WIDTH), F32),
        ],
        compiler_params=pltpu.CompilerParams(
            dimension_semantics=("arbitrary", "arbitrary"), vmem_limit_bytes=VMEM_LIMIT),
        name="inproj_gmlp_conv",
    )(x, x, x, mod, wb, b_in.reshape(1, -1), wqt, bq, wvt, bv, ln_g, ln_b, ws, bsb,
      conv_w, conv_b, conv_ln_g, conv_ln_b)


def _first_argmax(vals, iota_f, width):
    m = jnp.max(vals, axis=0, keepdims=True)
    idx = jnp.min(jnp.where(vals == m, iota_f, float(width)), axis=0, keepdims=True)
    return m, idx


def _mixer_kernel(seq_len, tiles_per_seq, qt_ref, kp_ref, kc_ref, kn_ref, vtp_ref, vtc_ref, vtn_ref,
                  bias_ref, sink_ref, yc_ref, ya_ref, x_ref, mod_ref, wo_ref, bo_ref, lng_ref, lnb_ref,
                  rw_ref, rb_ref, lt_ref, before_ref, x1_ref, xs_ref, pos_ref, gate_ref, cnt_ref,
                  ot_scr, s_scr, h2_scr, pos_scr):
    n = pl.program_id(0)
    last_tile = pl.num_programs(0) - 2
    i = jnp.minimum(n, last_tile) % tiles_per_seq
    t0 = i * TT
    m = mod_ref[...]

    @pl.when(n == 0)
    def _():
        h2_scr[...] = jnp.zeros_like(h2_scr)
        pos_scr[...] = jnp.zeros_like(pos_scr)

    pos_prev = pos_scr[...]
    sort_rows = R_TILE // N_SORT_BLOCKS

    def sort_block(j):
        iota_r = j * sort_rows + lax.broadcasted_iota(I32, (sort_rows, TT), 0)
        onehot = jnp.where((iota_r == pos_prev[0:1]) | (iota_r == pos_prev[1:2]), 1.0, 0.0).astype(BF16)
        xs_ref[j * sort_rows:(j + 1) * sort_rows, :] = _dot(onehot, h2_scr[...]).astype(BF16)

    def out_proj(yb):
        return (_dot(ya_ref[0], wo_ref[0:A_WIDTH, :])
                + _dot(yb, wo_ref[A_WIDTH:A_WIDTH + B_WIDTH, :])
                + _dot(yc_ref[0], wo_ref[A_WIDTH + B_WIDTH:, :]) + bo_ref[...])

    def route(y):
        x1 = _ln(ALPHA * x_ref[0] + (1.0 + m[2:3]) * y, lng_ref[...], lnb_ref[...])
        x1_ref[0] = x1
        h2 = (x1 * (1.0 + m[4:5]) + m[3:4]).astype(BF16)
        scores = _sigmoid(_dot_nt(rw_ref[...], h2))
        sel = scores + rb_ref[...]
        iota_f = lax.broadcasted_iota(I32, (EXPERTS_PER_GROUP, TT), 0).astype(F32)
        best = None
        for g in range(N_GROUPS):
            sl = slice(g * EXPERTS_PER_GROUP, (g + 1) * EXPERTS_PER_GROUP)
            sg = sel[sl]
            m1, i1 = _first_argmax(sg, iota_f, EXPERTS_PER_GROUP)
            m2, i2 = _first_argmax(jnp.where(iota_f == i1, -jnp.inf, sg), iota_f, EXPERTS_PER_GROUP)
            sc = scores[sl]
            s1 = jnp.sum(jnp.where(iota_f == i1, sc, 0.0), axis=0, keepdims=True)
            s2 = jnp.sum(jnp.where(iota_f == i2, sc, 0.0), axis=0, keepdims=True)
            cand = (m1 + m2, i1 + g * EXPERTS_PER_GROUP, i2 + g * EXPERTS_PER_GROUP, s1, s2)
            if best is None:
                best = cand
            else:
                take = cand[0] > best[0]
                best = tuple(jnp.where(take, c, b) for c, b in zip(cand, best))
        _, e1, e2, s1, s2 = best
        gate_ref[0] = jnp.concatenate([s1, s2], axis=0) / (s1 + s2)

        iota_e = lax.broadcasted_iota(I32, (N_EXPERTS, TT), 0).astype(F32)
        in0 = iota_e == e1
        in1 = iota_e == e2
        member = jnp.where(in0 | in1, 1.0, 0.0)
        cnt = jnp.sum(member, axis=1, keepdims=True)
        cnt_ref[0] = jnp.broadcast_to(cnt, (N_EXPERTS, LANES))
        nch = jnp.floor((cnt + (CH - 1)) * (1.0 / CH))
        nch_pad = jnp.concatenate([jnp.broadcast_to(nch, (N_EXPERTS, LANES)),
                                   jnp.zeros((LANES - N_EXPERTS, LANES), F32)], axis=0).astype(BF16)
        padoff = _dot(lt_ref[...], nch_pad)[:, 0:1] * float(CH)
        rank = _dot(member.astype(BF16), before_ref[...])
        posf = padoff + rank
        pos0 = jnp.sum(jnp.where(in0, posf, 0.0), axis=0, keepdims=True).astype(I32)
        pos1 = jnp.sum(jnp.where(in1, posf, 0.0), axis=0, keepdims=True).astype(I32)
        pos = jnp.concatenate([pos0, pos1], axis=0)
        pos_ref[0] = pos
        return pos, h2

    kfull = jnp.concatenate([kp_ref[0], kc_ref[0], kn_ref[0]], axis=0)
    vtfull = jnp.concatenate([vtp_ref[0], vtc_ref[0], vtn_ref[0]], axis=1)
    grp = B_HEADS // B_KV_HEADS
    n_qb = TT // WINDOW
    units = [(jb, g) for jb in range(n_qb) for g in range(B_KV_HEADS)]
    key_i = lax.broadcasted_iota(I32, (3 * WINDOW, 1), 0)

    def scores(u):
        jb, g = units[u]
        kb = kfull[jb * WINDOW:(jb + 3) * WINDOW, g * HEAD_DIM:(g + 1) * HEAD_DIM]
        qt = jnp.concatenate(
            [qt_ref[0, h * HEAD_DIM:(h + 1) * HEAD_DIM, jb * WINDOW:(jb + 1) * WINDOW]
             for h in range(g * grp, (g + 1) * grp)], axis=1)
        s = _dot(kb, qt) + bias_ref[g]
        if jb == 0 or jb == n_qb - 1:
            kpos = t0 + (jb - 1) * WINDOW + key_i
            s = jnp.where((kpos >= 0) & (kpos < seq_len), s, NEG_INF)
        s_scr[u % N_SBUF] = s

    def values(u):
        jb, g = units[u]
        s = s_scr[u % N_SBUF]
        sink = sink_ref[g:g + 1, :]
        mx = jnp.maximum(jnp.max(s, axis=0, keepdims=True), sink)
        p = jnp.exp2(s - mx)
        den = jnp.sum(p, axis=0, keepdims=True) + jnp.exp2(sink - mx)
        vt = vtfull[g * HEAD_DIM:(g + 1) * HEAD_DIM, jb * WINDOW:(jb + 3) * WINDOW]
        ot = _dot(vt, p.astype(BF16)) / den
        for hh in range(grp):
            h = g * grp + hh
            ot_scr[h * HEAD_DIM:(h + 1) * HEAD_DIM, jb * WINDOW:(jb + 1) * WINDOW] = (
                ot[:, hh * WINDOW:(hh + 1) * WINDOW])

    for u in range(N_SBUF - 1):
        scores(u)
    for u in range(len(units)):
        if u + N_SBUF - 1 < len(units):
            scores(u + N_SBUF - 1)
        values(u)
    y = out_proj(jnp.transpose(ot_scr[...]).astype(BF16))
    for j in range(N_SORT_BLOCKS):
        sort_block(j)
    new_pos, new_h2 = route(y)
    pos_scr[...] = new_pos
    h2_scr[...] = new_h2


def _mixer_call(seq_len, qt, k, vt, bias, sink, yc, ya, x, mod, layer, w_out, b_out, ln_g, ln_b, rw, rbias):
    bsz, seq, d = x.shape
    nt = seq // TT
    n_tiles = bsz * nt
    kb = TT // WINDOW
    grp = B_HEADS // B_KV_HEADS
    const2 = lambda n: (0, 0)

    def at_tile(fn):
        def index_map(n):
            t = jnp.minimum(n, n_tiles - 1)
            return fn(t // nt, t % nt)
        return index_map

    row = at_tile(lambda b, i: (b, i, 0))
    colblk = at_tile(lambda b, i: (b, 0, i))
    prev_k = at_tile(lambda b, i: (b, jnp.maximum(i * kb - 1, 0), 0))
    next_k = at_tile(lambda b, i: (b, jnp.minimum((i + 1) * kb, seq // WINDOW - 1), 0))
    prev_v = at_tile(lambda b, i: (b, 0, jnp.maximum(i * kb - 1, 0)))
    next_v = at_tile(lambda b, i: (b, 0, jnp.minimum((i + 1) * kb, seq // WINDOW - 1)))
    tile = at_tile(lambda b, i: (b * nt + i, 0, 0))
    lt = (jnp.arange(LANES)[None, :] < jnp.arange(N_EXPERTS)[:, None]).astype(BF16)
    before = (jnp.arange(TT)[:, None] < jnp.arange(TT)[None, :]).astype(BF16)
    return pl.pallas_call(
        functools.partial(_mixer_kernel, seq_len, nt),
        out_shape=[
            jax.ShapeDtypeStruct((bsz, seq, d), F32),
            jax.ShapeDtypeStruct(((n_tiles + 1) * R_TILE, d), BF16),
            jax.ShapeDtypeStruct((n_tiles, TOP_K, TT), I32),
            jax.ShapeDtypeStruct((n_tiles, TOP_K, TT), F32),
            jax.ShapeDtypeStruct((n_tiles, N_EXPERTS, LANES), F32),
        ],
        grid=(n_tiles + 1,),
        in_specs=[
            pl.BlockSpec((1, B_WIDTH, TT), colblk),
            pl.BlockSpec((1, WINDOW, KV_WIDTH), prev_k),
            pl.BlockSpec((1, TT, KV_WIDTH), row),
            pl.BlockSpec((1, WINDOW, KV_WIDTH), next_k),
            pl.BlockSpec((1, KV_WIDTH, WINDOW), prev_v),
            pl.BlockSpec((1, KV_WIDTH, TT), colblk),
            pl.BlockSpec((1, KV_WIDTH, WINDOW), next_v),
            pl.BlockSpec((B_KV_HEADS, 3 * WINDOW, grp * WINDOW), lambda n: (0, 0, 0)),
            pl.BlockSpec((B_KV_HEADS, grp * WINDOW), const2),
            pl.BlockSpec((1, TT, C_WIDTH), row),
            pl.BlockSpec((1, TT, A_WIDTH), row),
            pl.BlockSpec((1, TT, d), row),
            pl.BlockSpec((None, None, 6, d), at_tile(lambda b, i: (layer, b, 0, 0))),
            pl.BlockSpec((d, d), const2),
            pl.BlockSpec((1, d), const2),
            pl.BlockSpec((1, d), const2),
            pl.BlockSpec((1, d), const2),
            pl.BlockSpec((N_EXPERTS, d), const2),
            pl.BlockSpec((N_EXPERTS, 1), const2),
            pl.BlockSpec((N_EXPERTS, LANES), const2),
            pl.BlockSpec((TT, TT), const2),
        ],
        out_specs=[
            pl.BlockSpec((1, TT, d), row),
            pl.BlockSpec((R_TILE, d), lambda n: (n, 0)),
            pl.BlockSpec((1, TOP_K, TT), tile),
            pl.BlockSpec((1, TOP_K, TT), tile),
            pl.BlockSpec((1, N_EXPERTS, LANES), tile),
        ],
        scratch_shapes=[
            pltpu.VMEM((B_WIDTH, TT), F32),
            pltpu.VMEM((N_SBUF, 3 * WINDOW, grp * WINDOW), F32),
            pltpu.VMEM((TT, d), BF16),
            pltpu.VMEM((TOP_K, TT), I32),
        ],
        compiler_params=pltpu.CompilerParams(
            dimension_semantics=("arbitrary",), vmem_limit_bytes=VMEM_LIMIT),
        name="mixer_router_sort",
    )(qt, k, k, k, vt, vt, vt, bias, sink, yc, ya, x, mod, w_out, b_out, ln_g, ln_b, rw, rbias, lt, before)


def _expert_kernel(layer, src_ref, dst_ref, bexp_ref, enext_ref, nused_ref, xs_hbm, wg_hbm, wu_hbm, wd_hbm,
                   ys_hbm, xbuf, ybuf, wg_st, wu_st, wd_st, wgb, wub, wdb, sem_in, sem_out, sem_w):
    n_blk = bexp_ref.shape[0]
    nused = nused_ref[0]
    chunks_per_tile = R_TILE // CH
    zero_chunk = chunks_per_tile - 1
    scratch_chunk = 0
    de = wgb.shape[2]
    half = de // 2

    def in_copy(chunk, s, c):
        return pltpu.make_async_copy(
            xs_hbm.at[pl.ds(pl.multiple_of(chunk * CH, CH), CH)],
            xbuf.at[s, pl.ds(c * CH, CH)], sem_in.at[s])

    def out_copy(chunk, s, c):
        return pltpu.make_async_copy(
            ybuf.at[s, pl.ds(c * CH, CH)],
            ys_hbm.at[pl.ds(pl.multiple_of(chunk * CH, CH), CH)], sem_out.at[s])

    def weight_copies(e, s):
        return [pltpu.make_async_copy(wg_hbm.at[layer, e], wg_st.at[s], sem_w.at[s]),
                pltpu.make_async_copy(wu_hbm.at[layer, e], wu_st.at[s], sem_w.at[s]),
                pltpu.make_async_copy(wd_hbm.at[layer, e], wd_st.at[s], sem_w.at[s])]

    def gather_chunk(b, c):
        return jnp.where(b < nused, src_ref[jnp.minimum(b, n_blk - 1) * NCH + c], zero_chunk)

    def write_back_chunk(b, c):
        real = (b >= 0) & (b < nused)
        return jnp.where(real, dst_ref[jnp.clip(b, 0, n_blk - 1) * NCH + c], scratch_chunk + (b % 2) * NCH + c)

    ybuf[...] = jnp.zeros_like(ybuf)
    wgb[...] = jnp.zeros_like(wgb)
    wub[...] = jnp.zeros_like(wub)
    wdb[...] = jnp.zeros_like(wdb)
    for b in range(4):
        for c in range(NCH):
            in_copy(gather_chunk(b, c), b, c).start()
    for b in range(2):
        for c in range(NCH):
            out_copy(scratch_chunk + b * NCH + c, b, c).start()

    @pl.when(nused > 0)
    def _():
        for cp in weight_copies(bexp_ref[0], 0):
            cp.start()

    def pair(i, n_changes):
        blocks = (2 * i, 2 * i + 1)
        wslot = []
        for blk in blocks:
            jb = jnp.minimum(blk, n_blk - 1)
            expert = bexp_ref[jb]
            prev_expert = bexp_ref[jnp.maximum(jb - 1, 0)]
            change = (blk < nused) & ((blk == 0) | (expert != prev_expert))

            @pl.when(change)
            def _(expert=expert, jb=jb, n_changes=n_changes):
                ws = n_changes % 2
                for cp in weight_copies(expert, ws):
                    cp.wait()
                wgb[ws] = wg_st[ws].astype(BF16)
                wub[ws] = wu_st[ws].astype(BF16)
                wdb[ws] = wd_st[ws].astype(BF16)
                upcoming = enext_ref[jb]

                @pl.when(upcoming != expert)
                def _():
                    for cp in weight_copies(upcoming, 1 - ws):
                        cp.start()

            n_changes = n_changes + change.astype(I32)
            wslot.append(jnp.maximum(n_changes - 1, 0) % 2)

        for blk in blocks:
            for c in range(NCH):
                in_copy(0, blk % N_XRING, c).wait()
            for c in range(NCH):
                out_copy(0, blk % N_YRING, c).wait()

        dmas = []
        for blk in blocks:
            for c in range(NCH):
                dmas.append(functools.partial(
                    lambda b, c: out_copy(write_back_chunk(b, c), b % N_YRING, c).start(), blk - 2, c))
        for blk in blocks:
            for c in range(NCH):
                dmas.append(functools.partial(
                    lambda b, c: in_copy(gather_chunk(b, c), b % N_XRING, c).start(priority=1), blk + 4, c))
        n_groups = 8
        per_group = len(dmas) // n_groups

        def issue(gidx):
            for start in dmas[gidx * per_group:(gidx + 1) * per_group]:
                start()

        gidx = 0
        for k, blk in enumerate(blocks):
            x = xbuf[blk % N_XRING]
            ws = wslot[k]
            hmid = []
            for h in range(2):
                cols = slice(h * half, (h + 1) * half)
                g = _dot(x, wgb[ws, :, cols])
                issue(gidx)
                u = _dot(x, wub[ws, :, cols])
                issue(gidx + 1)
                gidx += 2
                hmid.append(((g * _sigmoid(g)) * u).astype(BF16))
            y = _dot(hmid[0], wdb[ws, 0:half, :]) + _dot(hmid[1], wdb[ws, half:, :])
            ybuf[blk % N_YRING] = y.astype(BF16)
        return n_changes

    n_pairs = (nused + 1) // 2
    lax.fori_loop(0, n_pairs, pair, jnp.int32(0))

    last = 2 * n_pairs
    for blk in range(4):
        for c in range(NCH):
            in_copy(0, (last + blk) % N_XRING, c).wait()
    for blk in (last, last + 1):
        for c in range(NCH):
            out_copy(0, blk % N_YRING, c).wait()
    for blk in (last - 2, last - 1):
        for c in range(NCH):
            out_copy(write_back_chunk(blk, c), blk % N_YRING, c).start()
    for blk in (last - 2, last - 1):
        for c in range(NCH):
            out_copy(0, blk % N_YRING, c).wait()


def _expert_call(src, dst, bexp, enext, nused, xs, layer, w_gate, w_up, w_down):
    d = xs.shape[-1]
    de = w_gate.shape[-1]
    grid_spec = pltpu.PrefetchScalarGridSpec(
        num_scalar_prefetch=5,
        grid=(1,),
        in_specs=[pl.BlockSpec(memory_space=pl.ANY)] * 4,
        out_specs=pl.BlockSpec(memory_space=pl.ANY),
        scratch_shapes=[
            pltpu.VMEM((N_XRING, BM, d), BF16),
            pltpu.VMEM((N_YRING, BM, d), BF16),
            pltpu.VMEM((2, d, de), F32),
            pltpu.VMEM((2, d, de), F32),
            pltpu.VMEM((2, de, d), F32),
            pltpu.VMEM((2, d, de), BF16),
            pltpu.VMEM((2, d, de), BF16),
            pltpu.VMEM((2, de, d), BF16),
            pltpu.SemaphoreType.DMA((N_XRING,)),
            pltpu.SemaphoreType.DMA((N_YRING,)),
            pltpu.SemaphoreType.DMA((2,)),
        ],
    )
    return pl.pallas_call(
        functools.partial(_expert_kernel, layer),
        out_shape=jax.ShapeDtypeStruct(xs.shape, xs.dtype),
        grid_spec=grid_spec,
        input_output_aliases={5: 0},
        compiler_params=pltpu.CompilerParams(
            dimension_semantics=("arbitrary",), vmem_limit_bytes=VMEM_LIMIT),
        name="moe_experts",
    )(src, dst, bexp, enext, nused, xs, w_gate, w_up, w_down)


def _combine_kernel(ysa_ref, ysb_ref, pos_ref, gate_ref, x1_ref, mod_ref, lng_ref, lnb_ref, o_ref):
    m = mod_ref[...]
    for k, ys_ref in enumerate((ysa_ref, ysb_ref)):
        rows = slice(k * TT, (k + 1) * TT)
        posc = jnp.transpose(pos_ref[k].astype(F32))
        gatec = jnp.transpose(gate_ref[k])
        iota_c = lax.broadcasted_iota(I32, (TT, R_TILE), 1).astype(F32)
        pick = (jnp.where(iota_c == posc[:, 0:1], gatec[:, 0:1], 0.0)
                + jnp.where(iota_c == posc[:, 1:2], gatec[:, 1:2], 0.0)).astype(BF16)
        y = _dot(pick, ys_ref[...])
        o_ref[rows, :] = _ln(ALPHA * x1_ref[rows, :] + (1.0 + m[5:6]) * y, lng_ref[...], lnb_ref[...])


def _combine_call(ys, pos, gate, x1, mod, layer, tiles_per_seq, ln_g, ln_b):
    t, d = x1.shape
    n_tiles = t // TT
    assert tiles_per_seq % 2 == 0
    const2 = lambda n: (0, 0)
    return pl.pallas_call(
        _combine_kernel,
        out_shape=jax.ShapeDtypeStruct((t, d), F32),
        grid=(n_tiles // 2,),
        in_specs=[
            pl.BlockSpec((R_TILE, d), lambda n: (2 * n + 1, 0)),
            pl.BlockSpec((R_TILE, d), lambda n: (2 * n + 2, 0)),
            pl.BlockSpec((2, TOP_K, TT), lambda n: (n, 0, 0)),
            pl.BlockSpec((2, TOP_K, TT), lambda n: (n, 0, 0)),
            pl.BlockSpec((2 * TT, d), lambda n: (n, 0)),
            pl.BlockSpec((None, None, 6, d), lambda n: (layer, 2 * n // tiles_per_seq, 0, 0)),
            pl.BlockSpec((1, d), const2),
            pl.BlockSpec((1, d), const2),
        ],
        out_specs=pl.BlockSpec((2 * TT, d), lambda n: (n, 0)),
        compiler_params=pltpu.CompilerParams(
            dimension_semantics=("arbitrary",), vmem_limit_bytes=VMEM_LIMIT),
        name="moe_combine",
    )(ys, ys, pos, gate, x1, mod, ln_g, ln_b)


def _dispatch_plan(cnt, n_blocks):
    n_tiles = cnt.shape[0]
    chunks_per_tile = R_TILE // CH
    nch = (cnt + CH - 1) // CH
    padoff_ch = jnp.cumsum(nch, axis=1) - nch
    tot = jnp.sum(nch, axis=0)
    totpad = (tot + NCH - 1) // NCH * NCH
    eend = jnp.cumsum(totpad)
    ebase = eend - totpad
    tbase = jnp.cumsum(nch, axis=0) - nch
    start = (ebase[None, :] + tbase).T.reshape(-1)
    base = ((jnp.arange(n_tiles, dtype=I32)[:, None] + 1) * chunks_per_tile + padoff_ch).T.reshape(-1)
    vals = jnp.stack([start, base, nch.T.reshape(-1)], axis=1)
    delta = vals - jnp.concatenate([jnp.zeros((1, 3), I32), vals[:-1]], axis=0)
    digits = jnp.concatenate([delta // LANES, delta % LANES], axis=1).astype(BF16)
    slot = jnp.arange(n_blocks * NCH, dtype=I32)
    started = (start[None, :] <= slot[:, None]).astype(BF16)
    got = jnp.dot(started, digits, preferred_element_type=F32).astype(I32)
    seg = got[:, :3] * LANES + got[:, 3:]
    j = slot - seg[:, 0]
    valid = (j < seg[:, 2]) & (slot < eend[-1])
    zero_chunk = chunks_per_tile - 1
    src = jnp.where(valid, seg[:, 1] + j, zero_chunk).astype(I32)
    blk = slot // NCH
    scratch = (blk % 2) * NCH + slot % NCH
    dst = jnp.where(valid, src, scratch).astype(I32)
    first = jnp.arange(n_blocks, dtype=I32) * NCH
    bexp = jnp.minimum(jnp.sum(eend[None, :] <= first[:, None], axis=1), N_EXPERTS - 1).astype(I32)
    nused = (eend[-1] // NCH).astype(I32).reshape(1)
    ids = jnp.arange(N_EXPERTS, dtype=I32)
    later = jnp.where((ids[None, :] > ids[:, None]) & (totpad > 0)[None, :], ids[None, :], N_EXPERTS)
    next_of = jnp.min(later, axis=1)
    next_of = jnp.where(next_of == N_EXPERTS, ids, next_of)
    enext = jnp.sum(jnp.where(bexp[:, None] == ids[None, :], next_of[None, :], 0), axis=1).astype(I32)
    return src, dst, bexp, enext, nused


def _t5_bucket(rel):
    nb = N_BUCKETS // 2
    max_exact = nb // 2
    ret = jnp.where(rel > 0, nb, 0)
    n = jnp.abs(rel)
    nf = jnp.maximum(n, 1).astype(jnp.float32)
    large = max_exact + (jnp.log(nf / max_exact) / math.log(MAX_DISTANCE / max_exact)
                         * (nb - max_exact)).astype(jnp.int32)
    large = jnp.minimum(large, nb - 1)
    return ret + jnp.where(n < max_exact, n, large)


def _band_bias(rel_bias):
    qi = jnp.arange(WINDOW)
    kj = jnp.arange(3 * WINDOW)
    rel = kj[None, :] - WINDOW - qi[:, None]
    pick = _t5_bucket(rel)[:, :, None, None] == jnp.arange(N_BUCKETS)[None, None, :, None]
    bias = jnp.sum(jnp.where(pick, rel_bias.astype(F32)[None, None], 0.0), axis=2) * LOG2E
    bias = jnp.where((jnp.abs(rel) <= WINDOW)[:, :, None], bias, NEG_INF)
    grp = B_HEADS // B_KV_HEADS
    bias = jnp.transpose(bias, (2, 1, 0)).reshape(B_KV_HEADS, grp, 3 * WINDOW, WINDOW)
    return jnp.transpose(bias, (0, 2, 1, 3)).reshape(B_KV_HEADS, 3 * WINDOW, grp * WINDOW)


def kernel(x, c, ada_w, ada_b, w_in, b_in, gmlp_ln_g, gmlp_ln_b, gmlp_ws, gmlp_bs, attn_sink, conv_w,
           conv_b, conv_ln_g, conv_ln_b, w_out, b_out, ln_mix_g, ln_mix_b, w_gate, w_up, w_down,
           ln_ffn_g, ln_ffn_b, rel_bias, router_w, router_bias):
    bsz, seq, d = x.shape
    n_layers = ada_w.shape[0]
    t = bsz * seq
    nt = seq // TT
    n_tiles = t // TT
    max_chunks = n_tiles * (TOP_K * TT // CH + N_EXPERTS) + N_EXPERTS * (NCH - 1)
    n_blocks = -(-max_chunks // NCH)

    mod = _ada_call(c, ada_w, ada_b).reshape(n_layers, bsz, 6, d)
    bias = _band_bias(rel_bias)
    rw = router_w.T.astype(BF16)
    rbias = router_bias.astype(F32).reshape(N_EXPERTS, 1)
    row = lambda a: a.reshape(1, -1)

    for l in range(n_layers):
        bsb = jnp.repeat(gmlp_bs[l].T, HEAD_DIM, axis=1)
        ya, qt, k, vt, yc = _inproj_call(
            x, mod, l, w_in[l], b_in[l], row(gmlp_ln_g[l]), row(gmlp_ln_b[l]),
            gmlp_ws[l].astype(BF16), bsb, conv_w[l], row(conv_b[l]), row(conv_ln_g[l]), row(conv_ln_b[l]))
        sink = jnp.repeat(attn_sink[l].astype(F32) * LOG2E, WINDOW).reshape(B_KV_HEADS, -1)
        x1, xs, pos, gate, cntb = _mixer_call(
            seq, qt, k, vt, bias, sink, yc, ya, x, mod, l, w_out[l].astype(BF16),
            row(b_out[l]), row(ln_mix_g[l]), row(ln_mix_b[l]), rw, rbias)
        cnt = cntb[:, :, 0].astype(I32)
        src, dst, bexp, enext, nused = _dispatch_plan(cnt, n_blocks)
        ys = _expert_call(src, dst, bexp, enext, nused, xs, l, w_gate, w_up, w_down)
        x = _combine_call(ys, pos, gate, x1.reshape(t, d), mod, l, nt, row(ln_ffn_g[l]),
                          row(ln_ffn_b[l])).reshape(bsz, seq, d)
    return x
```

```python
import functools
import math

import jax
import jax.numpy as jnp
from jax import lax
from jax.experimental import pallas as pl
from jax.experimental.pallas import tpu as pltpu

F32 = jnp.float32
BF16 = jnp.bfloat16
I32 = jnp.int32

D_MODEL = 1024
DEPTH = 2
HEAD_DIM = 64
A_WIDTH = 256
A_HEADS = 4
CHUNK = 128
B_WIDTH = 512
B_HEADS = 8
B_KV_HEADS = 2
KV_WIDTH = B_KV_HEADS * HEAD_DIM
WINDOW = 128
N_BUCKETS = 32
MAX_DISTANCE = 128
C_WIDTH = 256
CONV_WIDTH = 31
CONV_PAD = CONV_WIDTH // 2
IN_WIDTH = 2 * A_WIDTH + B_WIDTH + 2 * KV_WIDTH + 2 * C_WIDTH
N_EXPERTS = 32
N_GROUPS = 4
EXPERTS_PER_GROUP = N_EXPERTS // N_GROUPS
TOP_K = 2
D_EXPERT = D_MODEL // 2
ALPHA = (2 * DEPTH) ** 0.25
LN_EPS = 1e-5
NEG_INF = -1e30
LOG2E = 1.4426950408889634
Q_SCALE = HEAD_DIM ** -0.5 * LOG2E

LANES = 128
SUBLANES = 8
BF16_SUBLANES = 16
VMEM_LIMIT = 48 * 1024 * 1024

ADA_TN = 1536
TS = 1024
TT = 512
CH = BF16_SUBLANES
R_TILE = 1536
BM = 512
NCH = BM // CH
N_XRING = 6
N_YRING = 4
N_SORT_BLOCKS = 6
N_SBUF = 2
HALO = 16

assert R_TILE >= TOP_K * TT + N_EXPERTS * (CH - 1) + CH
assert R_TILE >= 2 * NCH * CH + CH


def _sigmoid(x):
    return 1.0 / (1.0 + jnp.exp(-x))


def _gelu_tanh(x):
    return x * (0.5 * (1.0 + jnp.tanh(0.7978845608028654 * (x + 0.044715 * (x * x * x)))))


def _ln(x, g, b):
    mu = jnp.mean(x, axis=-1, keepdims=True)
    xc = x - mu
    var = jnp.mean(xc * xc, axis=-1, keepdims=True)
    return xc * lax.rsqrt(var + LN_EPS) * g + b


def _dot(a, b):
    return jnp.dot(a, b, preferred_element_type=F32)


def _dot_nt(a, b):
    return lax.dot_general(a, b, (((1,), (1,)), ((), ())), preferred_element_type=F32)


def _ada_kernel(c_ref, w_ref, b_ref, o_ref):
    c = c_ref[...]
    s = (c * _sigmoid(c)).astype(BF16)
    o_ref[0] = _dot(s, w_ref[0].astype(BF16)) + b_ref[0]


def _ada_call(c, ada_w, ada_b):
    nl, d, n = ada_w.shape
    bsz = c.shape[0]
    return pl.pallas_call(
        _ada_kernel,
        out_shape=jax.ShapeDtypeStruct((nl, bsz, n), F32),
        grid=(nl, n // ADA_TN),
        in_specs=[
            pl.BlockSpec((bsz, d), lambda l, j: (0, 0)),
            pl.BlockSpec((1, d, ADA_TN), lambda l, j: (l, 0, j)),
            pl.BlockSpec((1, 1, ADA_TN), lambda l, j: (l, 0, j)),
        ],
        out_specs=pl.BlockSpec((1, bsz, ADA_TN), lambda l, j: (l, 0, j)),
        compiler_params=pltpu.CompilerParams(
            dimension_semantics=("arbitrary", "arbitrary"), vmem_limit_bytes=VMEM_LIMIT),
        name="ada_mod",
    )(c, ada_w, ada_b.reshape(nl, 1, n))


def _inproj_kernel(x_ref, xp_ref, xn_ref, mod_ref, w_ref, b_ref, wqt_ref, bq_ref, wvt_ref, bv_ref,
                   lng_ref, lnb_ref, ws_ref, bsb_ref, cw_ref, cb_ref, clg_ref, clb_ref,
                   ya_ref, qt_ref, k_ref, vt_ref, yc_ref, conv_scr, z_scr):
    i = pl.program_id(1)
    n_i = pl.num_programs(1)
    m = mod_ref[...]

    def modulate(xv):
        return (xv * (1.0 + m[1:2]) + m[0:1]).astype(BF16)

    hb = modulate(x_ref[0])
    col_u, col_v = 0, A_WIDTH
    col_k = 2 * A_WIDTH + B_WIDTH
    col_a = col_k + 2 * KV_WIDTH
    col_g = col_a + C_WIDTH

    def proj(lhs, c0, width):
        return _dot(lhs, w_ref[:, c0:c0 + width]) + b_ref[:, c0:c0 + width]

    hx = jnp.concatenate([modulate(xp_ref[0]), hb, modulate(xn_ref[0])], axis=0)
    yg = proj(hx, col_a, C_WIDTH) * _sigmoid(proj(hx, col_g, C_WIDTH))
    conv_scr[0:HALO, :] = yg[0:HALO] * jnp.where(i > 0, 1.0, 0.0)
    conv_scr[HALO:HALO + TS, :] = yg[HALO:HALO + TS]
    conv_scr[HALO + TS:, :] = yg[HALO + TS:] * jnp.where(i < n_i - 1, 1.0, 0.0)
    first = HALO - CONV_PAD
    acc = jnp.zeros((TS, C_WIDTH), F32) + cb_ref[...]
    for r in range(SUBLANES):
        z = None
        for a in range(-(-(first + CONV_WIDTH) // SUBLANES)):
            w = a * SUBLANES + r - first
            if 0 <= w < CONV_WIDTH:
                term = conv_scr[a * SUBLANES:a * SUBLANES + TS + SUBLANES, :] * cw_ref[w:w + 1, :]
                z = term if z is None else z + term
        if r == 0:
            acc = acc + z[0:TS]
        else:
            z_scr[r] = z
            acc = acc + z_scr[r, r:r + TS, :]
    yc = _ln(acc, clg_ref[...], clb_ref[...])
    yc_ref[0] = (yc * _sigmoid(yc)).astype(BF16)

    u = _gelu_tanh(proj(hb, col_u, A_WIDTH))
    v = _gelu_tanh(proj(hb, col_v, A_WIDTH))
    qt_ref[0] = ((_dot_nt(wqt_ref[...], hb) + bq_ref[...]) * Q_SCALE).astype(BF16)
    k_ref[0] = proj(hb, col_k, KV_WIDTH).astype(BF16)
    vt_ref[0] = (_dot_nt(wvt_ref[...], hb) + bv_ref[...]).astype(BF16)

    vb = _ln(v, lng_ref[...], lnb_ref[...]).astype(BF16)
    head_of_lane = lax.broadcasted_iota(I32, (CHUNK, A_WIDTH), 1) // HEAD_DIM
    for ch in range(TS // CHUNK):
        vc = vb[ch * CHUNK:(ch + 1) * CHUNK]
        acc = bsb_ref[...]
        for hh in range(A_HEADS):
            acc = acc + _dot(ws_ref[hh], jnp.where(head_of_lane == hh, vc, jnp.zeros_like(vc)))
        ya_ref[0, ch * CHUNK:(ch + 1) * CHUNK, :] = (u[ch * CHUNK:(ch + 1) * CHUNK] * acc).astype(BF16)


def _inproj_call(x, mod, layer, w_in, b_in, ln_g, ln_b, ws, bsb, conv_w, conv_b, conv_ln_g, conv_ln_b):
    bsz, seq, d = x.shape
    grid = (bsz, seq // TS)
    hb = TS // HALO
    const2 = lambda b, i: (0, 0)
    row = lambda b, i: (b, i, 0)
    colblk = lambda b, i: (b, 0, i)
    prev_h = lambda b, i: (b, jnp.maximum(i * hb - 1, 0), 0)
    next_h = lambda b, i: (b, jnp.minimum((i + 1) * hb, seq // HALO - 1), 0)
    q0 = 2 * A_WIDTH
    v0 = q0 + B_WIDTH + KV_WIDTH
    wb = w_in.astype(BF16)
    wqt = w_in[:, q0:q0 + B_WIDTH].T.astype(BF16)
    wvt = w_in[:, v0:v0 + KV_WIDTH].T.astype(BF16)
    bq = b_in[q0:q0 + B_WIDTH].reshape(B_WIDTH, 1)
    bv = b_in[v0:v0 + KV_WIDTH].reshape(KV_WIDTH, 1)

    def out(width):
        return jax.ShapeDtypeStruct((bsz, seq, width), BF16), pl.BlockSpec((1, TS, width), row)

    def out_t(width):
        return jax.ShapeDtypeStruct((bsz, width, seq), BF16), pl.BlockSpec((1, width, TS), colblk)

    outs = [out(A_WIDTH), out_t(B_WIDTH), out(KV_WIDTH), out_t(KV_WIDTH), out(C_WIDTH)]
    return pl.pallas_call(
        _inproj_kernel,
        out_shape=[o[0] for o in outs],
        grid=grid,
        in_specs=[
            pl.BlockSpec((1, TS, d), row),
            pl.BlockSpec((1, HALO, d), prev_h),
            pl.BlockSpec((1, HALO, d), next_h),
            pl.BlockSpec((None, None, 6, d), lambda b, i: (layer, b, 0, 0)),
            pl.BlockSpec((d, IN_WIDTH), const2),
            pl.BlockSpec((1, IN_WIDTH), const2),
            pl.BlockSpec((B_WIDTH, d), const2),
            pl.BlockSpec((B_WIDTH, 1), const2),
            pl.BlockSpec((KV_WIDTH, d), const2),
            pl.BlockSpec((KV_WIDTH, 1), const2),
            pl.BlockSpec((1, A_WIDTH), const2),
            pl.BlockSpec((1, A_WIDTH), const2),
            pl.BlockSpec((A_HEADS, CHUNK, CHUNK), lambda b, i: (0, 0, 0)),
            pl.BlockSpec((CHUNK, A_WIDTH), const2),
            pl.BlockSpec((CONV_WIDTH, C_WIDTH), const2),
            pl.BlockSpec((1, C_WIDTH), const2),
            pl.BlockSpec((1, C_WIDTH), const2),
            pl.BlockSpec((1, C_WIDTH), const2),
        ],
        out_specs=[o[1] for o in outs],
        scratch_shapes=[
            pltpu.VMEM((TS + 2 * HALO, C_WIDTH), F32),
            pltpu.VMEM((SUBLANES, TS + SUBLANES, C_WIDTH), F32),
        ],
        compiler_params=pltpu.CompilerParams(
            dimension_semantics=("arbitrary", "arbitrary"), vmem_limit_bytes=VMEM_LIMIT),
        name="inproj_gmlp_conv",
    )(x, x, x, mod, wb, b_in.reshape(1, -1), wqt, bq, wvt, bv, ln_g, ln_b, ws, bsb,
      conv_w, conv_b, conv_ln_g, conv_ln_b)


def _first_argmax(vals, iota_f, width):
    m = jnp.max(vals, axis=0, keepdims=True)
    idx = jnp.min(jnp.where(vals == m, iota_f, float(width)), axis=0, keepdims=True)
    return m, idx


def _mixer_kernel(seq_len, tiles_per_seq, qt_ref, kp_ref, kc_ref, kn_ref, vtp_ref, vtc_ref, vtn_ref,
                  bias_ref, sink_ref, yc_ref, ya_ref, x_ref, mod_ref, wo_ref, bo_ref, lng_ref, lnb_ref,
                  rw_ref, rb_ref, lt_ref, before_ref, x1_ref, xs_ref, pos_ref, gate_ref, cnt_ref,
                  ot_scr, s_scr, h2_scr, pos_scr):
    n = pl.program_id(0)
    last_tile = pl.num_programs(0) - 2
    i = jnp.minimum(n, last_tile) % tiles_per_seq
    t0 = i * TT
    m = mod_ref[...]

    @pl.when(n == 0)
    def _():
        h2_scr[...] = jnp.zeros_like(h2_scr)
        pos_scr[...] = jnp.zeros_like(pos_scr)

    pos_prev = pos_scr[...]
    sort_rows = R_TILE // N_SORT_BLOCKS

    def sort_block(j):
        iota_r = j * sort_rows + lax.broadcasted_iota(I32, (sort_rows, TT), 0)
        onehot = jnp.where(iota_r == pos_prev[0:1], 1.0,
                           jnp.where(iota_r == pos_prev[1:2], 1.0, 0.0)).astype(BF16)
        xs_ref[j * sort_rows:(j + 1) * sort_rows, :] = _dot(onehot, h2_scr[...]).astype(BF16)

    def out_proj(yb):
        return (_dot(ya_ref[0], wo_ref[0:A_WIDTH, :])
                + _dot(yb, wo_ref[A_WIDTH:A_WIDTH + B_WIDTH, :])
                + _dot(yc_ref[0], wo_ref[A_WIDTH + B_WIDTH:, :]) + bo_ref[...])

    def route(y):
        x1 = _ln(ALPHA * x_ref[0] + (1.0 + m[2:3]) * y, lng_ref[...], lnb_ref[...])
        x1_ref[0] = x1
        h2 = (x1 * (1.0 + m[4:5]) + m[3:4]).astype(BF16)
        scores = _sigmoid(_dot_nt(rw_ref[...], h2))
        sel = scores + rb_ref[...]
        iota_f = lax.broadcasted_iota(I32, (EXPERTS_PER_GROUP, TT), 0).astype(F32)
        best = None
        for g in range(N_GROUPS):
            sl = slice(g * EXPERTS_PER_GROUP, (g + 1) * EXPERTS_PER_GROUP)
            sg = sel[sl]
            m1, i1 = _first_argmax(sg, iota_f, EXPERTS_PER_GROUP)
            m2, i2 = _first_argmax(jnp.where(iota_f == i1, -jnp.inf, sg), iota_f, EXPERTS_PER_GROUP)
            sc = scores[sl]
            s1 = jnp.sum(jnp.where(iota_f == i1, sc, 0.0), axis=0, keepdims=True)
            s2 = jnp.sum(jnp.where(iota_f == i2, sc, 0.0), axis=0, keepdims=True)
            cand = (m1 + m2, i1 + g * EXPERTS_PER_GROUP, i2 + g * EXPERTS_PER_GROUP, s1, s2)
            if best is None:
                best = cand
            else:
                take = cand[0] > best[0]
                best = tuple(jnp.where(take, c, b) for c, b in zip(cand, best))
        _, e1, e2, s1, s2 = best
        gate_ref[0] = jnp.concatenate([s1, s2], axis=0) / (s1 + s2)

        iota_e = lax.broadcasted_iota(I32, (N_EXPERTS, TT), 0).astype(F32)
        in0 = iota_e == e1
        in1 = iota_e == e2
        member = jnp.where(in0 | in1, 1.0, 0.0)
        cnt = jnp.sum(member, axis=1, keepdims=True)
        cnt_ref[0] = jnp.broadcast_to(cnt, (N_EXPERTS, LANES))
        nch = jnp.floor((cnt + (CH - 1)) * (1.0 / CH))
        nch_pad = jnp.concatenate([jnp.broadcast_to(nch, (N_EXPERTS, LANES)),
                                   jnp.zeros((LANES - N_EXPERTS, LANES), F32)], axis=0).astype(BF16)
        padoff = _dot(lt_ref[...], nch_pad)[:, 0:1] * float(CH)
        rank = _dot(member.astype(BF16), before_ref[...])
        posf = padoff + rank
        pos0 = jnp.sum(jnp.where(in0, posf, 0.0), axis=0, keepdims=True).astype(I32)
        pos1 = jnp.sum(jnp.where(in1, posf, 0.0), axis=0, keepdims=True).astype(I32)
        pos = jnp.concatenate([pos0, pos1], axis=0)
        pos_ref[0] = pos
        return pos, h2

    kfull = jnp.concatenate([kp_ref[0], kc_ref[0], kn_ref[0]], axis=0)
    vtfull = jnp.concatenate([vtp_ref[0], vtc_ref[0], vtn_ref[0]], axis=1)
    grp = B_HEADS // B_KV_HEADS
    n_qb = TT // WINDOW
    units = [(jb, g) for jb in range(n_qb) for g in range(B_KV_HEADS)]
    key_i = lax.broadcasted_iota(I32, (3 * WINDOW, 1), 0)

    def scores(u):
        jb, g = units[u]
        kb = kfull[jb * WINDOW:(jb + 3) * WINDOW, g * HEAD_DIM:(g + 1) * HEAD_DIM]
        qt = jnp.concatenate(
            [qt_ref[0, h * HEAD_DIM:(h + 1) * HEAD_DIM, jb * WINDOW:(jb + 1) * WINDOW]
             for h in range(g * grp, (g + 1) * grp)], axis=1)
        s = _dot(kb, qt) + bias_ref[g]
        if jb == 0 or jb == n_qb - 1:
            kpos = t0 + (jb - 1) * WINDOW + key_i
            s = jnp.where((kpos >= 0) & (kpos < seq_len), s, NEG_INF)
        s_scr[u % N_SBUF] = s

    def values(u):
        jb, g = units[u]
        s = s_scr[u % N_SBUF]
        sink = sink_ref[g:g + 1, :]
        mx = jnp.maximum(jnp.max(s, axis=0, keepdims=True), sink)
        p = jnp.exp2(s - mx)
        den = jnp.sum(p, axis=0, keepdims=True) + jnp.exp2(sink - mx)
        vt = vtfull[g * HEAD_DIM:(g + 1) * HEAD_DIM, jb * WINDOW:(jb + 3) * WINDOW]
        ot = _dot(vt, p.astype(BF16)) / den
        for hh in range(grp):
            h = g * grp + hh
            ot_scr[h * HEAD_DIM:(h + 1) * HEAD_DIM, jb * WINDOW:(jb + 1) * WINDOW] = (
                ot[:, hh * WINDOW:(hh + 1) * WINDOW])

    for u in range(N_SBUF - 1):
        scores(u)
    for u in range(len(units)):
        if u + N_SBUF - 1 < len(units):
            scores(u + N_SBUF - 1)
        values(u)
    y = out_proj(jnp.transpose(ot_scr[...]).astype(BF16))
    for j in range(N_SORT_BLOCKS):
        sort_block(j)
    new_pos, new_h2 = route(y)
    pos_scr[...] = new_pos
    h2_scr[...] = new_h2


def _mixer_call(seq_len, qt, k, vt, bias, sink, yc, ya, x, mod, layer, w_out, b_out, ln_g, ln_b, rw, rbias):
    bsz, seq, d = x.shape
    nt = seq // TT
    n_tiles = bsz * nt
    kb = TT // WINDOW
    grp = B_HEADS // B_KV_HEADS
    const2 = lambda n: (0, 0)

    def at_tile(fn):
        def index_map(n):
            t = jnp.minimum(n, n_tiles - 1)
            return fn(t // nt, t % nt)
        return index_map

    row = at_tile(lambda b, i: (b, i, 0))
    colblk = at_tile(lambda b, i: (b, 0, i))
    prev_k = at_tile(lambda b, i: (b, jnp.maximum(i * kb - 1, 0), 0))
    next_k = at_tile(lambda b, i: (b, jnp.minimum((i + 1) * kb, seq // WINDOW - 1), 0))
    prev_v = at_tile(lambda b, i: (b, 0, jnp.maximum(i * kb - 1, 0)))
    next_v = at_tile(lambda b, i: (b, 0, jnp.minimum((i + 1) * kb, seq // WINDOW - 1)))
    tile = at_tile(lambda b, i: (b * nt + i, 0, 0))
    lt = (jnp.arange(LANES)[None, :] < jnp.arange(N_EXPERTS)[:, None]).astype(BF16)
    before = (jnp.arange(TT)[:, None] < jnp.arange(TT)[None, :]).astype(BF16)
    return pl.pallas_call(
        functools.partial(_mixer_kernel, seq_len, nt),
        out_shape=[
            jax.ShapeDtypeStruct((bsz, seq, d), F32),
            jax.ShapeDtypeStruct(((n_tiles + 1) * R_TILE, d), BF16),
            jax.ShapeDtypeStruct((n_tiles, TOP_K, TT), I32),
            jax.ShapeDtypeStruct((n_tiles, TOP_K, TT), F32),
            jax.ShapeDtypeStruct((n_tiles, N_EXPERTS, LANES), F32),
        ],
        grid=(n_tiles + 1,),
        in_specs=[
            pl.BlockSpec((1, B_WIDTH, TT), colblk),
            pl.BlockSpec((1, WINDOW, KV_WIDTH), prev_k),
            pl.BlockSpec((1, TT, KV_WIDTH), row),
            pl.BlockSpec((1, WINDOW, KV_WIDTH), next_k),
            pl.BlockSpec((1, KV_WIDTH, WINDOW), prev_v),
            pl.BlockSpec((1, KV_WIDTH, TT), colblk),
            pl.BlockSpec((1, KV_WIDTH, WINDOW), next_v),
            pl.BlockSpec((B_KV_HEADS, 3 * WINDOW, grp * WINDOW), lambda n: (0, 0, 0)),
            pl.BlockSpec((B_KV_HEADS, grp * WINDOW), const2),
            pl.BlockSpec((1, TT, C_WIDTH), row),
            pl.BlockSpec((1, TT, A_WIDTH), row),
            pl.BlockSpec((1, TT, d), row),
            pl.BlockSpec((None, None, 6, d), at_tile(lambda b, i: (layer, b, 0, 0))),
            pl.BlockSpec((d, d), const2),
            pl.BlockSpec((1, d), const2),
            pl.BlockSpec((1, d), const2),
            pl.BlockSpec((1, d), const2),
            pl.BlockSpec((N_EXPERTS, d), const2),
            pl.BlockSpec((N_EXPERTS, 1), const2),
            pl.BlockSpec((N_EXPERTS, LANES), const2),
            pl.BlockSpec((TT, TT), const2),
        ],
        out_specs=[
            pl.BlockSpec((1, TT, d), row),
            pl.BlockSpec((R_TILE, d), lambda n: (n, 0)),
            pl.BlockSpec((1, TOP_K, TT), tile),
            pl.BlockSpec((1, TOP_K, TT), tile),
            pl.BlockSpec((1, N_EXPERTS, LANES), tile),
        ],
        scratch_shapes=[
            pltpu.VMEM((B_WIDTH, TT), F32),
            pltpu.VMEM((N_SBUF, 3 * WINDOW, grp * WINDOW), F32),
            pltpu.VMEM((TT, d), BF16),
            pltpu.VMEM((TOP_K, TT), I32),
        ],
        compiler_params=pltpu.CompilerParams(
            dimension_semantics=("arbitrary",), vmem_limit_bytes=VMEM_LIMIT),
        name="mixer_router_sort",
    )(qt, k, k, k, vt, vt, vt, bias, sink, yc, ya, x, mod, w_out, b_out, ln_g, ln_b, rw, rbias, lt, before)


def _expert_kernel(layer, src_ref, dst_ref, bexp_ref, enext_ref, nused_ref, xs_hbm, wg_hbm, wu_hbm, wd_hbm,
                   ys_hbm, xbuf, ybuf, wg_st, wu_st, wd_st, wgb, wub, wdb, sem_in, sem_out, sem_w):
    n_blk = bexp_ref.shape[0]
    nused = nused_ref[0]
    chunks_per_tile = R_TILE // CH
    zero_chunk = chunks_per_tile - 1
    scratch_chunk = 0
    de = wgb.shape[2]
    half = de // 2

    def in_copy(chunk, s, c):
        return pltpu.make_async_copy(
            xs_hbm.at[pl.ds(pl.multiple_of(chunk * CH, CH), CH)],
            xbuf.at[s, pl.ds(c * CH, CH)], sem_in.at[s])

    def out_copy(chunk, s, c):
        return pltpu.make_async_copy(
            ybuf.at[s, pl.ds(c * CH, CH)],
            ys_hbm.at[pl.ds(pl.multiple_of(chunk * CH, CH), CH)], sem_out.at[s])

    def weight_copies(e, s):
        return [pltpu.make_async_copy(wg_hbm.at[layer, e], wg_st.at[s], sem_w.at[s]),
                pltpu.make_async_copy(wu_hbm.at[layer, e], wu_st.at[s], sem_w.at[s]),
                pltpu.make_async_copy(wd_hbm.at[layer, e], wd_st.at[s], sem_w.at[s])]

    def gather_chunk(b, c):
        return jnp.where(b < nused, src_ref[jnp.minimum(b, n_blk - 1) * NCH + c], zero_chunk)

    def write_back_chunk(b, c):
        real = (b >= 0) & (b < nused)
        return jnp.where(real, dst_ref[jnp.clip(b, 0, n_blk - 1) * NCH + c], scratch_chunk + (b % 2) * NCH + c)

    ybuf[...] = jnp.zeros_like(ybuf)
    wgb[...] = jnp.zeros_like(wgb)
    wub[...] = jnp.zeros_like(wub)
    wdb[...] = jnp.zeros_like(wdb)
    for b in range(4):
        for c in range(NCH):
            in_copy(gather_chunk(b, c), b, c).start()
    for b in range(2):
        for c in range(NCH):
            out_copy(scratch_chunk + b * NCH + c, b, c).start()

    @pl.when(nused > 0)
    def _():
        for cp in weight_copies(bexp_ref[0], 0):
            cp.start()

    def pair(i, n_changes):
        blocks = (2 * i, 2 * i + 1)
        wslot = []
        for blk in blocks:
            jb = jnp.minimum(blk, n_blk - 1)
            expert = bexp_ref[jb]
            prev_expert = bexp_ref[jnp.maximum(jb - 1, 0)]
            change = (blk < nused) & ((blk == 0) | (expert != prev_expert))

            @pl.when(change)
            def _(expert=expert, jb=jb, n_changes=n_changes):
                ws = n_changes % 2
                for cp in weight_copies(expert, ws):
                    cp.wait()
                wgb[ws] = wg_st[ws].astype(BF16)
                wub[ws] = wu_st[ws].astype(BF16)
                wdb[ws] = wd_st[ws].astype(BF16)
                upcoming = enext_ref[jb]

                @pl.when(upcoming != expert)
                def _():
                    for cp in weight_copies(upcoming, 1 - ws):
                        cp.start()

            n_changes = n_changes + change.astype(I32)
            wslot.append(jnp.maximum(n_changes - 1, 0) % 2)

        for blk in blocks:
            for c in range(NCH):
                in_copy(0, blk % N_XRING, c).wait()
            for c in range(NCH):
                out_copy(0, blk % N_YRING, c).wait()

        dmas = []
        for blk in blocks:
            for c in range(NCH):
                dmas.append(functools.partial(
                    lambda b, c: out_copy(write_back_chunk(b, c), b % N_YRING, c).start(), blk - 2, c))
        for blk in blocks:
            for c in range(NCH):
                dmas.append(functools.partial(
                    lambda b, c: in_copy(gather_chunk(b, c), b % N_XRING, c).start(priority=1), blk + 4, c))
        n_groups = 8
        per_group = len(dmas) // n_groups

        def issue(gidx):
            for start in dmas[gidx * per_group:(gidx + 1) * per_group]:
                start()

        gidx = 0
        for k, blk in enumerate(blocks):
            x = xbuf[blk % N_XRING]
            ws = wslot[k]
            hmid = []
            for h in range(2):
                cols = slice(h * half, (h + 1) * half)
                g = _dot(x, wgb[ws, :, cols])
                issue(gidx)
                u = _dot(x, wub[ws, :, cols])
                issue(gidx + 1)
                gidx += 2
                hmid.append(((g * _sigmoid(g)) * u).astype(BF16))
            y = _dot(hmid[0], wdb[ws, 0:half, :]) + _dot(hmid[1], wdb[ws, half:, :])
            ybuf[blk % N_YRING] = y.astype(BF16)
        return n_changes

    n_pairs = (nused + 1) // 2
    lax.fori_loop(0, n_pairs, pair, jnp.int32(0))

    last = 2 * n_pairs
    for blk in range(4):
        for c in range(NCH):
            in_copy(0, (last + blk) % N_XRING, c).wait()
    for blk in (last, last + 1):
        for c in range(NCH):
            out_copy(0, blk % N_YRING, c).wait()
    for blk in (last - 2, last - 1):
        for c in range(NCH):
            out_copy(write_back_chunk(blk, c), blk % N_YRING, c).start()
    for blk in (last - 2, last - 1):
        for c in range(NCH):
            out_copy(0, blk % N_YRING, c).wait()


def _expert_call(src, dst, bexp, enext, nused, xs, layer, w_gate, w_up, w_down):
    d = xs.shape[-1]
    de = w_gate.shape[-1]
    grid_spec = pltpu.PrefetchScalarGridSpec(
        num_scalar_prefetch=5,
        grid=(1,),
        in_specs=[pl.BlockSpec(memory_space=pl.ANY)] * 4,
        out_specs=pl.BlockSpec(memory_space=pl.ANY),
        scratch_shapes=[
            pltpu.VMEM((N_XRING, BM, d), BF16),
            pltpu.VMEM((N_YRING, BM, d), BF16),
            pltpu.VMEM((2, d, de), F32),
            pltpu.VMEM((2, d, de), F32),
            pltpu.VMEM((2, de, d), F32),
            pltpu.VMEM((2, d, de), BF16),
            pltpu.VMEM((2, d, de), BF16),
            pltpu.VMEM((2, de, d), BF16),
            pltpu.SemaphoreType.DMA((N_XRING,)),
            pltpu.SemaphoreType.DMA((N_YRING,)),
            pltpu.SemaphoreType.DMA((2,)),
        ],
    )
    return pl.pallas_call(
        functools.partial(_expert_kernel, layer),
        out_shape=jax.ShapeDtypeStruct(xs.shape, xs.dtype),
        grid_spec=grid_spec,
        input_output_aliases={5: 0},
        compiler_params=pltpu.CompilerParams(
            dimension_semantics=("arbitrary",), vmem_limit_bytes=VMEM_LIMIT),
        name="moe_experts",
    )(src, dst, bexp, enext, nused, xs, w_gate, w_up, w_down)


def _combine_kernel(ysa_ref, ysb_ref, pos_ref, gate_ref, x1_ref, mod_ref, lng_ref, lnb_ref, o_ref):
    m = mod_ref[...]
    for k, ys_ref in enumerate((ysa_ref, ysb_ref)):
        rows = slice(k * TT, (k + 1) * TT)
        posc = jnp.transpose(pos_ref[k].astype(F32)).astype(I32)
        gatec = jnp.transpose(gate_ref[k])
        iota_c = lax.broadcasted_iota(I32, (TT, R_TILE), 1)
        pick = jnp.where(iota_c == posc[:, 0:1], gatec[:, 0:1],
                         jnp.where(iota_c == posc[:, 1:2], gatec[:, 1:2], 0.0)).astype(BF16)
        y = _dot(pick, ys_ref[...])
        o_ref[rows, :] = _ln(ALPHA * x1_ref[rows, :] + (1.0 + m[5:6]) * y, lng_ref[...], lnb_ref[...])


def _combine_call(ys, pos, gate, x1, mod, layer, tiles_per_seq, ln_g, ln_b):
    t, d = x1.shape
    n_tiles = t // TT
    assert tiles_per_seq % 2 == 0
    const2 = lambda n: (0, 0)
    return pl.pallas_call(
        _combine_kernel,
        out_shape=jax.ShapeDtypeStruct((t, d), F32),
        grid=(n_tiles // 2,),
        in_specs=[
            pl.BlockSpec((R_TILE, d), lambda n: (2 * n + 1, 0)),
            pl.BlockSpec((R_TILE, d), lambda n: (2 * n + 2, 0)),
            pl.BlockSpec((2, TOP_K, TT), lambda n: (n, 0, 0)),
            pl.BlockSpec((2, TOP_K, TT), lambda n: (n, 0, 0)),
            pl.BlockSpec((2 * TT, d), lambda n: (n, 0)),
            pl.BlockSpec((None, None, 6, d), lambda n: (layer, 2 * n // tiles_per_seq, 0, 0)),
            pl.BlockSpec((1, d), const2),
            pl.BlockSpec((1, d), const2),
        ],
        out_specs=pl.BlockSpec((2 * TT, d), lambda n: (n, 0)),
        compiler_params=pltpu.CompilerParams(
            dimension_semantics=("arbitrary",), vmem_limit_bytes=VMEM_LIMIT),
        name="moe_combine",
    )(ys, ys, pos, gate, x1, mod, ln_g, ln_b)


def _dispatch_plan(cnt, n_blocks):
    n_tiles = cnt.shape[0]
    chunks_per_tile = R_TILE // CH
    nch = (cnt + CH - 1) // CH
    padoff_ch = jnp.cumsum(nch, axis=1) - nch
    tot = jnp.sum(nch, axis=0)
    totpad = (tot + NCH - 1) // NCH * NCH
    eend = jnp.cumsum(totpad)
    ebase = eend - totpad
    tbase = jnp.cumsum(nch, axis=0) - nch
    start = (ebase[None, :] + tbase).T.reshape(-1)
    base = ((jnp.arange(n_tiles, dtype=I32)[:, None] + 1) * chunks_per_tile + padoff_ch).T.reshape(-1)
    vals = jnp.stack([start, base, nch.T.reshape(-1)], axis=1)
    delta = vals - jnp.concatenate([jnp.zeros((1, 3), I32), vals[:-1]], axis=0)
    digits = jnp.concatenate([delta // LANES, delta % LANES], axis=1).astype(BF16)
    slot = jnp.arange(n_blocks * NCH, dtype=I32)
    started = (start[None, :] <= slot[:, None]).astype(BF16)
    got = jnp.dot(started, digits, preferred_element_type=F32).astype(I32)
    seg = got[:, :3] * LANES + got[:, 3:]
    j = slot - seg[:, 0]
    valid = (j < seg[:, 2]) & (slot < eend[-1])
    zero_chunk = chunks_per_tile - 1
    src = jnp.where(valid, seg[:, 1] + j, zero_chunk).astype(I32)
    blk = slot // NCH
    scratch = (blk % 2) * NCH + slot % NCH
    dst = jnp.where(valid, src, scratch).astype(I32)
    first = jnp.arange(n_blocks, dtype=I32) * NCH
    bexp = jnp.minimum(jnp.sum(eend[None, :] <= first[:, None], axis=1), N_EXPERTS - 1).astype(I32)
    nused = (eend[-1] // NCH).astype(I32).reshape(1)
    ids = jnp.arange(N_EXPERTS, dtype=I32)
    later = jnp.where((ids[None, :] > ids[:, None]) & (totpad > 0)[None, :], ids[None, :], N_EXPERTS)
    next_of = jnp.min(later, axis=1)
    next_of = jnp.where(next_of == N_EXPERTS, ids, next_of)
    enext = jnp.sum(jnp.where(bexp[:, None] == ids[None, :], next_of[None, :], 0), axis=1).astype(I32)
    return src, dst, bexp, enext, nused


def _t5_bucket(rel):
    nb = N_BUCKETS // 2
    max_exact = nb // 2
    ret = jnp.where(rel > 0, nb, 0)
    n = jnp.abs(rel)
    nf = jnp.maximum(n, 1).astype(jnp.float32)
    large = max_exact + (jnp.log(nf / max_exact) / math.log(MAX_DISTANCE / max_exact)
                         * (nb - max_exact)).astype(jnp.int32)
    large = jnp.minimum(large, nb - 1)
    return ret + jnp.where(n < max_exact, n, large)


def _band_bias(rel_bias):
    qi = jnp.arange(WINDOW)
    kj = jnp.arange(3 * WINDOW)
    rel = kj[None, :] - WINDOW - qi[:, None]
    pick = _t5_bucket(rel)[:, :, None, None] == jnp.arange(N_BUCKETS)[None, None, :, None]
    bias = jnp.sum(jnp.where(pick, rel_bias.astype(F32)[None, None], 0.0), axis=2) * LOG2E
    bias = jnp.where((jnp.abs(rel) <= WINDOW)[:, :, None], bias, NEG_INF)
    grp = B_HEADS // B_KV_HEADS
    bias = jnp.transpose(bias, (2, 1, 0)).reshape(B_KV_HEADS, grp, 3 * WINDOW, WINDOW)
    return jnp.transpose(bias, (0, 2, 1, 3)).reshape(B_KV_HEADS, 3 * WINDOW, grp * WINDOW)


def kernel(x, c, ada_w, ada_b, w_in, b_in, gmlp_ln_g, gmlp_ln_b, gmlp_ws, gmlp_bs, attn_sink, conv_w,
           conv_b, conv_ln_g, conv_ln_b, w_out, b_out, ln_mix_g, ln_mix_b, w_gate, w_up, w_down,
           ln_ffn_g, ln_ffn_b, rel_bias, router_w, router_bias):
    bsz, seq, d = x.shape
    n_layers = ada_w.shape[0]
    t = bsz * seq
    nt = seq // TT
    n_tiles = t // TT
    max_chunks = n_tiles * (TOP_K * TT // CH + N_EXPERTS) + N_EXPERTS * (NCH - 1)
    n_blocks = -(-max_chunks // NCH)

    mod = _ada_call(c, ada_w, ada_b).reshape(n_layers, bsz, 6, d)
    bias = _band_bias(rel_bias)
    rw = router_w.T.astype(BF16)
    rbias = router_bias.astype(F32).reshape(N_EXPERTS, 1)
    row = lambda a: a.reshape(1, -1)

    for l in range(n_layers):
        bsb = jnp.repeat(gmlp_bs[l].T, HEAD_DIM, axis=1)
        ya, qt, k, vt, yc = _inproj_call(
            x, mod, l, w_in[l], b_in[l], row(gmlp_ln_g[l]), row(gmlp_ln_b[l]),
            gmlp_ws[l].astype(BF16), bsb, conv_w[l], row(conv_b[l]), row(conv_ln_g[l]), row(conv_ln_b[l]))
        sink = jnp.repeat(attn_sink[l].astype(F32) * LOG2E, WINDOW).reshape(B_KV_HEADS, -1)
        x1, xs, pos, gate, cntb = _mixer_call(
            seq, qt, k, vt, bias, sink, yc, ya, x, mod, l, w_out[l].astype(BF16),
            row(b_out[l]), row(ln_mix_g[l]), row(ln_mix_b[l]), rw, rbias)
        cnt = cntb[:, :, 0].astype(I32)
        src, dst, bexp, enext, nused = _dispatch_plan(cnt, n_blocks)
        ys = _expert_call(src, dst, bexp, enext, nused, xs, l, w_gate, w_up, w_down)
        x = _combine_call(ys, pos, gate, x1.reshape(t, d), mod, l, nt, row(ln_ffn_g[l]),
                          row(ln_ffn_b[l])).reshape(bsz, seq, d)
    return x
```

```python
import functools
import math

import jax
import jax.numpy as jnp
from jax import lax
from jax.experimental import pallas as pl
from jax.experimental.pallas import tpu as pltpu

F32 = jnp.float32
BF16 = jnp.bfloat16
I32 = jnp.int32

D_MODEL = 1024
DEPTH = 2
HEAD_DIM = 64
A_WIDTH = 256
A_HEADS = 4
CHUNK = 128
B_WIDTH = 512
B_HEADS = 8
B_KV_HEADS = 2
KV_WIDTH = B_KV_HEADS * HEAD_DIM
WINDOW = 128
N_BUCKETS = 32
MAX_DISTANCE = 128
C_WIDTH = 256
CONV_WIDTH = 31
CONV_PAD = CONV_WIDTH // 2
IN_WIDTH = 2 * A_WIDTH + B_WIDTH + 2 * KV_WIDTH + 2 * C_WIDTH
N_EXPERTS = 32
N_GROUPS = 4
EXPERTS_PER_GROUP = N_EXPERTS // N_GROUPS
TOP_K = 2
D_EXPERT = D_MODEL // 2
ALPHA = (2 * DEPTH) ** 0.25
LN_EPS = 1e-5
NEG_INF = -1e30
LOG2E = 1.4426950408889634
Q_SCALE = HEAD_DIM ** -0.5 * LOG2E

LANES = 128
SUBLANES = 8
BF16_SUBLANES = 16
VMEM_LIMIT = 48 * 1024 * 1024

ADA_TN = 1536
TS = 1024
TT = 512
CH = BF16_SUBLANES
R_TILE = 1536
BM = 512
NCH = BM // CH
N_XRING = 6
N_YRING = 4
N_SORT_BLOCKS = 6
N_SBUF = 2
HALO = 16

assert R_TILE >= TOP_K * TT + N_EXPERTS * (CH - 1) + CH
assert R_TILE >= 2 * NCH * CH + CH


def _sigmoid(x):
    return 1.0 / (1.0 + jnp.exp(-x))


def _gelu_tanh(x):
    return x * (0.5 * (1.0 + jnp.tanh(0.7978845608028654 * (x + 0.044715 * (x * x * x)))))


def _ln(x, g, b):
    mu = jnp.mean(x, axis=-1, keepdims=True)
    xc = x - mu
    var = jnp.mean(xc * xc, axis=-1, keepdims=True)
    return xc * lax.rsqrt(var + LN_EPS) * g + b


def _dot(a, b):
    return jnp.dot(a, b, preferred_element_type=F32)


def _dot_nt(a, b):
    return lax.dot_general(a, b, (((1,), (1,)), ((), ())), preferred_element_type=F32)


def _ada_kernel(c_ref, w_ref, b_ref, o_ref):
    c = c_ref[...]
    s = (c * _sigmoid(c)).astype(BF16)
    o_ref[0] = _dot(s, w_ref[0].astype(BF16)) + b_ref[0]


def _ada_call(c, ada_w, ada_b):
    nl, d, n = ada_w.shape
    bsz = c.shape[0]
    return pl.pallas_call(
        _ada_kernel,
        out_shape=jax.ShapeDtypeStruct((nl, bsz, n), F32),
        grid=(nl, n // ADA_TN),
        in_specs=[
            pl.BlockSpec((bsz, d), lambda l, j: (0, 0)),
            pl.BlockSpec((1, d, ADA_TN), lambda l, j: (l, 0, j)),
            pl.BlockSpec((1, 1, ADA_TN), lambda l, j: (l, 0, j)),
        ],
        out_specs=pl.BlockSpec((1, bsz, ADA_TN), lambda l, j: (l, 0, j)),
        compiler_params=pltpu.CompilerParams(
            dimension_semantics=("arbitrary", "arbitrary"), vmem_limit_bytes=VMEM_LIMIT),
        name="ada_mod",
    )(c, ada_w, ada_b.reshape(nl, 1, n))


def _inproj_kernel(x_ref, xp_ref, xn_ref, mod_ref, w_ref, b_ref, wqt_ref, bq_ref, wvt_ref, bv_ref,
                   lng_ref, lnb_ref, ws_ref, bsb_ref, cw_ref, cb_ref, clg_ref, clb_ref,
                   ya_ref, qt_ref, k_ref, vt_ref, yc_ref, conv_scr, z_scr):
    i = pl.program_id(1)
    n_i = pl.num_programs(1)
    m = mod_ref[...]

    def modulate(xv):
        return (xv * (1.0 + m[1:2]) + m[0:1]).astype(BF16)

    hb = modulate(x_ref[0])
    col_u, col_v = 0, A_WIDTH
    col_k = 2 * A_WIDTH + B_WIDTH
    col_a = col_k + 2 * KV_WIDTH
    col_g = col_a + C_WIDTH

    def proj(lhs, c0, width):
        return _dot(lhs, w_ref[:, c0:c0 + width]) + b_ref[:, c0:c0 + width]

    hx = jnp.concatenate([modulate(xp_ref[0]), hb, modulate(xn_ref[0])], axis=0)
    yg = proj(hx, col_a, C_WIDTH) * _sigmoid(proj(hx, col_g, C_WIDTH))
    conv_scr[0:HALO, :] = yg[0:HALO] * jnp.where(i > 0, 1.0, 0.0)
    conv_scr[HALO:HALO + TS, :] = yg[HALO:HALO + TS]
    conv_scr[HALO + TS:, :] = yg[HALO + TS:] * jnp.where(i < n_i - 1, 1.0, 0.0)
    first = HALO - CONV_PAD
    acc = jnp.zeros((TS, C_WIDTH), F32) + cb_ref[...]
    for r in range(SUBLANES):
        z = None
        for a in range(-(-(first + CONV_WIDTH) // SUBLANES)):
            w = a * SUBLANES + r - first
            if 0 <= w < CONV_WIDTH:
                term = conv_scr[a * SUBLANES:a * SUBLANES + TS + SUBLANES, :] * cw_ref[w:w + 1, :]
                z = term if z is None else z + term
        if r == 0:
            acc = acc + z[0:TS]
        else:
            z_scr[r] = z
            acc = acc + z_scr[r, r:r + TS, :]
    yc = _ln(acc, clg_ref[...], clb_ref[...])
    yc_ref[0] = (yc * _sigmoid(yc)).astype(BF16)

    u = _gelu_tanh(proj(hb, col_u, A_WIDTH))
    v = _gelu_tanh(proj(hb, col_v, A_WIDTH))
    qt_ref[0] = ((_dot_nt(wqt_ref[...], hb) + bq_ref[...]) * Q_SCALE).astype(BF16)
    k_ref[0] = proj(hb, col_k, KV_WIDTH).astype(BF16)
    vt_ref[0] = (_dot_nt(wvt_ref[...], hb) + bv_ref[...]).astype(BF16)

    vb = _ln(v, lng_ref[...], lnb_ref[...]).astype(BF16)
    head_of_lane = lax.broadcasted_iota(I32, (CHUNK, A_WIDTH), 1) // HEAD_DIM
    for ch in range(TS // CHUNK):
        vc = vb[ch * CHUNK:(ch + 1) * CHUNK]
        acc = bsb_ref[...]
        for hh in range(A_HEADS):
            acc = acc + _dot(ws_ref[hh], jnp.where(head_of_lane == hh, vc, jnp.zeros_like(vc)))
        ya_ref[0, ch * CHUNK:(ch + 1) * CHUNK, :] = (u[ch * CHUNK:(ch + 1) * CHUNK] * acc).astype(BF16)


def _inproj_call(x, mod, layer, w_in, b_in, ln_g, ln_b, ws, bsb, conv_w, conv_b, conv_ln_g, conv_ln_b):
    bsz, seq, d = x.shape
    grid = (bsz, seq // TS)
    hb = TS // HALO
    const2 = lambda b, i: (0, 0)
    row = lambda b, i: (b, i, 0)
    colblk = lambda b, i: (b, 0, i)
    prev_h = lambda b, i: (b, jnp.maximum(i * hb - 1, 0), 0)
    next_h = lambda b, i: (b, jnp.minimum((i + 1) * hb, seq // HALO - 1), 0)
    q0 = 2 * A_WIDTH
    v0 = q0 + B_WIDTH + KV_WIDTH
    wb = w_in.astype(BF16)
    wqt = w_in[:, q0:q0 + B_WIDTH].T.astype(BF16)
    wvt = w_in[:, v0:v0 + KV_WIDTH].T.astype(BF16)
    bq = b_in[q0:q0 + B_WIDTH].reshape(B_WIDTH, 1)
    bv = b_in[v0:v0 + KV_WIDTH].reshape(KV_WIDTH, 1)

    def out(width):
        return jax.ShapeDtypeStruct((bsz, seq, width), BF16), pl.BlockSpec((1, TS, width), row)

    def out_t(width):
        return jax.ShapeDtypeStruct((bsz, width, seq), BF16), pl.BlockSpec((1, width, TS), colblk)

    outs = [out(A_WIDTH), out_t(B_WIDTH), out(KV_WIDTH), out_t(KV_WIDTH), out(C_WIDTH)]
    return pl.pallas_call(
        _inproj_kernel,
        out_shape=[o[0] for o in outs],
        grid=grid,
        in_specs=[
            pl.BlockSpec((1, TS, d), row),
            pl.BlockSpec((1, HALO, d), prev_h),
            pl.BlockSpec((1, HALO, d), next_h),
            pl.BlockSpec((None, None, 6, d), lambda b, i: (layer, b, 0, 0)),
            pl.BlockSpec((d, IN_WIDTH), const2),
            pl.BlockSpec((1, IN_WIDTH), const2),
            pl.BlockSpec((B_WIDTH, d), const2),
            pl.BlockSpec((B_WIDTH, 1), const2),
            pl.BlockSpec((KV_WIDTH, d), const2),
            pl.BlockSpec((KV_WIDTH, 1), const2),
            pl.BlockSpec((1, A_WIDTH), const2),
            pl.BlockSpec((1, A_WIDTH), const2),
            pl.BlockSpec((A_HEADS, CHUNK, CHUNK), lambda b, i: (0, 0, 0)),
            pl.BlockSpec((CHUNK, A_WIDTH), const2),
            pl.BlockSpec((CONV_WIDTH, C_WIDTH), const2),
            pl.BlockSpec((1, C_WIDTH), const2),
            pl.BlockSpec((1, C_WIDTH), const2),
            pl.BlockSpec((1, C_WIDTH), const2),
        ],
        out_specs=[o[1] for o in outs],
        scratch_shapes=[
            pltpu.VMEM((TS + 2 * HALO, C_WIDTH), F32),
            pltpu.VMEM((SUBLANES, TS + SUBLANES, C_WIDTH), F32),
        ],
        compiler_params=pltpu.CompilerParams(
            dimension_semantics=("arbitrary", "arbitrary"), vmem_limit_bytes=VMEM_LIMIT),
        name="inproj_gmlp_conv",
    )(x, x, x, mod, wb, b_in.reshape(1, -1), wqt, bq, wvt, bv, ln_g, ln_b, ws, bsb,
      conv_w, conv_b, conv_ln_g, conv_ln_b)


def _first_argmax(vals, iota_f, width):
    m = jnp.max(vals, axis=0, keepdims=True)
    idx = jnp.min(jnp.where(vals == m, iota_f, float(width)), axis=0, keepdims=True)
    return m, idx


def _mixer_kernel(seq_len, tiles_per_seq, qt_ref, kp_ref, kc_ref, kn_ref, vtp_ref, vtc_ref, vtn_ref,
                  bias_ref, sink_ref, yc_ref, ya_ref, x_ref, mod_ref, wo_ref, bo_ref, lng_ref, lnb_ref,
                  rw_ref, rb_ref, lt_ref, before_ref, x1_ref, xs_ref, pos_ref, gate_ref, cnt_ref,
                  ot_scr, s_scr, h2_scr, pos_scr):
    n = pl.program_id(0)
    last_tile = pl.num_programs(0) - 2
    i = jnp.minimum(n, last_tile) % tiles_per_seq
    t0 = i * TT
    m = mod_ref[...]

    @pl.when(n == 0)
    def _():
        h2_scr[...] = jnp.zeros_like(h2_scr)
        pos_scr[...] = jnp.zeros_like(pos_scr)

    pos_prev = pos_scr[...]
    sort_rows = R_TILE // N_SORT_BLOCKS

    def sort_block(j):
        iota_r = j * sort_rows + lax.broadcasted_iota(I32, (sort_rows, TT), 0)
        onehot = jnp.where(iota_r == pos_prev[0:1], 1.0,
                           jnp.where(iota_r == pos_prev[1:2], 1.0, 0.0)).astype(BF16)
        xs_ref[j * sort_rows:(j + 1) * sort_rows, :] = _dot(onehot, h2_scr[...]).astype(BF16)

    def out_proj(yb):
        return (_dot(ya_ref[0], wo_ref[0:A_WIDTH, :])
                + _dot(yb, wo_ref[A_WIDTH:A_WIDTH + B_WIDTH, :])
                + _dot(yc_ref[0], wo_ref[A_WIDTH + B_WIDTH:, :]) + bo_ref[...])

    def moe_input(y):
        x1 = _ln(ALPHA * x_ref[0] + (1.0 + m[2:3]) * y, lng_ref[...], lnb_ref[...])
        x1_ref[0] = x1
        h2 = (x1 * (1.0 + m[4:5]) + m[3:4]).astype(BF16)
        return h2, _dot_nt(rw_ref[...], h2)

    def choose(logits):
        scores = _sigmoid(logits)
        sel = scores + rb_ref[...]
        iota_f = lax.broadcasted_iota(I32, (EXPERTS_PER_GROUP, TT), 0).astype(F32)
        best = None
        for g in range(N_GROUPS):
            sl = slice(g * EXPERTS_PER_GROUP, (g + 1) * EXPERTS_PER_GROUP)
            sg = sel[sl]
            m1, i1 = _first_argmax(sg, iota_f, EXPERTS_PER_GROUP)
            m2, i2 = _first_argmax(jnp.where(iota_f == i1, -jnp.inf, sg), iota_f, EXPERTS_PER_GROUP)
            sc = scores[sl]
            s1 = jnp.sum(jnp.where(iota_f == i1, sc, 0.0), axis=0, keepdims=True)
            s2 = jnp.sum(jnp.where(iota_f == i2, sc, 0.0), axis=0, keepdims=True)
            cand = (m1 + m2, i1 + g * EXPERTS_PER_GROUP, i2 + g * EXPERTS_PER_GROUP, s1, s2)
            if best is None:
                best = cand
            else:
                take = cand[0] > best[0]
                best = tuple(jnp.where(take, c, b) for c, b in zip(cand, best))
        _, e1, e2, s1, s2 = best
        gate_ref[0] = jnp.concatenate([s1, s2], axis=0) / (s1 + s2)

        iota_e = lax.broadcasted_iota(I32, (N_EXPERTS, TT), 0).astype(F32)
        in0 = iota_e == e1
        in1 = iota_e == e2
        member = jnp.where(in0 | in1, 1.0, 0.0)
        cnt = jnp.sum(member, axis=1, keepdims=True)
        cnt_ref[0] = jnp.broadcast_to(cnt, (N_EXPERTS, LANES))
        nch = jnp.floor((cnt + (CH - 1)) * (1.0 / CH))
        nch_pad = jnp.concatenate([jnp.broadcast_to(nch, (N_EXPERTS, LANES)),
                                   jnp.zeros((LANES - N_EXPERTS, LANES), F32)], axis=0).astype(BF16)
        return in0, in1, member.astype(BF16), nch_pad

    def positions(in0, in1, member, nch_pad):
        padoff = _dot(lt_ref[...], nch_pad)[:, 0:1] * float(CH)
        rank = _dot(member, before_ref[...])
        posf = padoff + rank
        pos0 = jnp.sum(jnp.where(in0, posf, 0.0), axis=0, keepdims=True).astype(I32)
        pos1 = jnp.sum(jnp.where(in1, posf, 0.0), axis=0, keepdims=True).astype(I32)
        pos = jnp.concatenate([pos0, pos1], axis=0)
        pos_ref[0] = pos
        return pos

    kfull = jnp.concatenate([kp_ref[0], kc_ref[0], kn_ref[0]], axis=0)
    vtfull = jnp.concatenate([vtp_ref[0], vtc_ref[0], vtn_ref[0]], axis=1)
    grp = B_HEADS // B_KV_HEADS
    n_qb = TT // WINDOW
    units = [(jb, g) for jb in range(n_qb) for g in range(B_KV_HEADS)]
    key_i = lax.broadcasted_iota(I32, (3 * WINDOW, 1), 0)

    def scores(u):
        jb, g = units[u]
        kb = kfull[jb * WINDOW:(jb + 3) * WINDOW, g * HEAD_DIM:(g + 1) * HEAD_DIM]
        qt = jnp.concatenate(
            [qt_ref[0, h * HEAD_DIM:(h + 1) * HEAD_DIM, jb * WINDOW:(jb + 1) * WINDOW]
             for h in range(g * grp, (g + 1) * grp)], axis=1)
        s = _dot(kb, qt) + bias_ref[g]
        if jb == 0 or jb == n_qb - 1:
            kpos = t0 + (jb - 1) * WINDOW + key_i
            s = jnp.where((kpos >= 0) & (kpos < seq_len), s, NEG_INF)
        s_scr[u % N_SBUF] = s

    def values(u):
        jb, g = units[u]
        s = s_scr[u % N_SBUF]
        sink = sink_ref[g:g + 1, :]
        mx = jnp.maximum(jnp.max(s, axis=0, keepdims=True), sink)
        p = jnp.exp2(s - mx)
        den = jnp.sum(p, axis=0, keepdims=True) + jnp.exp2(sink - mx)
        vt = vtfull[g * HEAD_DIM:(g + 1) * HEAD_DIM, jb * WINDOW:(jb + 3) * WINDOW]
        ot = _dot(vt, p.astype(BF16)) / den
        for hh in range(grp):
            h = g * grp + hh
            ot_scr[h * HEAD_DIM:(h + 1) * HEAD_DIM, jb * WINDOW:(jb + 1) * WINDOW] = (
                ot[:, hh * WINDOW:(hh + 1) * WINDOW])

    for u in range(N_SBUF - 1):
        scores(u)
    for u in range(len(units)):
        if u + N_SBUF - 1 < len(units):
            scores(u + N_SBUF - 1)
        values(u)
    y = out_proj(jnp.transpose(ot_scr[...]).astype(BF16))
    early = N_SORT_BLOCKS - 1
    for j in range(early):
        sort_block(j)
    new_h2, logits = moe_input(y)
    for j in range(early, N_SORT_BLOCKS):
        sort_block(j)
    new_pos = positions(*choose(logits))
    pos_scr[...] = new_pos
    h2_scr[...] = new_h2


def _mixer_call(seq_len, qt, k, vt, bias, sink, yc, ya, x, mod, layer, w_out, b_out, ln_g, ln_b, rw, rbias):
    bsz, seq, d = x.shape
    nt = seq // TT
    n_tiles = bsz * nt
    kb = TT // WINDOW
    grp = B_HEADS // B_KV_HEADS
    const2 = lambda n: (0, 0)

    def at_tile(fn):
        def index_map(n):
            t = jnp.minimum(n, n_tiles - 1)
            return fn(t // nt, t % nt)
        return index_map

    row = at_tile(lambda b, i: (b, i, 0))
    colblk = at_tile(lambda b, i: (b, 0, i))
    prev_k = at_tile(lambda b, i: (b, jnp.maximum(i * kb - 1, 0), 0))
    next_k = at_tile(lambda b, i: (b, jnp.minimum((i + 1) * kb, seq // WINDOW - 1), 0))
    prev_v = at_tile(lambda b, i: (b, 0, jnp.maximum(i * kb - 1, 0)))
    next_v = at_tile(lambda b, i: (b, 0, jnp.minimum((i + 1) * kb, seq // WINDOW - 1)))
    tile = at_tile(lambda b, i: (b * nt + i, 0, 0))
    lt = (jnp.arange(LANES)[None, :] < jnp.arange(N_EXPERTS)[:, None]).astype(BF16)
    before = (jnp.arange(TT)[:, None] < jnp.arange(TT)[None, :]).astype(BF16)
    return pl.pallas_call(
        functools.partial(_mixer_kernel, seq_len, nt),
        out_shape=[
            jax.ShapeDtypeStruct((bsz, seq, d), F32),
            jax.ShapeDtypeStruct(((n_tiles + 1) * R_TILE, d), BF16),
            jax.ShapeDtypeStruct((n_tiles, TOP_K, TT), I32),
            jax.ShapeDtypeStruct((n_tiles, TOP_K, TT), F32),
            jax.ShapeDtypeStruct((n_tiles, N_EXPERTS, LANES), F32),
        ],
        grid=(n_tiles + 1,),
        in_specs=[
            pl.BlockSpec((1, B_WIDTH, TT), colblk),
            pl.BlockSpec((1, WINDOW, KV_WIDTH), prev_k),
            pl.BlockSpec((1, TT, KV_WIDTH), row),
            pl.BlockSpec((1, WINDOW, KV_WIDTH), next_k),
            pl.BlockSpec((1, KV_WIDTH, WINDOW), prev_v),
            pl.BlockSpec((1, KV_WIDTH, TT), colblk),
            pl.BlockSpec((1, KV_WIDTH, WINDOW), next_v),
            pl.BlockSpec((B_KV_HEADS, 3 * WINDOW, grp * WINDOW), lambda n: (0, 0, 0)),
            pl.BlockSpec((B_KV_HEADS, grp * WINDOW), const2),
            pl.BlockSpec((1, TT, C_WIDTH), row),
            pl.BlockSpec((1, TT, A_WIDTH), row),
            pl.BlockSpec((1, TT, d), row),
            pl.BlockSpec((None, None, 6, d), at_tile(lambda b, i: (layer, b, 0, 0))),
            pl.BlockSpec((d, d), const2),
            pl.BlockSpec((1, d), const2),
            pl.BlockSpec((1, d), const2),
            pl.BlockSpec((1, d), const2),
            pl.BlockSpec((N_EXPERTS, d), const2),
            pl.BlockSpec((N_EXPERTS, 1), const2),
            pl.BlockSpec((N_EXPERTS, LANES), const2),
            pl.BlockSpec((TT, TT), const2),
        ],
        out_specs=[
            pl.BlockSpec((1, TT, d), row),
            pl.BlockSpec((R_TILE, d), lambda n: (n, 0)),
            pl.BlockSpec((1, TOP_K, TT), tile),
            pl.BlockSpec((1, TOP_K, TT), tile),
            pl.BlockSpec((1, N_EXPERTS, LANES), tile),
        ],
        scratch_shapes=[
            pltpu.VMEM((B_WIDTH, TT), F32),
            pltpu.VMEM((N_SBUF, 3 * WINDOW, grp * WINDOW), F32),
            pltpu.VMEM((TT, d), BF16),
            pltpu.VMEM((TOP_K, TT), I32),
        ],
        compiler_params=pltpu.CompilerParams(
            dimension_semantics=("arbitrary",), vmem_limit_bytes=VMEM_LIMIT),
        name="mixer_router_sort",
    )(qt, k, k, k, vt, vt, vt, bias, sink, yc, ya, x, mod, w_out, b_out, ln_g, ln_b, rw, rbias, lt, before)


def _expert_kernel(layer, src_ref, dst_ref, bexp_ref, enext_ref, nused_ref, xs_hbm, wg_hbm, wu_hbm, wd_hbm,
                   ys_hbm, xbuf, ybuf, wg_st, wu_st, wd_st, wgb, wub, wdb, sem_in, sem_out, sem_w):
    n_blk = bexp_ref.shape[0]
    nused = nused_ref[0]
    chunks_per_tile = R_TILE // CH
    zero_chunk = chunks_per_tile - 1
    scratch_chunk = 0
    de = wgb.shape[2]
    half = de // 2

    def in_copy(chunk, s, c):
        return pltpu.make_async_copy(
            xs_hbm.at[pl.ds(pl.multiple_of(chunk * CH, CH), CH)],
            xbuf.at[s, pl.ds(c * CH, CH)], sem_in.at[s])

    def out_copy(chunk, s, c):
        return pltpu.make_async_copy(
            ybuf.at[s, pl.ds(c * CH, CH)],
            ys_hbm.at[pl.ds(pl.multiple_of(chunk * CH, CH), CH)], sem_out.at[s])

    def weight_copies(e, s):
        return [pltpu.make_async_copy(wg_hbm.at[layer, e], wg_st.at[s], sem_w.at[s]),
                pltpu.make_async_copy(wu_hbm.at[layer, e], wu_st.at[s], sem_w.at[s]),
                pltpu.make_async_copy(wd_hbm.at[layer, e], wd_st.at[s], sem_w.at[s])]

    def gather_chunk(b, c):
        return jnp.where(b < nused, src_ref[jnp.minimum(b, n_blk - 1) * NCH + c], zero_chunk)

    def write_back_chunk(b, c):
        real = (b >= 0) & (b < nused)
        return jnp.where(real, dst_ref[jnp.clip(b, 0, n_blk - 1) * NCH + c], scratch_chunk + (b % 2) * NCH + c)

    ybuf[...] = jnp.zeros_like(ybuf)
    wgb[...] = jnp.zeros_like(wgb)
    wub[...] = jnp.zeros_like(wub)
    wdb[...] = jnp.zeros_like(wdb)
    for b in range(4):
        for c in range(NCH):
            in_copy(gather_chunk(b, c), b, c).start()
    for b in range(2):
        for c in range(NCH):
            out_copy(scratch_chunk + b * NCH + c, b, c).start()

    @pl.when(nused > 0)
    def _():
        for cp in weight_copies(bexp_ref[0], 0):
            cp.start()

    def pair(i, n_changes):
        blocks = (2 * i, 2 * i + 1)
        wslot = []
        for blk in blocks:
            jb = jnp.minimum(blk, n_blk - 1)
            expert = bexp_ref[jb]
            prev_expert = bexp_ref[jnp.maximum(jb - 1, 0)]
            change = (blk < nused) & ((blk == 0) | (expert != prev_expert))

            @pl.when(change)
            def _(expert=expert, jb=jb, n_changes=n_changes):
                ws = n_changes % 2
                for cp in weight_copies(expert, ws):
                    cp.wait()
                wgb[ws] = wg_st[ws].astype(BF16)
                wub[ws] = wu_st[ws].astype(BF16)
                wdb[ws] = wd_st[ws].astype(BF16)
                upcoming = enext_ref[jb]

                @pl.when(upcoming != expert)
                def _():
                    for cp in weight_copies(upcoming, 1 - ws):
                        cp.start()

            n_changes = n_changes + change.astype(I32)
            wslot.append(jnp.maximum(n_changes - 1, 0) % 2)

        for blk in blocks:
            for c in range(NCH):
                in_copy(0, blk % N_XRING, c).wait()
            for c in range(NCH):
                out_copy(0, blk % N_YRING, c).wait()

        dmas = []
        for blk in blocks:
            for c in range(NCH):
                dmas.append(functools.partial(
                    lambda b, c: out_copy(write_back_chunk(b, c), b % N_YRING, c).start(), blk - 2, c))
        for blk in blocks:
            for c in range(NCH):
                dmas.append(functools.partial(
                    lambda b, c: in_copy(gather_chunk(b, c), b % N_XRING, c).start(priority=1), blk + 4, c))
        n_groups = 8
        per_group = len(dmas) // n_groups

        def issue(gidx):
            for start in dmas[gidx * per_group:(gidx + 1) * per_group]:
                start()

        gidx = 0
        for k, blk in enumerate(blocks):
            x = xbuf[blk % N_XRING]
            ws = wslot[k]
            hmid = []
            for h in range(2):
                cols = slice(h * half, (h + 1) * half)
                g = _dot(x, wgb[ws, :, cols])
                issue(gidx)
                u = _dot(x, wub[ws, :, cols])
                issue(gidx + 1)
                gidx += 2
                hmid.append(((g * _sigmoid(g)) * u).astype(BF16))
            y = _dot(hmid[0], wdb[ws, 0:half, :]) + _dot(hmid[1], wdb[ws, half:, :])
            ybuf[blk % N_YRING] = y.astype(BF16)
        return n_changes

    n_pairs = (nused + 1) // 2
    lax.fori_loop(0, n_pairs, pair, jnp.int32(0))

    last = 2 * n_pairs
    for blk in range(4):
        for c in range(NCH):
            in_copy(0, (last + blk) % N_XRING, c).wait()
    for blk in (last, last + 1):
        for c in range(NCH):
            out_copy(0, blk % N_YRING, c).wait()
    for blk in (last - 2, last - 1):
        for c in range(NCH):
            out_copy(write_back_chunk(blk, c), blk % N_YRING, c).start()
    for blk in (last - 2, last - 1):
        for c in range(NCH):
            out_copy(0, blk % N_YRING, c).wait()


def _expert_call(src, dst, bexp, enext, nused, xs, layer, w_gate, w_up, w_down):
    d = xs.shape[-1]
    de = w_gate.shape[-1]
    grid_spec = pltpu.PrefetchScalarGridSpec(
        num_scalar_prefetch=5,
        grid=(1,),
        in_specs=[pl.BlockSpec(memory_space=pl.ANY)] * 4,
        out_specs=pl.BlockSpec(memory_space=pl.ANY),
        scratch_shapes=[
            pltpu.VMEM((N_XRING, BM, d), BF16),
            pltpu.VMEM((N_YRING, BM, d), BF16),
            pltpu.VMEM((2, d, de), F32),
            pltpu.VMEM((2, d, de), F32),
            pltpu.VMEM((2, de, d), F32),
            pltpu.VMEM((2, d, de), BF16),
            pltpu.VMEM((2, d, de), BF16),
            pltpu.VMEM((2, de, d), BF16),
            pltpu.SemaphoreType.DMA((N_XRING,)),
            pltpu.SemaphoreType.DMA((N_YRING,)),
            pltpu.SemaphoreType.DMA((2,)),
        ],
    )
    return pl.pallas_call(
        functools.partial(_expert_kernel, layer),
        out_shape=jax.ShapeDtypeStruct(xs.shape, xs.dtype),
        grid_spec=grid_spec,
        input_output_aliases={5: 0},
        compiler_params=pltpu.CompilerParams(
            dimension_semantics=("arbitrary",), vmem_limit_bytes=VMEM_LIMIT),
        name="moe_experts",
    )(src, dst, bexp, enext, nused, xs, w_gate, w_up, w_down)


def _combine_kernel(ysa_ref, ysb_ref, pos_ref, gate_ref, x1_ref, mod_ref, lng_ref, lnb_ref, o_ref):
    m = mod_ref[...]
    for k, ys_ref in enumerate((ysa_ref, ysb_ref)):
        rows = slice(k * TT, (k + 1) * TT)
        posc = jnp.transpose(pos_ref[k].astype(F32)).astype(I32)
        gatec = jnp.transpose(gate_ref[k])
        iota_c = lax.broadcasted_iota(I32, (TT, R_TILE), 1)
        pick = jnp.where(iota_c == posc[:, 0:1], gatec[:, 0:1],
                         jnp.where(iota_c == posc[:, 1:2], gatec[:, 1:2], 0.0)).astype(BF16)
        y = _dot(pick, ys_ref[...])
        o_ref[rows, :] = _ln(ALPHA * x1_ref[rows, :] + (1.0 + m[5:6]) * y, lng_ref[...], lnb_ref[...])


def _combine_call(ys, pos, gate, x1, mod, layer, tiles_per_seq, ln_g, ln_b):
    t, d = x1.shape
    n_tiles = t // TT
    assert tiles_per_seq % 2 == 0
    const2 = lambda n: (0, 0)
    return pl.pallas_call(
        _combine_kernel,
        out_shape=jax.ShapeDtypeStruct((t, d), F32),
        grid=(n_tiles // 2,),
        in_specs=[
            pl.BlockSpec((R_TILE, d), lambda n: (2 * n + 1, 0)),
            pl.BlockSpec((R_TILE, d), lambda n: (2 * n + 2, 0)),
            pl.BlockSpec((2, TOP_K, TT), lambda n: (n, 0, 0)),
            pl.BlockSpec((2, TOP_K, TT), lambda n: (n, 0, 0)),
            pl.BlockSpec((2 * TT, d), lambda n: (n, 0)),
            pl.BlockSpec((None, None, 6, d), lambda n: (layer, 2 * n // tiles_per_seq, 0, 0)),
            pl.BlockSpec((1, d), const2),
            pl.BlockSpec((1, d), const2),
        ],
        out_specs=pl.BlockSpec((2 * TT, d), lambda n: (n, 0)),
        compiler_params=pltpu.CompilerParams(
            dimension_semantics=("arbitrary",), vmem_limit_bytes=VMEM_LIMIT),
        name="moe_combine",
    )(ys, ys, pos, gate, x1, mod, ln_g, ln_b)


def _dispatch_plan(cnt, n_blocks):
    n_tiles = cnt.shape[0]
    chunks_per_tile = R_TILE // CH
    nch = (cnt + CH - 1) // CH
    padoff_ch = jnp.cumsum(nch, axis=1) - nch
    tot = jnp.sum(nch, axis=0)
    totpad = (tot + NCH - 1) // NCH * NCH
    eend = jnp.cumsum(totpad)
    ebase = eend - totpad
    tbase = jnp.cumsum(nch, axis=0) - nch
    start = (ebase[None, :] + tbase).T.reshape(-1)
    base = ((jnp.arange(n_tiles, dtype=I32)[:, None] + 1) * chunks_per_tile + padoff_ch).T.reshape(-1)
    vals = jnp.stack([start, base, nch.T.reshape(-1)], axis=1)
    delta = vals - jnp.concatenate([jnp.zeros((1, 3), I32), vals[:-1]], axis=0)
    digits = jnp.concatenate([delta // LANES, delta % LANES], axis=1).astype(BF16)
    slot = jnp.arange(n_blocks * NCH, dtype=I32)
    started = (start[None, :] <= slot[:, None]).astype(BF16)
    got = jnp.dot(started, digits, preferred_element_type=F32).astype(I32)
    seg = got[:, :3] * LANES + got[:, 3:]
    j = slot - seg[:, 0]
    valid = (j < seg[:, 2]) & (slot < eend[-1])
    zero_chunk = chunks_per_tile - 1
    src = jnp.where(valid, seg[:, 1] + j, zero_chunk).astype(I32)
    blk = slot // NCH
    scratch = (blk % 2) * NCH + slot % NCH
    dst = jnp.where(valid, src, scratch).astype(I32)
    first = jnp.arange(n_blocks, dtype=I32) * NCH
    bexp = jnp.minimum(jnp.sum(eend[None, :] <= first[:, None], axis=1), N_EXPERTS - 1).astype(I32)
    nused = (eend[-1] // NCH).astype(I32).reshape(1)
    ids = jnp.arange(N_EXPERTS, dtype=I32)
    later = jnp.where((ids[None, :] > ids[:, None]) & (totpad > 0)[None, :], ids[None, :], N_EXPERTS)
    next_of = jnp.min(later, axis=1)
    next_of = jnp.where(next_of == N_EXPERTS, ids, next_of)
    enext = jnp.sum(jnp.where(bexp[:, None] == ids[None, :], next_of[None, :], 0), axis=1).astype(I32)
    return src, dst, bexp, enext, nused


def _t5_bucket(rel):
    nb = N_BUCKETS // 2
    max_exact = nb // 2
    ret = jnp.where(rel > 0, nb, 0)
    n = jnp.abs(rel)
    nf = jnp.maximum(n, 1).astype(jnp.float32)
    large = max_exact + (jnp.log(nf / max_exact) / math.log(MAX_DISTANCE / max_exact)
                         * (nb - max_exact)).astype(jnp.int32)
    large = jnp.minimum(large, nb - 1)
    return ret + jnp.where(n < max_exact, n, large)


def _band_bias(rel_bias):
    qi = jnp.arange(WINDOW)
    kj = jnp.arange(3 * WINDOW)
    rel = kj[None, :] - WINDOW - qi[:, None]
    pick = _t5_bucket(rel)[:, :, None, None] == jnp.arange(N_BUCKETS)[None, None, :, None]
    bias = jnp.sum(jnp.where(pick, rel_bias.astype(F32)[None, None], 0.0), axis=2) * LOG2E
    bias = jnp.where((jnp.abs(rel) <= WINDOW)[:, :, None], bias, NEG_INF)
    grp = B_HEADS // B_KV_HEADS
    bias = jnp.transpose(bias, (2, 1, 0)).reshape(B_KV_HEADS, grp, 3 * WINDOW, WINDOW)
    return jnp.transpose(bias, (0, 2, 1, 3)).reshape(B_KV_HEADS, 3 * WINDOW, grp * WINDOW)


def kernel(x, c, ada_w, ada_b, w_in, b_in, gmlp_ln_g, gmlp_ln_b, gmlp_ws, gmlp_bs, attn_sink, conv_w,
           conv_b, conv_ln_g, conv_ln_b, w_out, b_out, ln_mix_g, ln_mix_b, w_gate, w_up, w_down,
           ln_ffn_g, ln_ffn_b, rel_bias, router_w, router_bias):
    bsz, seq, d = x.shape
    n_layers = ada_w.shape[0]
    t = bsz * seq
    nt = seq // TT
    n_tiles = t // TT
    max_chunks = n_tiles * (TOP_K * TT // CH + N_EXPERTS) + N_EXPERTS * (NCH - 1)
    n_blocks = -(-max_chunks // NCH)

    mod = _ada_call(c, ada_w, ada_b).reshape(n_layers, bsz, 6, d)
    bias = _band_bias(rel_bias)
    rw = router_w.T.astype(BF16)
    rbias = router_bias.astype(F32).reshape(N_EXPERTS, 1)
    row = lambda a: a.reshape(1, -1)

    for l in range(n_layers):
        bsb = jnp.repeat(gmlp_bs[l].T, HEAD_DIM, axis=1)
        ya, qt, k, vt, yc = _inproj_call(
            x, mod, l, w_in[l], b_in[l], row(gmlp_ln_g[l]), row(gmlp_ln_b[l]),
            gmlp_ws[l].astype(BF16), bsb, conv_w[l], row(conv_b[l]), row(conv_ln_g[l]), row(conv_ln_b[l]))
        sink = jnp.repeat(attn_sink[l].astype(F32) * LOG2E, WINDOW).reshape(B_KV_HEADS, -1)
        x1, xs, pos, gate, cntb = _mixer_call(
            seq, qt, k, vt, bias, sink, yc, ya, x, mod, l, w_out[l].astype(BF16),
            row(b_out[l]), row(ln_mix_g[l]), row(ln_mix_b[l]), rw, rbias)
        cnt = cntb[:, :, 0].astype(I32)
        src, dst, bexp, enext, nused = _dispatch_plan(cnt, n_blocks)
        ys = _expert_call(src, dst, bexp, enext, nused, xs, l, w_gate, w_up, w_down)
        x = _combine_call(ys, pos, gate, x1.reshape(t, d), mod, l, nt, row(ln_ffn_g[l]),
                          row(ln_ffn_b[l])).reshape(bsz, seq, d)
    return x
```

```python
import functools
import math

import jax
import jax.numpy as jnp
from jax import lax
from jax.experimental import pallas as pl
from jax.experimental.pallas import tpu as pltpu

F32 = jnp.float32
BF16 = jnp.bfloat16
I32 = jnp.int32

D_MODEL = 1024
DEPTH = 2
HEAD_DIM = 64
A_WIDTH = 256
A_HEADS = 4
CHUNK = 128
B_WIDTH = 512
B_HEADS = 8
B_KV_HEADS = 2
KV_WIDTH = B_KV_HEADS * HEAD_DIM
WINDOW = 128
N_BUCKETS = 32
MAX_DISTANCE = 128
C_WIDTH = 256
CONV_WIDTH = 31
CONV_PAD = CONV_WIDTH // 2
IN_WIDTH = 2 * A_WIDTH + B_WIDTH + 2 * KV_WIDTH + 2 * C_WIDTH
N_EXPERTS = 32
N_GROUPS = 4
EXPERTS_PER_GROUP = N_EXPERTS // N_GROUPS
TOP_K = 2
D_EXPERT = D_MODEL // 2
ALPHA = (2 * DEPTH) ** 0.25
LN_EPS = 1e-5
NEG_INF = -1e30
LOG2E = 1.4426950408889634
Q_SCALE = HEAD_DIM ** -0.5 * LOG2E

LANES = 128
SUBLANES = 8
BF16_SUBLANES = 16
VMEM_LIMIT = 48 * 1024 * 1024

ADA_TN = 1536
TS = 1024
TT = 512
CH = BF16_SUBLANES
R_TILE = 1536
BM = 512
NCH = BM // CH
N_XRING = 6
N_YRING = 4
N_SORT_BLOCKS = 6
N_SBUF = 2
HALO = 16

assert R_TILE >= TOP_K * TT + N_EXPERTS * (CH - 1) + CH
assert R_TILE >= 2 * NCH * CH + CH


def _sigmoid(x):
    return 1.0 / (1.0 + jnp.exp(-x))


def _gelu_tanh(x):
    return x * (0.5 * (1.0 + jnp.tanh(0.7978845608028654 * (x + 0.044715 * (x * x * x)))))


def _ln(x, g, b):
    mu = jnp.mean(x, axis=-1, keepdims=True)
    xc = x - mu
    var = jnp.mean(xc * xc, axis=-1, keepdims=True)
    return xc * lax.rsqrt(var + LN_EPS) * g + b


def _dot(a, b):
    return jnp.dot(a, b, preferred_element_type=F32)


def _dot_nt(a, b):
    return lax.dot_general(a, b, (((1,), (1,)), ((), ())), preferred_element_type=F32)


def _ada_kernel(c_ref, w_ref, b_ref, o_ref):
    c = c_ref[...]
    s = (c * _sigmoid(c)).astype(BF16)
    o_ref[0] = _dot(s, w_ref[0].astype(BF16)) + b_ref[0]


def _ada_call(c, ada_w, ada_b):
    nl, d, n = ada_w.shape
    bsz = c.shape[0]
    return pl.pallas_call(
        _ada_kernel,
        out_shape=jax.ShapeDtypeStruct((nl, bsz, n), F32),
        grid=(nl, n // ADA_TN),
        in_specs=[
            pl.BlockSpec((bsz, d), lambda l, j: (0, 0)),
            pl.BlockSpec((1, d, ADA_TN), lambda l, j: (l, 0, j)),
            pl.BlockSpec((1, 1, ADA_TN), lambda l, j: (l, 0, j)),
        ],
        out_specs=pl.BlockSpec((1, bsz, ADA_TN), lambda l, j: (l, 0, j)),
        compiler_params=pltpu.CompilerParams(
            dimension_semantics=("arbitrary", "arbitrary"), vmem_limit_bytes=VMEM_LIMIT),
        name="ada_mod",
    )(c, ada_w, ada_b.reshape(nl, 1, n))


def _inproj_kernel(x_ref, xp_ref, xn_ref, mod_ref, w_ref, b_ref, wqt_ref, bq_ref, wvt_ref, bv_ref,
                   lng_ref, lnb_ref, ws_ref, bsb_ref, cw_ref, cb_ref, clg_ref, clb_ref,
                   ya_ref, qt_ref, k_ref, vt_ref, yc_ref, conv_scr, z_scr):
    i = pl.program_id(1)
    n_i = pl.num_programs(1)
    m = mod_ref[...]

    def modulate(xv):
        return (xv * (1.0 + m[1:2]) + m[0:1]).astype(BF16)

    hb = modulate(x_ref[0])
    col_u, col_v = 0, A_WIDTH
    col_k = 2 * A_WIDTH + B_WIDTH
    col_a = col_k + 2 * KV_WIDTH
    col_g = col_a + C_WIDTH

    def proj(lhs, c0, width):
        return _dot(lhs, w_ref[:, c0:c0 + width]) + b_ref[:, c0:c0 + width]

    hx = jnp.concatenate([modulate(xp_ref[0]), hb, modulate(xn_ref[0])], axis=0)
    yg = proj(hx, col_a, C_WIDTH) * _sigmoid(proj(hx, col_g, C_WIDTH))
    conv_scr[0:HALO, :] = yg[0:HALO] * jnp.where(i > 0, 1.0, 0.0)
    conv_scr[HALO:HALO + TS, :] = yg[HALO:HALO + TS]
    conv_scr[HALO + TS:, :] = yg[HALO + TS:] * jnp.where(i < n_i - 1, 1.0, 0.0)
    first = HALO - CONV_PAD
    acc = jnp.zeros((TS, C_WIDTH), F32) + cb_ref[...]
    for r in range(SUBLANES):
        z = None
        for a in range(-(-(first + CONV_WIDTH) // SUBLANES)):
            w = a * SUBLANES + r - first
            if 0 <= w < CONV_WIDTH:
                term = conv_scr[a * SUBLANES:a * SUBLANES + TS + SUBLANES, :] * cw_ref[w:w + 1, :]
                z = term if z is None else z + term
        if r == 0:
            acc = acc + z[0:TS]
        else:
            z_scr[r] = z
            acc = acc + z_scr[r, r:r + TS, :]
    yc = _ln(acc, clg_ref[...], clb_ref[...])
    yc_ref[0] = (yc * _sigmoid(yc)).astype(BF16)

    u = _gelu_tanh(proj(hb, col_u, A_WIDTH))
    v = _gelu_tanh(proj(hb, col_v, A_WIDTH))
    qt_ref[0] = ((_dot_nt(wqt_ref[...], hb) + bq_ref[...]) * Q_SCALE).astype(BF16)
    k_ref[0] = proj(hb, col_k, KV_WIDTH).astype(BF16)
    vt_ref[0] = (_dot_nt(wvt_ref[...], hb) + bv_ref[...]).astype(BF16)

    vb = _ln(v, lng_ref[...], lnb_ref[...]).astype(BF16)
    head_of_lane = lax.broadcasted_iota(I32, (CHUNK, A_WIDTH), 1) // HEAD_DIM
    for ch in range(TS // CHUNK):
        vc = vb[ch * CHUNK:(ch + 1) * CHUNK]
        acc = bsb_ref[...]
        for hh in range(A_HEADS):
            acc = acc + _dot(ws_ref[hh], jnp.where(head_of_lane == hh, vc, jnp.zeros_like(vc)))
        ya_ref[0, ch * CHUNK:(ch + 1) * CHUNK, :] = (u[ch * CHUNK:(ch + 1) * CHUNK] * acc).astype(BF16)


def _inproj_call(x, mod, layer, w_in, b_in, ln_g, ln_b, ws, bsb, conv_w, conv_b, conv_ln_g, conv_ln_b):
    bsz, seq, d = x.shape
    grid = (bsz, seq // TS)
    hb = TS // HALO
    const2 = lambda b, i: (0, 0)
    row = lambda b, i: (b, i, 0)
    colblk = lambda b, i: (b, 0, i)
    prev_h = lambda b, i: (b, jnp.maximum(i * hb - 1, 0), 0)
    next_h = lambda b, i: (b, jnp.minimum((i + 1) * hb, seq // HALO - 1), 0)
    q0 = 2 * A_WIDTH
    v0 = q0 + B_WIDTH + KV_WIDTH
    wb = w_in.astype(BF16)
    wqt = w_in[:, q0:q0 + B_WIDTH].T.astype(BF16)
    wvt = w_in[:, v0:v0 + KV_WIDTH].T.astype(BF16)
    bq = b_in[q0:q0 + B_WIDTH].reshape(B_WIDTH, 1)
    bv = b_in[v0:v0 + KV_WIDTH].reshape(KV_WIDTH, 1)

    def out(width):
        return jax.ShapeDtypeStruct((bsz, seq, width), BF16), pl.BlockSpec((1, TS, width), row)

    def out_t(width):
        return jax.ShapeDtypeStruct((bsz, width, seq), BF16), pl.BlockSpec((1, width, TS), colblk)

    outs = [out(A_WIDTH), out_t(B_WIDTH), out(KV_WIDTH), out_t(KV_WIDTH), out(C_WIDTH)]
    return pl.pallas_call(
        _inproj_kernel,
        out_shape=[o[0] for o in outs],
        grid=grid,
        in_specs=[
            pl.BlockSpec((1, TS, d), row),
            pl.BlockSpec((1, HALO, d), prev_h),
            pl.BlockSpec((1, HALO, d), next_h),
            pl.BlockSpec((None, None, 6, d), lambda b, i: (layer, b, 0, 0)),
            pl.BlockSpec((d, IN_WIDTH), const2),
            pl.BlockSpec((1, IN_WIDTH), const2),
            pl.BlockSpec((B_WIDTH, d), const2),
            pl.BlockSpec((B_WIDTH, 1), const2),
            pl.BlockSpec((KV_WIDTH, d), const2),
            pl.BlockSpec((KV_WIDTH, 1), const2),
            pl.BlockSpec((1, A_WIDTH), const2),
            pl.BlockSpec((1, A_WIDTH), const2),
            pl.BlockSpec((A_HEADS, CHUNK, CHUNK), lambda b, i: (0, 0, 0)),
            pl.BlockSpec((CHUNK, A_WIDTH), const2),
            pl.BlockSpec((CONV_WIDTH, C_WIDTH), const2),
            pl.BlockSpec((1, C_WIDTH), const2),
            pl.BlockSpec((1, C_WIDTH), const2),
            pl.BlockSpec((1, C_WIDTH), const2),
        ],
        out_specs=[o[1] for o in outs],
        scratch_shapes=[
            pltpu.VMEM((TS + 2 * HALO, C_WIDTH), F32),
            pltpu.VMEM((SUBLANES, TS + SUBLANES, C_WIDTH), F32),
        ],
        compiler_params=pltpu.CompilerParams(
            dimension_semantics=("arbitrary", "arbitrary"), vmem_limit_bytes=VMEM_LIMIT),
        name="inproj_gmlp_conv",
    )(x, x, x, mod, wb, b_in.reshape(1, -1), wqt, bq, wvt, bv, ln_g, ln_b, ws, bsb,
      conv_w, conv_b, conv_ln_g, conv_ln_b)


def _first_argmax(vals, iota_f, width):
    m = jnp.max(vals, axis=0, keepdims=True)
    idx = jnp.min(jnp.where(vals == m, iota_f, float(width)), axis=0, keepdims=True)
    return m, idx


def _mixer_kernel(seq_len, tiles_per_seq, qt_ref, kp_ref, kc_ref, kn_ref, vtp_ref, vtc_ref, vtn_ref,
                  bias_ref, sink_ref, yc_ref, ya_ref, x_ref, mod_ref, wo_ref, bo_ref, lng_ref, lnb_ref,
                  rw_ref, rb_ref, lt_ref, before_ref, x1_ref, xs_ref, pos_ref, gate_ref, cnt_ref,
                  ot_scr, s_scr, h2_scr, pos_scr):
    n = pl.program_id(0)
    last_tile = pl.num_programs(0) - 2
    i = jnp.minimum(n, last_tile) % tiles_per_seq
    t0 = i * TT
    m = mod_ref[...]

    @pl.when(n == 0)
    def _():
        h2_scr[...] = jnp.zeros_like(h2_scr)
        pos_scr[...] = jnp.zeros_like(pos_scr)

    pos_prev = pos_scr[...]
    sort_rows = R_TILE // N_SORT_BLOCKS

    def sort_block(j):
        iota_r = j * sort_rows + lax.broadcasted_iota(I32, (sort_rows, TT), 0)
        onehot = jnp.where(iota_r == pos_prev[0:1], 1.0,
                           jnp.where(iota_r == pos_prev[1:2], 1.0, 0.0)).astype(BF16)
        xs_ref[j * sort_rows:(j + 1) * sort_rows, :] = _dot(onehot, h2_scr[...]).astype(BF16)

    def moe_input(y):
        x1 = _ln(ALPHA * x_ref[0] + (1.0 + m[2:3]) * y, lng_ref[...], lnb_ref[...])
        x1_ref[0] = x1
        h2 = (x1 * (1.0 + m[4:5]) + m[3:4]).astype(BF16)
        return h2, _dot_nt(rw_ref[...], h2)

    def choose(logits):
        scores = _sigmoid(logits)
        sel = scores + rb_ref[...]
        iota_f = lax.broadcasted_iota(I32, (EXPERTS_PER_GROUP, TT), 0).astype(F32)
        best = None
        for g in range(N_GROUPS):
            sl = slice(g * EXPERTS_PER_GROUP, (g + 1) * EXPERTS_PER_GROUP)
            sg = sel[sl]
            m1, i1 = _first_argmax(sg, iota_f, EXPERTS_PER_GROUP)
            m2, i2 = _first_argmax(jnp.where(iota_f == i1, -jnp.inf, sg), iota_f, EXPERTS_PER_GROUP)
            sc = scores[sl]
            s1 = jnp.sum(jnp.where(iota_f == i1, sc, 0.0), axis=0, keepdims=True)
            s2 = jnp.sum(jnp.where(iota_f == i2, sc, 0.0), axis=0, keepdims=True)
            cand = (m1 + m2, i1 + g * EXPERTS_PER_GROUP, i2 + g * EXPERTS_PER_GROUP, s1, s2)
            if best is None:
                best = cand
            else:
                take = cand[0] > best[0]
                best = tuple(jnp.where(take, c, b) for c, b in zip(cand, best))
        _, e1, e2, s1, s2 = best
        gate_ref[0] = jnp.concatenate([s1, s2], axis=0) / (s1 + s2)

        iota_e = lax.broadcasted_iota(I32, (N_EXPERTS, TT), 0).astype(F32)
        in0 = iota_e == e1
        in1 = iota_e == e2
        member = jnp.where(in0 | in1, 1.0, 0.0)
        cnt = jnp.sum(member, axis=1, keepdims=True)
        cnt_ref[0] = jnp.broadcast_to(cnt, (N_EXPERTS, LANES))
        nch = jnp.floor((cnt + (CH - 1)) * (1.0 / CH))
        nch_pad = jnp.concatenate([jnp.broadcast_to(nch, (N_EXPERTS, LANES)),
                                   jnp.zeros((LANES - N_EXPERTS, LANES), F32)], axis=0).astype(BF16)
        return in0, in1, member.astype(BF16), nch_pad

    def positions(in0, in1, member, nch_pad):
        padoff = _dot(lt_ref[...], nch_pad)[:, 0:1] * float(CH)
        rank = _dot(member, before_ref[...])
        posf = padoff + rank
        pos0 = jnp.sum(jnp.where(in0, posf, 0.0), axis=0, keepdims=True).astype(I32)
        pos1 = jnp.sum(jnp.where(in1, posf, 0.0), axis=0, keepdims=True).astype(I32)
        pos = jnp.concatenate([pos0, pos1], axis=0)
        pos_ref[0] = pos
        return pos

    kfull = jnp.concatenate([kp_ref[0], kc_ref[0], kn_ref[0]], axis=0)
    vtfull = jnp.concatenate([vtp_ref[0], vtc_ref[0], vtn_ref[0]], axis=1)
    grp = B_HEADS // B_KV_HEADS
    n_qb = TT // WINDOW
    units = [(jb, g) for jb in range(n_qb) for g in range(B_KV_HEADS)]
    key_i = lax.broadcasted_iota(I32, (3 * WINDOW, 1), 0)

    def scores(u):
        jb, g = units[u]
        kb = kfull[jb * WINDOW:(jb + 3) * WINDOW, g * HEAD_DIM:(g + 1) * HEAD_DIM]
        qt = jnp.concatenate(
            [qt_ref[0, h * HEAD_DIM:(h + 1) * HEAD_DIM, jb * WINDOW:(jb + 1) * WINDOW]
             for h in range(g * grp, (g + 1) * grp)], axis=1)
        s = _dot(kb, qt) + bias_ref[g]
        if jb == 0 or jb == n_qb - 1:
            kpos = t0 + (jb - 1) * WINDOW + key_i
            s = jnp.where((kpos >= 0) & (kpos < seq_len), s, NEG_INF)
        s_scr[u % N_SBUF] = s

    def values(u):
        jb, g = units[u]
        s = s_scr[u % N_SBUF]
        sink = sink_ref[g:g + 1, :]
        mx = jnp.maximum(jnp.max(s, axis=0, keepdims=True), sink)
        p = jnp.exp2(s - mx)
        den = jnp.sum(p, axis=0, keepdims=True) + jnp.exp2(sink - mx)
        vt = vtfull[g * HEAD_DIM:(g + 1) * HEAD_DIM, jb * WINDOW:(jb + 3) * WINDOW]
        ot = _dot(vt, p.astype(BF16)) / den
        for hh in range(grp):
            h = g * grp + hh
            ot_scr[h * HEAD_DIM:(h + 1) * HEAD_DIM, jb * WINDOW:(jb + 1) * WINDOW] = (
                ot[:, hh * WINDOW:(hh + 1) * WINDOW])

    for u in range(N_SBUF - 1):
        scores(u)
    y_other = None
    for u in range(len(units)):
        if u + N_SBUF - 1 < len(units):
            scores(u + N_SBUF - 1)
        if u == 2:
            y_other = _dot(ya_ref[0], wo_ref[0:A_WIDTH, :]) + bo_ref[...]
        if u == 5:
            y_other = y_other + _dot(yc_ref[0], wo_ref[A_WIDTH + B_WIDTH:, :])
        values(u)
    yb = jnp.transpose(ot_scr[...]).astype(BF16)
    y = y_other + _dot(yb, wo_ref[A_WIDTH:A_WIDTH + B_WIDTH, :])
    early = N_SORT_BLOCKS - 1
    for j in range(early):
        sort_block(j)
    new_h2, logits = moe_input(y)
    for j in range(early, N_SORT_BLOCKS):
        sort_block(j)
    new_pos = positions(*choose(logits))
    pos_scr[...] = new_pos
    h2_scr[...] = new_h2


def _mixer_call(seq_len, qt, k, vt, bias, sink, yc, ya, x, mod, layer, w_out, b_out, ln_g, ln_b, rw, rbias):
    bsz, seq, d = x.shape
    nt = seq // TT
    n_tiles = bsz * nt
    kb = TT // WINDOW
    grp = B_HEADS // B_KV_HEADS
    const2 = lambda n: (0, 0)

    def at_tile(fn):
        def index_map(n):
            t = jnp.minimum(n, n_tiles - 1)
            return fn(t // nt, t % nt)
        return index_map

    row = at_tile(lambda b, i: (b, i, 0))
    colblk = at_tile(lambda b, i: (b, 0, i))
    prev_k = at_tile(lambda b, i: (b, jnp.maximum(i * kb - 1, 0), 0))
    next_k = at_tile(lambda b, i: (b, jnp.minimum((i + 1) * kb, seq // WINDOW - 1), 0))
    prev_v = at_tile(lambda b, i: (b, 0, jnp.maximum(i * kb - 1, 0)))
    next_v = at_tile(lambda b, i: (b, 0, jnp.minimum((i + 1) * kb, seq // WINDOW - 1)))
    tile = at_tile(lambda b, i: (b * nt + i, 0, 0))
    lt = (jnp.arange(LANES)[None, :] < jnp.arange(N_EXPERTS)[:, None]).astype(BF16)
    before = (jnp.arange(TT)[:, None] < jnp.arange(TT)[None, :]).astype(BF16)
    return pl.pallas_call(
        functools.partial(_mixer_kernel, seq_len, nt),
        out_shape=[
            jax.ShapeDtypeStruct((bsz, seq, d), F32),
            jax.ShapeDtypeStruct(((n_tiles + 1) * R_TILE, d), BF16),
            jax.ShapeDtypeStruct((n_tiles, TOP_K, TT), I32),
            jax.ShapeDtypeStruct((n_tiles, TOP_K, TT), F32),
            jax.ShapeDtypeStruct((n_tiles, N_EXPERTS, LANES), F32),
        ],
        grid=(n_tiles + 1,),
        in_specs=[
            pl.BlockSpec((1, B_WIDTH, TT), colblk),
            pl.BlockSpec((1, WINDOW, KV_WIDTH), prev_k),
            pl.BlockSpec((1, TT, KV_WIDTH), row),
            pl.BlockSpec((1, WINDOW, KV_WIDTH), next_k),
            pl.BlockSpec((1, KV_WIDTH, WINDOW), prev_v),
            pl.BlockSpec((1, KV_WIDTH, TT), colblk),
            pl.BlockSpec((1, KV_WIDTH, WINDOW), next_v),
            pl.BlockSpec((B_KV_HEADS, 3 * WINDOW, grp * WINDOW), lambda n: (0, 0, 0)),
            pl.BlockSpec((B_KV_HEADS, grp * WINDOW), const2),
            pl.BlockSpec((1, TT, C_WIDTH), row),
            pl.BlockSpec((1, TT, A_WIDTH), row),
            pl.BlockSpec((1, TT, d), row),
            pl.BlockSpec((None, None, 6, d), at_tile(lambda b, i: (layer, b, 0, 0))),
            pl.BlockSpec((d, d), const2),
            pl.BlockSpec((1, d), const2),
            pl.BlockSpec((1, d), const2),
            pl.BlockSpec((1, d), const2),
            pl.BlockSpec((N_EXPERTS, d), const2),
            pl.BlockSpec((N_EXPERTS, 1), const2),
            pl.BlockSpec((N_EXPERTS, LANES), const2),
            pl.BlockSpec((TT, TT), const2),
        ],
        out_specs=[
            pl.BlockSpec((1, TT, d), row),
            pl.BlockSpec((R_TILE, d), lambda n: (n, 0)),
            pl.BlockSpec((1, TOP_K, TT), tile),
            pl.BlockSpec((1, TOP_K, TT), tile),
            pl.BlockSpec((1, N_EXPERTS, LANES), tile),
        ],
        scratch_shapes=[
            pltpu.VMEM((B_WIDTH, TT), F32),
            pltpu.VMEM((N_SBUF, 3 * WINDOW, grp * WINDOW), F32),
            pltpu.VMEM((TT, d), BF16),
            pltpu.VMEM((TOP_K, TT), I32),
        ],
        compiler_params=pltpu.CompilerParams(
            dimension_semantics=("arbitrary",), vmem_limit_bytes=VMEM_LIMIT),
        name="mixer_router_sort",
    )(qt, k, k, k, vt, vt, vt, bias, sink, yc, ya, x, mod, w_out, b_out, ln_g, ln_b, rw, rbias, lt, before)


def _expert_kernel(layer, src_ref, dst_ref, bexp_ref, enext_ref, nused_ref, xs_hbm, wg_hbm, wu_hbm, wd_hbm,
                   ys_hbm, xbuf, ybuf, wg_st, wu_st, wd_st, wgb, wub, wdb, sem_in, sem_out, sem_w):
    n_blk = bexp_ref.shape[0]
    nused = nused_ref[0]
    chunks_per_tile = R_TILE // CH
    zero_chunk = chunks_per_tile - 1
    scratch_chunk = 0
    de = wgb.shape[2]
    half = de // 2

    def in_copy(chunk, s, c):
        return pltpu.make_async_copy(
            xs_hbm.at[pl.ds(pl.multiple_of(chunk * CH, CH), CH)],
            xbuf.at[s, pl.ds(c * CH, CH)], sem_in.at[s])

    def out_copy(chunk, s, c):
        return pltpu.make_async_copy(
            ybuf.at[s, pl.ds(c * CH, CH)],
            ys_hbm.at[pl.ds(pl.multiple_of(chunk * CH, CH), CH)], sem_out.at[s])

    def weight_copies(e, s):
        return [pltpu.make_async_copy(wg_hbm.at[layer, e], wg_st.at[s], sem_w.at[s]),
                pltpu.make_async_copy(wu_hbm.at[layer, e], wu_st.at[s], sem_w.at[s]),
                pltpu.make_async_copy(wd_hbm.at[layer, e], wd_st.at[s], sem_w.at[s])]

    def gather_chunk(b, c):
        return jnp.where(b < nused, src_ref[jnp.minimum(b, n_blk - 1) * NCH + c], zero_chunk)

    def write_back_chunk(b, c):
        real = (b >= 0) & (b < nused)
        return jnp.where(real, dst_ref[jnp.clip(b, 0, n_blk - 1) * NCH + c], scratch_chunk + (b % 2) * NCH + c)

    ybuf[...] = jnp.zeros_like(ybuf)
    wgb[...] = jnp.zeros_like(wgb)
    wub[...] = jnp.zeros_like(wub)
    wdb[...] = jnp.zeros_like(wdb)
    for b in range(4):
        for c in range(NCH):
            in_copy(gather_chunk(b, c), b, c).start()
    for b in range(2):
        for c in range(NCH):
            out_copy(scratch_chunk + b * NCH + c, b, c).start()

    @pl.when(nused > 0)
    def _():
        for cp in weight_copies(bexp_ref[0], 0):
            cp.start()

    def pair(i, n_changes):
        blocks = (2 * i, 2 * i + 1)
        wslot = []
        for blk in blocks:
            jb = jnp.minimum(blk, n_blk - 1)
            expert = bexp_ref[jb]
            prev_expert = bexp_ref[jnp.maximum(jb - 1, 0)]
            change = (blk < nused) & ((blk == 0) | (expert != prev_expert))

            @pl.when(change)
            def _(expert=expert, jb=jb, n_changes=n_changes):
                ws = n_changes % 2
                for cp in weight_copies(expert, ws):
                    cp.wait()
                wgb[ws] = wg_st[ws].astype(BF16)
                wub[ws] = wu_st[ws].astype(BF16)
                wdb[ws] = wd_st[ws].astype(BF16)
                upcoming = enext_ref[jb]

                @pl.when(upcoming != expert)
                def _():
                    for cp in weight_copies(upcoming, 1 - ws):
                        cp.start()

            n_changes = n_changes + change.astype(I32)
            wslot.append(jnp.maximum(n_changes - 1, 0) % 2)

        for blk in blocks:
            for c in range(NCH):
                in_copy(0, blk % N_XRING, c).wait()
            for c in range(NCH):
                out_copy(0, blk % N_YRING, c).wait()

        dmas = []
        for blk in blocks:
            for c in range(NCH):
                dmas.append(functools.partial(
                    lambda b, c: out_copy(write_back_chunk(b, c), b % N_YRING, c).start(), blk - 2, c))
        for blk in blocks:
            for c in range(NCH):
                dmas.append(functools.partial(
                    lambda b, c: in_copy(gather_chunk(b, c), b % N_XRING, c).start(priority=1), blk + 4, c))
        n_groups = 8
        per_group = len(dmas) // n_groups

        def issue(gidx):
            for start in dmas[gidx * per_group:(gidx + 1) * per_group]:
                start()

        gidx = 0
        for k, blk in enumerate(blocks):
            x = xbuf[blk % N_XRING]
            ws = wslot[k]
            hmid = []
            for h in range(2):
                cols = slice(h * half, (h + 1) * half)
                g = _dot(x, wgb[ws, :, cols])
                issue(gidx)
                u = _dot(x, wub[ws, :, cols])
                issue(gidx + 1)
                gidx += 2
                hmid.append(((g * _sigmoid(g)) * u).astype(BF16))
            y = _dot(hmid[0], wdb[ws, 0:half, :]) + _dot(hmid[1], wdb[ws, half:, :])
            ybuf[blk % N_YRING] = y.astype(BF16)
        return n_changes

    n_pairs = (nused + 1) // 2
    lax.fori_loop(0, n_pairs, pair, jnp.int32(0))

    last = 2 * n_pairs
    for blk in range(4):
        for c in range(NCH):
            in_copy(0, (last + blk) % N_XRING, c).wait()
    for blk in (last, last + 1):
        for c in range(NCH):
            out_copy(0, blk % N_YRING, c).wait()
    for blk in (last - 2, last - 1):
        for c in range(NCH):
            out_copy(write_back_chunk(blk, c), blk % N_YRING, c).start()
    for blk in (last - 2, last - 1):
        for c in range(NCH):
            out_copy(0, blk % N_YRING, c).wait()


def _expert_call(src, dst, bexp, enext, nused, xs, layer, w_gate, w_up, w_down):
    d = xs.shape[-1]
    de = w_gate.shape[-1]
    grid_spec = pltpu.PrefetchScalarGridSpec(
        num_scalar_prefetch=5,
        grid=(1,),
        in_specs=[pl.BlockSpec(memory_space=pl.ANY)] * 4,
        out_specs=pl.BlockSpec(memory_space=pl.ANY),
        scratch_shapes=[
            pltpu.VMEM((N_XRING, BM, d), BF16),
            pltpu.VMEM((N_YRING, BM, d), BF16),
            pltpu.VMEM((2, d, de), F32),
            pltpu.VMEM((2, d, de), F32),
            pltpu.VMEM((2, de, d), F32),
            pltpu.VMEM((2, d, de), BF16),
            pltpu.VMEM((2, d, de), BF16),
            pltpu.VMEM((2, de, d), BF16),
            pltpu.SemaphoreType.DMA((N_XRING,)),
            pltpu.SemaphoreType.DMA((N_YRING,)),
            pltpu.SemaphoreType.DMA((2,)),
        ],
    )
    return pl.pallas_call(
        functools.partial(_expert_kernel, layer),
        out_shape=jax.ShapeDtypeStruct(xs.shape, xs.dtype),
        grid_spec=grid_spec,
        input_output_aliases={5: 0},
        compiler_params=pltpu.CompilerParams(
            dimension_semantics=("arbitrary",), vmem_limit_bytes=VMEM_LIMIT),
        name="moe_experts",
    )(src, dst, bexp, enext, nused, xs, w_gate, w_up, w_down)


def _combine_kernel(ysa_ref, ysb_ref, pos_ref, gate_ref, x1_ref, mod_ref, lng_ref, lnb_ref, o_ref):
    m = mod_ref[...]
    for k, ys_ref in enumerate((ysa_ref, ysb_ref)):
        rows = slice(k * TT, (k + 1) * TT)
        posc = jnp.transpose(pos_ref[k].astype(F32)).astype(I32)
        gatec = jnp.transpose(gate_ref[k])
        iota_c = lax.broadcasted_iota(I32, (TT, R_TILE), 1)
        pick = jnp.where(iota_c == posc[:, 0:1], gatec[:, 0:1],
                         jnp.where(iota_c == posc[:, 1:2], gatec[:, 1:2], 0.0)).astype(BF16)
        y = _dot(pick, ys_ref[...])
        o_ref[rows, :] = _ln(ALPHA * x1_ref[rows, :] + (1.0 + m[5:6]) * y, lng_ref[...], lnb_ref[...])


def _combine_call(ys, pos, gate, x1, mod, layer, tiles_per_seq, ln_g, ln_b):
    t, d = x1.shape
    n_tiles = t // TT
    assert tiles_per_seq % 2 == 0
    const2 = lambda n: (0, 0)
    return pl.pallas_call(
        _combine_kernel,
        out_shape=jax.ShapeDtypeStruct((t, d), F32),
        grid=(n_tiles // 2,),
        in_specs=[
            pl.BlockSpec((R_TILE, d), lambda n: (2 * n + 1, 0)),
            pl.BlockSpec((R_TILE, d), lambda n: (2 * n + 2, 0)),
            pl.BlockSpec((2, TOP_K, TT), lambda n: (n, 0, 0)),
            pl.BlockSpec((2, TOP_K, TT), lambda n: (n, 0, 0)),
            pl.BlockSpec((2 * TT, d), lambda n: (n, 0)),
            pl.BlockSpec((None, None, 6, d), lambda n: (layer, 2 * n // tiles_per_seq, 0, 0)),
            pl.BlockSpec((1, d), const2),
            pl.BlockSpec((1, d), const2),
        ],
        out_specs=pl.BlockSpec((2 * TT, d), lambda n: (n, 0)),
        compiler_params=pltpu.CompilerParams(
            dimension_semantics=("arbitrary",), vmem_limit_bytes=VMEM_LIMIT),
        name="moe_combine",
    )(ys, ys, pos, gate, x1, mod, ln_g, ln_b)


def _dispatch_plan(cnt, n_blocks):
    n_tiles = cnt.shape[0]
    chunks_per_tile = R_TILE // CH
    nch = (cnt + CH - 1) // CH
    padoff_ch = jnp.cumsum(nch, axis=1) - nch
    tot = jnp.sum(nch, axis=0)
    totpad = (tot + NCH - 1) // NCH * NCH
    eend = jnp.cumsum(totpad)
    ebase = eend - totpad
    tbase = jnp.cumsum(nch, axis=0) - nch
    start = (ebase[None, :] + tbase).T.reshape(-1)
    base = ((jnp.arange(n_tiles, dtype=I32)[:, None] + 1) * chunks_per_tile + padoff_ch).T.reshape(-1)
    vals = jnp.stack([start, base, nch.T.reshape(-1)], axis=1)
    delta = vals - jnp.concatenate([jnp.zeros((1, 3), I32), vals[:-1]], axis=0)
    digits = jnp.concatenate([delta // LANES, delta % LANES], axis=1).astype(BF16)
    slot = jnp.arange(n_blocks * NCH, dtype=I32)
    started = (start[None, :] <= slot[:, None]).astype(BF16)
    got = jnp.dot(started, digits, preferred_element_type=F32).astype(I32)
    seg = got[:, :3] * LANES + got[:, 3:]
    j = slot - seg[:, 0]
    valid = (j < seg[:, 2]) & (slot < eend[-1])
    zero_chunk = chunks_per_tile - 1
    src = jnp.where(valid, seg[:, 1] + j, zero_chunk).astype(I32)
    blk = slot // NCH
    scratch = (blk % 2) * NCH + slot % NCH
    dst = jnp.where(valid, src, scratch).astype(I32)
    first = jnp.arange(n_blocks, dtype=I32) * NCH
    bexp = jnp.minimum(jnp.sum(eend[None, :] <= first[:, None], axis=1), N_EXPERTS - 1).astype(I32)
    nused = (eend[-1] // NCH).astype(I32).reshape(1)
    ids = jnp.arange(N_EXPERTS, dtype=I32)
    later = jnp.where((ids[None, :] > ids[:, None]) & (totpad > 0)[None, :], ids[None, :], N_EXPERTS)
    next_of = jnp.min(later, axis=1)
    next_of = jnp.where(next_of == N_EXPERTS, ids, next_of)
    enext = jnp.sum(jnp.where(bexp[:, None] == ids[None, :], next_of[None, :], 0), axis=1).astype(I32)
    return src, dst, bexp, enext, nused


def _t5_bucket(rel):
    nb = N_BUCKETS // 2
    max_exact = nb // 2
    ret = jnp.where(rel > 0, nb, 0)
    n = jnp.abs(rel)
    nf = jnp.maximum(n, 1).astype(jnp.float32)
    large = max_exact + (jnp.log(nf / max_exact) / math.log(MAX_DISTANCE / max_exact)
                         * (nb - max_exact)).astype(jnp.int32)
    large = jnp.minimum(large, nb - 1)
    return ret + jnp.where(n < max_exact, n, large)


def _band_bias(rel_bias):
    qi = jnp.arange(WINDOW)
    kj = jnp.arange(3 * WINDOW)
    rel = kj[None, :] - WINDOW - qi[:, None]
    pick = _t5_bucket(rel)[:, :, None, None] == jnp.arange(N_BUCKETS)[None, None, :, None]
    bias = jnp.sum(jnp.where(pick, rel_bias.astype(F32)[None, None], 0.0), axis=2) * LOG2E
    bias = jnp.where((jnp.abs(rel) <= WINDOW)[:, :, None], bias, NEG_INF)
    grp = B_HEADS // B_KV_HEADS
    bias = jnp.transpose(bias, (2, 1, 0)).reshape(B_KV_HEADS, grp, 3 * WINDOW, WINDOW)
    return jnp.transpose(bias, (0, 2, 1, 3)).reshape(B_KV_HEADS, 3 * WINDOW, grp * WINDOW)


def kernel(x, c, ada_w, ada_b, w_in, b_in, gmlp_ln_g, gmlp_ln_b, gmlp_ws, gmlp_bs, attn_sink, conv_w,
           conv_b, conv_ln_g, conv_ln_b, w_out, b_out, ln_mix_g, ln_mix_b, w_gate, w_up, w_down,
           ln_ffn_g, ln_ffn_b, rel_bias, router_w, router_bias):
    bsz, seq, d = x.shape
    n_layers = ada_w.shape[0]
    t = bsz * seq
    nt = seq // TT
    n_tiles = t // TT
    max_chunks = n_tiles * (TOP_K * TT // CH + N_EXPERTS) + N_EXPERTS * (NCH - 1)
    n_blocks = -(-max_chunks // NCH)

    mod = _ada_call(c, ada_w, ada_b).reshape(n_layers, bsz, 6, d)
    bias = _band_bias(rel_bias)
    rw = router_w.T.astype(BF16)
    rbias = router_bias.astype(F32).reshape(N_EXPERTS, 1)
    row = lambda a: a.reshape(1, -1)

    for l in range(n_layers):
        bsb = jnp.repeat(gmlp_bs[l].T, HEAD_DIM, axis=1)
        ya, qt, k, vt, yc = _inproj_call(
            x, mod, l, w_in[l], b_in[l], row(gmlp_ln_g[l]), row(gmlp_ln_b[l]),
            gmlp_ws[l].astype(BF16), bsb, conv_w[l], row(conv_b[l]), row(conv_ln_g[l]), row(conv_ln_b[l]))
        sink = jnp.repeat(attn_sink[l].astype(F32) * LOG2E, WINDOW).reshape(B_KV_HEADS, -1)
        x1, xs, pos, gate, cntb = _mixer_call(
            seq, qt, k, vt, bias, sink, yc, ya, x, mod, l, w_out[l].astype(BF16),
            row(b_out[l]), row(ln_mix_g[l]), row(ln_mix_b[l]), rw, rbias)
        cnt = cntb[:, :, 0].astype(I32)
        src, dst, bexp, enext, nused = _dispatch_plan(cnt, n_blocks)
        ys = _expert_call(src, dst, bexp, enext, nused, xs, l, w_gate, w_up, w_down)
        x = _combine_call(ys, pos, gate, x1.reshape(t, d), mod, l, nt, row(ln_ffn_g[l]),
                          row(ln_ffn_b[l])).reshape(bsz, seq, d)
    return x
```

```python
import functools
import math

import jax
import jax.numpy as jnp
from jax import lax
from jax.experimental import pallas as pl
from jax.experimental.pallas import tpu as pltpu

F32 = jnp.float32
BF16 = jnp.bfloat16
I32 = jnp.int32

D_MODEL = 1024
DEPTH = 2
HEAD_DIM = 64
A_WIDTH = 256
A_HEADS = 4
CHUNK = 128
B_WIDTH = 512
B_HEADS = 8
B_KV_HEADS = 2
KV_WIDTH = B_KV_HEADS * HEAD_DIM
WINDOW = 128
N_BUCKETS = 32
MAX_DISTANCE = 128
C_WIDTH = 256
CONV_WIDTH = 31
CONV_PAD = CONV_WIDTH // 2
IN_WIDTH = 2 * A_WIDTH + B_WIDTH + 2 * KV_WIDTH + 2 * C_WIDTH
N_EXPERTS = 32
N_GROUPS = 4
EXPERTS_PER_GROUP = N_EXPERTS // N_GROUPS
TOP_K = 2
D_EXPERT = D_MODEL // 2
ALPHA = (2 * DEPTH) ** 0.25
LN_EPS = 1e-5
NEG_INF = -1e30
LOG2E = 1.4426950408889634
Q_SCALE = HEAD_DIM ** -0.5 * LOG2E

LANES = 128
SUBLANES = 8
BF16_SUBLANES = 16
VMEM_LIMIT = 48 * 1024 * 1024

ADA_TN = 1536
TS = 1024
TT = 512
CH = BF16_SUBLANES
R_TILE = 1536
BM = 512
NCH = BM // CH
N_XRING = 6
N_YRING = 4
N_SORT_BLOCKS = 6
N_SORT_EARLY = 3
N_SBUF = 2
HALO = 16

assert R_TILE >= TOP_K * TT + N_EXPERTS * (CH - 1) + CH
assert R_TILE >= 2 * NCH * CH + CH


def _sigmoid(x):
    return 1.0 / (1.0 + jnp.exp(-x))


def _gelu_tanh(x):
    return x * (0.5 * (1.0 + jnp.tanh(0.7978845608028654 * (x + 0.044715 * (x * x * x)))))


def _ln(x, g, b):
    mu = jnp.mean(x, axis=-1, keepdims=True)
    xc = x - mu
    var = jnp.mean(xc * xc, axis=-1, keepdims=True)
    return xc * lax.rsqrt(var + LN_EPS) * g + b


def _dot(a, b):
    return jnp.dot(a, b, preferred_element_type=F32)


def _dot_nt(a, b):
    return lax.dot_general(a, b, (((1,), (1,)), ((), ())), preferred_element_type=F32)


def _ada_kernel(c_ref, w_ref, b_ref, o_ref):
    c = c_ref[...]
    s = (c * _sigmoid(c)).astype(BF16)
    o_ref[0] = _dot(s, w_ref[0].astype(BF16)) + b_ref[0]


def _ada_call(c, ada_w, ada_b):
    nl, d, n = ada_w.shape
    bsz = c.shape[0]
    return pl.pallas_call(
        _ada_kernel,
        out_shape=jax.ShapeDtypeStruct((nl, bsz, n), F32),
        grid=(nl, n // ADA_TN),
        in_specs=[
            pl.BlockSpec((bsz, d), lambda l, j: (0, 0)),
            pl.BlockSpec((1, d, ADA_TN), lambda l, j: (l, 0, j)),
            pl.BlockSpec((1, 1, ADA_TN), lambda l, j: (l, 0, j)),
        ],
        out_specs=pl.BlockSpec((1, bsz, ADA_TN), lambda l, j: (l, 0, j)),
        compiler_params=pltpu.CompilerParams(
            dimension_semantics=("arbitrary", "arbitrary"), vmem_limit_bytes=VMEM_LIMIT),
        name="ada_mod",
    )(c, ada_w, ada_b.reshape(nl, 1, n))


def _inproj_kernel(x_ref, xp_ref, xn_ref, mod_ref, w_ref, b_ref, wqt_ref, bq_ref, wvt_ref, bv_ref,
                   lng_ref, lnb_ref, ws_ref, bsb_ref, cw_ref, cb_ref, clg_ref, clb_ref,
                   ya_ref, qt_ref, k_ref, vt_ref, yc_ref, conv_scr, z_scr):
    i = pl.program_id(1)
    n_i = pl.num_programs(1)
    m = mod_ref[...]

    def modulate(xv):
        return (xv * (1.0 + m[1:2]) + m[0:1]).astype(BF16)

    hb = modulate(x_ref[0])
    col_u, col_v = 0, A_WIDTH
    col_k = 2 * A_WIDTH + B_WIDTH
    col_a = col_k + 2 * KV_WIDTH
    col_g = col_a + C_WIDTH

    def proj(lhs, c0, width):
        return _dot(lhs, w_ref[:, c0:c0 + width]) + b_ref[:, c0:c0 + width]

    hx = jnp.concatenate([modulate(xp_ref[0]), hb, modulate(xn_ref[0])], axis=0)
    yg = proj(hx, col_a, C_WIDTH) * _sigmoid(proj(hx, col_g, C_WIDTH))
    conv_scr[0:HALO, :] = yg[0:HALO] * jnp.where(i > 0, 1.0, 0.0)
    conv_scr[HALO:HALO + TS, :] = yg[HALO:HALO + TS]
    conv_scr[HALO + TS:, :] = yg[HALO + TS:] * jnp.where(i < n_i - 1, 1.0, 0.0)
    first = HALO - CONV_PAD
    acc = jnp.zeros((TS, C_WIDTH), F32) + cb_ref[...]
    for r in range(SUBLANES):
        z = None
        for a in range(-(-(first + CONV_WIDTH) // SUBLANES)):
            w = a * SUBLANES + r - first
            if 0 <= w < CONV_WIDTH:
                term = conv_scr[a * SUBLANES:a * SUBLANES + TS + SUBLANES, :] * cw_ref[w:w + 1, :]
                z = term if z is None else z + term
        if r == 0:
            acc = acc + z[0:TS]
        else:
            z_scr[r] = z
            acc = acc + z_scr[r, r:r + TS, :]
    yc = _ln(acc, clg_ref[...], clb_ref[...])
    yc_ref[0] = (yc * _sigmoid(yc)).astype(BF16)

    u = _gelu_tanh(proj(hb, col_u, A_WIDTH))
    v = _gelu_tanh(proj(hb, col_v, A_WIDTH))
    qt_ref[0] = ((_dot_nt(wqt_ref[...], hb) + bq_ref[...]) * Q_SCALE).astype(BF16)
    k_ref[0] = proj(hb, col_k, KV_WIDTH).astype(BF16)
    vt_ref[0] = (_dot_nt(wvt_ref[...], hb) + bv_ref[...]).astype(BF16)

    vb = _ln(v, lng_ref[...], lnb_ref[...]).astype(BF16)
    head_of_lane = lax.broadcasted_iota(I32, (CHUNK, A_WIDTH), 1) // HEAD_DIM
    for ch in range(TS // CHUNK):
        vc = vb[ch * CHUNK:(ch + 1) * CHUNK]
        acc = bsb_ref[...]
        for hh in range(A_HEADS):
            acc = acc + _dot(ws_ref[hh], jnp.where(head_of_lane == hh, vc, jnp.zeros_like(vc)))
        ya_ref[0, ch * CHUNK:(ch + 1) * CHUNK, :] = (u[ch * CHUNK:(ch + 1) * CHUNK] * acc).astype(BF16)


def _inproj_call(x, mod, layer, w_in, b_in, ln_g, ln_b, ws, bsb, conv_w, conv_b, conv_ln_g, conv_ln_b):
    bsz, seq, d = x.shape
    grid = (bsz, seq // TS)
    hb = TS // HALO
    const2 = lambda b, i: (0, 0)
    row = lambda b, i: (b, i, 0)
    colblk = lambda b, i: (b, 0, i)
    prev_h = lambda b, i: (b, jnp.maximum(i * hb - 1, 0), 0)
    next_h = lambda b, i: (b, jnp.minimum((i + 1) * hb, seq // HALO - 1), 0)
    q0 = 2 * A_WIDTH
    v0 = q0 + B_WIDTH + KV_WIDTH
    wb = w_in.astype(BF16)
    wqt = w_in[:, q0:q0 + B_WIDTH].T.astype(BF16)
    wvt = w_in[:, v0:v0 + KV_WIDTH].T.astype(BF16)
    bq = b_in[q0:q0 + B_WIDTH].reshape(B_WIDTH, 1)
    bv = b_in[v0:v0 + KV_WIDTH].reshape(KV_WIDTH, 1)

    def out(width):
        return jax.ShapeDtypeStruct((bsz, seq, width), BF16), pl.BlockSpec((1, TS, width), row)

    def out_t(width):
        return jax.ShapeDtypeStruct((bsz, width, seq), BF16), pl.BlockSpec((1, width, TS), colblk)

    outs = [out(A_WIDTH), out_t(B_WIDTH), out(KV_WIDTH), out_t(KV_WIDTH), out(C_WIDTH)]
    return pl.pallas_call(
        _inproj_kernel,
        out_shape=[o[0] for o in outs],
        grid=grid,
        in_specs=[
            pl.BlockSpec((1, TS, d), row),
            pl.BlockSpec((1, HALO, d), prev_h),
            pl.BlockSpec((1, HALO, d), next_h),
            pl.BlockSpec((None, None, 6, d), lambda b, i: (layer, b, 0, 0)),
            pl.BlockSpec((d, IN_WIDTH), const2),
            pl.BlockSpec((1, IN_WIDTH), const2),
            pl.BlockSpec((B_WIDTH, d), const2),
            pl.BlockSpec((B_WIDTH, 1), const2),
            pl.BlockSpec((KV_WIDTH, d), const2),
            pl.BlockSpec((KV_WIDTH, 1), const2),
            pl.BlockSpec((1, A_WIDTH), const2),
            pl.BlockSpec((1, A_WIDTH), const2),
            pl.BlockSpec((A_HEADS, CHUNK, CHUNK), lambda b, i: (0, 0, 0)),
            pl.BlockSpec((CHUNK, A_WIDTH), const2),
            pl.BlockSpec((CONV_WIDTH, C_WIDTH), const2),
            pl.BlockSpec((1, C_WIDTH), const2),
            pl.BlockSpec((1, C_WIDTH), const2),
            pl.BlockSpec((1, C_WIDTH), const2),
        ],
        out_specs=[o[1] for o in outs],
        scratch_shapes=[
            pltpu.VMEM((TS + 2 * HALO, C_WIDTH), F32),
            pltpu.VMEM((SUBLANES, TS + SUBLANES, C_WIDTH), F32),
        ],
        compiler_params=pltpu.CompilerParams(
            dimension_semantics=("arbitrary", "arbitrary"), vmem_limit_bytes=VMEM_LIMIT),
        name="inproj_gmlp_conv",
    )(x, x, x, mod, wb, b_in.reshape(1, -1), wqt, bq, wvt, bv, ln_g, ln_b, ws, bsb,
      conv_w, conv_b, conv_ln_g, conv_ln_b)


def _first_argmax(vals, iota_f, width):
    m = jnp.max(vals, axis=0, keepdims=True)
    idx = jnp.min(jnp.where(vals == m, iota_f, float(width)), axis=0, keepdims=True)
    return m, idx


def _mixer_kernel(seq_len, tiles_per_seq, qt_ref, kp_ref, kc_ref, kn_ref, vtp_ref, vtc_ref, vtn_ref,
                  bias_ref, sink_ref, yc_ref, ya_ref, x_ref, mod_ref, wo_ref, bo_ref, lng_ref, lnb_ref,
                  rw_ref, rb_ref, lt_ref, before_ref, x1_ref, xs_ref, pos_ref, gate_ref, cnt_ref,
                  ot_scr, s_scr, h2_scr, pos_scr):
    n = pl.program_id(0)
    last_tile = pl.num_programs(0) - 2
    i = jnp.minimum(n, last_tile) % tiles_per_seq
    t0 = i * TT
    m = mod_ref[...]

    @pl.when(n == 0)
    def _():
        h2_scr[...] = jnp.zeros_like(h2_scr)
        pos_scr[...] = jnp.zeros_like(pos_scr)

    pos_prev = pos_scr[...]
    sort_rows = R_TILE // N_SORT_BLOCKS

    def sort_block(j):
        iota_r = j * sort_rows + lax.broadcasted_iota(I32, (sort_rows, TT), 0)
        onehot = jnp.where(iota_r == pos_prev[0:1], 1.0,
                           jnp.where(iota_r == pos_prev[1:2], 1.0, 0.0)).astype(BF16)
        xs_ref[j * sort_rows:(j + 1) * sort_rows, :] = _dot(onehot, h2_scr[...]).astype(BF16)

    def moe_input(y):
        x1 = _ln(ALPHA * x_ref[0] + (1.0 + m[2:3]) * y, lng_ref[...], lnb_ref[...])
        x1_ref[0] = x1
        h2 = (x1 * (1.0 + m[4:5]) + m[3:4]).astype(BF16)
        return h2, _dot_nt(rw_ref[...], h2)

    def choose(logits):
        scores = _sigmoid(logits)
        sel = scores + rb_ref[...]
        iota_f = lax.broadcasted_iota(I32, (EXPERTS_PER_GROUP, TT), 0).astype(F32)
        best = None
        for g in range(N_GROUPS):
            sl = slice(g * EXPERTS_PER_GROUP, (g + 1) * EXPERTS_PER_GROUP)
            sg = sel[sl]
            m1, i1 = _first_argmax(sg, iota_f, EXPERTS_PER_GROUP)
            m2, i2 = _first_argmax(jnp.where(iota_f == i1, -jnp.inf, sg), iota_f, EXPERTS_PER_GROUP)
            sc = scores[sl]
            s1 = jnp.sum(jnp.where(iota_f == i1, sc, 0.0), axis=0, keepdims=True)
            s2 = jnp.sum(jnp.where(iota_f == i2, sc, 0.0), axis=0, keepdims=True)
            cand = (m1 + m2, i1 + g * EXPERTS_PER_GROUP, i2 + g * EXPERTS_PER_GROUP, s1, s2)
            if best is None:
                best = cand
            else:
                take = cand[0] > best[0]
                best = tuple(jnp.where(take, c, b) for c, b in zip(cand, best))
        _, e1, e2, s1, s2 = best
        gate_ref[0] = jnp.concatenate([s1, s2], axis=0) / (s1 + s2)

        iota_e = lax.broadcasted_iota(I32, (N_EXPERTS, TT), 0).astype(F32)
        in0 = iota_e == e1
        in1 = iota_e == e2
        member = jnp.where(in0 | in1, 1.0, 0.0)
        cnt = jnp.sum(member, axis=1, keepdims=True)
        cnt_ref[0] = jnp.broadcast_to(cnt, (N_EXPERTS, LANES))
        nch = jnp.floor((cnt + (CH - 1)) * (1.0 / CH))
        nch_pad = jnp.concatenate([jnp.broadcast_to(nch, (N_EXPERTS, LANES)),
                                   jnp.zeros((LANES - N_EXPERTS, LANES), F32)], axis=0).astype(BF16)
        return in0, in1, member.astype(BF16), nch_pad

    def positions(in0, in1, member, nch_pad):
        padoff = _dot(lt_ref[...], nch_pad)[:, 0:1] * float(CH)
        rank = _dot(member, before_ref[...])
        posf = padoff + rank
        pos0 = jnp.sum(jnp.where(in0, posf, 0.0), axis=0, keepdims=True).astype(I32)
        pos1 = jnp.sum(jnp.where(in1, posf, 0.0), axis=0, keepdims=True).astype(I32)
        pos = jnp.concatenate([pos0, pos1], axis=0)
        pos_ref[0] = pos
        return pos

    kfull = jnp.concatenate([kp_ref[0], kc_ref[0], kn_ref[0]], axis=0)
    vtfull = jnp.concatenate([vtp_ref[0], vtc_ref[0], vtn_ref[0]], axis=1)
    grp = B_HEADS // B_KV_HEADS
    n_qb = TT // WINDOW
    units = [(jb, g) for jb in range(n_qb) for g in range(B_KV_HEADS)]
    key_i = lax.broadcasted_iota(I32, (3 * WINDOW, 1), 0)

    def scores(u):
        jb, g = units[u]
        kb = kfull[jb * WINDOW:(jb + 3) * WINDOW, g * HEAD_DIM:(g + 1) * HEAD_DIM]
        qt = jnp.concatenate(
            [qt_ref[0, h * HEAD_DIM:(h + 1) * HEAD_DIM, jb * WINDOW:(jb + 1) * WINDOW]
             for h in range(g * grp, (g + 1) * grp)], axis=1)
        s = _dot(kb, qt) + bias_ref[g]
        if jb == 0 or jb == n_qb - 1:
            kpos = t0 + (jb - 1) * WINDOW + key_i
            s = jnp.where((kpos >= 0) & (kpos < seq_len), s, NEG_INF)
        s_scr[u % N_SBUF] = s

    def values(u):
        jb, g = units[u]
        s = s_scr[u % N_SBUF]
        sink = sink_ref[g:g + 1, :]
        mx = jnp.maximum(jnp.max(s, axis=0, keepdims=True), sink)
        p = jnp.exp2(s - mx)
        den = jnp.sum(p, axis=0, keepdims=True) + jnp.exp2(sink - mx)
        vt = vtfull[g * HEAD_DIM:(g + 1) * HEAD_DIM, jb * WINDOW:(jb + 3) * WINDOW]
        ot = _dot(vt, p.astype(BF16)) / den
        for hh in range(grp):
            h = g * grp + hh
            ot_scr[h * HEAD_DIM:(h + 1) * HEAD_DIM, jb * WINDOW:(jb + 1) * WINDOW] = (
                ot[:, hh * WINDOW:(hh + 1) * WINDOW])

    for u in range(N_SBUF - 1):
        scores(u)
    y_other = None
    for u in range(len(units)):
        if u + N_SBUF - 1 < len(units):
            scores(u + N_SBUF - 1)
        if u == 2:
            y_other = _dot(ya_ref[0], wo_ref[0:A_WIDTH, :]) + bo_ref[...]
        if u == 5:
            y_other = y_other + _dot(yc_ref[0], wo_ref[A_WIDTH + B_WIDTH:, :])
        if u >= len(units) - N_SORT_EARLY:
            sort_block(u - (len(units) - N_SORT_EARLY))
        values(u)
    yb = jnp.transpose(ot_scr[...]).astype(BF16)
    y = y_other + _dot(yb, wo_ref[A_WIDTH:A_WIDTH + B_WIDTH, :])
    early = N_SORT_BLOCKS - 1
    for j in range(N_SORT_EARLY, early):
        sort_block(j)
    new_h2, logits = moe_input(y)
    for j in range(early, N_SORT_BLOCKS):
        sort_block(j)
    new_pos = positions(*choose(logits))
    pos_scr[...] = new_pos
    h2_scr[...] = new_h2


def _mixer_call(seq_len, qt, k, vt, bias, sink, yc, ya, x, mod, layer, w_out, b_out, ln_g, ln_b, rw, rbias):
    bsz, seq, d = x.shape
    nt = seq // TT
    n_tiles = bsz * nt
    kb = TT // WINDOW
    grp = B_HEADS // B_KV_HEADS
    const2 = lambda n: (0, 0)

    def at_tile(fn):
        def index_map(n):
            t = jnp.minimum(n, n_tiles - 1)
            return fn(t // nt, t % nt)
        return index_map

    row = at_tile(lambda b, i: (b, i, 0))
    colblk = at_tile(lambda b, i: (b, 0, i))
    prev_k = at_tile(lambda b, i: (b, jnp.maximum(i * kb - 1, 0), 0))
    next_k = at_tile(lambda b, i: (b, jnp.minimum((i + 1) * kb, seq // WINDOW - 1), 0))
    prev_v = at_tile(lambda b, i: (b, 0, jnp.maximum(i * kb - 1, 0)))
    next_v = at_tile(lambda b, i: (b, 0, jnp.minimum((i + 1) * kb, seq // WINDOW - 1)))
    tile = at_tile(lambda b, i: (b * nt + i, 0, 0))
    lt = (jnp.arange(LANES)[None, :] < jnp.arange(N_EXPERTS)[:, None]).astype(BF16)
    before = (jnp.arange(TT)[:, None] < jnp.arange(TT)[None, :]).astype(BF16)
    return pl.pallas_call(
        functools.partial(_mixer_kernel, seq_len, nt),
        out_shape=[
            jax.ShapeDtypeStruct((bsz, seq, d), F32),
            jax.ShapeDtypeStruct(((n_tiles + 1) * R_TILE, d), BF16),
            jax.ShapeDtypeStruct((n_tiles, TOP_K, TT), I32),
            jax.ShapeDtypeStruct((n_tiles, TOP_K, TT), F32),
            jax.ShapeDtypeStruct((n_tiles, N_EXPERTS, LANES), F32),
        ],
        grid=(n_tiles + 1,),
        in_specs=[
            pl.BlockSpec((1, B_WIDTH, TT), colblk),
            pl.BlockSpec((1, WINDOW, KV_WIDTH), prev_k),
            pl.BlockSpec((1, TT, KV_WIDTH), row),
            pl.BlockSpec((1, WINDOW, KV_WIDTH), next_k),
            pl.BlockSpec((1, KV_WIDTH, WINDOW), prev_v),
            pl.BlockSpec((1, KV_WIDTH, TT), colblk),
            pl.BlockSpec((1, KV_WIDTH, WINDOW), next_v),
            pl.BlockSpec((B_KV_HEADS, 3 * WINDOW, grp * WINDOW), lambda n: (0, 0, 0)),
            pl.BlockSpec((B_KV_HEADS, grp * WINDOW), const2),
            pl.BlockSpec((1, TT, C_WIDTH), row),
            pl.BlockSpec((1, TT, A_WIDTH), row),
            pl.BlockSpec((1, TT, d), row),
            pl.BlockSpec((None, None, 6, d), at_tile(lambda b, i: (layer, b, 0, 0))),
            pl.BlockSpec((d, d), const2),
            pl.BlockSpec((1, d), const2),
            pl.BlockSpec((1, d), const2),
            pl.BlockSpec((1, d), const2),
            pl.BlockSpec((N_EXPERTS, d), const2),
            pl.BlockSpec((N_EXPERTS, 1), const2),
            pl.BlockSpec((N_EXPERTS, LANES), const2),
            pl.BlockSpec((TT, TT), const2),
        ],
        out_specs=[
            pl.BlockSpec((1, TT, d), row),
            pl.BlockSpec((R_TILE, d), lambda n: (n, 0)),
            pl.BlockSpec((1, TOP_K, TT), tile),
            pl.BlockSpec((1, TOP_K, TT), tile),
            pl.BlockSpec((1, N_EXPERTS, LANES), tile),
        ],
        scratch_shapes=[
            pltpu.VMEM((B_WIDTH, TT), F32),
            pltpu.VMEM((N_SBUF, 3 * WINDOW, grp * WINDOW), F32),
            pltpu.VMEM((TT, d), BF16),
            pltpu.VMEM((TOP_K, TT), I32),
        ],
        compiler_params=pltpu.CompilerParams(
            dimension_semantics=("arbitrary",), vmem_limit_bytes=VMEM_LIMIT),
        name="mixer_router_sort",
    )(qt, k, k, k, vt, vt, vt, bias, sink, yc, ya, x, mod, w_out, b_out, ln_g, ln_b, rw, rbias, lt, before)


def _expert_kernel(layer, src_ref, dst_ref, bexp_ref, enext_ref, nused_ref, xs_hbm, wg_hbm, wu_hbm, wd_hbm,
                   ys_hbm, xbuf, ybuf, wg_st, wu_st, wd_st, wgb, wub, wdb, sem_in, sem_out, sem_w):
    n_blk = bexp_ref.shape[0]
    nused = nused_ref[0]
    chunks_per_tile = R_TILE // CH
    zero_chunk = chunks_per_tile - 1
    scratch_chunk = 0
    de = wgb.shape[2]
    half = de // 2

    def in_copy(chunk, s, c):
        return pltpu.make_async_copy(
            xs_hbm.at[pl.ds(pl.multiple_of(chunk * CH, CH), CH)],
            xbuf.at[s, pl.ds(c * CH, CH)], sem_in.at[s])

    def out_copy(chunk, s, c):
        return pltpu.make_async_copy(
            ybuf.at[s, pl.ds(c * CH, CH)],
            ys_hbm.at[pl.ds(pl.multiple_of(chunk * CH, CH), CH)], sem_out.at[s])

    def weight_copies(e, s):
        return [pltpu.make_async_copy(wg_hbm.at[layer, e], wg_st.at[s], sem_w.at[s]),
                pltpu.make_async_copy(wu_hbm.at[layer, e], wu_st.at[s], sem_w.at[s]),
                pltpu.make_async_copy(wd_hbm.at[layer, e], wd_st.at[s], sem_w.at[s])]

    def gather_chunk(b, c):
        return jnp.where(b < nused, src_ref[jnp.minimum(b, n_blk - 1) * NCH + c], zero_chunk)

    def write_back_chunk(b, c):
        real = (b >= 0) & (b < nused)
        return jnp.where(real, dst_ref[jnp.clip(b, 0, n_blk - 1) * NCH + c], scratch_chunk + (b % 2) * NCH + c)

    ybuf[...] = jnp.zeros_like(ybuf)
    wgb[...] = jnp.zeros_like(wgb)
    wub[...] = jnp.zeros_like(wub)
    wdb[...] = jnp.zeros_like(wdb)
    for b in range(4):
        for c in range(NCH):
            in_copy(gather_chunk(b, c), b, c).start()
    for b in range(2):
        for c in range(NCH):
            out_copy(scratch_chunk + b * NCH + c, b, c).start()

    @pl.when(nused > 0)
    def _():
        for cp in weight_copies(bexp_ref[0], 0):
            cp.start()

    def pair(i, n_changes):
        blocks = (2 * i, 2 * i + 1)
        wslot = []
        for blk in blocks:
            jb = jnp.minimum(blk, n_blk - 1)
            expert = bexp_ref[jb]
            prev_expert = bexp_ref[jnp.maximum(jb - 1, 0)]
            change = (blk < nused) & ((blk == 0) | (expert != prev_expert))

            @pl.when(change)
            def _(expert=expert, jb=jb, n_changes=n_changes):
                ws = n_changes % 2
                for cp in weight_copies(expert, ws):
                    cp.wait()
                wgb[ws] = wg_st[ws].astype(BF16)
                wub[ws] = wu_st[ws].astype(BF16)
                wdb[ws] = wd_st[ws].astype(BF16)
                upcoming = enext_ref[jb]

                @pl.when(upcoming != expert)
                def _():
                    for cp in weight_copies(upcoming, 1 - ws):
                        cp.start()

            n_changes = n_changes + change.astype(I32)
            wslot.append(jnp.maximum(n_changes - 1, 0) % 2)

        for blk in blocks:
            for c in range(NCH):
                in_copy(0, blk % N_XRING, c).wait()
            for c in range(NCH):
                out_copy(0, blk % N_YRING, c).wait()

        dmas = []
        for blk in blocks:
            for c in range(NCH):
                dmas.append(functools.partial(
                    lambda b, c: out_copy(write_back_chunk(b, c), b % N_YRING, c).start(), blk - 2, c))
        for blk in blocks:
            for c in range(NCH):
                dmas.append(functools.partial(
                    lambda b, c: in_copy(gather_chunk(b, c), b % N_XRING, c).start(priority=1), blk + 4, c))
        n_groups = 8
        per_group = len(dmas) // n_groups

        def issue(gidx):
            for start in dmas[gidx * per_group:(gidx + 1) * per_group]:
                start()

        gidx = 0
        for k, blk in enumerate(blocks):
            x = xbuf[blk % N_XRING]
            ws = wslot[k]
            hmid = []
            for h in range(2):
                cols = slice(h * half, (h + 1) * half)
                g = _dot(x, wgb[ws, :, cols])
                issue(gidx)
                u = _dot(x, wub[ws, :, cols])
                issue(gidx + 1)
                gidx += 2
                hmid.append(((g * _sigmoid(g)) * u).astype(BF16))
            y = _dot(hmid[0], wdb[ws, 0:half, :]) + _dot(hmid[1], wdb[ws, half:, :])
            ybuf[blk % N_YRING] = y.astype(BF16)
        return n_changes

    n_pairs = (nused + 1) // 2
    lax.fori_loop(0, n_pairs, pair, jnp.int32(0))

    last = 2 * n_pairs
    for blk in range(4):
        for c in range(NCH):
            in_copy(0, (last + blk) % N_XRING, c).wait()
    for blk in (last, last + 1):
        for c in range(NCH):
            out_copy(0, blk % N_YRING, c).wait()
    for blk in (last - 2, last - 1):
        for c in range(NCH):
            out_copy(write_back_chunk(blk, c), blk % N_YRING, c).start()
    for blk in (last - 2, last - 1):
        for c in range(NCH):
            out_copy(0, blk % N_YRING, c).wait()


def _expert_call(src, dst, bexp, enext, nused, xs, layer, w_gate, w_up, w_down):
    d = xs.shape[-1]
    de = w_gate.shape[-1]
    grid_spec = pltpu.PrefetchScalarGridSpec(
        num_scalar_prefetch=5,
        grid=(1,),
        in_specs=[pl.BlockSpec(memory_space=pl.ANY)] * 4,
        out_specs=pl.BlockSpec(memory_space=pl.ANY),
        scratch_shapes=[
            pltpu.VMEM((N_XRING, BM, d), BF16),
            pltpu.VMEM((N_YRING, BM, d), BF16),
            pltpu.VMEM((2, d, de), F32),
            pltpu.VMEM((2, d, de), F32),
            pltpu.VMEM((2, de, d), F32),
            pltpu.VMEM((2, d, de), BF16),
            pltpu.VMEM((2, d, de), BF16),
            pltpu.VMEM((2, de, d), BF16),
            pltpu.SemaphoreType.DMA((N_XRING,)),
            pltpu.SemaphoreType.DMA((N_YRING,)),
            pltpu.SemaphoreType.DMA((2,)),
        ],
    )
    return pl.pallas_call(
        functools.partial(_expert_kernel, layer),
        out_shape=jax.ShapeDtypeStruct(xs.shape, xs.dtype),
        grid_spec=grid_spec,
        input_output_aliases={5: 0},
        compiler_params=pltpu.CompilerParams(
            dimension_semantics=("arbitrary",), vmem_limit_bytes=VMEM_LIMIT),
        name="moe_experts",
    )(src, dst, bexp, enext, nused, xs, w_gate, w_up, w_down)


def _combine_kernel(ysa_ref, ysb_ref, pos_ref, gate_ref, x1_ref, mod_ref, lng_ref, lnb_ref, o_ref):
    m = mod_ref[...]
    for k, ys_ref in enumerate((ysa_ref, ysb_ref)):
        rows = slice(k * TT, (k + 1) * TT)
        posc = jnp.transpose(pos_ref[k].astype(F32)).astype(I32)
        gatec = jnp.transpose(gate_ref[k])
        iota_c = lax.broadcasted_iota(I32, (TT, R_TILE), 1)
        pick = jnp.where(iota_c == posc[:, 0:1], gatec[:, 0:1],
                         jnp.where(iota_c == posc[:, 1:2], gatec[:, 1:2], 0.0)).astype(BF16)
        y = _dot(pick, ys_ref[...])
        o_ref[rows, :] = _ln(ALPHA * x1_ref[rows, :] + (1.0 + m[5:6]) * y, lng_ref[...], lnb_ref[...])


def _combine_call(ys, pos, gate, x1, mod, layer, tiles_per_seq, ln_g, ln_b):
    t, d = x1.shape
    n_tiles = t // TT
    assert tiles_per_seq % 2 == 0
    const2 = lambda n: (0, 0)
    return pl.pallas_call(
        _combine_kernel,
        out_shape=jax.ShapeDtypeStruct((t, d), F32),
        grid=(n_tiles // 2,),
        in_specs=[
            pl.BlockSpec((R_TILE, d), lambda n: (2 * n + 1, 0)),
            pl.BlockSpec((R_TILE, d), lambda n: (2 * n + 2, 0)),
            pl.BlockSpec((2, TOP_K, TT), lambda n: (n, 0, 0)),
            pl.BlockSpec((2, TOP_K, TT), lambda n: (n, 0, 0)),
            pl.BlockSpec((2 * TT, d), lambda n: (n, 0)),
            pl.BlockSpec((None, None, 6, d), lambda n: (layer, 2 * n // tiles_per_seq, 0, 0)),
            pl.BlockSpec((1, d), const2),
            pl.BlockSpec((1, d), const2),
        ],
        out_specs=pl.BlockSpec((2 * TT, d), lambda n: (n, 0)),
        compiler_params=pltpu.CompilerParams(
            dimension_semantics=("arbitrary",), vmem_limit_bytes=VMEM_LIMIT),
        name="moe_combine",
    )(ys, ys, pos, gate, x1, mod, ln_g, ln_b)


def _dispatch_plan(cnt, n_blocks):
    n_tiles = cnt.shape[0]
    chunks_per_tile = R_TILE // CH
    nch = (cnt + CH - 1) // CH
    padoff_ch = jnp.cumsum(nch, axis=1) - nch
    tot = jnp.sum(nch, axis=0)
    totpad = (tot + NCH - 1) // NCH * NCH
    eend = jnp.cumsum(totpad)
    ebase = eend - totpad
    tbase = jnp.cumsum(nch, axis=0) - nch
    start = (ebase[None, :] + tbase).T.reshape(-1)
    base = ((jnp.arange(n_tiles, dtype=I32)[:, None] + 1) * chunks_per_tile + padoff_ch).T.reshape(-1)
    vals = jnp.stack([start, base, nch.T.reshape(-1)], axis=1)
    delta = vals - jnp.concatenate([jnp.zeros((1, 3), I32), vals[:-1]], axis=0)
    digits = jnp.concatenate([delta // LANES, delta % LANES], axis=1).astype(BF16)
    slot = jnp.arange(n_blocks * NCH, dtype=I32)
    started = (start[None, :] <= slot[:, None]).astype(BF16)
    got = jnp.dot(started, digits, preferred_element_type=F32).astype(I32)
    seg = got[:, :3] * LANES + got[:, 3:]
    j = slot - seg[:, 0]
    valid = (j < seg[:, 2]) & (slot < eend[-1])
    zero_chunk = chunks_per_tile - 1
    src = jnp.where(valid, seg[:, 1] + j, zero_chunk).astype(I32)
    blk = slot // NCH
    scratch = (blk % 2) * NCH + slot % NCH
    dst = jnp.where(valid, src, scratch).astype(I32)
    first = jnp.arange(n_blocks, dtype=I32) * NCH
    bexp = jnp.minimum(jnp.sum(eend[None, :] <= first[:, None], axis=1), N_EXPERTS - 1).astype(I32)
    nused = (eend[-1] // NCH).astype(I32).reshape(1)
    ids = jnp.arange(N_EXPERTS, dtype=I32)
    later = jnp.where((ids[None, :] > ids[:, None]) & (totpad > 0)[None, :], ids[None, :], N_EXPERTS)
    next_of = jnp.min(later, axis=1)
    next_of = jnp.where(next_of == N_EXPERTS, ids, next_of)
    enext = jnp.sum(jnp.where(bexp[:, None] == ids[None, :], next_of[None, :], 0), axis=1).astype(I32)
    return src, dst, bexp, enext, nused


def _t5_bucket(rel):
    nb = N_BUCKETS // 2
    max_exact = nb // 2
    ret = jnp.where(rel > 0, nb, 0)
    n = jnp.abs(rel)
    nf = jnp.maximum(n, 1).astype(jnp.float32)
    large = max_exact + (jnp.log(nf / max_exact) / math.log(MAX_DISTANCE / max_exact)
                         * (nb - max_exact)).astype(jnp.int32)
    large = jnp.minimum(large, nb - 1)
    return ret + jnp.where(n < max_exact, n, large)


def _band_bias(rel_bias):
    qi = jnp.arange(WINDOW)
    kj = jnp.arange(3 * WINDOW)
    rel = kj[None, :] - WINDOW - qi[:, None]
    pick = _t5_bucket(rel)[:, :, None, None] == jnp.arange(N_BUCKETS)[None, None, :, None]
    bias = jnp.sum(jnp.where(pick, rel_bias.astype(F32)[None, None], 0.0), axis=2) * LOG2E
    bias = jnp.where((jnp.abs(rel) <= WINDOW)[:, :, None], bias, NEG_INF)
    grp = B_HEADS // B_KV_HEADS
    bias = jnp.transpose(bias, (2, 1, 0)).reshape(B_KV_HEADS, grp, 3 * WINDOW, WINDOW)
    return jnp.transpose(bias, (0, 2, 1, 3)).reshape(B_KV_HEADS, 3 * WINDOW, grp * WINDOW)


def kernel(x, c, ada_w, ada_b, w_in, b_in, gmlp_ln_g, gmlp_ln_b, gmlp_ws, gmlp_bs, attn_sink, conv_w,
           conv_b, conv_ln_g, conv_ln_b, w_out, b_out, ln_mix_g, ln_mix_b, w_gate, w_up, w_down,
           ln_ffn_g, ln_ffn_b, rel_bias, router_w, router_bias):
    bsz, seq, d = x.shape
    n_layers = ada_w.shape[0]
    t = bsz * seq
    nt = seq // TT
    n_tiles = t // TT
    max_chunks = n_tiles * (TOP_K * TT // CH + N_EXPERTS) + N_EXPERTS * (NCH - 1)
    n_blocks = -(-max_chunks // NCH)

    mod = _ada_call(c, ada_w, ada_b).reshape(n_layers, bsz, 6, d)
    bias = _band_bias(rel_bias)
    rw = router_w.T.astype(BF16)
    rbias = router_bias.astype(F32).reshape(N_EXPERTS, 1)
    row = lambda a: a.reshape(1, -1)

    for l in range(n_layers):
        bsb = jnp.repeat(gmlp_bs[l].T, HEAD_DIM, axis=1)
        ya, qt, k, vt, yc = _inproj_call(
            x, mod, l, w_in[l], b_in[l], row(gmlp_ln_g[l]), row(gmlp_ln_b[l]),
            gmlp_ws[l].astype(BF16), bsb, conv_w[l], row(conv_b[l]), row(conv_ln_g[l]), row(conv_ln_b[l]))
        sink = jnp.repeat(attn_sink[l].astype(F32) * LOG2E, WINDOW).reshape(B_KV_HEADS, -1)
        x1, xs, pos, gate, cntb = _mixer_call(
            seq, qt, k, vt, bias, sink, yc, ya, x, mod, l, w_out[l].astype(BF16),
            row(b_out[l]), row(ln_mix_g[l]), row(ln_mix_b[l]), rw, rbias)
        cnt = cntb[:, :, 0].astype(I32)
        src, dst, bexp, enext, nused = _dispatch_plan(cnt, n_blocks)
        ys = _expert_call(src, dst, bexp, enext, nused, xs, l, w_gate, w_up, w_down)
        x = _combine_call(ys, pos, gate, x1.reshape(t, d), mod, l, nt, row(ln_ffn_g[l]),
                          row(ln_ffn_b[l])).reshape(bsz, seq, d)
    return x
```

```python
import functools
import math

import jax
import jax.numpy as jnp
from jax import lax
from jax.experimental import pallas as pl
from jax.experimental.pallas import tpu as pltpu

F32 = jnp.float32
BF16 = jnp.bfloat16
I32 = jnp.int32

D_MODEL = 1024
DEPTH = 2
HEAD_DIM = 64
A_WIDTH = 256
A_HEADS = 4
CHUNK = 128
B_WIDTH = 512
B_HEADS = 8
B_KV_HEADS = 2
KV_WIDTH = B_KV_HEADS * HEAD_DIM
WINDOW = 128
N_BUCKETS = 32
MAX_DISTANCE = 128
C_WIDTH = 256
CONV_WIDTH = 31
CONV_PAD = CONV_WIDTH // 2
IN_WIDTH = 2 * A_WIDTH + B_WIDTH + 2 * KV_WIDTH + 2 * C_WIDTH
N_EXPERTS = 32
N_GROUPS = 4
EXPERTS_PER_GROUP = N_EXPERTS // N_GROUPS
TOP_K = 2
D_EXPERT = D_MODEL // 2
ALPHA = (2 * DEPTH) ** 0.25
LN_EPS = 1e-5
NEG_INF = -1e30
LOG2E = 1.4426950408889634
Q_SCALE = HEAD_DIM ** -0.5 * LOG2E

LANES = 128
SUBLANES = 8
BF16_SUBLANES = 16
VMEM_LIMIT = 48 * 1024 * 1024

ADA_TN = 1536
TS = 1024
TT = 512
CH = BF16_SUBLANES
R_TILE = 1536
BM = 512
NCH = BM // CH
N_XRING = 6
N_YRING = 4
N_SORT_BLOCKS = 6
N_SBUF = 2
N_COMBINE_RING = 3
HALO = 16

assert R_TILE >= TOP_K * TT + N_EXPERTS * (CH - 1) + CH
assert R_TILE >= 2 * NCH * CH + CH


def _sigmoid(x):
    return 1.0 / (1.0 + jnp.exp(-x))


def _gelu_tanh(x):
    return x * (0.5 * (1.0 + jnp.tanh(0.7978845608028654 * (x + 0.044715 * (x * x * x)))))


def _ln(x, g, b):
    mu = jnp.mean(x, axis=-1, keepdims=True)
    xc = x - mu
    var = jnp.mean(xc * xc, axis=-1, keepdims=True)
    return xc * lax.rsqrt(var + LN_EPS) * g + b


def _dot(a, b):
    return jnp.dot(a, b, preferred_element_type=F32)


def _dot_nt(a, b):
    return lax.dot_general(a, b, (((1,), (1,)), ((), ())), preferred_element_type=F32)


def _ada_kernel(c_ref, w_ref, b_ref, o_ref):
    c = c_ref[...]
    s = (c * _sigmoid(c)).astype(BF16)
    o_ref[0] = _dot(s, w_ref[0].astype(BF16)) + b_ref[0]


def _ada_call(c, ada_w, ada_b):
    nl, d, n = ada_w.shape
    bsz = c.shape[0]
    return pl.pallas_call(
        _ada_kernel,
        out_shape=jax.ShapeDtypeStruct((nl, bsz, n), F32),
        grid=(nl, n // ADA_TN),
        in_specs=[
            pl.BlockSpec((bsz, d), lambda l, j: (0, 0)),
            pl.BlockSpec((1, d, ADA_TN), lambda l, j: (l, 0, j)),
            pl.BlockSpec((1, 1, ADA_TN), lambda l, j: (l, 0, j)),
        ],
        out_specs=pl.BlockSpec((1, bsz, ADA_TN), lambda l, j: (l, 0, j)),
        compiler_params=pltpu.CompilerParams(
            dimension_semantics=("arbitrary", "arbitrary"), vmem_limit_bytes=VMEM_LIMIT),
        name="ada_mod",
    )(c, ada_w, ada_b.reshape(nl, 1, n))


def _inproj_kernel(x_ref, xp_ref, xn_ref, mod_ref, w_ref, b_ref, wqt_ref, bq_ref, wvt_ref, bv_ref,
                   lng_ref, lnb_ref, ws_ref, bsb_ref, cw_ref, cb_ref, clg_ref, clb_ref,
                   ya_ref, qt_ref, k_ref, vt_ref, yc_ref, conv_scr, z_scr):
    i = pl.program_id(1)
    n_i = pl.num_programs(1)
    m = mod_ref[...]

    def modulate(xv):
        return (xv * (1.0 + m[1:2]) + m[0:1]).astype(BF16)

    hb = modulate(x_ref[0])
    col_u, col_v = 0, A_WIDTH
    col_k = 2 * A_WIDTH + B_WIDTH
    col_a = col_k + 2 * KV_WIDTH
    col_g = col_a + C_WIDTH

    def proj(lhs, c0, width):
        return _dot(lhs, w_ref[:, c0:c0 + width]) + b_ref[:, c0:c0 + width]

    hx = jnp.concatenate([modulate(xp_ref[0]), hb, modulate(xn_ref[0])], axis=0)
    yg = proj(hx, col_a, C_WIDTH) * _sigmoid(proj(hx, col_g, C_WIDTH))
    conv_scr[0:HALO, :] = yg[0:HALO] * jnp.where(i > 0, 1.0, 0.0)
    conv_scr[HALO:HALO + TS, :] = yg[HALO:HALO + TS]
    conv_scr[HALO + TS:, :] = yg[HALO + TS:] * jnp.where(i < n_i - 1, 1.0, 0.0)
    first = HALO - CONV_PAD
    acc = jnp.zeros((TS, C_WIDTH), F32) + cb_ref[...]
    for r in range(SUBLANES):
        z = None
        for a in range(-(-(first + CONV_WIDTH) // SUBLANES)):
            w = a * SUBLANES + r - first
            if 0 <= w < CONV_WIDTH:
                term = conv_scr[a * SUBLANES:a * SUBLANES + TS + SUBLANES, :] * cw_ref[w:w + 1, :]
                z = term if z is None else z + term
        if r == 0:
            acc = acc + z[0:TS]
        else:
            z_scr[r] = z
            acc = acc + z_scr[r, r:r + TS, :]
    yc = _ln(acc, clg_ref[...], clb_ref[...])
    yc_ref[0] = (yc * _sigmoid(yc)).astype(BF16)

    u = _gelu_tanh(proj(hb, col_u, A_WIDTH))
    v = _gelu_tanh(proj(hb, col_v, A_WIDTH))
    qt_ref[0] = ((_dot_nt(wqt_ref[...], hb) + bq_ref[...]) * Q_SCALE).astype(BF16)
    k_ref[0] = proj(hb, col_k, KV_WIDTH).astype(BF16)
    vt_ref[0] = (_dot_nt(wvt_ref[...], hb) + bv_ref[...]).astype(BF16)

    vb = _ln(v, lng_ref[...], lnb_ref[...]).astype(BF16)
    head_of_lane = lax.broadcasted_iota(I32, (CHUNK, A_WIDTH), 1) // HEAD_DIM
    for ch in range(TS // CHUNK):
        vc = vb[ch * CHUNK:(ch + 1) * CHUNK]
        acc = bsb_ref[...]
        for hh in range(A_HEADS):
            acc = acc + _dot(ws_ref[hh], jnp.where(head_of_lane == hh, vc, jnp.zeros_like(vc)))
        ya_ref[0, ch * CHUNK:(ch + 1) * CHUNK, :] = (u[ch * CHUNK:(ch + 1) * CHUNK] * acc).astype(BF16)


def _inproj_call(x, mod, layer, w_in, b_in, ln_g, ln_b, ws, bsb, conv_w, conv_b, conv_ln_g, conv_ln_b):
    bsz, seq, d = x.shape
    grid = (bsz, seq // TS)
    hb = TS // HALO
    const2 = lambda b, i: (0, 0)
    row = lambda b, i: (b, i, 0)
    colblk = lambda b, i: (b, 0, i)
    prev_h = lambda b, i: (b, jnp.maximum(i * hb - 1, 0), 0)
    next_h = lambda b, i: (b, jnp.minimum((i + 1) * hb, seq // HALO - 1), 0)
    q0 = 2 * A_WIDTH
    v0 = q0 + B_WIDTH + KV_WIDTH
    wb = w_in.astype(BF16)
    wqt = w_in[:, q0:q0 + B_WIDTH].T.astype(BF16)
    wvt = w_in[:, v0:v0 + KV_WIDTH].T.astype(BF16)
    bq = b_in[q0:q0 + B_WIDTH].reshape(B_WIDTH, 1)
    bv = b_in[v0:v0 + KV_WIDTH].reshape(KV_WIDTH, 1)

    def out(width):
        return jax.ShapeDtypeStruct((bsz, seq, width), BF16), pl.BlockSpec((1, TS, width), row)

    def out_t(width):
        return jax.ShapeDtypeStruct((bsz, width, seq), BF16), pl.BlockSpec((1, width, TS), colblk)

    outs = [out(A_WIDTH), out_t(B_WIDTH), out(KV_WIDTH), out_t(KV_WIDTH), out(C_WIDTH)]
    return pl.pallas_call(
        _inproj_kernel,
        out_shape=[o[0] for o in outs],
        grid=grid,
        in_specs=[
            pl.BlockSpec((1, TS, d), row),
            pl.BlockSpec((1, HALO, d), prev_h),
            pl.BlockSpec((1, HALO, d), next_h),
            pl.BlockSpec((None, None, 6, d), lambda b, i: (layer, b, 0, 0)),
            pl.BlockSpec((d, IN_WIDTH), const2),
            pl.BlockSpec((1, IN_WIDTH), const2),
            pl.BlockSpec((B_WIDTH, d), const2),
            pl.BlockSpec((B_WIDTH, 1), const2),
            pl.BlockSpec((KV_WIDTH, d), const2),
            pl.BlockSpec((KV_WIDTH, 1), const2),
            pl.BlockSpec((1, A_WIDTH), const2),
            pl.BlockSpec((1, A_WIDTH), const2),
            pl.BlockSpec((A_HEADS, CHUNK, CHUNK), lambda b, i: (0, 0, 0)),
            pl.BlockSpec((CHUNK, A_WIDTH), const2),
            pl.BlockSpec((CONV_WIDTH, C_WIDTH), const2),
            pl.BlockSpec((1, C_WIDTH), const2),
            pl.BlockSpec((1, C_WIDTH), const2),
            pl.BlockSpec((1, C_WIDTH), const2),
        ],
        out_specs=[o[1] for o in outs],
        scratch_shapes=[
            pltpu.VMEM((TS + 2 * HALO, C_WIDTH), F32),
            pltpu.VMEM((SUBLANES, TS + SUBLANES, C_WIDTH), F32),
        ],
        compiler_params=pltpu.CompilerParams(
            dimension_semantics=("arbitrary", "arbitrary"), vmem_limit_bytes=VMEM_LIMIT),
        name="inproj_gmlp_conv",
    )(x, x, x, mod, wb, b_in.reshape(1, -1), wqt, bq, wvt, bv, ln_g, ln_b, ws, bsb,
      conv_w, conv_b, conv_ln_g, conv_ln_b)


def _first_argmax(vals, iota_f, width):
    m = jnp.max(vals, axis=0, keepdims=True)
    idx = jnp.min(jnp.where(vals == m, iota_f, float(width)), axis=0, keepdims=True)
    return m, idx


def _mixer_kernel(seq_len, tiles_per_seq, qt_ref, kp_ref, kc_ref, kn_ref, vtp_ref, vtc_ref, vtn_ref,
                  bias_ref, sink_ref, yc_ref, ya_ref, x_ref, mod_ref, wo_ref, bo_ref, lng_ref, lnb_ref,
                  rw_ref, rb_ref, lt_ref, before_ref, x1_ref, xs_ref, pos_ref, gate_ref, cnt_ref,
                  ot_scr, s_scr, h2_scr, pos_scr):
    n = pl.program_id(0)
    last_tile = pl.num_programs(0) - 2
    i = jnp.minimum(n, last_tile) % tiles_per_seq
    t0 = i * TT
    m = mod_ref[...]

    @pl.when(n == 0)
    def _():
        h2_scr[...] = jnp.zeros_like(h2_scr)
        pos_scr[...] = jnp.zeros_like(pos_scr)

    pos_prev = pos_scr[...]
    sort_rows = R_TILE // N_SORT_BLOCKS

    def sort_block(j):
        iota_r = j * sort_rows + lax.broadcasted_iota(I32, (sort_rows, TT), 0)
        onehot = jnp.where(iota_r == pos_prev[0:1], 1.0,
                           jnp.where(iota_r == pos_prev[1:2], 1.0, 0.0)).astype(BF16)
        xs_ref[j * sort_rows:(j + 1) * sort_rows, :] = _dot(onehot, h2_scr[...]).astype(BF16)

    def moe_input(y):
        x1 = _ln(ALPHA * x_ref[0] + (1.0 + m[2:3]) * y, lng_ref[...], lnb_ref[...])
        x1_ref[0] = x1
        h2 = (x1 * (1.0 + m[4:5]) + m[3:4]).astype(BF16)
        return h2, _dot_nt(rw_ref[...], h2)

    def choose(logits):
        scores = _sigmoid(logits)
        sel = scores + rb_ref[...]
        iota_f = lax.broadcasted_iota(I32, (EXPERTS_PER_GROUP, TT), 0).astype(F32)
        best = None
        for g in range(N_GROUPS):
            sl = slice(g * EXPERTS_PER_GROUP, (g + 1) * EXPERTS_PER_GROUP)
            sg = sel[sl]
            m1, i1 = _first_argmax(sg, iota_f, EXPERTS_PER_GROUP)
            m2, i2 = _first_argmax(jnp.where(iota_f == i1, -jnp.inf, sg), iota_f, EXPERTS_PER_GROUP)
            sc = scores[sl]
            s1 = jnp.sum(jnp.where(iota_f == i1, sc, 0.0), axis=0, keepdims=True)
            s2 = jnp.sum(jnp.where(iota_f == i2, sc, 0.0), axis=0, keepdims=True)
            cand = (m1 + m2, i1 + g * EXPERTS_PER_GROUP, i2 + g * EXPERTS_PER_GROUP, s1, s2)
            if best is None:
                best = cand
            else:
                take = cand[0] > best[0]
                best = tuple(jnp.where(take, c, b) for c, b in zip(cand, best))
        _, e1, e2, s1, s2 = best
        gate_ref[0] = jnp.concatenate([s1, s2], axis=0) / (s1 + s2)

        iota_e = lax.broadcasted_iota(I32, (N_EXPERTS, TT), 0).astype(F32)
        in0 = iota_e == e1
        in1 = iota_e == e2
        member = jnp.where(in0 | in1, 1.0, 0.0)
        cnt = jnp.sum(member, axis=1, keepdims=True)
        cnt_ref[0] = jnp.broadcast_to(cnt, (N_EXPERTS, LANES))
        nch = jnp.floor((cnt + (CH - 1)) * (1.0 / CH))
        nch_pad = jnp.concatenate([jnp.broadcast_to(nch, (N_EXPERTS, LANES)),
                                   jnp.zeros((LANES - N_EXPERTS, LANES), F32)], axis=0).astype(BF16)
        return in0, in1, member.astype(BF16), nch_pad

    def positions(in0, in1, member, nch_pad):
        padoff = _dot(lt_ref[...], nch_pad)[:, 0:1] * float(CH)
        rank = _dot(member, before_ref[...])
        posf = padoff + rank
        pos0 = jnp.sum(jnp.where(in0, posf, 0.0), axis=0, keepdims=True).astype(I32)
        pos1 = jnp.sum(jnp.where(in1, posf, 0.0), axis=0, keepdims=True).astype(I32)
        pos = jnp.concatenate([pos0, pos1], axis=0)
        pos_ref[0] = pos
        return pos

    kfull = jnp.concatenate([kp_ref[0], kc_ref[0], kn_ref[0]], axis=0)
    vtfull = jnp.concatenate([vtp_ref[0], vtc_ref[0], vtn_ref[0]], axis=1)
    grp = B_HEADS // B_KV_HEADS
    n_qb = TT // WINDOW
    units = [(jb, g) for jb in range(n_qb) for g in range(B_KV_HEADS)]
    key_i = lax.broadcasted_iota(I32, (3 * WINDOW, 1), 0)

    def scores(u):
        jb, g = units[u]
        kb = kfull[jb * WINDOW:(jb + 3) * WINDOW, g * HEAD_DIM:(g + 1) * HEAD_DIM]
        qt = jnp.concatenate(
            [qt_ref[0, h * HEAD_DIM:(h + 1) * HEAD_DIM, jb * WINDOW:(jb + 1) * WINDOW]
             for h in range(g * grp, (g + 1) * grp)], axis=1)
        s = _dot(kb, qt) + bias_ref[g]
        if jb == 0 or jb == n_qb - 1:
            kpos = t0 + (jb - 1) * WINDOW + key_i
            s = jnp.where((kpos >= 0) & (kpos < seq_len), s, NEG_INF)
        s_scr[u % N_SBUF] = s

    def values(u):
        jb, g = units[u]
        s = s_scr[u % N_SBUF]
        sink = sink_ref[g:g + 1, :]
        mx = jnp.maximum(jnp.max(s, axis=0, keepdims=True), sink)
        p = jnp.exp2(s - mx)
        den = jnp.sum(p, axis=0, keepdims=True) + jnp.exp2(sink - mx)
        vt = vtfull[g * HEAD_DIM:(g + 1) * HEAD_DIM, jb * WINDOW:(jb + 3) * WINDOW]
        ot = _dot(vt, p.astype(BF16)) / den
        for hh in range(grp):
            h = g * grp + hh
            ot_scr[h * HEAD_DIM:(h + 1) * HEAD_DIM, jb * WINDOW:(jb + 1) * WINDOW] = (
                ot[:, hh * WINDOW:(hh + 1) * WINDOW])

    for u in range(N_SBUF - 1):
        scores(u)
    y_other = None
    for u in range(len(units)):
        if u + N_SBUF - 1 < len(units):
            scores(u + N_SBUF - 1)
        if u == 2:
            y_other = _dot(ya_ref[0], wo_ref[0:A_WIDTH, :]) + bo_ref[...]
        if u == 5:
            y_other = y_other + _dot(yc_ref[0], wo_ref[A_WIDTH + B_WIDTH:, :])
        values(u)
    yb = jnp.transpose(ot_scr[...]).astype(BF16)
    y = y_other + _dot(yb, wo_ref[A_WIDTH:A_WIDTH + B_WIDTH, :])
    early = N_SORT_BLOCKS - 1
    for j in range(early):
        sort_block(j)
    new_h2, logits = moe_input(y)
    for j in range(early, N_SORT_BLOCKS):
        sort_block(j)
    new_pos = positions(*choose(logits))
    pos_scr[...] = new_pos
    h2_scr[...] = new_h2


def _mixer_call(seq_len, qt, k, vt, bias, sink, yc, ya, x, mod, layer, w_out, b_out, ln_g, ln_b, rw, rbias):
    bsz, seq, d = x.shape
    nt = seq // TT
    n_tiles = bsz * nt
    kb = TT // WINDOW
    grp = B_HEADS // B_KV_HEADS
    const2 = lambda n: (0, 0)

    def at_tile(fn):
        def index_map(n):
            t = jnp.minimum(n, n_tiles - 1)
            return fn(t // nt, t % nt)
        return index_map

    row = at_tile(lambda b, i: (b, i, 0))
    colblk = at_tile(lambda b, i: (b, 0, i))
    prev_k = at_tile(lambda b, i: (b, jnp.maximum(i * kb - 1, 0), 0))
    next_k = at_tile(lambda b, i: (b, jnp.minimum((i + 1) * kb, seq // WINDOW - 1), 0))
    prev_v = at_tile(lambda b, i: (b, 0, jnp.maximum(i * kb - 1, 0)))
    next_v = at_tile(lambda b, i: (b, 0, jnp.minimum((i + 1) * kb, seq // WINDOW - 1)))
    tile = at_tile(lambda b, i: (b * nt + i, 0, 0))
    lt = (jnp.arange(LANES)[None, :] < jnp.arange(N_EXPERTS)[:, None]).astype(BF16)
    before = (jnp.arange(TT)[:, None] < jnp.arange(TT)[None, :]).astype(BF16)
    return pl.pallas_call(
        functools.partial(_mixer_kernel, seq_len, nt),
        out_shape=[
            jax.ShapeDtypeStruct((bsz, seq, d), F32),
            jax.ShapeDtypeStruct(((n_tiles + 1) * R_TILE, d), BF16),
            jax.ShapeDtypeStruct((n_tiles, TOP_K, TT), I32),
            jax.ShapeDtypeStruct((n_tiles, TOP_K, TT), F32),
            jax.ShapeDtypeStruct((n_tiles, N_EXPERTS, LANES), F32),
        ],
        grid=(n_tiles + 1,),
        in_specs=[
            pl.BlockSpec((1, B_WIDTH, TT), colblk),
            pl.BlockSpec((1, WINDOW, KV_WIDTH), prev_k),
            pl.BlockSpec((1, TT, KV_WIDTH), row),
            pl.BlockSpec((1, WINDOW, KV_WIDTH), next_k),
            pl.BlockSpec((1, KV_WIDTH, WINDOW), prev_v),
            pl.BlockSpec((1, KV_WIDTH, TT), colblk),
            pl.BlockSpec((1, KV_WIDTH, WINDOW), next_v),
            pl.BlockSpec((B_KV_HEADS, 3 * WINDOW, grp * WINDOW), lambda n: (0, 0, 0)),
            pl.BlockSpec((B_KV_HEADS, grp * WINDOW), const2),
            pl.BlockSpec((1, TT, C_WIDTH), row),
            pl.BlockSpec((1, TT, A_WIDTH), row),
            pl.BlockSpec((1, TT, d), row),
            pl.BlockSpec((None, None, 6, d), at_tile(lambda b, i: (layer, b, 0, 0))),
            pl.BlockSpec((d, d), const2),
            pl.BlockSpec((1, d), const2),
            pl.BlockSpec((1, d), const2),
            pl.BlockSpec((1, d), const2),
            pl.BlockSpec((N_EXPERTS, d), const2),
            pl.BlockSpec((N_EXPERTS, 1), const2),
            pl.BlockSpec((N_EXPERTS, LANES), const2),
            pl.BlockSpec((TT, TT), const2),
        ],
        out_specs=[
            pl.BlockSpec((1, TT, d), row),
            pl.BlockSpec((R_TILE, d), lambda n: (n, 0)),
            pl.BlockSpec((1, TOP_K, TT), tile),
            pl.BlockSpec((1, TOP_K, TT), tile),
            pl.BlockSpec((1, N_EXPERTS, LANES), tile),
        ],
        scratch_shapes=[
            pltpu.VMEM((B_WIDTH, TT), F32),
            pltpu.VMEM((N_SBUF, 3 * WINDOW, grp * WINDOW), F32),
            pltpu.VMEM((TT, d), BF16),
            pltpu.VMEM((TOP_K, TT), I32),
        ],
        compiler_params=pltpu.CompilerParams(
            dimension_semantics=("arbitrary",), vmem_limit_bytes=VMEM_LIMIT),
        name="mixer_router_sort",
    )(qt, k, k, k, vt, vt, vt, bias, sink, yc, ya, x, mod, w_out, b_out, ln_g, ln_b, rw, rbias, lt, before)


def _expert_kernel(layer, src_ref, dst_ref, bexp_ref, enext_ref, nused_ref, xs_hbm, wg_hbm, wu_hbm, wd_hbm,
                   ys_hbm, xbuf, ybuf, wg_st, wu_st, wd_st, wgb, wub, wdb, sem_in, sem_out, sem_w):
    n_blk = bexp_ref.shape[0]
    nused = nused_ref[0]
    chunks_per_tile = R_TILE // CH
    zero_chunk = chunks_per_tile - 1
    scratch_chunk = 0
    de = wgb.shape[2]
    half = de // 2

    def in_copy(chunk, s, c):
        return pltpu.make_async_copy(
            xs_hbm.at[pl.ds(pl.multiple_of(chunk * CH, CH), CH)],
            xbuf.at[s, pl.ds(c * CH, CH)], sem_in.at[s])

    def out_copy(chunk, s, c):
        return pltpu.make_async_copy(
            ybuf.at[s, pl.ds(c * CH, CH)],
            ys_hbm.at[pl.ds(pl.multiple_of(chunk * CH, CH), CH)], sem_out.at[s])

    def weight_copies(e, s):
        return [pltpu.make_async_copy(wg_hbm.at[layer, e], wg_st.at[s], sem_w.at[s]),
                pltpu.make_async_copy(wu_hbm.at[layer, e], wu_st.at[s], sem_w.at[s]),
                pltpu.make_async_copy(wd_hbm.at[layer, e], wd_st.at[s], sem_w.at[s])]

    def gather_chunk(b, c):
        return jnp.where(b < nused, src_ref[jnp.minimum(b, n_blk - 1) * NCH + c], zero_chunk)

    def write_back_chunk(b, c):
        real = (b >= 0) & (b < nused)
        return jnp.where(real, dst_ref[jnp.clip(b, 0, n_blk - 1) * NCH + c], scratch_chunk + (b % 2) * NCH + c)

    ybuf[...] = jnp.zeros_like(ybuf)
    wgb[...] = jnp.zeros_like(wgb)
    wub[...] = jnp.zeros_like(wub)
    wdb[...] = jnp.zeros_like(wdb)
    for b in range(4):
        for c in range(NCH):
            in_copy(gather_chunk(b, c), b, c).start()
    for b in range(2):
        for c in range(NCH):
            out_copy(scratch_chunk + b * NCH + c, b, c).start()

    @pl.when(nused > 0)
    def _():
        for cp in weight_copies(bexp_ref[0], 0):
            cp.start()

    def pair(i, n_changes):
        blocks = (2 * i, 2 * i + 1)
        wslot = []
        for blk in blocks:
            jb = jnp.minimum(blk, n_blk - 1)
            expert = bexp_ref[jb]
            prev_expert = bexp_ref[jnp.maximum(jb - 1, 0)]
            change = (blk < nused) & ((blk == 0) | (expert != prev_expert))

            @pl.when(change)
            def _(expert=expert, jb=jb, n_changes=n_changes):
                ws = n_changes % 2
                for cp in weight_copies(expert, ws):
                    cp.wait()
                wgb[ws] = wg_st[ws].astype(BF16)
                wub[ws] = wu_st[ws].astype(BF16)
                wdb[ws] = wd_st[ws].astype(BF16)
                upcoming = enext_ref[jb]

                @pl.when(upcoming != expert)
                def _():
                    for cp in weight_copies(upcoming, 1 - ws):
                        cp.start()

            n_changes = n_changes + change.astype(I32)
            wslot.append(jnp.maximum(n_changes - 1, 0) % 2)

        for blk in blocks:
            for c in range(NCH):
                in_copy(0, blk % N_XRING, c).wait()
            for c in range(NCH):
                out_copy(0, blk % N_YRING, c).wait()

        dmas = []
        for blk in blocks:
            for c in range(NCH):
                dmas.append(functools.partial(
                    lambda b, c: out_copy(write_back_chunk(b, c), b % N_YRING, c).start(), blk - 2, c))
        for blk in blocks:
            for c in range(NCH):
                dmas.append(functools.partial(
                    lambda b, c: in_copy(gather_chunk(b, c), b % N_XRING, c).start(priority=1), blk + 4, c))
        n_groups = 8
        per_group = len(dmas) // n_groups

        def issue(gidx):
            for start in dmas[gidx * per_group:(gidx + 1) * per_group]:
                start()

        gidx = 0
        for k, blk in enumerate(blocks):
            x = xbuf[blk % N_XRING]
            ws = wslot[k]
            hmid = []
            for h in range(2):
                cols = slice(h * half, (h + 1) * half)
                g = _dot(x, wgb[ws, :, cols])
                issue(gidx)
                u = _dot(x, wub[ws, :, cols])
                issue(gidx + 1)
                gidx += 2
                hmid.append(((g * _sigmoid(g)) * u).astype(BF16))
            y = _dot(hmid[0], wdb[ws, 0:half, :]) + _dot(hmid[1], wdb[ws, half:, :])
            ybuf[blk % N_YRING] = y.astype(BF16)
        return n_changes

    n_pairs = (nused + 1) // 2
    lax.fori_loop(0, n_pairs, pair, jnp.int32(0))

    last = 2 * n_pairs
    for blk in range(4):
        for c in range(NCH):
            in_copy(0, (last + blk) % N_XRING, c).wait()
    for blk in (last, last + 1):
        for c in range(NCH):
            out_copy(0, blk % N_YRING, c).wait()
    for blk in (last - 2, last - 1):
        for c in range(NCH):
            out_copy(write_back_chunk(blk, c), blk % N_YRING, c).start()
    for blk in (last - 2, last - 1):
        for c in range(NCH):
            out_copy(0, blk % N_YRING, c).wait()


def _expert_call(src, dst, bexp, enext, nused, xs, layer, w_gate, w_up, w_down):
    d = xs.shape[-1]
    de = w_gate.shape[-1]
    grid_spec = pltpu.PrefetchScalarGridSpec(
        num_scalar_prefetch=5,
        grid=(1,),
        in_specs=[pl.BlockSpec(memory_space=pl.ANY)] * 4,
        out_specs=pl.BlockSpec(memory_space=pl.ANY),
        scratch_shapes=[
            pltpu.VMEM((N_XRING, BM, d), BF16),
            pltpu.VMEM((N_YRING, BM, d), BF16),
            pltpu.VMEM((2, d, de), F32),
            pltpu.VMEM((2, d, de), F32),
            pltpu.VMEM((2, de, d), F32),
            pltpu.VMEM((2, d, de), BF16),
            pltpu.VMEM((2, d, de), BF16),
            pltpu.VMEM((2, de, d), BF16),
            pltpu.SemaphoreType.DMA((N_XRING,)),
            pltpu.SemaphoreType.DMA((N_YRING,)),
            pltpu.SemaphoreType.DMA((2,)),
        ],
    )
    return pl.pallas_call(
        functools.partial(_expert_kernel, layer),
        out_shape=jax.ShapeDtypeStruct(xs.shape, xs.dtype),
        grid_spec=grid_spec,
        input_output_aliases={5: 0},
        compiler_params=pltpu.CompilerParams(
            dimension_semantics=("arbitrary",), vmem_limit_bytes=VMEM_LIMIT),
        name="moe_experts",
    )(src, dst, bexp, enext, nused, xs, w_gate, w_up, w_down)


def _combine_kernel(ys_hbm, pos_ref, gate_ref, x1_ref, mod_ref, lng_ref, lnb_ref, o_ref, ys_buf, ys_sem):
    n = pl.program_id(0)
    n_steps = pl.num_programs(0)

    def ys_copy(step):
        slot = step % N_COMBINE_RING
        return pltpu.make_async_copy(ys_hbm.at[pl.ds((2 * step + 1) * R_TILE, 2 * R_TILE)],
                                     ys_buf.at[slot], ys_sem.at[slot])

    @pl.when(n == 0)
    def _():
        for s in range(N_COMBINE_RING - 1):
            ys_copy(s).start()

    @pl.when(n + N_COMBINE_RING - 1 < n_steps)
    def _():
        ys_copy(n + N_COMBINE_RING - 1).start()

    ys_copy(n).wait()
    ys_ref = ys_buf.at[n % N_COMBINE_RING]
    m = mod_ref[...]
    for k in range(2):
        rows = slice(k * TT, (k + 1) * TT)
        posc = jnp.transpose(pos_ref[k].astype(F32)).astype(I32)
        gatec = jnp.transpose(gate_ref[k])
        iota_c = lax.broadcasted_iota(I32, (TT, R_TILE), 1)
        pick = jnp.where(iota_c == posc[:, 0:1], gatec[:, 0:1],
                         jnp.where(iota_c == posc[:, 1:2], gatec[:, 1:2], 0.0)).astype(BF16)
        y = _dot(pick, ys_ref[k * R_TILE:(k + 1) * R_TILE, :])
        o_ref[rows, :] = _ln(ALPHA * x1_ref[rows, :] + (1.0 + m[5:6]) * y, lng_ref[...], lnb_ref[...])


def _combine_call(ys, pos, gate, x1, mod, layer, tiles_per_seq, ln_g, ln_b):
    t, d = x1.shape
    n_tiles = t // TT
    assert tiles_per_seq % 2 == 0
    assert n_tiles // 2 >= N_COMBINE_RING - 1
    const2 = lambda n: (0, 0)
    return pl.pallas_call(
        _combine_kernel,
        out_shape=jax.ShapeDtypeStruct((t, d), F32),
        grid=(n_tiles // 2,),
        in_specs=[
            pl.BlockSpec(memory_space=pl.ANY),
            pl.BlockSpec((2, TOP_K, TT), lambda n: (n, 0, 0)),
            pl.BlockSpec((2, TOP_K, TT), lambda n: (n, 0, 0)),
            pl.BlockSpec((2 * TT, d), lambda n: (n, 0)),
            pl.BlockSpec((None, None, 6, d), lambda n: (layer, 2 * n // tiles_per_seq, 0, 0)),
            pl.BlockSpec((1, d), const2),
            pl.BlockSpec((1, d), const2),
        ],
        out_specs=pl.BlockSpec((2 * TT, d), lambda n: (n, 0)),
        scratch_shapes=[pltpu.VMEM((N_COMBINE_RING, 2 * R_TILE, d), BF16),
                        pltpu.SemaphoreType.DMA((N_COMBINE_RING,))],
        compiler_params=pltpu.CompilerParams(
            dimension_semantics=("arbitrary",), vmem_limit_bytes=VMEM_LIMIT),
        name="moe_combine",
    )(ys, pos, gate, x1, mod, ln_g, ln_b)


def _dispatch_plan(cnt, n_blocks):
    n_tiles = cnt.shape[0]
    chunks_per_tile = R_TILE // CH
    nch = (cnt + CH - 1) // CH
    padoff_ch = jnp.cumsum(nch, axis=1) - nch
    tot = jnp.sum(nch, axis=0)
    totpad = (tot + NCH - 1) // NCH * NCH
    eend = jnp.cumsum(totpad)
    ebase = eend - totpad
    tbase = jnp.cumsum(nch, axis=0) - nch
    start = (ebase[None, :] + tbase).T.reshape(-1)
    base = ((jnp.arange(n_tiles, dtype=I32)[:, None] + 1) * chunks_per_tile + padoff_ch).T.reshape(-1)
    vals = jnp.stack([start, base, nch.T.reshape(-1)], axis=1)
    delta = vals - jnp.concatenate([jnp.zeros((1, 3), I32), vals[:-1]], axis=0)
    digits = jnp.concatenate([delta // LANES, delta % LANES], axis=1).astype(BF16)
    slot = jnp.arange(n_blocks * NCH, dtype=I32)
    started = (start[None, :] <= slot[:, None]).astype(BF16)
    got = jnp.dot(started, digits, preferred_element_type=F32).astype(I32)
    seg = got[:, :3] * LANES + got[:, 3:]
    j = slot - seg[:, 0]
    valid = (j < seg[:, 2]) & (slot < eend[-1])
    zero_chunk = chunks_per_tile - 1
    src = jnp.where(valid, seg[:, 1] + j, zero_chunk).astype(I32)
    blk = slot // NCH
    scratch = (blk % 2) * NCH + slot % NCH
    dst = jnp.where(valid, src, scratch).astype(I32)
    first = jnp.arange(n_blocks, dtype=I32) * NCH
    bexp = jnp.minimum(jnp.sum(eend[None, :] <= first[:, None], axis=1), N_EXPERTS - 1).astype(I32)
    nused = (eend[-1] // NCH).astype(I32).reshape(1)
    ids = jnp.arange(N_EXPERTS, dtype=I32)
    later = jnp.where((ids[None, :] > ids[:, None]) & (totpad > 0)[None, :], ids[None, :], N_EXPERTS)
    next_of = jnp.min(later, axis=1)
    next_of = jnp.where(next_of == N_EXPERTS, ids, next_of)
    enext = jnp.sum(jnp.where(bexp[:, None] == ids[None, :], next_of[None, :], 0), axis=1).astype(I32)
    return src, dst, bexp, enext, nused


def _t5_bucket(rel):
    nb = N_BUCKETS // 2
    max_exact = nb // 2
    ret = jnp.where(rel > 0, nb, 0)
    n = jnp.abs(rel)
    nf = jnp.maximum(n, 1).astype(jnp.float32)
    large = max_exact + (jnp.log(nf / max_exact) / math.log(MAX_DISTANCE / max_exact)
                         * (nb - max_exact)).astype(jnp.int32)
    large = jnp.minimum(large, nb - 1)
    return ret + jnp.where(n < max_exact, n, large)


def _band_bias(rel_bias):
    qi = jnp.arange(WINDOW)
    kj = jnp.arange(3 * WINDOW)
    rel = kj[None, :] - WINDOW - qi[:, None]
    pick = _t5_bucket(rel)[:, :, None, None] == jnp.arange(N_BUCKETS)[None, None, :, None]
    bias = jnp.sum(jnp.where(pick, rel_bias.astype(F32)[None, None], 0.0), axis=2) * LOG2E
    bias = jnp.where((jnp.abs(rel) <= WINDOW)[:, :, None], bias, NEG_INF)
    grp = B_HEADS // B_KV_HEADS
    bias = jnp.transpose(bias, (2, 1, 0)).reshape(B_KV_HEADS, grp, 3 * WINDOW, WINDOW)
    return jnp.transpose(bias, (0, 2, 1, 3)).reshape(B_KV_HEADS, 3 * WINDOW, grp * WINDOW)


def kernel(x, c, ada_w, ada_b, w_in, b_in, gmlp_ln_g, gmlp_ln_b, gmlp_ws, gmlp_bs, attn_sink, conv_w,
           conv_b, conv_ln_g, conv_ln_b, w_out, b_out, ln_mix_g, ln_mix_b, w_gate, w_up, w_down,
           ln_ffn_g, ln_ffn_b, rel_bias, router_w, router_bias):
    bsz, seq, d = x.shape
    n_layers = ada_w.shape[0]
    t = bsz * seq
    nt = seq // TT
    n_tiles = t // TT
    max_chunks = n_tiles * (TOP_K * TT // CH + N_EXPERTS) + N_EXPERTS * (NCH - 1)
    n_blocks = -(-max_chunks // NCH)

    mod = _ada_call(c, ada_w, ada_b).reshape(n_layers, bsz, 6, d)
    bias = _band_bias(rel_bias)
    rw = router_w.T.astype(BF16)
    rbias = router_bias.astype(F32).reshape(N_EXPERTS, 1)
    row = lambda a: a.reshape(1, -1)

    for l in range(n_layers):
        bsb = jnp.repeat(gmlp_bs[l].T, HEAD_DIM, axis=1)
        ya, qt, k, vt, yc = _inproj_call(
            x, mod, l, w_in[l], b_in[l], row(gmlp_ln_g[l]), row(gmlp_ln_b[l]),
            gmlp_ws[l].astype(BF16), bsb, conv_w[l], row(conv_b[l]), row(conv_ln_g[l]), row(conv_ln_b[l]))
        sink = jnp.repeat(attn_sink[l].astype(F32) * LOG2E, WINDOW).reshape(B_KV_HEADS, -1)
        x1, xs, pos, gate, cntb = _mixer_call(
            seq, qt, k, vt, bias, sink, yc, ya, x, mod, l, w_out[l].astype(BF16),
            row(b_out[l]), row(ln_mix_g[l]), row(ln_mix_b[l]), rw, rbias)
        cnt = cntb[:, :, 0].astype(I32)
        src, dst, bexp, enext, nused = _dispatch_plan(cnt, n_blocks)
        ys = _expert_call(src, dst, bexp, enext, nused, xs, l, w_gate, w_up, w_down)
        x = _combine_call(ys, pos, gate, x1.reshape(t, d), mod, l, nt, row(ln_ffn_g[l]),
                          row(ln_ffn_b[l])).reshape(bsz, seq, d)
    return x
```
